```python
import jax, jax.numpy as jnp
from jax import lax
import numpy as np

D_MODEL = 1024
BATCH = 4
SEQ = 4096
DEPTH = 1
DEC_BATCH = 128
DEC_SEQ = 8
PAST_LEN = 16384
PAGE_SIZE = 128

HEAD_DIM = 64
ATTN_WIDTH = D_MODEL // 2
N_HEADS = ATTN_WIDTH // HEAD_DIM
N_KV_HEADS = N_HEADS // 4
GROUP = N_HEADS // N_KV_HEADS
KV_WIDTH = N_KV_HEADS * HEAD_DIM
WINDOW = 128
BLOCK = WINDOW
POOL_WIDTH = D_MODEL - ATTN_WIDTH
POOL_WINDOWS = (2, 4, 8, 16)
N_POOL_GROUPS = len(POOL_WINDOWS)
POOL_GROUP_WIDTH = POOL_WIDTH // N_POOL_GROUPS
POOL_HIST = max(POOL_WINDOWS) - 1
IN_WIDTH = ATTN_WIDTH + 2 * KV_WIDTH + POOL_WIDTH
N_MEM = 256
N_CROSS_HEADS = 4
CROSS_HEAD_DIM = D_MODEL // N_CROSS_HEADS
D_FF = 4 * D_MODEL
RMS_EPS = 1e-5
NEG_INF = -1e30

kernel_name = 'hymba_swa_sink_pool_memxattn_step'


def _rmsnorm(x, g):
    xf = x.astype(jnp.float32)
    xf = xf * lax.rsqrt(jnp.mean(xf * xf, axis=-1, keepdims=True) + RMS_EPS)
    return (xf * g.astype(jnp.float32)).astype(x.dtype)


def _alibi_slopes():
    return 2.0 ** (-8.0 * jnp.arange(1, N_HEADS + 1, dtype=jnp.float32) / N_HEADS)


def _mixer_in(x, g_mix, w_in):
    B, L, _ = x.shape
    h = _rmsnorm(x, g_mix)
    proj = h @ w_in
    q = proj[..., :ATTN_WIDTH].reshape(B, L, N_KV_HEADS, GROUP, HEAD_DIM)
    k = proj[..., ATTN_WIDTH:ATTN_WIDTH + KV_WIDTH].reshape(B, L, N_KV_HEADS, HEAD_DIM)
    v = proj[..., ATTN_WIDTH + KV_WIDTH:ATTN_WIDTH + 2 * KV_WIDTH].reshape(B, L, N_KV_HEADS, HEAD_DIM)
    u = proj[..., ATTN_WIDTH + 2 * KV_WIDTH:]
    return q, k, v, u


def _window_probs(scores, qpos, kpos, sinks):
    dist = qpos[..., :, None] - kpos[..., None, :]
    valid = (dist >= 0) & (dist <= WINDOW) & (kpos[..., None, :] >= 0)
    dist = dist[..., None, None, :, :].astype(jnp.float32)
    valid = valid[..., None, None, :, :]
    slopes = _alibi_slopes().reshape(N_KV_HEADS, GROUP, 1, 1)
    logits = jnp.where(valid, scores - slopes * dist, NEG_INF)
    sink = jnp.broadcast_to(sinks.astype(jnp.float32).reshape(N_KV_HEADS, GROUP, 1, 1),
                            logits.shape[:-1] + (1,))
    p = jax.nn.softmax(jnp.concatenate([logits, sink], axis=-1), axis=-1)
    return p[..., :-1]


def _window_attn_prompt(q, k, v, sinks):
    B, S = q.shape[:2]
    nb = S // BLOCK
    qb = q.reshape(B, nb, BLOCK, N_KV_HEADS, GROUP, HEAD_DIM)
    kb = k.reshape(B, nb, BLOCK, N_KV_HEADS, HEAD_DIM)
    vb = v.reshape(B, nb, BLOCK, N_KV_HEADS, HEAD_DIM)
    pad = jnp.zeros_like(kb[:, :1])
    k2 = jnp.concatenate([jnp.concatenate([pad, kb[:, :-1]], axis=1), kb], axis=2)
    v2 = jnp.concatenate([jnp.concatenate([pad, vb[:, :-1]], axis=1), vb], axis=2)
    scores = jnp.einsum('bnqkgd,bnskd->bnkgqs', qb, k2).astype(jnp.float32) * (HEAD_DIM ** -0.5)
    blk = jnp.arange(nb, dtype=jnp.int32)[:, None] * BLOCK
    qpos = blk + jnp.arange(BLOCK, dtype=jnp.int32)[None, :]
    kpos = blk - BLOCK + jnp.arange(2 * BLOCK, dtype=jnp.int32)[None, :]
    p = _window_probs(scores, qpos, kpos, sinks)
    o = jnp.einsum('bnkgqs,bnskd->bnqkgd', p.astype(v.dtype), v2)
    return o.reshape(B, S, ATTN_WIDTH)


def _window_attn_sample(q, k, v, cache_k, cache_v, sinks):
    B, T = q.shape[:2]
    kc = jnp.concatenate([cache_k, k], axis=1)
    vc = jnp.concatenate([cache_v, v], axis=1)
    n_keys = kc.shape[1]
    scores = jnp.einsum('btkgd,bskd->bkgts', q, kc).astype(jnp.float32) * (HEAD_DIM ** -0.5)
    qpos = PAST_LEN + jnp.arange(T, dtype=jnp.int32)
    kpos = PAST_LEN - cache_k.shape[1] + jnp.arange(n_keys, dtype=jnp.int32)
    p = _window_probs(scores, qpos, kpos, sinks)
    o = jnp.einsum('bkgts,bskd->btkgd', p.astype(v.dtype), vc)
    return o.reshape(B, T, ATTN_WIDTH), kc[:, -WINDOW:], vc[:, -WINDOW:]


def _pool_mix(u, u_prev, start_pos, w_pool, pool_scale):
    B, L, _ = u.shape
    full = jnp.concatenate([u_prev, u], axis=1)
    cs = jnp.cumsum(full.astype(jnp.float32), axis=1)
    cs = jnp.pad(cs, ((0, 0), (1, 0), (0, 0)))
    hi = cs[:, POOL_HIST + 1:]
    pos = start_pos + jnp.arange(L, dtype=jnp.int32)
    means = []
    for g, w in enumerate(POOL_WINDOWS):
        ch = slice(g * POOL_GROUP_WIDTH, (g + 1) * POOL_GROUP_WIDTH)
        lo = cs[:, POOL_HIST + 1 - w:POOL_HIST + 1 - w + L, ch]
        cnt = jnp.minimum(pos + 1, w).astype(jnp.float32)[None, :, None]
        means.append((hi[..., ch] - lo) / cnt)
    d = (jnp.concatenate(means, axis=-1) - u.astype(jnp.float32)).astype(u.dtype)
    d = d.reshape(B, L, N_POOL_GROUPS, POOL_GROUP_WIDTH)
    y = jnp.einsum('blgc,gce->blge', d, w_pool).reshape(B, L, POOL_WIDTH)
    return y * pool_scale, full[:, -POOL_HIST:]


def _mem_kv(mem, g_mem, w_ck, w_cv):
    B = mem.shape[0]
    hm = _rmsnorm(mem, g_mem)
    k = (hm @ w_ck).reshape(B, N_MEM, N_CROSS_HEADS, CROSS_HEAD_DIM)
    v = (hm @ w_cv).reshape(B, N_MEM, N_CROSS_HEADS, CROSS_HEAD_DIM)
    return k, v


def _layer_tail(x, attn_o, pool_o, mem_k, mem_v, w_out, g_cross, w_cq, w_co, g_ffn, w_up, w_down):
    B, L, _ = x.shape
    x = x + jnp.concatenate([attn_o, pool_o], axis=-1) @ w_out
    h = _rmsnorm(x, g_cross)
    q = (h @ w_cq).reshape(B, L, N_CROSS_HEADS, CROSS_HEAD_DIM)
    s = jnp.einsum('blhd,bmhd->bhlm', q, mem_k).astype(jnp.float32) * (CROSS_HEAD_DIM ** -0.5)
    p = jax.nn.softmax(s, axis=-1)
    o = jnp.einsum('bhlm,bmhd->blhd', p.astype(mem_v.dtype), mem_v).reshape(B, L, D_MODEL)
    x = x + o @ w_co
    h = _rmsnorm(x, g_ffn)
    x = x + jnp.square(jax.nn.relu(h @ w_up)) @ w_down
    return x


def setup_inputs(seed: int = 0) -> dict:
    key = jax.random.key(seed)
    ks = jax.random.split(key, 26)
    f32 = jnp.float32

    def nrm(k, shape, scale=1.0):
        return jax.random.normal(k, shape, f32) * scale

    def gain(k, shape):
        return 1.0 + 0.05 * jax.random.normal(k, shape, f32)

    return {
        'x_prompt': nrm(ks[0], (BATCH, SEQ, D_MODEL)),
        'x_sample': nrm(ks[1], (DEC_BATCH, DEC_SEQ, D_MODEL)),
        'cache_win_k': nrm(ks[2], (DEPTH, DEC_BATCH, WINDOW, N_KV_HEADS, HEAD_DIM)),
        'cache_win_v': nrm(ks[3], (DEPTH, DEC_BATCH, WINDOW, N_KV_HEADS, HEAD_DIM)),
        'state_pool': nrm(ks[4], (DEPTH, DEC_BATCH, POOL_HIST, POOL_WIDTH)),
        'cache_mem_k': nrm(ks[5], (DEPTH, DEC_BATCH, N_MEM, N_CROSS_HEADS, CROSS_HEAD_DIM)),
        'cache_mem_v': nrm(ks[6], (DEPTH, DEC_BATCH, N_MEM, N_CROSS_HEADS, CROSS_HEAD_DIM)),
        'mem_prompt': nrm(ks[7], (BATCH, N_MEM, D_MODEL)),
        'g_mix': gain(ks[8], (DEPTH, D_MODEL)),
        'w_in': nrm(ks[9], (DEPTH, D_MODEL, IN_WIDTH), D_MODEL ** -0.5),
        'attn_sinks': nrm(ks[10], (DEPTH, N_HEADS)),
        'w_pool': nrm(ks[11], (DEPTH, N_POOL_GROUPS, POOL_GROUP_WIDTH, POOL_GROUP_WIDTH), POOL_GROUP_WIDTH ** -0.5),
        'pool_scale': gain(ks[12], (DEPTH, POOL_WIDTH)),
        'w_out': nrm(ks[13], (DEPTH, D_MODEL, D_MODEL), D_MODEL ** -0.5),
        'g_cross': gain(ks[14], (DEPTH, D_MODEL)),
        'g_mem': gain(ks[15], (DEPTH, D_MODEL)),
        'w_cq': nrm(ks[16], (DEPTH, D_MODEL, D_MODEL), D_MODEL ** -0.5),
        'w_ck': nrm(ks[17], (DEPTH, D_MODEL, D_MODEL), D_MODEL ** -0.5),
        'w_cv': nrm(ks[18], (DEPTH, D_MODEL, D_MODEL), D_MODEL ** -0.5),
        'w_co': nrm(ks[19], (DEPTH, D_MODEL, D_MODEL), D_MODEL ** -0.5),
        'g_ffn': gain(ks[20], (DEPTH, D_MODEL)),
        'w_up': nrm(ks[21], (DEPTH, D_MODEL, D_FF), D_MODEL ** -0.5),
        'w_down': nrm(ks[22], (DEPTH, D_FF, D_MODEL), D_FF ** -0.5),
        'g_final': gain(ks[23], (D_MODEL,)),
    }


def reference(x_prompt, x_sample, cache_win_k, cache_win_v, state_pool, cache_mem_k, cache_mem_v,
              mem_prompt, g_mix, w_in, attn_sinks, w_pool, pool_scale, w_out, g_cross, g_mem,
              w_cq, w_ck, w_cv, w_co, g_ffn, w_up, w_down, g_final):
    xp, xs = x_prompt, x_sample
    wk_p, wv_p, pool_p, mk_p, mv_p = [], [], [], [], []
    wk_s, wv_s, pool_s = [], [], []
    for l in range(DEPTH):
        q, k, v, u = _mixer_in(xp, g_mix[l], w_in[l])
        attn_o = _window_attn_prompt(q, k, v, attn_sinks[l])
        u_prev = jnp.zeros((u.shape[0], POOL_HIST, POOL_WIDTH), u.dtype)
        pool_o, pool_new = _pool_mix(u, u_prev, 0, w_pool[l], pool_scale[l])
        mem_k, mem_v = _mem_kv(mem_prompt, g_mem[l], w_ck[l], w_cv[l])
        xp = _layer_tail(xp, attn_o, pool_o, mem_k, mem_v, w_out[l], g_cross[l], w_cq[l], w_co[l],
                         g_ffn[l], w_up[l], w_down[l])
        wk_p.append(k[:, -WINDOW:])
        wv_p.append(v[:, -WINDOW:])
        pool_p.append(pool_new)
        mk_p.append(mem_k)
        mv_p.append(mem_v)
        q, k, v, u = _mixer_in(xs, g_mix[l], w_in[l])
        attn_o, k_buf, v_buf = _window_attn_sample(q, k, v, cache_win_k[l], cache_win_v[l], attn_sinks[l])
        pool_o, pool_new = _pool_mix(u, state_pool[l], PAST_LEN, w_pool[l], pool_scale[l])
        xs = _layer_tail(xs, attn_o, pool_o, cache_mem_k[l], cache_mem_v[l], w_out[l], g_cross[l],
                         w_cq[l], w_co[l], g_ffn[l], w_up[l], w_down[l])
        wk_s.append(k_buf)
        wv_s.append(v_buf)
        pool_s.append(pool_new)
    y_prompt = _rmsnorm(xp, g_final)
    y_sample = _rmsnorm(xs, g_final)
    return (y_prompt, y_sample,
            jnp.stack(wk_p), jnp.stack(wv_p), jnp.stack(pool_p), jnp.stack(mk_p), jnp.stack(mv_p),
            jnp.stack(wk_s), jnp.stack(wv_s), jnp.stack(pool_s))
```

```python
import functools

import jax
import jax.numpy as jnp
import numpy as np
from jax import lax
from jax.experimental import pallas as pl
from jax.experimental.pallas import tpu as pltpu

D_MODEL = 1024
PAST_LEN = 16384
HEAD_DIM = 64
N_HEADS = 8
N_KV_HEADS = 2
GROUP = N_HEADS // N_KV_HEADS
ATTN_WIDTH = N_HEADS * HEAD_DIM
KV_WIDTH = N_KV_HEADS * HEAD_DIM
WINDOW = 128
BLOCK = WINDOW
POOL_WIDTH = D_MODEL - ATTN_WIDTH
POOL_WINDOWS = (2, 4, 8, 16)
POOL_GROUP_WIDTH = 128
POOL_HIST = 15
POOL_PAD = 16
IN_WIDTH = ATTN_WIDTH + 2 * KV_WIDTH + POOL_WIDTH
N_MEM = 256
N_CROSS_HEADS = 4
CROSS_HEAD_DIM = 256
D_FF = 4 * D_MODEL
FF_CHUNK = 1024
RMS_EPS = 1e-5
NEG_INF = -1e30
Q_SCALE = HEAD_DIM ** -0.5
CQ_SCALE = CROSS_HEAD_DIM ** -0.5
K_OFF = ATTN_WIDTH
V_OFF = ATTN_WIDTH + KV_WIDTH
U_OFF = ATTN_WIDTH + 2 * KV_WIDTH

_HEAD_ORDER = (0, 4, 1, 5, 2, 6, 3, 7)
_Q_PERM = np.concatenate([np.arange(h * HEAD_DIM, (h + 1) * HEAD_DIM) for h in _HEAD_ORDER])

VMEM_LIMIT_BYTES = 56 * 1024 * 1024

_BF = jnp.bfloat16
_F32 = jnp.float32


def _slope(h):
    return 2.0 ** (-8.0 * (h + 1) / N_HEADS)


def _dot(a, b):
    return jnp.dot(a, b, preferred_element_type=_F32)


def _rms(x, g):
    ms = jnp.mean(x * x, axis=-1, keepdims=True)
    return x * lax.rsqrt(ms + RMS_EPS) * g


def _softmax_with_sink(s, sink):
    m = jnp.maximum(jnp.max(s, axis=-1, keepdims=True), sink)
    e = jnp.exp(s - m)
    den = jnp.sum(e, axis=-1, keepdims=True) + jnp.exp(sink - m)
    return e * (1.0 / den)


def _softmax(s):
    m = jnp.max(s, axis=-1, keepdims=True)
    e = jnp.exp(s - m)
    return e * (1.0 / jnp.sum(e, axis=-1, keepdims=True))


def _window_sums(u_ext, axis):
    out = {}
    s = u_ext
    w = 1
    while w < max(POOL_WINDOWS):
        s = s + pltpu.roll(s, w, axis)
        w *= 2
        out[w] = s
    return out


def _ffn_final(x2, gffn, wup_ref, wdown_ref, gfinal):
    hn = _rms(x2, gffn).astype(_BF)
    acc = x2
    for c in range(D_FF // FF_CHUNK):
        hc = _dot(hn, wup_ref[:, c * FF_CHUNK:(c + 1) * FF_CHUNK])
        hc = jnp.maximum(hc, 0.0)
        hc = (hc * hc).astype(_BF)
        acc = acc + _dot(hc, wdown_ref[c * FF_CHUNK:(c + 1) * FF_CHUNK, :])
    return _rms(acc, gfinal)


def _prompt_mixer_kernel(x_ref, gmix_ref, win_ref, sink_ref, wpool_ref, pscale_ref, wout_ref,
                         x1_ref, klast_ref, vlast_ref, plast_ref,
                         q_scr, kt_scr, v_scr, u_scr, bias_scr, cat_scr, *, tile):
    b = pl.program_id(0)
    s = pl.program_id(1)
    nb = tile // BLOCK

    @pl.when((b == 0) & (s == 0))
    def _():
        qi = lax.broadcasted_iota(jnp.int32, (BLOCK, 2 * BLOCK), 0)
        kc = lax.broadcasted_iota(jnp.int32, (BLOCK, 2 * BLOCK), 1)
        dist = qi + BLOCK - kc
        valid = (dist >= 0) & (dist <= WINDOW)
        valid_first = valid & (kc >= BLOCK)
        distf = dist.astype(_F32)
        for h in range(N_HEADS):
            ali = -_slope(h) * distf
            bias_scr[0, h] = jnp.where(valid, ali, NEG_INF)
            bias_scr[1, h] = jnp.where(valid_first, ali, NEG_INF)

    @pl.when(s == 0)
    def _():
        kt_scr[0] = jnp.zeros((KV_WIDTH, BLOCK), _BF)
        v_scr[0] = jnp.zeros((BLOCK, KV_WIDTH), _BF)
        u_scr[0:POOL_PAD, :] = jnp.zeros((POOL_PAD, POOL_WIDTH), _F32)

    x = x_ref[0]
    xn = _rms(x, gmix_ref[...]).astype(_BF)
    proj = _dot(xn, win_ref[...])
    q_scr[...] = proj[:, :ATTN_WIDTH] * Q_SCALE
    k = proj[:, K_OFF:K_OFF + KV_WIDTH]
    v = proj[:, V_OFF:V_OFF + KV_WIDTH]
    for j in range(nb):
        kt_scr[j + 1] = k[j * BLOCK:(j + 1) * BLOCK].T.astype(_BF)
        v_scr[j + 1] = v[j * BLOCK:(j + 1) * BLOCK].astype(_BF)
    u_scr[POOL_PAD:, :] = proj[:, U_OFF:]
    klast_ref[0] = k[tile - BLOCK:]
    vlast_ref[0] = v[tile - BLOCK:]
    plast_ref[0] = proj[tile - POOL_PAD:, U_OFF:]

    lane = lax.broadcasted_iota(jnp.int32, (BLOCK, 2 * HEAD_DIM), 1)
    lo = lane < HEAD_DIM
    row = lax.broadcasted_iota(jnp.int32, (BLOCK, POOL_GROUP_WIDTH), 0)

    def block_body(j, carry):
        r0 = pl.multiple_of(j * BLOCK, BLOCK)
        first = ((s == 0) & (j == 0)).astype(jnp.int32)
        kt2 = jnp.concatenate([kt_scr[j], kt_scr[j + 1]], axis=1)
        v2 = jnp.concatenate([v_scr[j], v_scr[j + 1]], axis=0)
        for p in range(GROUP):
            slab = q_scr[pl.ds(r0, BLOCK), p * 128:(p + 1) * 128]
            outs = []
            for half, h in ((0, p), (1, p + GROUP)):
                qm = jnp.where(lo if half == 0 else ~lo, slab, 0.0).astype(_BF)
                sc = _dot(qm, kt2) + bias_scr[first, h]
                pr = _softmax_with_sink(sc, sink_ref[h])
                outs.append(_dot(pr.astype(_BF), v2))
            cat_scr[pl.ds(r0, BLOCK), p * 128:(p + 1) * 128] = jnp.where(lo, outs[0], outs[1]).astype(_BF)
        pos = s * tile + r0 + row
        for g, w in enumerate(POOL_WINDOWS):
            c0 = g * POOL_GROUP_WIDTH
            u_ext = u_scr[pl.ds(r0, BLOCK + POOL_PAD), c0:c0 + POOL_GROUP_WIDTH]
            sw = _window_sums(u_ext, 0)[w][POOL_PAD:]
            cnt = jnp.minimum(pos + 1, w).astype(_F32)
            d = (sw / cnt - u_ext[POOL_PAD:]).astype(_BF)
            y = _dot(d, wpool_ref[g]) * pscale_ref[:, c0:c0 + POOL_GROUP_WIDTH]
            cat_scr[pl.ds(r0, BLOCK), ATTN_WIDTH + c0:ATTN_WIDTH + c0 + POOL_GROUP_WIDTH] = y.astype(_BF)
        return carry

    lax.fori_loop(0, nb, block_body, 0)

    kt_scr[0] = kt_scr[nb]
    v_scr[0] = v_scr[nb]
    u_scr[0:POOL_PAD, :] = u_scr[tile:tile + POOL_PAD, :]
    x1_ref[0] = x + _dot(cat_scr[...], wout_ref[...])


def _prompt_mixer(x, gmix, win, sinks, wpool, pscale, wout, tile):
    B, S, D = x.shape
    ns = S // tile
    nb = tile // BLOCK
    const = lambda *shape: pl.BlockSpec(shape, lambda b, s: (0,) * len(shape))
    return pl.pallas_call(
        functools.partial(_prompt_mixer_kernel, tile=tile),
        grid=(B, ns),
        in_specs=[
            pl.BlockSpec((1, tile, D), lambda b, s: (b, s, 0)),
            const(1, D),
            const(D, IN_WIDTH),
            pl.BlockSpec(memory_space=pltpu.SMEM),
            const(len(POOL_WINDOWS), POOL_GROUP_WIDTH, POOL_GROUP_WIDTH),
            const(1, POOL_WIDTH),
            const(D, D),
        ],
        out_specs=[
            pl.BlockSpec((1, tile, D), lambda b, s: (b, s, 0)),
            pl.BlockSpec((1, BLOCK, KV_WIDTH), lambda b, s: (b, 0, 0)),
            pl.BlockSpec((1, BLOCK, KV_WIDTH), lambda b, s: (b, 0, 0)),
            pl.BlockSpec((1, POOL_PAD, POOL_WIDTH), lambda b, s: (b, 0, 0)),
        ],
        out_shape=[
            jax.ShapeDtypeStruct((B, S, D), _F32),
            jax.ShapeDtypeStruct((B, BLOCK, KV_WIDTH), _F32),
            jax.ShapeDtypeStruct((B, BLOCK, KV_WIDTH), _F32),
            jax.ShapeDtypeStruct((B, POOL_PAD, POOL_WIDTH), _F32),
        ],
        scratch_shapes=[
            pltpu.VMEM((tile, ATTN_WIDTH), _F32),
            pltpu.VMEM((nb + 1, KV_WIDTH, BLOCK), _BF),
            pltpu.VMEM((nb + 1, BLOCK, KV_WIDTH), _BF),
            pltpu.VMEM((tile + POOL_PAD, POOL_WIDTH), _F32),
            pltpu.VMEM((2, N_HEADS, BLOCK, 2 * BLOCK), _F32),
            pltpu.VMEM((tile, D), _BF),
        ],
        compiler_params=pltpu.CompilerParams(
            dimension_semantics=("arbitrary", "arbitrary"), vmem_limit_bytes=VMEM_LIMIT_BYTES),
        name="prompt_mixer",
    )(x, gmix, win, sinks, wpool, pscale, wout)


def _mem_kv_kernel(mem_ref, gmem_ref, wck_ref, wcv_ref, k_ref, v_ref, kt_ref, vb_ref):
    hm = _rms(mem_ref[0], gmem_ref[...]).astype(_BF)
    k = _dot(hm, wck_ref[...])
    v = _dot(hm, wcv_ref[...])
    k_ref[0] = k
    v_ref[0] = v
    vb_ref[0] = v.astype(_BF)
    for h in range(N_CROSS_HEADS):
        kt_ref[0, h] = k[:, h * CROSS_HEAD_DIM:(h + 1) * CROSS_HEAD_DIM].T.astype(_BF)


def _mem_kv(mem, gmem, wck, wcv):
    B, M, D = mem.shape
    const = lambda *shape: pl.BlockSpec(shape, lambda b: (0,) * len(shape))
    return pl.pallas_call(
        _mem_kv_kernel,
        grid=(B,),
        in_specs=[pl.BlockSpec((1, M, D), lambda b: (b, 0, 0)), const(1, D), const(D, D), const(D, D)],
        out_specs=[
            pl.BlockSpec((1, M, D), lambda b: (b, 0, 0)),
            pl.BlockSpec((1, M, D), lambda b: (b, 0, 0)),
            pl.BlockSpec((1, N_CROSS_HEADS, CROSS_HEAD_DIM, M), lambda b: (b, 0, 0, 0)),
            pl.BlockSpec((1, M, D), lambda b: (b, 0, 0)),
        ],
        out_shape=[
            jax.ShapeDtypeStruct((B, M, D), _F32),
            jax.ShapeDtypeStruct((B, M, D), _F32),
            jax.ShapeDtypeStruct((B, N_CROSS_HEADS, CROSS_HEAD_DIM, M), _BF),
            jax.ShapeDtypeStruct((B, M, D), _BF),
        ],
        compiler_params=pltpu.CompilerParams(
            dimension_semantics=("arbitrary",), vmem_limit_bytes=VMEM_LIMIT_BYTES),
        name="prompt_mem_kv",
    )(mem, gmem, wck, wcv)


def _prompt_tail_kernel(x1_ref, gcross_ref, wcq_ref, kt_ref, vb_ref, wco_ref, gffn_ref, wup_ref, wdown_ref,
                        gfinal_ref, y_ref):
    x1 = x1_ref[0]
    q = (_dot(_rms(x1, gcross_ref[...]).astype(_BF), wcq_ref[...]) * CQ_SCALE).astype(_BF)
    outs = []
    for h in range(N_CROSS_HEADS):
        c0 = h * CROSS_HEAD_DIM
        sc = _dot(q[:, c0:c0 + CROSS_HEAD_DIM], kt_ref[0, h])
        pr = _softmax(sc).astype(_BF)
        outs.append(_dot(pr, vb_ref[0, :, c0:c0 + CROSS_HEAD_DIM]).astype(_BF))
    o = jnp.concatenate(outs, axis=1)
    x2 = x1 + _dot(o, wco_ref[...])
    y_ref[0] = _ffn_final(x2, gffn_ref[...], wup_ref, wdown_ref, gfinal_ref[...])


def _single(shape, index_map):
    return pl.BlockSpec(shape, index_map, pipeline_mode=pl.Buffered(1))


def _prompt_tail(x1, gcross, wcq, kt, vb, wco, gffn, wup, wdown, gfinal, tile):
    B, S, D = x1.shape
    const = lambda *shape: _single(shape, lambda b, s: (0,) * len(shape))
    return pl.pallas_call(
        _prompt_tail_kernel,
        grid=(B, S // tile),
        in_specs=[
            pl.BlockSpec((1, tile, D), lambda b, s: (b, s, 0)),
            const(1, D),
            const(D, D),
            pl.BlockSpec((1, N_CROSS_HEADS, CROSS_HEAD_DIM, N_MEM), lambda b, s: (b, 0, 0, 0)),
            pl.BlockSpec((1, N_MEM, D), lambda b, s: (b, 0, 0)),
            const(D, D),
            const(1, D),
            const(D, D_FF),
            const(D_FF, D),
            const(1, D),
        ],
        out_specs=pl.BlockSpec((1, tile, D), lambda b, s: (b, s, 0)),
        out_shape=jax.ShapeDtypeStruct((B, S, D), _F32),
        compiler_params=pltpu.CompilerParams(
            dimension_semantics=("arbitrary", "arbitrary"), vmem_limit_bytes=VMEM_LIMIT_BYTES),
        name="prompt_tail",
    )(x1, gcross, wcq, kt, vb, wco, gffn, wup, wdown, gfinal)


def _sample_mixer_kernel(x_ref, gmix_ref, win_ref, sinkcol_ref, ck_ref, cv_ref, st_ref, wpool_ref, pscale_ref,
                         wout_ref, gcross_ref, wcq_ref,
                         x1_ref, qc_ref, wk_ref, wv_ref, pool_ref,
                         kc_scr, vc_scr, *, nbatch, nseq, past_len):
    rows = nbatch * nseq
    nq = N_HEADS * nseq
    nkeys = 2 * WINDOW
    x = x_ref[...]
    xn = _rms(x, gmix_ref[...]).astype(_BF)
    proj = _dot(xn, win_ref[...])

    k_new = proj[:, K_OFF:K_OFF + KV_WIDTH].reshape(nbatch, nseq, KV_WIDTH)
    v_new = proj[:, V_OFF:V_OFF + KV_WIDTH].reshape(nbatch, nseq, KV_WIDTH)
    ck = ck_ref[...]
    cv = cv_ref[...]
    wk_ref[:, :WINDOW - nseq, :] = ck[:, nseq:, :]
    wk_ref[:, WINDOW - nseq:, :] = k_new
    wv_ref[:, :WINDOW - nseq, :] = cv[:, nseq:, :]
    wv_ref[:, WINDOW - nseq:, :] = v_new

    zpad = jnp.zeros((nbatch, nkeys - WINDOW, KV_WIDTH), _BF)
    kc_scr[:, WINDOW:, :] = zpad
    vc_scr[:, WINDOW:, :] = zpad
    kc_scr[:, :WINDOW, :] = ck.astype(_BF)
    vc_scr[:, :WINDOW, :] = cv.astype(_BF)
    kc_scr[:, WINDOW:WINDOW + nseq, :] = k_new.astype(_BF)
    vc_scr[:, WINDOW:WINDOW + nseq, :] = v_new.astype(_BF)

    t = lax.broadcasted_iota(jnp.int32, (nseq, nkeys), 0)
    c = lax.broadcasted_iota(jnp.int32, (nseq, nkeys), 1)
    dist = jnp.where(c < WINDOW, WINDOW + t - c, t - (c - WINDOW))
    valid = (dist >= 0) & (dist <= WINDOW) & (c < WINDOW + nseq)
    distf = dist.astype(_F32)
    bias = jnp.concatenate([jnp.where(valid, -_slope(h) * distf, NEG_INF) for h in range(N_HEADS)], axis=0)

    lane = lax.broadcasted_iota(jnp.int32, (rows, 2 * HEAD_DIM), 1)
    lo = lane < HEAD_DIM
    qsc = proj[:, :ATTN_WIDTH] * Q_SCALE
    q_lo, q_hi = [], []
    for p in range(GROUP):
        slab = qsc[:, p * 128:(p + 1) * 128]
        q_lo.append(jnp.where(lo, slab, 0.0).reshape(nbatch, nseq, 128))
        q_hi.append(jnp.where(lo, 0.0, slab).reshape(nbatch, nseq, 128))
    qm = jnp.concatenate(q_lo + q_hi, axis=1).astype(_BF)
    sc = jnp.einsum('bqd,bkd->bqk', qm, kc_scr[...], preferred_element_type=_F32) + bias[None]
    pr = _softmax_with_sink(sc, sinkcol_ref[...][None]).astype(_BF)
    o = jnp.einsum('bqk,bkd->bqd', pr, vc_scr[...], preferred_element_type=_F32)
    attn = []
    for p in range(GROUP):
        o_lo = o[:, p * nseq:(p + 1) * nseq, :].reshape(rows, 128)
        o_hi = o[:, (p + GROUP) * nseq:(p + GROUP + 1) * nseq, :].reshape(rows, 128)
        attn.append(jnp.where(lo, o_lo, o_hi).astype(_BF))

    u_new = proj[:, U_OFF:].reshape(nbatch, nseq, POOL_WIDTH)
    u_ext = jnp.concatenate([st_ref[...], u_new], axis=1)
    pool_ref[...] = u_ext[:, nseq:, :]
    tpos = past_len + lax.broadcasted_iota(jnp.int32, (nseq, POOL_GROUP_WIDTH), 0)
    pooled = []
    for g, w in enumerate(POOL_WINDOWS):
        c0 = g * POOL_GROUP_WIDTH
        ug = u_ext[:, :, c0:c0 + POOL_GROUP_WIDTH]
        sw = _window_sums(ug, 1)[w][:, POOL_PAD:, :]
        cnt = jnp.minimum(tpos + 1, w).astype(_F32)
        d = (sw / cnt[None] - ug[:, POOL_PAD:, :]).reshape(rows, POOL_GROUP_WIDTH).astype(_BF)
        y = _dot(d, wpool_ref[g]) * pscale_ref[:, c0:c0 + POOL_GROUP_WIDTH]
        pooled.append(y.astype(_BF))

    cat = jnp.concatenate(attn + pooled, axis=1)
    x1 = x + _dot(cat, wout_ref[...])
    x1_ref[...] = x1
    qc_ref[...] = _dot(_rms(x1, gcross_ref[...]).astype(_BF), wcq_ref[...]) * CQ_SCALE


def _sample_mixer(x, gmix, win, sinkcol, ck, cv, st, wpool, pscale, wout, gcross, wcq, nbatch, nseq, past_len):
    R, D = x.shape
    rows = nbatch * nseq
    const = lambda *shape: pl.BlockSpec(shape, lambda i: (0,) * len(shape))
    return pl.pallas_call(
        functools.partial(_sample_mixer_kernel, nbatch=nbatch, nseq=nseq, past_len=past_len),
        grid=(R // rows,),
        in_specs=[
            pl.BlockSpec((rows, D), lambda i: (i, 0)),
            const(1, D),
            const(D, IN_WIDTH),
            const(N_HEADS * nseq, 1),
            pl.BlockSpec((nbatch, WINDOW, KV_WIDTH), lambda i: (i, 0, 0)),
            pl.BlockSpec((nbatch, WINDOW, KV_WIDTH), lambda i: (i, 0, 0)),
            pl.BlockSpec((nbatch, POOL_PAD, POOL_WIDTH), lambda i: (i, 0, 0)),
            const(len(POOL_WINDOWS), POOL_GROUP_WIDTH, POOL_GROUP_WIDTH),
            const(1, POOL_WIDTH),
            const(D, D),
            const(1, D),
            const(D, D),
        ],
        out_specs=[
            pl.BlockSpec((rows, D), lambda i: (i, 0)),
            pl.BlockSpec((rows, D), lambda i: (i, 0)),
            pl.BlockSpec((nbatch, WINDOW, KV_WIDTH), lambda i: (i, 0, 0)),
            pl.BlockSpec((nbatch, WINDOW, KV_WIDTH), lambda i: (i, 0, 0)),
            pl.BlockSpec((nbatch, POOL_PAD, POOL_WIDTH), lambda i: (i, 0, 0)),
        ],
        out_shape=[
            jax.ShapeDtypeStruct((R, D), _F32),
            jax.ShapeDtypeStruct((R, D), _F32),
            jax.ShapeDtypeStruct((R // nseq, WINDOW, KV_WIDTH), _F32),
            jax.ShapeDtypeStruct((R // nseq, WINDOW, KV_WIDTH), _F32),
            jax.ShapeDtypeStruct((R // nseq, POOL_PAD, POOL_WIDTH), _F32),
        ],
        scratch_shapes=[
            pltpu.VMEM((nbatch, 2 * WINDOW, KV_WIDTH), _BF),
            pltpu.VMEM((nbatch, 2 * WINDOW, KV_WIDTH), _BF),
        ],
        compiler_params=pltpu.CompilerParams(
            dimension_semantics=("arbitrary",), vmem_limit_bytes=VMEM_LIMIT_BYTES),
        name="sample_mixer",
    )(x, gmix, win, sinkcol, ck, cv, st, wpool, pscale, wout, gcross, wcq)


def _sample_memattn_kernel(qc_ref, mk_ref, mv_ref, o_ref, *, nbatch, nseq):
    rows = nbatch * nseq
    q = qc_ref[...].reshape(nbatch, nseq, D_MODEL)
    outs = []
    for h in range(N_CROSS_HEADS):
        c0 = h * CROSS_HEAD_DIM
        qh = q[:, :, c0:c0 + CROSS_HEAD_DIM].astype(_BF)
        kh = mk_ref[:, :, c0:c0 + CROSS_HEAD_DIM].astype(_BF)
        vh = mv_ref[:, :, c0:c0 + CROSS_HEAD_DIM].astype(_BF)
        sc = jnp.einsum('bqd,bkd->bqk', qh, kh, preferred_element_type=_F32)
        pr = _softmax(sc).astype(_BF)
        outs.append(jnp.einsum('bqk,bkd->bqd', pr, vh, preferred_element_type=_F32))
    o_ref[...] = jnp.concatenate(outs, axis=2).reshape(rows, D_MODEL)


def _sample_memattn(qc, mk, mv, nbatch, nseq):
    R, D = qc.shape
    rows = nbatch * nseq
    return pl.pallas_call(
        functools.partial(_sample_memattn_kernel, nbatch=nbatch, nseq=nseq),
        grid=(R // rows,),
        in_specs=[
            pl.BlockSpec((rows, D), lambda i: (i, 0)),
            pl.BlockSpec((nbatch, N_MEM, D), lambda i: (i, 0, 0)),
            pl.BlockSpec((nbatch, N_MEM, D), lambda i: (i, 0, 0)),
        ],
        out_specs=pl.BlockSpec((rows, D), lambda i: (i, 0)),
        out_shape=jax.ShapeDtypeStruct((R, D), _F32),
        compiler_params=pltpu.CompilerParams(
            dimension_semantics=("arbitrary",), vmem_limit_bytes=VMEM_LIMIT_BYTES),
        name="sample_memattn",
    )(qc, mk, mv)


def _sample_tail_kernel(x1_ref, o_ref, wco_ref, gffn_ref, wup_ref, wdown_ref, gfinal_ref, y_ref):
    x2 = x1_ref[...] + _dot(o_ref[...].astype(_BF), wco_ref[...])
    y_ref[...] = _ffn_final(x2, gffn_ref[...], wup_ref, wdown_ref, gfinal_ref[...])


def _sample_tail(x1, o, wco, gffn, wup, wdown, gfinal, tile):
    R, D = x1.shape
    const = lambda *shape: _single(shape, lambda i: (0,) * len(shape))
    return pl.pallas_call(
        _sample_tail_kernel,
        grid=(R // tile,),
        in_specs=[
            pl.BlockSpec((tile, D), lambda i: (i, 0)),
            pl.BlockSpec((tile, D), lambda i: (i, 0)),
            const(D, D),
            const(1, D),
            const(D, D_FF),
            const(D_FF, D),
            const(1, D),
        ],
        out_specs=pl.BlockSpec((tile, D), lambda i: (i, 0)),
        out_shape=jax.ShapeDtypeStruct((R, D), _F32),
        compiler_params=pltpu.CompilerParams(
            dimension_semantics=("arbitrary",), vmem_limit_bytes=VMEM_LIMIT_BYTES),
        name="sample_tail",
    )(x1, o, wco, gffn, wup, wdown, gfinal)


PROMPT_TILE = 512
SAMPLE_MIXER_BATCH = 16
SAMPLE_MEMATTN_BATCH = 4
SAMPLE_TAIL_TILE = 512


def kernel(x_prompt, x_sample, cache_win_k, cache_win_v, state_pool, cache_mem_k, cache_mem_v, mem_prompt,
           g_mix, w_in, attn_sinks, w_pool, pool_scale, w_out, g_cross, g_mem, w_cq, w_ck, w_cv, w_co,
           g_ffn, w_up, w_down, g_final):
    depth = g_mix.shape[0]
    assert depth == 1, "one layer per step"
    B, S, D = x_prompt.shape
    DB, T, _ = x_sample.shape
    past_len = PAST_LEN
    l = 0

    qperm = jnp.asarray(_Q_PERM)
    win = jnp.concatenate([w_in[l][:, qperm], w_in[l][:, ATTN_WIDTH:]], axis=1).astype(_BF)
    wout = jnp.concatenate([w_out[l][qperm, :], w_out[l][ATTN_WIDTH:, :]], axis=0).astype(_BF)
    wpool = w_pool[l].astype(_BF)
    wcq, wck, wcv, wco = (w[l].astype(_BF) for w in (w_cq, w_ck, w_cv, w_co))
    wup, wdown = w_up[l].astype(_BF), w_down[l].astype(_BF)
    gmix, gcross, gmem, gffn = (g[l].reshape(1, D) for g in (g_mix, g_cross, g_mem, g_ffn))
    gfinal = g_final.reshape(1, D)
    pscale = pool_scale[l].reshape(1, POOL_WIDTH)
    sinks = attn_sinks[l]

    x1p, klast, vlast, plast = _prompt_mixer(x_prompt, gmix, win, sinks, wpool, pscale, wout, PROMPT_TILE)
    mem_k, mem_v, mem_kt, mem_vb = _mem_kv(mem_prompt, gmem, wck, wcv)
    y_prompt = _prompt_tail(x1p, gcross, wcq, mem_kt, mem_vb, wco, gffn, wup, wdown, gfinal, PROMPT_TILE)

    xs = x_sample.reshape(DB * T, D)
    ck = cache_win_k[l].reshape(DB, WINDOW, KV_WIDTH)
    cv = cache_win_v[l].reshape(DB, WINDOW, KV_WIDTH)
    st = jnp.pad(state_pool[l], ((0, 0), (POOL_PAD - POOL_HIST, 0), (0, 0)))
    sinkcol = jnp.repeat(sinks, T).reshape(N_HEADS * T, 1)
    x1s, qc, wk_s, wv_s, pool_s = _sample_mixer(xs, gmix, win, sinkcol, ck, cv, st, wpool, pscale, wout,
                                                gcross, wcq, SAMPLE_MIXER_BATCH, T, past_len)
    mk = cache_mem_k[l].reshape(DB, N_MEM, D)
    mv = cache_mem_v[l].reshape(DB, N_MEM, D)
    o_s = _sample_memattn(qc, mk, mv, SAMPLE_MEMATTN_BATCH, T)
    y_sample = _sample_tail(x1s, o_s, wco, gffn, wup, wdown, gfinal, SAMPLE_TAIL_TILE).reshape(DB, T, D)

    return (
        y_prompt,
        y_sample,
        klast.reshape(1, B, WINDOW, N_KV_HEADS, HEAD_DIM),
        vlast.reshape(1, B, WINDOW, N_KV_HEADS, HEAD_DIM),
        plast[:, POOL_PAD - POOL_HIST:, :][None],
        mem_k.reshape(1, B, N_MEM, N_CROSS_HEADS, CROSS_HEAD_DIM),
        mem_v.reshape(1, B, N_MEM, N_CROSS_HEADS, CROSS_HEAD_DIM),
        wk_s.reshape(1, DB, WINDOW, N_KV_HEADS, HEAD_DIM),
        wv_s.reshape(1, DB, WINDOW, N_KV_HEADS, HEAD_DIM),
        pool_s[:, POOL_PAD - POOL_HIST:, :][None],
    )
```

```python
import functools

import jax
import jax.numpy as jnp
import numpy as np
from jax import lax
from jax.experimental import pallas as pl
from jax.experimental.pallas import tpu as pltpu

D_MODEL = 1024
PAST_LEN = 16384
HEAD_DIM = 64
N_HEADS = 8
N_KV_HEADS = 2
GROUP = N_HEADS // N_KV_HEADS
ATTN_WIDTH = N_HEADS * HEAD_DIM
KV_WIDTH = N_KV_HEADS * HEAD_DIM
WINDOW = 128
BLOCK = WINDOW
POOL_WIDTH = D_MODEL - ATTN_WIDTH
POOL_WINDOWS = (2, 4, 8, 16)
POOL_GROUP_WIDTH = 128
POOL_HIST = 15
POOL_PAD = 16
IN_WIDTH = ATTN_WIDTH + 2 * KV_WIDTH + POOL_WIDTH
N_MEM = 256
N_CROSS_HEADS = 4
CROSS_HEAD_DIM = 256
MEM_ROWS = N_CROSS_HEADS * (CROSS_HEAD_DIM // 128)
D_FF = 4 * D_MODEL
FF_CHUNK = 1024
RMS_EPS = 1e-5
NEG_INF = -1e30
Q_SCALE = HEAD_DIM ** -0.5
CQ_SCALE = CROSS_HEAD_DIM ** -0.5
K_OFF = ATTN_WIDTH
V_OFF = ATTN_WIDTH + KV_WIDTH
U_OFF = ATTN_WIDTH + 2 * KV_WIDTH

_HEAD_ORDER = (0, 4, 1, 5, 2, 6, 3, 7)
_Q_PERM = np.concatenate([np.arange(h * HEAD_DIM, (h + 1) * HEAD_DIM) for h in _HEAD_ORDER])

VMEM_LIMIT_BYTES = 56 * 1024 * 1024

_BF = jnp.bfloat16
_F32 = jnp.float32


def _slope(h):
    return 2.0 ** (-8.0 * (h + 1) / N_HEADS)


def _dot(a, b):
    return jnp.dot(a, b, preferred_element_type=_F32)


def _rms(x, g):
    ms = jnp.mean(x * x, axis=-1, keepdims=True)
    return x * lax.rsqrt(ms + RMS_EPS) * g


def _softmax_with_sink(s, sink):
    m = jnp.maximum(jnp.max(s, axis=-1, keepdims=True), sink)
    e = jnp.exp(s - m)
    den = jnp.sum(e, axis=-1, keepdims=True) + jnp.exp(sink - m)
    return e * (1.0 / den)


def _softmax(s):
    m = jnp.max(s, axis=-1, keepdims=True)
    e = jnp.exp(s - m)
    return e * (1.0 / jnp.sum(e, axis=-1, keepdims=True))


def _window_sums(u_ext, axis):
    out = {}
    s = u_ext
    w = 1
    while w < max(POOL_WINDOWS):
        s = s + pltpu.roll(s, w, axis)
        w *= 2
        out[w] = s
    return out


def _ffn_final(x2, gffn, wup_ref, wdown_ref, gfinal):
    hn = _rms(x2, gffn).astype(_BF)
    acc = x2
    for c in range(D_FF // FF_CHUNK):
        hc = _dot(hn, wup_ref[:, c * FF_CHUNK:(c + 1) * FF_CHUNK])
        hc = jnp.maximum(hc, 0.0)
        hc = (hc * hc).astype(_BF)
        acc = acc + _dot(hc, wdown_ref[c * FF_CHUNK:(c + 1) * FF_CHUNK, :])
    return _rms(acc, gfinal)


def _prompt_mixer_kernel(x_ref, gmix_ref, win_ref, sink_ref, wpool_ref, pscale_ref, wout_ref,
                         x1_ref, klast_ref, vlast_ref, plast_ref,
                         q_scr, kt_scr, v_scr, u_scr, bias_scr, cat_scr, *, tile):
    b = pl.program_id(0)
    s = pl.program_id(1)
    nb = tile // BLOCK

    @pl.when((b == 0) & (s == 0))
    def _():
        qi = lax.broadcasted_iota(jnp.int32, (BLOCK, 2 * BLOCK), 0)
        kc = lax.broadcasted_iota(jnp.int32, (BLOCK, 2 * BLOCK), 1)
        dist = qi + BLOCK - kc
        valid = (dist >= 0) & (dist <= WINDOW)
        valid_first = valid & (kc >= BLOCK)
        distf = dist.astype(_F32)
        for h in range(N_HEADS):
            ali = -_slope(h) * distf
            bias_scr[0, h] = jnp.where(valid, ali, NEG_INF)
            bias_scr[1, h] = jnp.where(valid_first, ali, NEG_INF)

    @pl.when(s == 0)
    def _():
        kt_scr[0] = jnp.zeros((KV_WIDTH, BLOCK), _BF)
        v_scr[0] = jnp.zeros((BLOCK, KV_WIDTH), _BF)
        u_scr[0:POOL_PAD, :] = jnp.zeros((POOL_PAD, POOL_WIDTH), _F32)

    x = x_ref[0]
    xn = _rms(x, gmix_ref[...]).astype(_BF)
    proj = _dot(xn, win_ref[...])
    q_scr[...] = proj[:, :ATTN_WIDTH] * Q_SCALE
    k = proj[:, K_OFF:K_OFF + KV_WIDTH]
    v = proj[:, V_OFF:V_OFF + KV_WIDTH]
    for j in range(nb):
        kt_scr[j + 1] = k[j * BLOCK:(j + 1) * BLOCK].T.astype(_BF)
        v_scr[j + 1] = v[j * BLOCK:(j + 1) * BLOCK].astype(_BF)
    u_scr[POOL_PAD:, :] = proj[:, U_OFF:]
    klast_ref[0] = k[tile - BLOCK:]
    vlast_ref[0] = v[tile - BLOCK:]
    plast_ref[0] = proj[tile - POOL_PAD:, U_OFF:]

    lane = lax.broadcasted_iota(jnp.int32, (BLOCK, 2 * HEAD_DIM), 1)
    lo = lane < HEAD_DIM
    row = lax.broadcasted_iota(jnp.int32, (BLOCK, POOL_GROUP_WIDTH), 0)

    def block_body(j, carry):
        r0 = pl.multiple_of(j * BLOCK, BLOCK)
        first = ((s == 0) & (j == 0)).astype(jnp.int32)
        kt2 = jnp.concatenate([kt_scr[j], kt_scr[j + 1]], axis=1)
        v2 = jnp.concatenate([v_scr[j], v_scr[j + 1]], axis=0)
        for p in range(GROUP):
            slab = q_scr[pl.ds(r0, BLOCK), p * 128:(p + 1) * 128]
            outs = []
            for half, h in ((0, p), (1, p + GROUP)):
                qm = jnp.where(lo if half == 0 else ~lo, slab, 0.0).astype(_BF)
                sc = _dot(qm, kt2) + bias_scr[first, h]
                pr = _softmax_with_sink(sc, sink_ref[h])
                outs.append(_dot(pr.astype(_BF), v2))
            cat_scr[pl.ds(r0, BLOCK), p * 128:(p + 1) * 128] = jnp.where(lo, outs[0], outs[1]).astype(_BF)
        pos = s * tile + r0 + row
        for g, w in enumerate(POOL_WINDOWS):
            c0 = g * POOL_GROUP_WIDTH
            u_ext = u_scr[pl.ds(r0, BLOCK + POOL_PAD), c0:c0 + POOL_GROUP_WIDTH]
            sw = _window_sums(u_ext, 0)[w][POOL_PAD:]
            cnt = jnp.minimum(pos + 1, w).astype(_F32)
            d = (sw / cnt - u_ext[POOL_PAD:]).astype(_BF)
            y = _dot(d, wpool_ref[g]) * pscale_ref[:, c0:c0 + POOL_GROUP_WIDTH]
            cat_scr[pl.ds(r0, BLOCK), ATTN_WIDTH + c0:ATTN_WIDTH + c0 + POOL_GROUP_WIDTH] = y.astype(_BF)
        return carry

    lax.fori_loop(0, nb, block_body, 0)

    kt_scr[0] = kt_scr[nb]
    v_scr[0] = v_scr[nb]
    u_scr[0:POOL_PAD, :] = u_scr[tile:tile + POOL_PAD, :]
    x1_ref[0] = x + _dot(cat_scr[...], wout_ref[...])


def _prompt_mixer(x, gmix, win, sinks, wpool, pscale, wout, tile):
    B, S, D = x.shape
    ns = S // tile
    nb = tile // BLOCK
    const = lambda *shape: pl.BlockSpec(shape, lambda b, s: (0,) * len(shape))
    return pl.pallas_call(
        functools.partial(_prompt_mixer_kernel, tile=tile),
        grid=(B, ns),
        in_specs=[
            pl.BlockSpec((1, tile, D), lambda b, s: (b, s, 0)),
            const(1, D),
            const(D, IN_WIDTH),
            pl.BlockSpec(memory_space=pltpu.SMEM),
            const(len(POOL_WINDOWS), POOL_GROUP_WIDTH, POOL_GROUP_WIDTH),
            const(1, POOL_WIDTH),
            const(D, D),
        ],
        out_specs=[
            pl.BlockSpec((1, tile, D), lambda b, s: (b, s, 0)),
            pl.BlockSpec((1, BLOCK, KV_WIDTH), lambda b, s: (b, 0, 0)),
            pl.BlockSpec((1, BLOCK, KV_WIDTH), lambda b, s: (b, 0, 0)),
            pl.BlockSpec((1, POOL_PAD, POOL_WIDTH), lambda b, s: (b, 0, 0)),
        ],
        out_shape=[
            jax.ShapeDtypeStruct((B, S, D), _F32),
            jax.ShapeDtypeStruct((B, BLOCK, KV_WIDTH), _F32),
            jax.ShapeDtypeStruct((B, BLOCK, KV_WIDTH), _F32),
            jax.ShapeDtypeStruct((B, POOL_PAD, POOL_WIDTH), _F32),
        ],
        scratch_shapes=[
            pltpu.VMEM((tile, ATTN_WIDTH), _F32),
            pltpu.VMEM((nb + 1, KV_WIDTH, BLOCK), _BF),
            pltpu.VMEM((nb + 1, BLOCK, KV_WIDTH), _BF),
            pltpu.VMEM((tile + POOL_PAD, POOL_WIDTH), _F32),
            pltpu.VMEM((2, N_HEADS, BLOCK, 2 * BLOCK), _F32),
            pltpu.VMEM((tile, D), _BF),
        ],
        compiler_params=pltpu.CompilerParams(
            dimension_semantics=("arbitrary", "arbitrary"), vmem_limit_bytes=VMEM_LIMIT_BYTES),
        name="prompt_mixer",
    )(x, gmix, win, sinks, wpool, pscale, wout)


def _mem_kv_kernel(mem_ref, gmem_ref, wck_ref, wcv_ref, k_ref, v_ref, kt_ref, vb_ref):
    hm = _rms(mem_ref[0], gmem_ref[...]).astype(_BF)
    k = _dot(hm, wck_ref[...])
    v = _dot(hm, wcv_ref[...])
    k_ref[0] = k
    v_ref[0] = v
    vb_ref[0] = v.astype(_BF)
    for h in range(N_CROSS_HEADS):
        kt_ref[0, h] = k[:, h * CROSS_HEAD_DIM:(h + 1) * CROSS_HEAD_DIM].T.astype(_BF)


def _mem_kv(mem, gmem, wck, wcv):
    B, M, D = mem.shape
    const = lambda *shape: pl.BlockSpec(shape, lambda b: (0,) * len(shape))
    return pl.pallas_call(
        _mem_kv_kernel,
        grid=(B,),
        in_specs=[pl.BlockSpec((1, M, D), lambda b: (b, 0, 0)), const(1, D), const(D, D), const(D, D)],
        out_specs=[
            pl.BlockSpec((1, M, D), lambda b: (b, 0, 0)),
            pl.BlockSpec((1, M, D), lambda b: (b, 0, 0)),
            pl.BlockSpec((1, N_CROSS_HEADS, CROSS_HEAD_DIM, M), lambda b: (b, 0, 0, 0)),
            pl.BlockSpec((1, M, D), lambda b: (b, 0, 0)),
        ],
        out_shape=[
            jax.ShapeDtypeStruct((B, M, D), _F32),
            jax.ShapeDtypeStruct((B, M, D), _F32),
            jax.ShapeDtypeStruct((B, N_CROSS_HEADS, CROSS_HEAD_DIM, M), _BF),
            jax.ShapeDtypeStruct((B, M, D), _BF),
        ],
        compiler_params=pltpu.CompilerParams(
            dimension_semantics=("arbitrary",), vmem_limit_bytes=VMEM_LIMIT_BYTES),
        name="prompt_mem_kv",
    )(mem, gmem, wck, wcv)


def _prompt_tail_kernel(x1_ref, gcross_ref, wcq_ref, kt_ref, vb_ref, wco_ref, gffn_ref, wup_ref, wdown_ref,
                        gfinal_ref, y_ref):
    x1 = x1_ref[0]
    q = (_dot(_rms(x1, gcross_ref[...]).astype(_BF), wcq_ref[...]) * CQ_SCALE).astype(_BF)
    outs = []
    for h in range(N_CROSS_HEADS):
        c0 = h * CROSS_HEAD_DIM
        sc = _dot(q[:, c0:c0 + CROSS_HEAD_DIM], kt_ref[0, h])
        pr = _softmax(sc).astype(_BF)
        outs.append(_dot(pr, vb_ref[0, :, c0:c0 + CROSS_HEAD_DIM]).astype(_BF))
    o = jnp.concatenate(outs, axis=1)
    x2 = x1 + _dot(o, wco_ref[...])
    y_ref[0] = _ffn_final(x2, gffn_ref[...], wup_ref, wdown_ref, gfinal_ref[...])


def _single(shape, index_map):
    return pl.BlockSpec(shape, index_map, pipeline_mode=pl.Buffered(1))


def _prompt_tail(x1, gcross, wcq, kt, vb, wco, gffn, wup, wdown, gfinal, tile):
    B, S, D = x1.shape
    const = lambda *shape: _single(shape, lambda b, s: (0,) * len(shape))
    return pl.pallas_call(
        _prompt_tail_kernel,
        grid=(B, S // tile),
        in_specs=[
            pl.BlockSpec((1, tile, D), lambda b, s: (b, s, 0)),
            const(1, D),
            const(D, D),
            pl.BlockSpec((1, N_CROSS_HEADS, CROSS_HEAD_DIM, N_MEM), lambda b, s: (b, 0, 0, 0)),
            pl.BlockSpec((1, N_MEM, D), lambda b, s: (b, 0, 0)),
            const(D, D),
            const(1, D),
            const(D, D_FF),
            const(D_FF, D),
            const(1, D),
        ],
        out_specs=pl.BlockSpec((1, tile, D), lambda b, s: (b, s, 0)),
        out_shape=jax.ShapeDtypeStruct((B, S, D), _F32),
        compiler_params=pltpu.CompilerParams(
            dimension_semantics=("arbitrary", "arbitrary"), vmem_limit_bytes=VMEM_LIMIT_BYTES),
        name="prompt_tail",
    )(x1, gcross, wcq, kt, vb, wco, gffn, wup, wdown, gfinal)


def _sample_mixer_kernel(x_ref, gmix_ref, win_ref, sinkcol_ref, ck_ref, cv_ref, st_ref, wpool_ref, pscale_ref,
                         wout_ref, gcross_ref, wcq_ref,
                         x1_ref, qc_ref, wk_ref, wv_ref, pool_ref,
                         kc_scr, vc_scr, *, nbatch, nseq, past_len):
    rows = nbatch * nseq
    nq = N_HEADS * nseq
    nkeys = 2 * WINDOW
    x = x_ref[...]
    xn = _rms(x, gmix_ref[...]).astype(_BF)
    proj = _dot(xn, win_ref[...])

    k_new = proj[:, K_OFF:K_OFF + KV_WIDTH].reshape(nbatch, nseq, KV_WIDTH)
    v_new = proj[:, V_OFF:V_OFF + KV_WIDTH].reshape(nbatch, nseq, KV_WIDTH)
    ck = ck_ref[...]
    cv = cv_ref[...]
    wk_ref[:, :WINDOW - nseq, :] = ck[:, nseq:, :]
    wk_ref[:, WINDOW - nseq:, :] = k_new
    wv_ref[:, :WINDOW - nseq, :] = cv[:, nseq:, :]
    wv_ref[:, WINDOW - nseq:, :] = v_new

    zpad = jnp.zeros((nbatch, nkeys - WINDOW, KV_WIDTH), _BF)
    kc_scr[:, WINDOW:, :] = zpad
    vc_scr[:, WINDOW:, :] = zpad
    kc_scr[:, :WINDOW, :] = ck.astype(_BF)
    vc_scr[:, :WINDOW, :] = cv.astype(_BF)
    kc_scr[:, WINDOW:WINDOW + nseq, :] = k_new.astype(_BF)
    vc_scr[:, WINDOW:WINDOW + nseq, :] = v_new.astype(_BF)

    t = lax.broadcasted_iota(jnp.int32, (nseq, nkeys), 0)
    c = lax.broadcasted_iota(jnp.int32, (nseq, nkeys), 1)
    dist = jnp.where(c < WINDOW, WINDOW + t - c, t - (c - WINDOW))
    valid = (dist >= 0) & (dist <= WINDOW) & (c < WINDOW + nseq)
    distf = dist.astype(_F32)
    bias = jnp.concatenate([jnp.where(valid, -_slope(h) * distf, NEG_INF) for h in range(N_HEADS)], axis=0)

    lane = lax.broadcasted_iota(jnp.int32, (rows, 2 * HEAD_DIM), 1)
    lo = lane < HEAD_DIM
    qsc = proj[:, :ATTN_WIDTH] * Q_SCALE
    q_lo, q_hi = [], []
    for p in range(GROUP):
        slab = qsc[:, p * 128:(p + 1) * 128]
        q_lo.append(jnp.where(lo, slab, 0.0).reshape(nbatch, nseq, 128))
        q_hi.append(jnp.where(lo, 0.0, slab).reshape(nbatch, nseq, 128))
    qm = jnp.concatenate(q_lo + q_hi, axis=1).astype(_BF)
    sc = jnp.einsum('bqd,bkd->bqk', qm, kc_scr[...], preferred_element_type=_F32) + bias[None]
    pr = _softmax_with_sink(sc, sinkcol_ref[...][None]).astype(_BF)
    o = jnp.einsum('bqk,bkd->bqd', pr, vc_scr[...], preferred_element_type=_F32)
    attn = []
    for p in range(GROUP):
        o_lo = o[:, p * nseq:(p + 1) * nseq, :].reshape(rows, 128)
        o_hi = o[:, (p + GROUP) * nseq:(p + GROUP + 1) * nseq, :].reshape(rows, 128)
        attn.append(jnp.where(lo, o_lo, o_hi).astype(_BF))

    u_new = proj[:, U_OFF:].reshape(nbatch, nseq, POOL_WIDTH)
    u_ext = jnp.concatenate([st_ref[...], u_new], axis=1)
    pool_ref[...] = u_ext[:, nseq:, :]
    tpos = past_len + lax.broadcasted_iota(jnp.int32, (nseq, POOL_GROUP_WIDTH), 0)
    pooled = []
    for g, w in enumerate(POOL_WINDOWS):
        c0 = g * POOL_GROUP_WIDTH
        ug = u_ext[:, :, c0:c0 + POOL_GROUP_WIDTH]
        sw = _window_sums(ug, 1)[w][:, POOL_PAD:, :]
        cnt = jnp.minimum(tpos + 1, w).astype(_F32)
        d = (sw / cnt[None] - ug[:, POOL_PAD:, :]).reshape(rows, POOL_GROUP_WIDTH).astype(_BF)
        y = _dot(d, wpool_ref[g]) * pscale_ref[:, c0:c0 + POOL_GROUP_WIDTH]
        pooled.append(y.astype(_BF))

    cat = jnp.concatenate(attn + pooled, axis=1)
    x1 = x + _dot(cat, wout_ref[...])
    x1_ref[...] = x1
    qc_ref[...] = _dot(_rms(x1, gcross_ref[...]).astype(_BF), wcq_ref[...]) * CQ_SCALE


def _sample_mixer(x, gmix, win, sinkcol, ck, cv, st, wpool, pscale, wout, gcross, wcq, nbatch, nseq, past_len):
    R, D = x.shape
    rows = nbatch * nseq
    const = lambda *shape: pl.BlockSpec(shape, lambda i: (0,) * len(shape))
    return pl.pallas_call(
        functools.partial(_sample_mixer_kernel, nbatch=nbatch, nseq=nseq, past_len=past_len),
        grid=(R // rows,),
        in_specs=[
            pl.BlockSpec((rows, D), lambda i: (i, 0)),
            const(1, D),
            const(D, IN_WIDTH),
            const(N_HEADS * nseq, 1),
            pl.BlockSpec((nbatch, WINDOW, KV_WIDTH), lambda i: (i, 0, 0)),
            pl.BlockSpec((nbatch, WINDOW, KV_WIDTH), lambda i: (i, 0, 0)),
            pl.BlockSpec((nbatch, POOL_PAD, POOL_WIDTH), lambda i: (i, 0, 0)),
            const(len(POOL_WINDOWS), POOL_GROUP_WIDTH, POOL_GROUP_WIDTH),
            const(1, POOL_WIDTH),
            const(D, D),
            const(1, D),
            const(D, D),
        ],
        out_specs=[
            pl.BlockSpec((rows, D), lambda i: (i, 0)),
            pl.BlockSpec((rows, D), lambda i: (i, 0)),
            pl.BlockSpec((nbatch, WINDOW, KV_WIDTH), lambda i: (i, 0, 0)),
            pl.BlockSpec((nbatch, WINDOW, KV_WIDTH), lambda i: (i, 0, 0)),
            pl.BlockSpec((nbatch, POOL_PAD, POOL_WIDTH), lambda i: (i, 0, 0)),
        ],
        out_shape=[
            jax.ShapeDtypeStruct((R, D), _F32),
            jax.ShapeDtypeStruct((R, D), _F32),
            jax.ShapeDtypeStruct((R // nseq, WINDOW, KV_WIDTH), _F32),
            jax.ShapeDtypeStruct((R // nseq, WINDOW, KV_WIDTH), _F32),
            jax.ShapeDtypeStruct((R // nseq, POOL_PAD, POOL_WIDTH), _F32),
        ],
        scratch_shapes=[
            pltpu.VMEM((nbatch, 2 * WINDOW, KV_WIDTH), _BF),
            pltpu.VMEM((nbatch, 2 * WINDOW, KV_WIDTH), _BF),
        ],
        compiler_params=pltpu.CompilerParams(
            dimension_semantics=("arbitrary",), vmem_limit_bytes=VMEM_LIMIT_BYTES),
        name="sample_mixer",
    )(x, gmix, win, sinkcol, ck, cv, st, wpool, pscale, wout, gcross, wcq)


def _mem_cache_rows(cache):
    nb = cache.shape[0]
    c = cache.reshape(nb, N_MEM, N_CROSS_HEADS, CROSS_HEAD_DIM // 128, 128)
    return c.transpose(0, 1, 3, 2, 4).reshape(nb, N_MEM * MEM_ROWS, 128)


def _load_mem_head(ref, b, h):
    halves = [ref[b, pl.ds(half * N_CROSS_HEADS + h, N_MEM, stride=MEM_ROWS), :]
              for half in range(CROSS_HEAD_DIM // 128)]
    return jnp.concatenate(halves, axis=1)


def _sample_memattn_kernel(qc_ref, mk_ref, mv_ref, o_ref, *, nbatch, nseq):
    rows = nbatch * nseq
    q = qc_ref[...].reshape(nbatch, nseq, D_MODEL)
    outs = []
    for h in range(N_CROSS_HEADS):
        c0 = h * CROSS_HEAD_DIM
        qh = q[:, :, c0:c0 + CROSS_HEAD_DIM].astype(_BF)
        kh = jnp.stack([_load_mem_head(mk_ref, b, h).astype(_BF) for b in range(nbatch)])
        vh = jnp.stack([_load_mem_head(mv_ref, b, h).astype(_BF) for b in range(nbatch)])
        sc = jnp.einsum('bqd,bkd->bqk', qh, kh, preferred_element_type=_F32)
        pr = _softmax(sc).astype(_BF)
        outs.append(jnp.einsum('bqk,bkd->bqd', pr, vh, preferred_element_type=_F32))
    o_ref[...] = jnp.concatenate(outs, axis=2).reshape(rows, D_MODEL)


def _sample_memattn(qc, mk, mv, nbatch, nseq):
    R, D = qc.shape
    rows = nbatch * nseq
    return pl.pallas_call(
        functools.partial(_sample_memattn_kernel, nbatch=nbatch, nseq=nseq),
        grid=(R // rows,),
        in_specs=[
            pl.BlockSpec((rows, D), lambda i: (i, 0)),
            pl.BlockSpec((nbatch, N_MEM * MEM_ROWS, 128), lambda i: (i, 0, 0)),
            pl.BlockSpec((nbatch, N_MEM * MEM_ROWS, 128), lambda i: (i, 0, 0)),
        ],
        out_specs=pl.BlockSpec((rows, D), lambda i: (i, 0)),
        out_shape=jax.ShapeDtypeStruct((R, D), _F32),
        compiler_params=pltpu.CompilerParams(
            dimension_semantics=("arbitrary",), vmem_limit_bytes=VMEM_LIMIT_BYTES),
        name="sample_memattn",
    )(qc, mk, mv)


def _sample_tail_kernel(x1_ref, o_ref, wco_ref, gffn_ref, wup_ref, wdown_ref, gfinal_ref, y_ref):
    x2 = x1_ref[...] + _dot(o_ref[...].astype(_BF), wco_ref[...])
    y_ref[...] = _ffn_final(x2, gffn_ref[...], wup_ref, wdown_ref, gfinal_ref[...])


def _sample_tail(x1, o, wco, gffn, wup, wdown, gfinal, tile):
    R, D = x1.shape
    const = lambda *shape: _single(shape, lambda i: (0,) * len(shape))
    return pl.pallas_call(
        _sample_tail_kernel,
        grid=(R // tile,),
        in_specs=[
            pl.BlockSpec((tile, D), lambda i: (i, 0)),
            pl.BlockSpec((tile, D), lambda i: (i, 0)),
            const(D, D),
            const(1, D),
            const(D, D_FF),
            const(D_FF, D),
            const(1, D),
        ],
        out_specs=pl.BlockSpec((tile, D), lambda i: (i, 0)),
        out_shape=jax.ShapeDtypeStruct((R, D), _F32),
        compiler_params=pltpu.CompilerParams(
            dimension_semantics=("arbitrary",), vmem_limit_bytes=VMEM_LIMIT_BYTES),
        name="sample_tail",
    )(x1, o, wco, gffn, wup, wdown, gfinal)


PROMPT_TILE = 512
SAMPLE_MIXER_BATCH = 16
SAMPLE_MEMATTN_BATCH = 4
SAMPLE_TAIL_TILE = 512


def kernel(x_prompt, x_sample, cache_win_k, cache_win_v, state_pool, cache_mem_k, cache_mem_v, mem_prompt,
           g_mix, w_in, attn_sinks, w_pool, pool_scale, w_out, g_cross, g_mem, w_cq, w_ck, w_cv, w_co,
           g_ffn, w_up, w_down, g_final):
    depth = g_mix.shape[0]
    assert depth == 1, "one layer per step"
    B, S, D = x_prompt.shape
    DB, T, _ = x_sample.shape
    past_len = PAST_LEN
    l = 0

    qperm = jnp.asarray(_Q_PERM)
    win = jnp.concatenate([w_in[l][:, qperm], w_in[l][:, ATTN_WIDTH:]], axis=1).astype(_BF)
    wout = jnp.concatenate([w_out[l][qperm, :], w_out[l][ATTN_WIDTH:, :]], axis=0).astype(_BF)
    wpool = w_pool[l].astype(_BF)
    wcq, wck, wcv, wco = (w[l].astype(_BF) for w in (w_cq, w_ck, w_cv, w_co))
    wup, wdown = w_up[l].astype(_BF), w_down[l].astype(_BF)
    gmix, gcross, gmem, gffn = (g[l].reshape(1, D) for g in (g_mix, g_cross, g_mem, g_ffn))
    gfinal = g_final.reshape(1, D)
    pscale = pool_scale[l].reshape(1, POOL_WIDTH)
    sinks = attn_sinks[l]

    x1p, klast, vlast, plast = _prompt_mixer(x_prompt, gmix, win, sinks, wpool, pscale, wout, PROMPT_TILE)
    mem_k, mem_v, mem_kt, mem_vb = _mem_kv(mem_prompt, gmem, wck, wcv)
    y_prompt = _prompt_tail(x1p, gcross, wcq, mem_kt, mem_vb, wco, gffn, wup, wdown, gfinal, PROMPT_TILE)

    xs = x_sample.reshape(DB * T, D)
    ck = cache_win_k[l].reshape(DB, WINDOW, KV_WIDTH)
    cv = cache_win_v[l].reshape(DB, WINDOW, KV_WIDTH)
    st = jnp.pad(state_pool[l], ((0, 0), (POOL_PAD - POOL_HIST, 0), (0, 0)))
    sinkcol = jnp.repeat(sinks, T).reshape(N_HEADS * T, 1)
    x1s, qc, wk_s, wv_s, pool_s = _sample_mixer(xs, gmix, win, sinkcol, ck, cv, st, wpool, pscale, wout,
                                                gcross, wcq, SAMPLE_MIXER_BATCH, T, past_len)
    mk = _mem_cache_rows(cache_mem_k[l])
    mv = _mem_cache_rows(cache_mem_v[l])
    o_s = _sample_memattn(qc, mk, mv, SAMPLE_MEMATTN_BATCH, T)
    y_sample = _sample_tail(x1s, o_s, wco, gffn, wup, wdown, gfinal, SAMPLE_TAIL_TILE).reshape(DB, T, D)

    return (
        y_prompt,
        y_sample,
        klast.reshape(1, B, WINDOW, N_KV_HEADS, HEAD_DIM),
        vlast.reshape(1, B, WINDOW, N_KV_HEADS, HEAD_DIM),
        plast[:, POOL_PAD - POOL_HIST:, :][None],
        mem_k.reshape(1, B, N_MEM, N_CROSS_HEADS, CROSS_HEAD_DIM),
        mem_v.reshape(1, B, N_MEM, N_CROSS_HEADS, CROSS_HEAD_DIM),
        wk_s.reshape(1, DB, WINDOW, N_KV_HEADS, HEAD_DIM),
        wv_s.reshape(1, DB, WINDOW, N_KV_HEADS, HEAD_DIM),
        pool_s[:, POOL_PAD - POOL_HIST:, :][None],
    )
```

```python
import functools

import jax
import jax.numpy as jnp
import numpy as np
from jax import lax
from jax.experimental import pallas as pl
from jax.experimental.pallas import tpu as pltpu

D_MODEL = 1024
PAST_LEN = 16384
HEAD_DIM = 64
N_HEADS = 8
N_KV_HEADS = 2
GROUP = N_HEADS // N_KV_HEADS
ATTN_WIDTH = N_HEADS * HEAD_DIM
KV_WIDTH = N_KV_HEADS * HEAD_DIM
WINDOW = 128
BLOCK = WINDOW
POOL_WIDTH = D_MODEL - ATTN_WIDTH
POOL_WINDOWS = (2, 4, 8, 16)
POOL_GROUP_WIDTH = 128
POOL_HIST = 15
POOL_PAD = 16
IN_WIDTH = ATTN_WIDTH + 2 * KV_WIDTH + POOL_WIDTH
N_MEM = 256
N_CROSS_HEADS = 4
CROSS_HEAD_DIM = 256
MEM_ROWS = N_CROSS_HEADS * (CROSS_HEAD_DIM // 128)
D_FF = 4 * D_MODEL
FF_CHUNK = 1024
RMS_EPS = 1e-5
NEG_INF = -1e30
Q_SCALE = HEAD_DIM ** -0.5
CQ_SCALE = CROSS_HEAD_DIM ** -0.5
K_OFF = ATTN_WIDTH
V_OFF = ATTN_WIDTH + KV_WIDTH
U_OFF = ATTN_WIDTH + 2 * KV_WIDTH

_HEAD_ORDER = (0, 4, 1, 5, 2, 6, 3, 7)
_Q_PERM = np.concatenate([np.arange(h * HEAD_DIM, (h + 1) * HEAD_DIM) for h in _HEAD_ORDER])

VMEM_LIMIT_BYTES = 56 * 1024 * 1024

_BF = jnp.bfloat16
_F32 = jnp.float32


def _slope(h):
    return 2.0 ** (-8.0 * (h + 1) / N_HEADS)


def _dot(a, b):
    return jnp.dot(a, b, preferred_element_type=_F32)


def _rms(x, g):
    ms = jnp.mean(x * x, axis=-1, keepdims=True)
    return x * lax.rsqrt(ms + RMS_EPS) * g


def _softmax_with_sink(s, sink):
    m = jnp.maximum(jnp.max(s, axis=-1, keepdims=True), sink)
    e = jnp.exp(s - m)
    den = jnp.sum(e, axis=-1, keepdims=True) + jnp.exp(sink - m)
    return e * (1.0 / den)


def _softmax(s):
    m = jnp.max(s, axis=-1, keepdims=True)
    e = jnp.exp(s - m)
    return e * (1.0 / jnp.sum(e, axis=-1, keepdims=True))


def _window_sums(u_ext, axis):
    out = {}
    s = u_ext
    w = 1
    while w < max(POOL_WINDOWS):
        s = s + pltpu.roll(s, w, axis)
        w *= 2
        out[w] = s
    return out


def _ffn_final(x2, gffn, wup_ref, wdown_ref, gfinal):
    hn = _rms(x2, gffn).astype(_BF)
    acc = x2
    for c in range(D_FF // FF_CHUNK):
        hc = _dot(hn, wup_ref[:, c * FF_CHUNK:(c + 1) * FF_CHUNK])
        hc = jnp.maximum(hc, 0.0)
        hc = (hc * hc).astype(_BF)
        acc = acc + _dot(hc, wdown_ref[c * FF_CHUNK:(c + 1) * FF_CHUNK, :])
    return _rms(acc, gfinal)


def _prompt_mixer_kernel(x_ref, gmix_ref, win_ref, sink_ref, wpool_ref, pscale_ref, wout_ref,
                         x1_ref, klast_ref, vlast_ref, plast_ref,
                         kt_scr, v_scr, u_scr, bias_scr, *, tile):
    b = pl.program_id(0)
    s = pl.program_id(1)
    nb = tile // BLOCK

    @pl.when((b == 0) & (s == 0))
    def _():
        qi = lax.broadcasted_iota(jnp.int32, (BLOCK, 2 * BLOCK), 0)
        kc = lax.broadcasted_iota(jnp.int32, (BLOCK, 2 * BLOCK), 1)
        dist = qi + BLOCK - kc
        valid = (dist >= 0) & (dist <= WINDOW)
        valid_first = valid & (kc >= BLOCK)
        distf = dist.astype(_F32)
        for h in range(N_HEADS):
            ali = -_slope(h) * distf
            bias_scr[0, h] = jnp.where(valid, ali, NEG_INF)
            bias_scr[1, h] = jnp.where(valid_first, ali, NEG_INF)

    @pl.when(s == 0)
    def _():
        kt_scr[...] = jnp.zeros((KV_WIDTH, BLOCK), _BF)
        v_scr[...] = jnp.zeros((BLOCK, KV_WIDTH), _BF)
        u_scr[...] = jnp.zeros((POOL_PAD, POOL_WIDTH), _F32)

    lane = lax.broadcasted_iota(jnp.int32, (BLOCK, 2 * HEAD_DIM), 1)
    lo = lane < HEAD_DIM
    row = lax.broadcasted_iota(jnp.int32, (BLOCK, POOL_GROUP_WIDTH), 0)
    gmix = gmix_ref[...]

    def project(j):
        x = x_ref[0, j * BLOCK:(j + 1) * BLOCK, :]
        proj = _dot(_rms(x, gmix).astype(_BF), win_ref[...])
        k = proj[:, K_OFF:K_OFF + KV_WIDTH]
        v = proj[:, V_OFF:V_OFF + KV_WIDTH]
        u = proj[:, U_OFF:]
        if j == nb - 1:
            klast_ref[0] = k
            vlast_ref[0] = v
            plast_ref[0] = u[BLOCK - POOL_PAD:]
        return dict(x=x, q=proj[:, :ATTN_WIDTH] * Q_SCALE, kt=k.T.astype(_BF), v=v.astype(_BF), u=u)

    def pool(j, u_hist, u):
        pos = s * tile + j * BLOCK + row
        ys = []
        for g, w in enumerate(POOL_WINDOWS):
            c0 = g * POOL_GROUP_WIDTH
            u_ext = jnp.concatenate([u_hist[:, c0:c0 + POOL_GROUP_WIDTH], u[:, c0:c0 + POOL_GROUP_WIDTH]], axis=0)
            sw = _window_sums(u_ext, 0)[w][POOL_PAD:]
            cnt = jnp.minimum(pos + 1, w).astype(_F32)
            d = (sw / cnt - u_ext[POOL_PAD:]).astype(_BF)
            y = _dot(d, wpool_ref[g]) * pscale_ref[:, c0:c0 + POOL_GROUP_WIDTH]
            ys.append(y.astype(_BF))
        return ys

    def wave_scores(j, blk, kt_prev, pairs):
        first = ((s == 0) & (j == 0)).astype(jnp.int32) if j == 0 else 0
        kt2 = jnp.concatenate([kt_prev, blk["kt"]], axis=1)
        scores = []
        for p in pairs:
            slab = blk["q"][:, p * 128:(p + 1) * 128]
            for half, h in ((0, p), (1, p + GROUP)):
                qm = jnp.where(lo if half == 0 else ~lo, slab, 0.0).astype(_BF)
                scores.append(_dot(qm, kt2) + bias_scr[first, h])
        return scores

    def wave_values(blk, v_prev, pairs, scores):
        v2 = jnp.concatenate([v_prev, blk["v"]], axis=0)
        slabs = []
        for i, p in enumerate(pairs):
            outs = []
            for half, h in ((0, p), (1, p + GROUP)):
                pr = _softmax_with_sink(scores[2 * i + half], sink_ref[h])
                outs.append(_dot(pr.astype(_BF), v2))
            slabs.append(jnp.where(lo, outs[0], outs[1]).astype(_BF))
        return slabs

    def output(j, blk, slabs):
        cat = jnp.concatenate(slabs, axis=1)
        x1_ref[0, j * BLOCK:(j + 1) * BLOCK, :] = blk["x"] + _dot(cat, wout_ref[...])

    kt_prev, v_prev, u_hist = kt_scr[...], v_scr[...], u_scr[...]
    blk = project(0)
    done = None
    for j in range(nb):
        sc0 = wave_scores(j, blk, kt_prev, (0, 1))
        nxt = project(j + 1) if j + 1 < nb else None
        sc1 = wave_scores(j, blk, kt_prev, (2, 3))
        pooled = pool(j, u_hist, blk["u"])
        at0 = wave_values(blk, v_prev, (0, 1), sc0)
        if done is not None:
            output(*done)
        at1 = wave_values(blk, v_prev, (2, 3), sc1)
        done = (j, blk, at0 + at1 + pooled)
        kt_prev, v_prev, u_hist = blk["kt"], blk["v"], blk["u"][BLOCK - POOL_PAD:]
        blk = nxt
    output(*done)
    kt_scr[...] = kt_prev
    v_scr[...] = v_prev
    u_scr[...] = u_hist


def _prompt_mixer(x, gmix, win, sinks, wpool, pscale, wout, tile):
    B, S, D = x.shape
    ns = S // tile
    const = lambda *shape: pl.BlockSpec(shape, lambda b, s: (0,) * len(shape))
    return pl.pallas_call(
        functools.partial(_prompt_mixer_kernel, tile=tile),
        grid=(B, ns),
        in_specs=[
            pl.BlockSpec((1, tile, D), lambda b, s: (b, s, 0)),
            const(1, D),
            const(D, IN_WIDTH),
            pl.BlockSpec(memory_space=pltpu.SMEM),
            const(len(POOL_WINDOWS), POOL_GROUP_WIDTH, POOL_GROUP_WIDTH),
            const(1, POOL_WIDTH),
            const(D, D),
        ],
        out_specs=[
            pl.BlockSpec((1, tile, D), lambda b, s: (b, s, 0)),
            pl.BlockSpec((1, BLOCK, KV_WIDTH), lambda b, s: (b, 0, 0)),
            pl.BlockSpec((1, BLOCK, KV_WIDTH), lambda b, s: (b, 0, 0)),
            pl.BlockSpec((1, POOL_PAD, POOL_WIDTH), lambda b, s: (b, 0, 0)),
        ],
        out_shape=[
            jax.ShapeDtypeStruct((B, S, D), _F32),
            jax.ShapeDtypeStruct((B, BLOCK, KV_WIDTH), _F32),
            jax.ShapeDtypeStruct((B, BLOCK, KV_WIDTH), _F32),
            jax.ShapeDtypeStruct((B, POOL_PAD, POOL_WIDTH), _F32),
        ],
        scratch_shapes=[
            pltpu.VMEM((KV_WIDTH, BLOCK), _BF),
            pltpu.VMEM((BLOCK, KV_WIDTH), _BF),
            pltpu.VMEM((POOL_PAD, POOL_WIDTH), _F32),
            pltpu.VMEM((2, N_HEADS, BLOCK, 2 * BLOCK), _F32),
        ],
        compiler_params=pltpu.CompilerParams(
            dimension_semantics=("arbitrary", "arbitrary"), vmem_limit_bytes=VMEM_LIMIT_BYTES),
        name="prompt_mixer",
    )(x, gmix, win, sinks, wpool, pscale, wout)


def _mem_kv_kernel(mem_ref, gmem_ref, wck_ref, wcv_ref, k_ref, v_ref, kt_ref, vb_ref):
    hm = _rms(mem_ref[0], gmem_ref[...]).astype(_BF)
    k = _dot(hm, wck_ref[...])
    v = _dot(hm, wcv_ref[...])
    k_ref[0] = k
    v_ref[0] = v
    vb_ref[0] = v.astype(_BF)
    for h in range(N_CROSS_HEADS):
        kt_ref[0, h] = k[:, h * CROSS_HEAD_DIM:(h + 1) * CROSS_HEAD_DIM].T.astype(_BF)


def _mem_kv(mem, gmem, wck, wcv):
    B, M, D = mem.shape
    const = lambda *shape: pl.BlockSpec(shape, lambda b: (0,) * len(shape))
    return pl.pallas_call(
        _mem_kv_kernel,
        grid=(B,),
        in_specs=[pl.BlockSpec((1, M, D), lambda b: (b, 0, 0)), const(1, D), const(D, D), const(D, D)],
        out_specs=[
            pl.BlockSpec((1, M, D), lambda b: (b, 0, 0)),
            pl.BlockSpec((1, M, D), lambda b: (b, 0, 0)),
            pl.BlockSpec((1, N_CROSS_HEADS, CROSS_HEAD_DIM, M), lambda b: (b, 0, 0, 0)),
            pl.BlockSpec((1, M, D), lambda b: (b, 0, 0)),
        ],
        out_shape=[
            jax.ShapeDtypeStruct((B, M, D), _F32),
            jax.ShapeDtypeStruct((B, M, D), _F32),
            jax.ShapeDtypeStruct((B, N_CROSS_HEADS, CROSS_HEAD_DIM, M), _BF),
            jax.ShapeDtypeStruct((B, M, D), _BF),
        ],
        compiler_params=pltpu.CompilerParams(
            dimension_semantics=("arbitrary",), vmem_limit_bytes=VMEM_LIMIT_BYTES),
        name="prompt_mem_kv",
    )(mem, gmem, wck, wcv)


def _prompt_tail_kernel(x1_ref, gcross_ref, wcq_ref, kt_ref, vb_ref, wco_ref, gffn_ref, wup_ref, wdown_ref,
                        gfinal_ref, y_ref):
    x1 = x1_ref[0]
    q = (_dot(_rms(x1, gcross_ref[...]).astype(_BF), wcq_ref[...]) * CQ_SCALE).astype(_BF)
    outs = []
    for h in range(N_CROSS_HEADS):
        c0 = h * CROSS_HEAD_DIM
        sc = _dot(q[:, c0:c0 + CROSS_HEAD_DIM], kt_ref[0, h])
        pr = _softmax(sc).astype(_BF)
        outs.append(_dot(pr, vb_ref[0, :, c0:c0 + CROSS_HEAD_DIM]).astype(_BF))
    o = jnp.concatenate(outs, axis=1)
    x2 = x1 + _dot(o, wco_ref[...])
    y_ref[0] = _ffn_final(x2, gffn_ref[...], wup_ref, wdown_ref, gfinal_ref[...])


def _single(shape, index_map):
    return pl.BlockSpec(shape, index_map, pipeline_mode=pl.Buffered(1))


def _prompt_tail(x1, gcross, wcq, kt, vb, wco, gffn, wup, wdown, gfinal, tile):
    B, S, D = x1.shape
    const = lambda *shape: _single(shape, lambda b, s: (0,) * len(shape))
    return pl.pallas_call(
        _prompt_tail_kernel,
        grid=(B, S // tile),
        in_specs=[
            pl.BlockSpec((1, tile, D), lambda b, s: (b, s, 0)),
            const(1, D),
            const(D, D),
            pl.BlockSpec((1, N_CROSS_HEADS, CROSS_HEAD_DIM, N_MEM), lambda b, s: (b, 0, 0, 0)),
            pl.BlockSpec((1, N_MEM, D), lambda b, s: (b, 0, 0)),
            const(D, D),
            const(1, D),
            const(D, D_FF),
            const(D_FF, D),
            const(1, D),
        ],
        out_specs=pl.BlockSpec((1, tile, D), lambda b, s: (b, s, 0)),
        out_shape=jax.ShapeDtypeStruct((B, S, D), _F32),
        compiler_params=pltpu.CompilerParams(
            dimension_semantics=("arbitrary", "arbitrary"), vmem_limit_bytes=VMEM_LIMIT_BYTES),
        name="prompt_tail",
    )(x1, gcross, wcq, kt, vb, wco, gffn, wup, wdown, gfinal)


def _sample_mixer_kernel(x_ref, gmix_ref, win_ref, sinkcol_ref, ck_ref, cv_ref, st_ref, wpool_ref, pscale_ref,
                         wout_ref, gcross_ref, wcq_ref,
                         x1_ref, qc_ref, wk_ref, wv_ref, pool_ref,
                         kc_scr, vc_scr, *, nbatch, nseq, past_len):
    rows = nbatch * nseq
    nkeys = 2 * WINDOW
    x = x_ref[...]
    xn = _rms(x, gmix_ref[...]).astype(_BF)
    proj = _dot(xn, win_ref[...])

    k_new = proj[:, K_OFF:K_OFF + KV_WIDTH].reshape(nbatch, nseq, KV_WIDTH)
    v_new = proj[:, V_OFF:V_OFF + KV_WIDTH].reshape(nbatch, nseq, KV_WIDTH)
    ck = ck_ref[...]
    cv = cv_ref[...]
    wk_ref[:, :WINDOW - nseq, :] = ck[:, nseq:, :]
    wk_ref[:, WINDOW - nseq:, :] = k_new
    wv_ref[:, :WINDOW - nseq, :] = cv[:, nseq:, :]
    wv_ref[:, WINDOW - nseq:, :] = v_new

    zpad = jnp.zeros((nbatch, nkeys - WINDOW, KV_WIDTH), _BF)
    kc_scr[:, WINDOW:, :] = zpad
    vc_scr[:, WINDOW:, :] = zpad
    kc_scr[:, :WINDOW, :] = ck.astype(_BF)
    vc_scr[:, :WINDOW, :] = cv.astype(_BF)
    kc_scr[:, WINDOW:WINDOW + nseq, :] = k_new.astype(_BF)
    vc_scr[:, WINDOW:WINDOW + nseq, :] = v_new.astype(_BF)

    t = lax.broadcasted_iota(jnp.int32, (nseq, nkeys), 0)
    c = lax.broadcasted_iota(jnp.int32, (nseq, nkeys), 1)
    dist = jnp.where(c < WINDOW, WINDOW + t - c, t - (c - WINDOW))
    valid = (dist >= 0) & (dist <= WINDOW) & (c < WINDOW + nseq)
    distf = dist.astype(_F32)
    bias = jnp.concatenate([jnp.where(valid, -_slope(h) * distf, NEG_INF) for h in range(N_HEADS)], axis=0)

    lane = lax.broadcasted_iota(jnp.int32, (rows, 2 * HEAD_DIM), 1)
    lo = lane < HEAD_DIM
    qsc = proj[:, :ATTN_WIDTH] * Q_SCALE
    q_lo, q_hi = [], []
    for p in range(GROUP):
        slab = qsc[:, p * 128:(p + 1) * 128]
        q_lo.append(jnp.where(lo, slab, 0.0).reshape(nbatch, nseq, 128))
        q_hi.append(jnp.where(lo, 0.0, slab).reshape(nbatch, nseq, 128))
    qm = jnp.concatenate(q_lo + q_hi, axis=1).astype(_BF)
    sc = jnp.einsum('bqd,bkd->bqk', qm, kc_scr[...], preferred_element_type=_F32) + bias[None]
    pr = _softmax_with_sink(sc, sinkcol_ref[...][None]).astype(_BF)
    o = jnp.einsum('bqk,bkd->bqd', pr, vc_scr[...], preferred_element_type=_F32)
    attn = []
    for p in range(GROUP):
        o_lo = o[:, p * nseq:(p + 1) * nseq, :].reshape(rows, 128)
        o_hi = o[:, (p + GROUP) * nseq:(p + GROUP + 1) * nseq, :].reshape(rows, 128)
        attn.append(jnp.where(lo, o_lo, o_hi).astype(_BF))

    u_new = proj[:, U_OFF:].reshape(nbatch, nseq, POOL_WIDTH)
    u_ext = jnp.concatenate([st_ref[...], u_new], axis=1)
    pool_ref[...] = u_ext[:, nseq:, :]
    tpos = past_len + lax.broadcasted_iota(jnp.int32, (nseq, POOL_GROUP_WIDTH), 0)
    pooled = []
    for g, w in enumerate(POOL_WINDOWS):
        c0 = g * POOL_GROUP_WIDTH
        ug = u_ext[:, :, c0:c0 + POOL_GROUP_WIDTH]
        sw = _window_sums(ug, 1)[w][:, POOL_PAD:, :]
        cnt = jnp.minimum(tpos + 1, w).astype(_F32)
        d = (sw / cnt[None] - ug[:, POOL_PAD:, :]).reshape(rows, POOL_GROUP_WIDTH).astype(_BF)
        y = _dot(d, wpool_ref[g]) * pscale_ref[:, c0:c0 + POOL_GROUP_WIDTH]
        pooled.append(y.astype(_BF))

    cat = jnp.concatenate(attn + pooled, axis=1)
    x1 = x + _dot(cat, wout_ref[...])
    x1_ref[...] = x1
    qc_ref[...] = _dot(_rms(x1, gcross_ref[...]).astype(_BF), wcq_ref[...]) * CQ_SCALE


def _sample_mixer(x, gmix, win, sinkcol, ck, cv, st, wpool, pscale, wout, gcross, wcq, nbatch, nseq, past_len):
    R, D = x.shape
    rows = nbatch * nseq
    const = lambda *shape: pl.BlockSpec(shape, lambda i: (0,) * len(shape))
    return pl.pallas_call(
        functools.partial(_sample_mixer_kernel, nbatch=nbatch, nseq=nseq, past_len=past_len),
        grid=(R // rows,),
        in_specs=[
            pl.BlockSpec((rows, D), lambda i: (i, 0)),
            const(1, D),
            const(D, IN_WIDTH),
            const(N_HEADS * nseq, 1),
            pl.BlockSpec((nbatch, WINDOW, KV_WIDTH), lambda i: (i, 0, 0)),
            pl.BlockSpec((nbatch, WINDOW, KV_WIDTH), lambda i: (i, 0, 0)),
            pl.BlockSpec((nbatch, POOL_PAD, POOL_WIDTH), lambda i: (i, 0, 0)),
            const(len(POOL_WINDOWS), POOL_GROUP_WIDTH, POOL_GROUP_WIDTH),
            const(1, POOL_WIDTH),
            const(D, D),
            const(1, D),
            const(D, D),
        ],
        out_specs=[
            pl.BlockSpec((rows, D), lambda i: (i, 0)),
            pl.BlockSpec((rows, D), lambda i: (i, 0)),
            pl.BlockSpec((nbatch, WINDOW, KV_WIDTH), lambda i: (i, 0, 0)),
            pl.BlockSpec((nbatch, WINDOW, KV_WIDTH), lambda i: (i, 0, 0)),
            pl.BlockSpec((nbatch, POOL_PAD, POOL_WIDTH), lambda i: (i, 0, 0)),
        ],
        out_shape=[
            jax.ShapeDtypeStruct((R, D), _F32),
            jax.ShapeDtypeStruct((R, D), _F32),
            jax.ShapeDtypeStruct((R // nseq, WINDOW, KV_WIDTH), _F32),
            jax.ShapeDtypeStruct((R // nseq, WINDOW, KV_WIDTH), _F32),
            jax.ShapeDtypeStruct((R // nseq, POOL_PAD, POOL_WIDTH), _F32),
        ],
        scratch_shapes=[
            pltpu.VMEM((nbatch, 2 * WINDOW, KV_WIDTH), _BF),
            pltpu.VMEM((nbatch, 2 * WINDOW, KV_WIDTH), _BF),
        ],
        compiler_params=pltpu.CompilerParams(
            dimension_semantics=("arbitrary",), vmem_limit_bytes=VMEM_LIMIT_BYTES),
        name="sample_mixer",
    )(x, gmix, win, sinkcol, ck, cv, st, wpool, pscale, wout, gcross, wcq)


def _mem_cache_rows(cache):
    nb = cache.shape[0]
    c = cache.reshape(nb, N_MEM, N_CROSS_HEADS, CROSS_HEAD_DIM // 128, 128)
    return c.transpose(0, 1, 3, 2, 4).reshape(nb, N_MEM * MEM_ROWS, 128)


def _load_mem_head(ref, b, h):
    halves = [ref[b, pl.ds(half * N_CROSS_HEADS + h, N_MEM, stride=MEM_ROWS), :]
              for half in range(CROSS_HEAD_DIM // 128)]
    return jnp.concatenate(halves, axis=1)


def _sample_memattn_kernel(qc_ref, mk_ref, mv_ref, o_ref, *, nbatch, nseq):
    rows = nbatch * nseq
    q = qc_ref[...].reshape(nbatch, nseq, D_MODEL)
    outs = []
    for h in range(N_CROSS_HEADS):
        c0 = h * CROSS_HEAD_DIM
        qh = q[:, :, c0:c0 + CROSS_HEAD_DIM].astype(_BF)
        kh = jnp.stack([_load_mem_head(mk_ref, b, h).astype(_BF) for b in range(nbatch)])
        vh = jnp.stack([_load_mem_head(mv_ref, b, h).astype(_BF) for b in range(nbatch)])
        sc = jnp.einsum('bqd,bkd->bqk', qh, kh, preferred_element_type=_F32)
        pr = _softmax(sc).astype(_BF)
        outs.append(jnp.einsum('bqk,bkd->bqd', pr, vh, preferred_element_type=_F32))
    o_ref[...] = jnp.concatenate(outs, axis=2).reshape(rows, D_MODEL)


def _sample_memattn(qc, mk, mv, nbatch, nseq):
    R, D = qc.shape
    rows = nbatch * nseq
    return pl.pallas_call(
        functools.partial(_sample_memattn_kernel, nbatch=nbatch, nseq=nseq),
        grid=(R // rows,),
        in_specs=[
            pl.BlockSpec((rows, D), lambda i: (i, 0)),
            pl.BlockSpec((nbatch, N_MEM * MEM_ROWS, 128), lambda i: (i, 0, 0)),
            pl.BlockSpec((nbatch, N_MEM * MEM_ROWS, 128), lambda i: (i, 0, 0)),
        ],
        out_specs=pl.BlockSpec((rows, D), lambda i: (i, 0)),
        out_shape=jax.ShapeDtypeStruct((R, D), _F32),
        compiler_params=pltpu.CompilerParams(
            dimension_semantics=("arbitrary",), vmem_limit_bytes=VMEM_LIMIT_BYTES),
        name="sample_memattn",
    )(qc, mk, mv)


def _sample_tail_kernel(x1_ref, o_ref, wco_ref, gffn_ref, wup_ref, wdown_ref, gfinal_ref, y_ref):
    x2 = x1_ref[...] + _dot(o_ref[...].astype(_BF), wco_ref[...])
    y_ref[...] = _ffn_final(x2, gffn_ref[...], wup_ref, wdown_ref, gfinal_ref[...])


def _sample_tail(x1, o, wco, gffn, wup, wdown, gfinal, tile):
    R, D = x1.shape
    const = lambda *shape: _single(shape, lambda i: (0,) * len(shape))
    return pl.pallas_call(
        _sample_tail_kernel,
        grid=(R // tile,),
        in_specs=[
            pl.BlockSpec((tile, D), lambda i: (i, 0)),
            pl.BlockSpec((tile, D), lambda i: (i, 0)),
            const(D, D),
            const(1, D),
            const(D, D_FF),
            const(D_FF, D),
            const(1, D),
        ],
        out_specs=pl.BlockSpec((tile, D), lambda i: (i, 0)),
        out_shape=jax.ShapeDtypeStruct((R, D), _F32),
        compiler_params=pltpu.CompilerParams(
            dimension_semantics=("arbitrary",), vmem_limit_bytes=VMEM_LIMIT_BYTES),
        name="sample_tail",
    )(x1, o, wco, gffn, wup, wdown, gfinal)


PROMPT_TILE = 512
SAMPLE_MIXER_BATCH = 16
SAMPLE_MEMATTN_BATCH = 4
SAMPLE_TAIL_TILE = 512


def kernel(x_prompt, x_sample, cache_win_k, cache_win_v, state_pool, cache_mem_k, cache_mem_v, mem_prompt,
           g_mix, w_in, attn_sinks, w_pool, pool_scale, w_out, g_cross, g_mem, w_cq, w_ck, w_cv, w_co,
           g_ffn, w_up, w_down, g_final):
    depth = g_mix.shape[0]
    assert depth == 1, "one layer per step"
    B, S, D = x_prompt.shape
    DB, T, _ = x_sample.shape
    past_len = PAST_LEN
    l = 0

    qperm = jnp.asarray(_Q_PERM)
    win = jnp.concatenate([w_in[l][:, qperm], w_in[l][:, ATTN_WIDTH:]], axis=1).astype(_BF)
    wout = jnp.concatenate([w_out[l][qperm, :], w_out[l][ATTN_WIDTH:, :]], axis=0).astype(_BF)
    wpool = w_pool[l].astype(_BF)
    wcq, wck, wcv, wco = (w[l].astype(_BF) for w in (w_cq, w_ck, w_cv, w_co))
    wup, wdown = w_up[l].astype(_BF), w_down[l].astype(_BF)
    gmix, gcross, gmem, gffn = (g[l].reshape(1, D) for g in (g_mix, g_cross, g_mem, g_ffn))
    gfinal = g_final.reshape(1, D)
    pscale = pool_scale[l].reshape(1, POOL_WIDTH)
    sinks = attn_sinks[l]

    x1p, klast, vlast, plast = _prompt_mixer(x_prompt, gmix, win, sinks, wpool, pscale, wout, PROMPT_TILE)
    mem_k, mem_v, mem_kt, mem_vb = _mem_kv(mem_prompt, gmem, wck, wcv)
    y_prompt = _prompt_tail(x1p, gcross, wcq, mem_kt, mem_vb, wco, gffn, wup, wdown, gfinal, PROMPT_TILE)

    xs = x_sample.reshape(DB * T, D)
    ck = cache_win_k[l].reshape(DB, WINDOW, KV_WIDTH)
    cv = cache_win_v[l].reshape(DB, WINDOW, KV_WIDTH)
    st = jnp.pad(state_pool[l], ((0, 0), (POOL_PAD - POOL_HIST, 0), (0, 0)))
    sinkcol = jnp.repeat(sinks, T).reshape(N_HEADS * T, 1)
    x1s, qc, wk_s, wv_s, pool_s = _sample_mixer(xs, gmix, win, sinkcol, ck, cv, st, wpool, pscale, wout,
                                                gcross, wcq, SAMPLE_MIXER_BATCH, T, past_len)
    mk = _mem_cache_rows(cache_mem_k[l])
    mv = _mem_cache_rows(cache_mem_v[l])
    o_s = _sample_memattn(qc, mk, mv, SAMPLE_MEMATTN_BATCH, T)
    y_sample = _sample_tail(x1s, o_s, wco, gffn, wup, wdown, gfinal, SAMPLE_TAIL_TILE).reshape(DB, T, D)

    return (
        y_prompt,
        y_sample,
        klast.reshape(1, B, WINDOW, N_KV_HEADS, HEAD_DIM),
        vlast.reshape(1, B, WINDOW, N_KV_HEADS, HEAD_DIM),
        plast[:, POOL_PAD - POOL_HIST:, :][None],
        mem_k.reshape(1, B, N_MEM, N_CROSS_HEADS, CROSS_HEAD_DIM),
        mem_v.reshape(1, B, N_MEM, N_CROSS_HEADS, CROSS_HEAD_DIM),
        wk_s.reshape(1, DB, WINDOW, N_KV_HEADS, HEAD_DIM),
        wv_s.reshape(1, DB, WINDOW, N_KV_HEADS, HEAD_DIM),
        pool_s[:, POOL_PAD - POOL_HIST:, :][None],
    )
```

```python
import functools

import jax
import jax.numpy as jnp
from jax import lax
from jax.experimental import pallas as pl
from jax.experimental.pallas import tpu as pltpu

D_MODEL = 1024
PAST_LEN = 16384
HEAD_DIM = 64
N_HEADS = 8
N_KV_HEADS = 2
GROUP = N_HEADS // N_KV_HEADS
ATTN_WIDTH = N_HEADS * HEAD_DIM
KV_WIDTH = N_KV_HEADS * HEAD_DIM
WINDOW = 128
BLOCK = WINDOW
POOL_WIDTH = D_MODEL - ATTN_WIDTH
POOL_WINDOWS = (2, 4, 8, 16)
POOL_GROUP_WIDTH = 128
POOL_HIST = 15
POOL_PAD = 16
IN_WIDTH = ATTN_WIDTH + 2 * KV_WIDTH + POOL_WIDTH
N_MEM = 256
N_CROSS_HEADS = 4
CROSS_HEAD_DIM = 256
MEM_ROWS = N_CROSS_HEADS * (CROSS_HEAD_DIM // 128)
D_FF = 4 * D_MODEL
FF_CHUNK = 1024
RMS_EPS = 1e-5
NEG_INF = -1e30
Q_SCALE = HEAD_DIM ** -0.5
CQ_SCALE = CROSS_HEAD_DIM ** -0.5
K_OFF = ATTN_WIDTH
V_OFF = ATTN_WIDTH + KV_WIDTH
U_OFF = ATTN_WIDTH + 2 * KV_WIDTH


def _pair_heads(w, axis):
    shape = w.shape
    split = shape[:axis] + (N_KV_HEADS, GROUP, HEAD_DIM) + shape[axis + 1:]
    return jnp.swapaxes(w.reshape(split), axis, axis + 1).reshape(shape)


VMEM_LIMIT_BYTES = 56 * 1024 * 1024

_BF = jnp.bfloat16
_F32 = jnp.float32


def _slope(h):
    return 2.0 ** (-8.0 * (h + 1) / N_HEADS)


def _dot(a, b):
    return jnp.dot(a, b, preferred_element_type=_F32)


def _rms(x, g):
    ms = jnp.mean(x * x, axis=-1, keepdims=True)
    return x * lax.rsqrt(ms + RMS_EPS) * g


def _softmax_with_sink(s, sink):
    m = jnp.maximum(jnp.max(s, axis=-1, keepdims=True), sink)
    e = jnp.exp(s - m)
    den = jnp.sum(e, axis=-1, keepdims=True) + jnp.exp(sink - m)
    return e * (1.0 / den)


def _softmax(s):
    m = jnp.max(s, axis=-1, keepdims=True)
    e = jnp.exp(s - m)
    return e * (1.0 / jnp.sum(e, axis=-1, keepdims=True))


def _window_sums(u_ext, axis):
    out = {}
    s = u_ext
    w = 1
    while w < max(POOL_WINDOWS):
        s = s + pltpu.roll(s, w, axis)
        w *= 2
        out[w] = s
    return out


def _interleave(streams, skew):
    pending = list(streams)
    live = []
    rnd = 0
    while live or pending:
        if pending and rnd % skew == 0:
            live.append(pending.pop(0))
        for g in list(live):
            try:
                next(g)
            except StopIteration:
                live.remove(g)
        rnd += 1


def _ffn_final_stages(x2, gffn, wup_ref, wdown_ref, gfinal, store):
    hn = _rms(x2, gffn).astype(_BF)
    yield
    acc = x2
    for c in range(D_FF // FF_CHUNK):
        hc = _dot(hn, wup_ref[:, c * FF_CHUNK:(c + 1) * FF_CHUNK])
        yield
        hc = jnp.maximum(hc, 0.0)
        hc = (hc * hc).astype(_BF)
        acc = acc + _dot(hc, wdown_ref[c * FF_CHUNK:(c + 1) * FF_CHUNK, :])
        yield
    store(_rms(acc, gfinal))


def _prompt_mixer_kernel(x_ref, gmix_ref, win_ref, sink_ref, wpool_ref, pscale_ref, wout_ref,
                         x1_ref, klast_ref, vlast_ref, plast_ref,
                         kt_scr, v_scr, u_scr, bias_scr, *, tile):
    b = pl.program_id(0)
    s = pl.program_id(1)
    nb = tile // BLOCK

    @pl.when((b == 0) & (s == 0))
    def _():
        qi = lax.broadcasted_iota(jnp.int32, (BLOCK, 2 * BLOCK), 0)
        kc = lax.broadcasted_iota(jnp.int32, (BLOCK, 2 * BLOCK), 1)
        dist = qi + BLOCK - kc
        valid = (dist >= 0) & (dist <= WINDOW)
        valid_first = valid & (kc >= BLOCK)
        distf = dist.astype(_F32)
        for h in range(N_HEADS):
            ali = -_slope(h) * distf
            bias_scr[0, h] = jnp.where(valid, ali, NEG_INF)
            bias_scr[1, h] = jnp.where(valid_first, ali, NEG_INF)

    @pl.when(s == 0)
    def _():
        kt_scr[...] = jnp.zeros((KV_WIDTH, BLOCK), _BF)
        v_scr[...] = jnp.zeros((BLOCK, KV_WIDTH), _BF)
        u_scr[...] = jnp.zeros((POOL_PAD, POOL_WIDTH), _F32)

    lane = lax.broadcasted_iota(jnp.int32, (BLOCK, 2 * HEAD_DIM), 1)
    lo = lane < HEAD_DIM
    row = lax.broadcasted_iota(jnp.int32, (BLOCK, POOL_GROUP_WIDTH), 0)
    gmix = gmix_ref[...]

    def project(j):
        x = x_ref[0, j * BLOCK:(j + 1) * BLOCK, :]
        proj = _dot(_rms(x, gmix).astype(_BF), win_ref[...])
        k = proj[:, K_OFF:K_OFF + KV_WIDTH]
        v = proj[:, V_OFF:V_OFF + KV_WIDTH]
        u = proj[:, U_OFF:]
        if j == nb - 1:
            klast_ref[0] = k
            vlast_ref[0] = v
            plast_ref[0] = u[BLOCK - POOL_PAD:]
        return dict(x=x, q=proj[:, :ATTN_WIDTH] * Q_SCALE, kt=k.T.astype(_BF), v=v.astype(_BF), u=u)

    def pool(j, u_hist, u):
        pos = s * tile + j * BLOCK + row
        ys = []
        for g, w in enumerate(POOL_WINDOWS):
            c0 = g * POOL_GROUP_WIDTH
            u_ext = jnp.concatenate([u_hist[:, c0:c0 + POOL_GROUP_WIDTH], u[:, c0:c0 + POOL_GROUP_WIDTH]], axis=0)
            sw = _window_sums(u_ext, 0)[w][POOL_PAD:]
            cnt = jnp.minimum(pos + 1, w).astype(_F32)
            d = (sw / cnt - u_ext[POOL_PAD:]).astype(_BF)
            y = _dot(d, wpool_ref[g]) * pscale_ref[:, c0:c0 + POOL_GROUP_WIDTH]
            ys.append(y.astype(_BF))
        return ys

    def wave_scores(j, blk, kt_prev, pairs):
        first = ((s == 0) & (j == 0)).astype(jnp.int32) if j == 0 else 0
        kt2 = jnp.concatenate([kt_prev, blk["kt"]], axis=1)
        scores = []
        for p in pairs:
            slab = blk["q"][:, p * 128:(p + 1) * 128]
            for half, h in ((0, p), (1, p + GROUP)):
                qm = jnp.where(lo if half == 0 else ~lo, slab, 0.0).astype(_BF)
                scores.append(_dot(qm, kt2) + bias_scr[first, h])
        return scores

    def wave_values(blk, v_prev, pairs, scores):
        v2 = jnp.concatenate([v_prev, blk["v"]], axis=0)
        slabs = []
        for i, p in enumerate(pairs):
            outs = []
            for half, h in ((0, p), (1, p + GROUP)):
                pr = _softmax_with_sink(scores[2 * i + half], sink_ref[h])
                outs.append(_dot(pr.astype(_BF), v2))
            slabs.append(jnp.where(lo, outs[0], outs[1]).astype(_BF))
        return slabs

    def output(j, blk, slabs):
        cat = jnp.concatenate(slabs, axis=1)
        x1_ref[0, j * BLOCK:(j + 1) * BLOCK, :] = blk["x"] + _dot(cat, wout_ref[...])

    kt_prev, v_prev, u_hist = kt_scr[...], v_scr[...], u_scr[...]
    blk = project(0)
    done = None
    for j in range(nb):
        sc0 = wave_scores(j, blk, kt_prev, (0, 1))
        nxt = project(j + 1) if j + 1 < nb else None
        sc1 = wave_scores(j, blk, kt_prev, (2, 3))
        pooled = pool(j, u_hist, blk["u"])
        at0 = wave_values(blk, v_prev, (0, 1), sc0)
        if done is not None:
            output(*done)
        at1 = wave_values(blk, v_prev, (2, 3), sc1)
        done = (j, blk, at0 + at1 + pooled)
        kt_prev, v_prev, u_hist = blk["kt"], blk["v"], blk["u"][BLOCK - POOL_PAD:]
        blk = nxt
    output(*done)
    kt_scr[...] = kt_prev
    v_scr[...] = v_prev
    u_scr[...] = u_hist


def _prompt_mixer(x, gmix, win, sinks, wpool, pscale, wout, tile):
    B, S, D = x.shape
    ns = S // tile
    const = lambda *shape: pl.BlockSpec(shape, lambda b, s: (0,) * len(shape))
    return pl.pallas_call(
        functools.partial(_prompt_mixer_kernel, tile=tile),
        grid=(B, ns),
        in_specs=[
            pl.BlockSpec((1, tile, D), lambda b, s: (b, s, 0)),
            const(1, D),
            const(D, IN_WIDTH),
            pl.BlockSpec(memory_space=pltpu.SMEM),
            const(len(POOL_WINDOWS), POOL_GROUP_WIDTH, POOL_GROUP_WIDTH),
            const(1, POOL_WIDTH),
            const(D, D),
        ],
        out_specs=[
            pl.BlockSpec((1, tile, D), lambda b, s: (b, s, 0)),
            pl.BlockSpec((1, BLOCK, KV_WIDTH), lambda b, s: (b, 0, 0)),
            pl.BlockSpec((1, BLOCK, KV_WIDTH), lambda b, s: (b, 0, 0)),
            pl.BlockSpec((1, POOL_PAD, POOL_WIDTH), lambda b, s: (b, 0, 0)),
        ],
        out_shape=[
            jax.ShapeDtypeStruct((B, S, D), _F32),
            jax.ShapeDtypeStruct((B, BLOCK, KV_WIDTH), _F32),
            jax.ShapeDtypeStruct((B, BLOCK, KV_WIDTH), _F32),
            jax.ShapeDtypeStruct((B, POOL_PAD, POOL_WIDTH), _F32),
        ],
        scratch_shapes=[
            pltpu.VMEM((KV_WIDTH, BLOCK), _BF),
            pltpu.VMEM((BLOCK, KV_WIDTH), _BF),
            pltpu.VMEM((POOL_PAD, POOL_WIDTH), _F32),
            pltpu.VMEM((2, N_HEADS, BLOCK, 2 * BLOCK), _F32),
        ],
        compiler_params=pltpu.CompilerParams(
            dimension_semantics=("arbitrary", "arbitrary"), vmem_limit_bytes=VMEM_LIMIT_BYTES),
        name="prompt_mixer",
    )(x, gmix, win, sinks, wpool, pscale, wout)


def _mem_kv_kernel(mem_ref, gmem_ref, wck_ref, wcv_ref, k_ref, v_ref, kt_ref, vb_ref):
    hm = _rms(mem_ref[0], gmem_ref[...]).astype(_BF)
    k = _dot(hm, wck_ref[...])
    v = _dot(hm, wcv_ref[...])
    k_ref[0] = k
    v_ref[0] = v
    vb_ref[0] = v.astype(_BF)
    for h in range(N_CROSS_HEADS):
        kt_ref[0, h] = k[:, h * CROSS_HEAD_DIM:(h + 1) * CROSS_HEAD_DIM].T.astype(_BF)


def _mem_kv(mem, gmem, wck, wcv):
    B, M, D = mem.shape
    const = lambda *shape: pl.BlockSpec(shape, lambda b: (0,) * len(shape))
    return pl.pallas_call(
        _mem_kv_kernel,
        grid=(B,),
        in_specs=[pl.BlockSpec((1, M, D), lambda b: (b, 0, 0)), const(1, D), const(D, D), const(D, D)],
        out_specs=[
            pl.BlockSpec((1, M, D), lambda b: (b, 0, 0)),
            pl.BlockSpec((1, M, D), lambda b: (b, 0, 0)),
            pl.BlockSpec((1, N_CROSS_HEADS, CROSS_HEAD_DIM, M), lambda b: (b, 0, 0, 0)),
            pl.BlockSpec((1, M, D), lambda b: (b, 0, 0)),
        ],
        out_shape=[
            jax.ShapeDtypeStruct((B, M, D), _F32),
            jax.ShapeDtypeStruct((B, M, D), _F32),
            jax.ShapeDtypeStruct((B, N_CROSS_HEADS, CROSS_HEAD_DIM, M), _BF),
            jax.ShapeDtypeStruct((B, M, D), _BF),
        ],
        compiler_params=pltpu.CompilerParams(
            dimension_semantics=("arbitrary",), vmem_limit_bytes=VMEM_LIMIT_BYTES),
        name="prompt_mem_kv",
    )(mem, gmem, wck, wcv)


def _prompt_tail_kernel(x1_ref, gcross_ref, wcq_ref, kt_ref, vb_ref, wco_ref, gffn_ref, wup_ref, wdown_ref,
                        gfinal_ref, y_ref, *, tile, sub):
    gcross, gffn, gfinal = gcross_ref[...], gffn_ref[...], gfinal_ref[...]

    def stream(r0):
        x1 = x1_ref[0, r0:r0 + sub, :]
        hn = _rms(x1, gcross).astype(_BF)
        yield
        q = (_dot(hn, wcq_ref[...]) * CQ_SCALE).astype(_BF)
        yield
        heads = [slice(h * CROSS_HEAD_DIM, (h + 1) * CROSS_HEAD_DIM) for h in range(N_CROSS_HEADS)]
        scores = [_dot(q[:, c], kt_ref[0, h]) for h, c in enumerate(heads)]
        yield
        outs = [_dot(_softmax(sc).astype(_BF), vb_ref[0, :, c]).astype(_BF) for sc, c in zip(scores, heads)]
        yield
        x2 = x1 + _dot(jnp.concatenate(outs, axis=1), wco_ref[...])
        yield

        def store(y):
            y_ref[0, r0:r0 + sub, :] = y

        yield from _ffn_final_stages(x2, gffn, wup_ref, wdown_ref, gfinal, store)

    _interleave([stream(r0) for r0 in range(0, tile, sub)], TAIL_SKEW)


def _single(shape, index_map):
    return pl.BlockSpec(shape, index_map, pipeline_mode=pl.Buffered(1))


def _prompt_tail(x1, gcross, wcq, kt, vb, wco, gffn, wup, wdown, gfinal, tile, sub):
    B, S, D = x1.shape
    const = lambda *shape: _single(shape, lambda b, s: (0,) * len(shape))
    return pl.pallas_call(
        functools.partial(_prompt_tail_kernel, tile=tile, sub=sub),
        grid=(B, S // tile),
        in_specs=[
            pl.BlockSpec((1, tile, D), lambda b, s: (b, s, 0)),
            const(1, D),
            const(D, D),
            pl.BlockSpec((1, N_CROSS_HEADS, CROSS_HEAD_DIM, N_MEM), lambda b, s: (b, 0, 0, 0)),
            pl.BlockSpec((1, N_MEM, D), lambda b, s: (b, 0, 0)),
            const(D, D),
            const(1, D),
            const(D, D_FF),
            const(D_FF, D),
            const(1, D),
        ],
        out_specs=pl.BlockSpec((1, tile, D), lambda b, s: (b, s, 0)),
        out_shape=jax.ShapeDtypeStruct((B, S, D), _F32),
        compiler_params=pltpu.CompilerParams(
            dimension_semantics=("arbitrary", "arbitrary"), vmem_limit_bytes=VMEM_LIMIT_BYTES),
        name="prompt_tail",
    )(x1, gcross, wcq, kt, vb, wco, gffn, wup, wdown, gfinal)


def _sample_mixer_kernel(x_ref, gmix_ref, win_ref, sinkcol_ref, ck_ref, cv_ref, st_ref, wpool_ref, pscale_ref,
                         wout_ref, gcross_ref, wcq_ref,
                         x1_ref, qc_ref, wk_ref, wv_ref, pool_ref,
                         kc_scr, vc_scr, *, nbatch, nseq, past_len):
    rows = nbatch * nseq
    nkeys = 2 * WINDOW
    x = x_ref[...]
    xn = _rms(x, gmix_ref[...]).astype(_BF)
    proj = _dot(xn, win_ref[...])

    k_new = proj[:, K_OFF:K_OFF + KV_WIDTH].reshape(nbatch, nseq, KV_WIDTH)
    v_new = proj[:, V_OFF:V_OFF + KV_WIDTH].reshape(nbatch, nseq, KV_WIDTH)
    ck = ck_ref[...]
    cv = cv_ref[...]
    wk_ref[:, :WINDOW - nseq, :] = ck[:, nseq:, :]
    wk_ref[:, WINDOW - nseq:, :] = k_new
    wv_ref[:, :WINDOW - nseq, :] = cv[:, nseq:, :]
    wv_ref[:, WINDOW - nseq:, :] = v_new

    zpad = jnp.zeros((nbatch, nkeys - WINDOW, KV_WIDTH), _BF)
    kc_scr[:, WINDOW:, :] = zpad
    vc_scr[:, WINDOW:, :] = zpad
    kc_scr[:, :WINDOW, :] = ck.astype(_BF)
    vc_scr[:, :WINDOW, :] = cv.astype(_BF)
    kc_scr[:, WINDOW:WINDOW + nseq, :] = k_new.astype(_BF)
    vc_scr[:, WINDOW:WINDOW + nseq, :] = v_new.astype(_BF)

    t = lax.broadcasted_iota(jnp.int32, (nseq, nkeys), 0)
    c = lax.broadcasted_iota(jnp.int32, (nseq, nkeys), 1)
    dist = jnp.where(c < WINDOW, WINDOW + t - c, t - (c - WINDOW))
    valid = (dist >= 0) & (dist <= WINDOW) & (c < WINDOW + nseq)
    distf = dist.astype(_F32)
    bias = jnp.concatenate([jnp.where(valid, -_slope(h) * distf, NEG_INF) for h in range(N_HEADS)], axis=0)

    lane = lax.broadcasted_iota(jnp.int32, (rows, 2 * HEAD_DIM), 1)
    lo = lane < HEAD_DIM
    qsc = proj[:, :ATTN_WIDTH] * Q_SCALE
    q_lo, q_hi = [], []
    for p in range(GROUP):
        slab = qsc[:, p * 128:(p + 1) * 128]
        q_lo.append(jnp.where(lo, slab, 0.0).reshape(nbatch, nseq, 128))
        q_hi.append(jnp.where(lo, 0.0, slab).reshape(nbatch, nseq, 128))
    qm = jnp.concatenate(q_lo + q_hi, axis=1).astype(_BF)
    sc = jnp.einsum('bqd,bkd->bqk', qm, kc_scr[...], preferred_element_type=_F32) + bias[None]
    pr = _softmax_with_sink(sc, sinkcol_ref[...][None]).astype(_BF)
    o = jnp.einsum('bqk,bkd->bqd', pr, vc_scr[...], preferred_element_type=_F32)
    attn = []
    for p in range(GROUP):
        o_lo = o[:, p * nseq:(p + 1) * nseq, :].reshape(rows, 128)
        o_hi = o[:, (p + GROUP) * nseq:(p + GROUP + 1) * nseq, :].reshape(rows, 128)
        attn.append(jnp.where(lo, o_lo, o_hi).astype(_BF))

    u_new = proj[:, U_OFF:].reshape(nbatch, nseq, POOL_WIDTH)
    u_ext = jnp.concatenate([st_ref[...], u_new], axis=1)
    pool_ref[...] = u_ext[:, nseq:, :]
    tpos = past_len + lax.broadcasted_iota(jnp.int32, (nseq, POOL_GROUP_WIDTH), 0)
    pooled = []
    for g, w in enumerate(POOL_WINDOWS):
        c0 = g * POOL_GROUP_WIDTH
        ug = u_ext[:, :, c0:c0 + POOL_GROUP_WIDTH]
        sw = _window_sums(ug, 1)[w][:, POOL_PAD:, :]
        cnt = jnp.minimum(tpos + 1, w).astype(_F32)
        d = (sw / cnt[None] - ug[:, POOL_PAD:, :]).reshape(rows, POOL_GROUP_WIDTH).astype(_BF)
        y = _dot(d, wpool_ref[g]) * pscale_ref[:, c0:c0 + POOL_GROUP_WIDTH]
        pooled.append(y.astype(_BF))

    cat = jnp.concatenate(attn + pooled, axis=1)
    x1 = x + _dot(cat, wout_ref[...])
    x1_ref[...] = x1
    qc_ref[...] = _dot(_rms(x1, gcross_ref[...]).astype(_BF), wcq_ref[...]) * CQ_SCALE


def _sample_mixer(x, gmix, win, sinkcol, ck, cv, st, wpool, pscale, wout, gcross, wcq, nbatch, nseq, past_len):
    R, D = x.shape
    rows = nbatch * nseq
    const = lambda *shape: pl.BlockSpec(shape, lambda i: (0,) * len(shape))
    return pl.pallas_call(
        functools.partial(_sample_mixer_kernel, nbatch=nbatch, nseq=nseq, past_len=past_len),
        grid=(R // rows,),
        in_specs=[
            pl.BlockSpec((rows, D), lambda i: (i, 0)),
            const(1, D),
            const(D, IN_WIDTH),
            const(N_HEADS * nseq, 1),
            pl.BlockSpec((nbatch, WINDOW, KV_WIDTH), lambda i: (i, 0, 0)),
            pl.BlockSpec((nbatch, WINDOW, KV_WIDTH), lambda i: (i, 0, 0)),
            pl.BlockSpec((nbatch, POOL_PAD, POOL_WIDTH), lambda i: (i, 0, 0)),
            const(len(POOL_WINDOWS), POOL_GROUP_WIDTH, POOL_GROUP_WIDTH),
            const(1, POOL_WIDTH),
            const(D, D),
            const(1, D),
            const(D, D),
        ],
        out_specs=[
            pl.BlockSpec((rows, D), lambda i: (i, 0)),
            pl.BlockSpec((rows, D), lambda i: (i, 0)),
            pl.BlockSpec((nbatch, WINDOW, KV_WIDTH), lambda i: (i, 0, 0)),
            pl.BlockSpec((nbatch, WINDOW, KV_WIDTH), lambda i: (i, 0, 0)),
            pl.BlockSpec((nbatch, POOL_PAD, POOL_WIDTH), lambda i: (i, 0, 0)),
        ],
        out_shape=[
            jax.ShapeDtypeStruct((R, D), _F32),
            jax.ShapeDtypeStruct((R, D), _F32),
            jax.ShapeDtypeStruct((R // nseq, WINDOW, KV_WIDTH), _F32),
            jax.ShapeDtypeStruct((R // nseq, WINDOW, KV_WIDTH), _F32),
            jax.ShapeDtypeStruct((R // nseq, POOL_PAD, POOL_WIDTH), _F32),
        ],
        scratch_shapes=[
            pltpu.VMEM((nbatch, 2 * WINDOW, KV_WIDTH), _BF),
            pltpu.VMEM((nbatch, 2 * WINDOW, KV_WIDTH), _BF),
        ],
        compiler_params=pltpu.CompilerParams(
            dimension_semantics=("arbitrary",), vmem_limit_bytes=VMEM_LIMIT_BYTES),
        name="sample_mixer",
    )(x, gmix, win, sinkcol, ck, cv, st, wpool, pscale, wout, gcross, wcq)


def _mem_cache_rows(cache):
    nb = cache.shape[0]
    c = cache.reshape(nb, N_MEM, N_CROSS_HEADS, CROSS_HEAD_DIM // 128, 128)
    return c.transpose(0, 1, 3, 2, 4).reshape(nb, N_MEM * MEM_ROWS, 128)


def _load_mem_head(ref, b, h):
    halves = [ref[b, pl.ds(half * N_CROSS_HEADS + h, N_MEM, stride=MEM_ROWS), :]
              for half in range(CROSS_HEAD_DIM // 128)]
    return jnp.concatenate(halves, axis=1)


def _sample_memattn_kernel(qc_ref, mk_ref, mv_ref, o_ref, *, nbatch, nseq):
    rows = nbatch * nseq
    q = qc_ref[...].reshape(nbatch, nseq, D_MODEL)
    outs = []
    for h in range(N_CROSS_HEADS):
        c0 = h * CROSS_HEAD_DIM
        qh = q[:, :, c0:c0 + CROSS_HEAD_DIM].astype(_BF)
        kh = jnp.stack([_load_mem_head(mk_ref, b, h).astype(_BF) for b in range(nbatch)])
        vh = jnp.stack([_load_mem_head(mv_ref, b, h).astype(_BF) for b in range(nbatch)])
        sc = jnp.einsum('bqd,bkd->bqk', qh, kh, preferred_element_type=_F32)
        pr = _softmax(sc).astype(_BF)
        outs.append(jnp.einsum('bqk,bkd->bqd', pr, vh, preferred_element_type=_F32))
    o_ref[...] = jnp.concatenate(outs, axis=2).reshape(rows, D_MODEL)


def _sample_memattn(qc, mk, mv, nbatch, nseq):
    R, D = qc.shape
    rows = nbatch * nseq
    return pl.pallas_call(
        functools.partial(_sample_memattn_kernel, nbatch=nbatch, nseq=nseq),
        grid=(R // rows,),
        in_specs=[
            pl.BlockSpec((rows, D), lambda i: (i, 0)),
            pl.BlockSpec((nbatch, N_MEM * MEM_ROWS, 128), lambda i: (i, 0, 0)),
            pl.BlockSpec((nbatch, N_MEM * MEM_ROWS, 128), lambda i: (i, 0, 0)),
        ],
        out_specs=pl.BlockSpec((rows, D), lambda i: (i, 0)),
        out_shape=jax.ShapeDtypeStruct((R, D), _F32),
        compiler_params=pltpu.CompilerParams(
            dimension_semantics=("arbitrary",), vmem_limit_bytes=VMEM_LIMIT_BYTES),
        name="sample_memattn",
    )(qc, mk, mv)


def _sample_tail_kernel(x1_ref, o_ref, wco_ref, gffn_ref, wup_ref, wdown_ref, gfinal_ref, y_ref, *, tile, sub):
    gffn, gfinal = gffn_ref[...], gfinal_ref[...]

    def stream(r0):
        x2 = x1_ref[r0:r0 + sub, :] + _dot(o_ref[r0:r0 + sub, :].astype(_BF), wco_ref[...])
        yield

        def store(y):
            y_ref[r0:r0 + sub, :] = y

        yield from _ffn_final_stages(x2, gffn, wup_ref, wdown_ref, gfinal, store)

    _interleave([stream(r0) for r0 in range(0, tile, sub)], TAIL_SKEW)


def _sample_tail(x1, o, wco, gffn, wup, wdown, gfinal, tile, sub):
    R, D = x1.shape
    const = lambda *shape: _single(shape, lambda i: (0,) * len(shape))
    return pl.pallas_call(
        functools.partial(_sample_tail_kernel, tile=tile, sub=sub),
        grid=(R // tile,),
        in_specs=[
            pl.BlockSpec((tile, D), lambda i: (i, 0)),
            pl.BlockSpec((tile, D), lambda i: (i, 0)),
            const(D, D),
            const(1, D),
            const(D, D_FF),
            const(D_FF, D),
            const(1, D),
        ],
        out_specs=pl.BlockSpec((tile, D), lambda i: (i, 0)),
        out_shape=jax.ShapeDtypeStruct((R, D), _F32),
        compiler_params=pltpu.CompilerParams(
            dimension_semantics=("arbitrary",), vmem_limit_bytes=VMEM_LIMIT_BYTES),
        name="sample_tail",
    )(x1, o, wco, gffn, wup, wdown, gfinal)


PROMPT_TILE = 512
TAIL_TILE = 1024
TAIL_SUB = 256
TAIL_SKEW = 5
SAMPLE_MIXER_BATCH = 16
SAMPLE_MEMATTN_BATCH = 4


def kernel(x_prompt, x_sample, cache_win_k, cache_win_v, state_pool, cache_mem_k, cache_mem_v, mem_prompt,
           g_mix, w_in, attn_sinks, w_pool, pool_scale, w_out, g_cross, g_mem, w_cq, w_ck, w_cv, w_co,
           g_ffn, w_up, w_down, g_final):
    depth = g_mix.shape[0]
    assert depth == 1, "one layer per step"
    B, S, D = x_prompt.shape
    DB, T, _ = x_sample.shape
    past_len = PAST_LEN
    l = 0

    win = jnp.concatenate([_pair_heads(w_in[l][:, :ATTN_WIDTH], 1), w_in[l][:, ATTN_WIDTH:]], axis=1).astype(_BF)
    wout = jnp.concatenate([_pair_heads(w_out[l][:ATTN_WIDTH, :], 0), w_out[l][ATTN_WIDTH:, :]], axis=0).astype(_BF)
    wpool = w_pool[l].astype(_BF)
    wcq, wck, wcv, wco = (w[l].astype(_BF) for w in (w_cq, w_ck, w_cv, w_co))
    wup, wdown = w_up[l].astype(_BF), w_down[l].astype(_BF)
    gmix, gcross, gmem, gffn = (g[l].reshape(1, D) for g in (g_mix, g_cross, g_mem, g_ffn))
    gfinal = g_final.reshape(1, D)
    pscale = pool_scale[l].reshape(1, POOL_WIDTH)
    sinks = attn_sinks[l]

    x1p, klast, vlast, plast = _prompt_mixer(x_prompt, gmix, win, sinks, wpool, pscale, wout, PROMPT_TILE)
    mem_k, mem_v, mem_kt, mem_vb = _mem_kv(mem_prompt, gmem, wck, wcv)
    y_prompt = _prompt_tail(x1p, gcross, wcq, mem_kt, mem_vb, wco, gffn, wup, wdown, gfinal, TAIL_TILE, TAIL_SUB)

    xs = x_sample.reshape(DB * T, D)
    ck = cache_win_k[l].reshape(DB, WINDOW, KV_WIDTH)
    cv = cache_win_v[l].reshape(DB, WINDOW, KV_WIDTH)
    st = jnp.pad(state_pool[l], ((0, 0), (POOL_PAD - POOL_HIST, 0), (0, 0)))
    sinkcol = jnp.repeat(sinks, T).reshape(N_HEADS * T, 1)
    x1s, qc, wk_s, wv_s, pool_s = _sample_mixer(xs, gmix, win, sinkcol, ck, cv, st, wpool, pscale, wout,
                                                gcross, wcq, SAMPLE_MIXER_BATCH, T, past_len)
    mk = _mem_cache_rows(cache_mem_k[l])
    mv = _mem_cache_rows(cache_mem_v[l])
    o_s = _sample_memattn(qc, mk, mv, SAMPLE_MEMATTN_BATCH, T)
    y_sample = _sample_tail(x1s, o_s, wco, gffn, wup, wdown, gfinal, TAIL_TILE, TAIL_SUB).reshape(DB, T, D)

    return (
        y_prompt,
        y_sample,
        klast.reshape(1, B, WINDOW, N_KV_HEADS, HEAD_DIM),
        vlast.reshape(1, B, WINDOW, N_KV_HEADS, HEAD_DIM),
        plast[:, POOL_PAD - POOL_HIST:, :][None],
        mem_k.reshape(1, B, N_MEM, N_CROSS_HEADS, CROSS_HEAD_DIM),
        mem_v.reshape(1, B, N_MEM, N_CROSS_HEADS, CROSS_HEAD_DIM),
        wk_s.reshape(1, DB, WINDOW, N_KV_HEADS, HEAD_DIM),
        wv_s.reshape(1, DB, WINDOW, N_KV_HEADS, HEAD_DIM),
        pool_s[:, POOL_PAD - POOL_HIST:, :][None],
    )
```

```python
import functools

import jax
import jax.numpy as jnp
from jax import lax
from jax.experimental import pallas as pl
from jax.experimental.pallas import tpu as pltpu

D_MODEL = 1024
PAST_LEN = 16384
HEAD_DIM = 64
N_HEADS = 8
N_KV_HEADS = 2
GROUP = N_HEADS // N_KV_HEADS
ATTN_WIDTH = N_HEADS * HEAD_DIM
KV_WIDTH = N_KV_HEADS * HEAD_DIM
WINDOW = 128
BLOCK = WINDOW
POOL_WIDTH = D_MODEL - ATTN_WIDTH
POOL_WINDOWS = (2, 4, 8, 16)
POOL_GROUP_WIDTH = 128
POOL_HIST = 15
POOL_PAD = 16
IN_WIDTH = ATTN_WIDTH + 2 * KV_WIDTH + POOL_WIDTH
N_MEM = 256
N_CROSS_HEADS = 4
CROSS_HEAD_DIM = 256
MEM_ROWS = N_CROSS_HEADS * (CROSS_HEAD_DIM // 128)
D_FF = 4 * D_MODEL
FF_CHUNK = 1024
RMS_EPS = 1e-5
NEG_INF = -1e30
Q_SCALE = HEAD_DIM ** -0.5
CQ_SCALE = CROSS_HEAD_DIM ** -0.5
K_OFF = ATTN_WIDTH
V_OFF = ATTN_WIDTH + KV_WIDTH
U_OFF = ATTN_WIDTH + 2 * KV_WIDTH


def _pair_heads(w, axis):
    shape = w.shape
    split = shape[:axis] + (N_KV_HEADS, GROUP, HEAD_DIM) + shape[axis + 1:]
    return jnp.swapaxes(w.reshape(split), axis, axis + 1).reshape(shape)


VMEM_LIMIT_BYTES = 56 * 1024 * 1024

_BF = jnp.bfloat16
_F32 = jnp.float32


def _slope(h):
    return 2.0 ** (-8.0 * (h + 1) / N_HEADS)


def _dot(a, b):
    return jnp.dot(a, b, preferred_element_type=_F32)


def _rms(x, g):
    ms = jnp.mean(x * x, axis=-1, keepdims=True)
    return x * lax.rsqrt(ms + RMS_EPS) * g


def _softmax_with_sink(s, sink):
    m = jnp.maximum(jnp.max(s, axis=-1, keepdims=True), sink)
    e = jnp.exp(s - m)
    den = jnp.sum(e, axis=-1, keepdims=True) + jnp.exp(sink - m)
    return e * (1.0 / den)


def _softmax(s):
    m = jnp.max(s, axis=-1, keepdims=True)
    e = jnp.exp(s - m)
    return e * (1.0 / jnp.sum(e, axis=-1, keepdims=True))


def _window_sums(u_ext, axis):
    out = {}
    s = u_ext
    w = 1
    while w < max(POOL_WINDOWS):
        s = s + pltpu.roll(s, w, axis)
        w *= 2
        out[w] = s
    return out


def _interleave(streams, skew):
    pending = list(streams)
    live = []
    rnd = 0
    while live or pending:
        if pending and rnd % skew == 0:
            live.append(pending.pop(0))
        for g in list(live):
            try:
                next(g)
            except StopIteration:
                live.remove(g)
        rnd += 1


def _ffn_final_stages(x2, gffn, wup_ref, wdown_ref, gfinal, store):
    hn = _rms(x2, gffn).astype(_BF)
    yield
    acc = x2
    for c in range(D_FF // FF_CHUNK):
        hc = _dot(hn, wup_ref[:, c * FF_CHUNK:(c + 1) * FF_CHUNK])
        yield
        hc = jnp.maximum(hc, 0.0)
        hc = (hc * hc).astype(_BF)
        acc = acc + _dot(hc, wdown_ref[c * FF_CHUNK:(c + 1) * FF_CHUNK, :])
        yield
    store(_rms(acc, gfinal))


def _prompt_mixer_kernel(x_ref, gmix_ref, win_ref, sink_ref, wpool_ref, pscale_ref, wout_ref,
                         wcq_ref, wco_ref, wup_ref, wdown_ref,
                         x1_ref, klast_ref, vlast_ref, plast_ref, wcq_bf_ref, wco_bf_ref, wup_bf_ref, wdown_bf_ref,
                         kt_scr, v_scr, u_scr, bias_scr, *, tile):
    b = pl.program_id(0)
    s = pl.program_id(1)
    nb = tile // BLOCK

    for src, dst in ((wcq_ref, wcq_bf_ref), (wco_ref, wco_bf_ref), (wup_ref, wup_bf_ref), (wdown_ref, wdown_bf_ref)):
        dst[...] = src[...].astype(_BF)

    @pl.when((b == 0) & (s == 0))
    def _():
        qi = lax.broadcasted_iota(jnp.int32, (BLOCK, 2 * BLOCK), 0)
        kc = lax.broadcasted_iota(jnp.int32, (BLOCK, 2 * BLOCK), 1)
        dist = qi + BLOCK - kc
        valid = (dist >= 0) & (dist <= WINDOW)
        valid_first = valid & (kc >= BLOCK)
        distf = dist.astype(_F32)
        for h in range(N_HEADS):
            ali = -_slope(h) * distf
            bias_scr[0, h] = jnp.where(valid, ali, NEG_INF)
            bias_scr[1, h] = jnp.where(valid_first, ali, NEG_INF)

    @pl.when(s == 0)
    def _():
        kt_scr[...] = jnp.zeros((KV_WIDTH, BLOCK), _BF)
        v_scr[...] = jnp.zeros((BLOCK, KV_WIDTH), _BF)
        u_scr[...] = jnp.zeros((POOL_PAD, POOL_WIDTH), _F32)

    lane = lax.broadcasted_iota(jnp.int32, (BLOCK, 2 * HEAD_DIM), 1)
    lo = lane < HEAD_DIM
    row = lax.broadcasted_iota(jnp.int32, (BLOCK, POOL_GROUP_WIDTH), 0)
    gmix = gmix_ref[...]

    def project(j):
        x = x_ref[0, j * BLOCK:(j + 1) * BLOCK, :]
        proj = _dot(_rms(x, gmix).astype(_BF), win_ref[...])
        k = proj[:, K_OFF:K_OFF + KV_WIDTH]
        v = proj[:, V_OFF:V_OFF + KV_WIDTH]
        u = proj[:, U_OFF:]
        if j == nb - 1:
            klast_ref[0] = k
            vlast_ref[0] = v
            plast_ref[0] = u[BLOCK - POOL_PAD:]
        return dict(x=x, q=proj[:, :ATTN_WIDTH] * Q_SCALE, kt=k.T.astype(_BF), v=v.astype(_BF), u=u)

    def pool(j, u_hist, u):
        pos = s * tile + j * BLOCK + row
        ys = []
        for g, w in enumerate(POOL_WINDOWS):
            c0 = g * POOL_GROUP_WIDTH
            u_ext = jnp.concatenate([u_hist[:, c0:c0 + POOL_GROUP_WIDTH], u[:, c0:c0 + POOL_GROUP_WIDTH]], axis=0)
            sw = _window_sums(u_ext, 0)[w][POOL_PAD:]
            cnt = jnp.minimum(pos + 1, w).astype(_F32)
            d = (sw / cnt - u_ext[POOL_PAD:]).astype(_BF)
            y = _dot(d, wpool_ref[g]) * pscale_ref[:, c0:c0 + POOL_GROUP_WIDTH]
            ys.append(y.astype(_BF))
        return ys

    def wave_scores(j, blk, kt_prev, pairs):
        first = ((s == 0) & (j == 0)).astype(jnp.int32) if j == 0 else 0
        kt2 = jnp.concatenate([kt_prev, blk["kt"]], axis=1)
        scores = []
        for p in pairs:
            slab = blk["q"][:, p * 128:(p + 1) * 128]
            for half, h in ((0, p), (1, p + GROUP)):
                qm = jnp.where(lo if half == 0 else ~lo, slab, 0.0).astype(_BF)
                scores.append(_dot(qm, kt2) + bias_scr[first, h])
        return scores

    def wave_values(blk, v_prev, pairs, scores):
        v2 = jnp.concatenate([v_prev, blk["v"]], axis=0)
        slabs = []
        for i, p in enumerate(pairs):
            outs = []
            for half, h in ((0, p), (1, p + GROUP)):
                pr = _softmax_with_sink(scores[2 * i + half], sink_ref[h])
                outs.append(_dot(pr.astype(_BF), v2))
            slabs.append(jnp.where(lo, outs[0], outs[1]).astype(_BF))
        return slabs

    def output(j, blk, slabs):
        cat = jnp.concatenate(slabs, axis=1)
        x1_ref[0, j * BLOCK:(j + 1) * BLOCK, :] = blk["x"] + _dot(cat, wout_ref[...])

    kt_prev, v_prev, u_hist = kt_scr[...], v_scr[...], u_scr[...]
    blk = project(0)
    done = None
    for j in range(nb):
        sc0 = wave_scores(j, blk, kt_prev, (0, 1))
        nxt = project(j + 1) if j + 1 < nb else None
        sc1 = wave_scores(j, blk, kt_prev, (2, 3))
        pooled = pool(j, u_hist, blk["u"])
        at0 = wave_values(blk, v_prev, (0, 1), sc0)
        if done is not None:
            output(*done)
        at1 = wave_values(blk, v_prev, (2, 3), sc1)
        done = (j, blk, at0 + at1 + pooled)
        kt_prev, v_prev, u_hist = blk["kt"], blk["v"], blk["u"][BLOCK - POOL_PAD:]
        blk = nxt
    output(*done)
    kt_scr[...] = kt_prev
    v_scr[...] = v_prev
    u_scr[...] = u_hist


def _prompt_mixer(x, gmix, win, sinks, wpool, pscale, wout, tail_weights, tile):
    B, S, D = x.shape
    ns = S // tile
    nsteps = B * ns
    const = lambda *shape: pl.BlockSpec(shape, lambda b, s: (0,) * len(shape))
    chunk = lambda w: pl.BlockSpec((w.shape[0] // nsteps, w.shape[1]), lambda b, s: (b * ns + s, 0))
    assert all(w.shape[0] % (16 * nsteps) == 0 for w in tail_weights)
    return pl.pallas_call(
        functools.partial(_prompt_mixer_kernel, tile=tile),
        grid=(B, ns),
        in_specs=[
            pl.BlockSpec((1, tile, D), lambda b, s: (b, s, 0)),
            const(1, D),
            const(D, IN_WIDTH),
            pl.BlockSpec(memory_space=pltpu.SMEM),
            const(len(POOL_WINDOWS), POOL_GROUP_WIDTH, POOL_GROUP_WIDTH),
            const(1, POOL_WIDTH),
            const(D, D),
        ] + [chunk(w) for w in tail_weights],
        out_specs=[
            pl.BlockSpec((1, tile, D), lambda b, s: (b, s, 0)),
            pl.BlockSpec((1, BLOCK, KV_WIDTH), lambda b, s: (b, 0, 0)),
            pl.BlockSpec((1, BLOCK, KV_WIDTH), lambda b, s: (b, 0, 0)),
            pl.BlockSpec((1, POOL_PAD, POOL_WIDTH), lambda b, s: (b, 0, 0)),
        ] + [chunk(w) for w in tail_weights],
        out_shape=[
            jax.ShapeDtypeStruct((B, S, D), _F32),
            jax.ShapeDtypeStruct((B, BLOCK, KV_WIDTH), _F32),
            jax.ShapeDtypeStruct((B, BLOCK, KV_WIDTH), _F32),
            jax.ShapeDtypeStruct((B, POOL_PAD, POOL_WIDTH), _F32),
        ] + [jax.ShapeDtypeStruct(w.shape, _BF) for w in tail_weights],
        scratch_shapes=[
            pltpu.VMEM((KV_WIDTH, BLOCK), _BF),
            pltpu.VMEM((BLOCK, KV_WIDTH), _BF),
            pltpu.VMEM((POOL_PAD, POOL_WIDTH), _F32),
            pltpu.VMEM((2, N_HEADS, BLOCK, 2 * BLOCK), _F32),
        ],
        compiler_params=pltpu.CompilerParams(
            dimension_semantics=("arbitrary", "arbitrary"), vmem_limit_bytes=VMEM_LIMIT_BYTES),
        name="prompt_mixer",
    )(x, gmix, win, sinks, wpool, pscale, wout, *tail_weights)


def _mem_kv_kernel(mem_ref, gmem_ref, wck_ref, wcv_ref, k_ref, v_ref, kt_ref, vb_ref, wck_scr, wcv_scr):
    @pl.when(pl.program_id(0) == 0)
    def _():
        wck_scr[...] = wck_ref[...].astype(_BF)
        wcv_scr[...] = wcv_ref[...].astype(_BF)

    hm = _rms(mem_ref[0], gmem_ref[...]).astype(_BF)
    k = _dot(hm, wck_scr[...])
    v = _dot(hm, wcv_scr[...])
    vb_ref[0] = v.astype(_BF)
    for h in range(N_CROSS_HEADS):
        kt_ref[0, h] = k[:, h * CROSS_HEAD_DIM:(h + 1) * CROSS_HEAD_DIM].T.astype(_BF)
        for half in range(CROSS_HEAD_DIM // 128):
            c0 = h * CROSS_HEAD_DIM + half * 128
            rows = pl.ds(half * N_CROSS_HEADS + h, N_MEM, stride=MEM_ROWS)
            k_ref[0, rows, :] = k[:, c0:c0 + 128]
            v_ref[0, rows, :] = v[:, c0:c0 + 128]


def _mem_kv(mem, gmem, wck, wcv):
    B, M, D = mem.shape
    const = lambda *shape: pl.BlockSpec(shape, lambda b: (0,) * len(shape))
    return pl.pallas_call(
        _mem_kv_kernel,
        grid=(B,),
        in_specs=[pl.BlockSpec((1, M, D), lambda b: (b, 0, 0)), const(1, D), const(D, D), const(D, D)],
        out_specs=[
            pl.BlockSpec((1, M * MEM_ROWS, 128), lambda b: (b, 0, 0)),
            pl.BlockSpec((1, M * MEM_ROWS, 128), lambda b: (b, 0, 0)),
            pl.BlockSpec((1, N_CROSS_HEADS, CROSS_HEAD_DIM, M), lambda b: (b, 0, 0, 0)),
            pl.BlockSpec((1, M, D), lambda b: (b, 0, 0)),
        ],
        out_shape=[
            jax.ShapeDtypeStruct((B, M * MEM_ROWS, 128), _F32),
            jax.ShapeDtypeStruct((B, M * MEM_ROWS, 128), _F32),
            jax.ShapeDtypeStruct((B, N_CROSS_HEADS, CROSS_HEAD_DIM, M), _BF),
            jax.ShapeDtypeStruct((B, M, D), _BF),
        ],
        scratch_shapes=[pltpu.VMEM((D, D), _BF), pltpu.VMEM((D, D), _BF)],
        compiler_params=pltpu.CompilerParams(
            dimension_semantics=("arbitrary",), vmem_limit_bytes=VMEM_LIMIT_BYTES),
        name="prompt_mem_kv",
    )(mem, gmem, wck, wcv)


def _mem_cache_unrows(rows):
    nb = rows.shape[0]
    c = rows.reshape(nb, N_MEM, CROSS_HEAD_DIM // 128, N_CROSS_HEADS, 128)
    return c.transpose(0, 1, 3, 2, 4).reshape(nb, N_MEM, N_CROSS_HEADS, CROSS_HEAD_DIM)


def _prompt_tail_kernel(x1_ref, gcross_ref, wcq_ref, kt_ref, vb_ref, wco_ref, gffn_ref, wup_ref, wdown_ref,
                        gfinal_ref, y_ref, *, tile, sub):
    gcross, gffn, gfinal = gcross_ref[...], gffn_ref[...], gfinal_ref[...]

    def stream(r0):
        x1 = x1_ref[0, r0:r0 + sub, :]
        hn = _rms(x1, gcross).astype(_BF)
        yield
        q = (_dot(hn, wcq_ref[...]) * CQ_SCALE).astype(_BF)
        yield
        heads = [slice(h * CROSS_HEAD_DIM, (h + 1) * CROSS_HEAD_DIM) for h in range(N_CROSS_HEADS)]
        scores = [_dot(q[:, c], kt_ref[0, h]) for h, c in enumerate(heads)]
        yield
        outs = [_dot(_softmax(sc).astype(_BF), vb_ref[0, :, c]).astype(_BF) for sc, c in zip(scores, heads)]
        yield
        x2 = x1 + _dot(jnp.concatenate(outs, axis=1), wco_ref[...])
        yield

        def store(y):
            y_ref[0, r0:r0 + sub, :] = y

        yield from _ffn_final_stages(x2, gffn, wup_ref, wdown_ref, gfinal, store)

    _interleave([stream(r0) for r0 in range(0, tile, sub)], TAIL_SKEW)


def _single(shape, index_map):
    return pl.BlockSpec(shape, index_map, pipeline_mode=pl.Buffered(1))


def _prompt_tail(x1, gcross, wcq, kt, vb, wco, gffn, wup, wdown, gfinal, tile, sub):
    B, S, D = x1.shape
    const = lambda *shape: _single(shape, lambda b, s: (0,) * len(shape))
    return pl.pallas_call(
        functools.partial(_prompt_tail_kernel, tile=tile, sub=sub),
        grid=(B, S // tile),
        in_specs=[
            pl.BlockSpec((1, tile, D), lambda b, s: (b, s, 0)),
            const(1, D),
            const(D, D),
            pl.BlockSpec((1, N_CROSS_HEADS, CROSS_HEAD_DIM, N_MEM), lambda b, s: (b, 0, 0, 0)),
            pl.BlockSpec((1, N_MEM, D), lambda b, s: (b, 0, 0)),
            const(D, D),
            const(1, D),
            const(D, D_FF),
            const(D_FF, D),
            const(1, D),
        ],
        out_specs=pl.BlockSpec((1, tile, D), lambda b, s: (b, s, 0)),
        out_shape=jax.ShapeDtypeStruct((B, S, D), _F32),
        compiler_params=pltpu.CompilerParams(
            dimension_semantics=("arbitrary", "arbitrary"), vmem_limit_bytes=VMEM_LIMIT_BYTES),
        name="prompt_tail",
    )(x1, gcross, wcq, kt, vb, wco, gffn, wup, wdown, gfinal)


def _sample_mixer_kernel(x_ref, gmix_ref, win_ref, sinkcol_ref, ck_ref, cv_ref, st_ref, wpool_ref, pscale_ref,
                         wout_ref, gcross_ref, wcq_ref,
                         x1_ref, qc_ref, wk_ref, wv_ref, pool_ref,
                         kc_scr, vc_scr, *, nbatch, nseq, past_len):
    rows = nbatch * nseq
    nkeys = 2 * WINDOW
    x = x_ref[...]
    xn = _rms(x, gmix_ref[...]).astype(_BF)
    proj = _dot(xn, win_ref[...])

    k_new = proj[:, K_OFF:K_OFF + KV_WIDTH].reshape(nbatch, nseq, KV_WIDTH)
    v_new = proj[:, V_OFF:V_OFF + KV_WIDTH].reshape(nbatch, nseq, KV_WIDTH)
    ck = ck_ref[...]
    cv = cv_ref[...]
    wk_ref[:, :WINDOW - nseq, :] = ck[:, nseq:, :]
    wk_ref[:, WINDOW - nseq:, :] = k_new
    wv_ref[:, :WINDOW - nseq, :] = cv[:, nseq:, :]
    wv_ref[:, WINDOW - nseq:, :] = v_new

    zpad = jnp.zeros((nbatch, nkeys - WINDOW, KV_WIDTH), _BF)
    kc_scr[:, WINDOW:, :] = zpad
    vc_scr[:, WINDOW:, :] = zpad
    kc_scr[:, :WINDOW, :] = ck.astype(_BF)
    vc_scr[:, :WINDOW, :] = cv.astype(_BF)
    kc_scr[:, WINDOW:WINDOW + nseq, :] = k_new.astype(_BF)
    vc_scr[:, WINDOW:WINDOW + nseq, :] = v_new.astype(_BF)

    t = lax.broadcasted_iota(jnp.int32, (nseq, nkeys), 0)
    c = lax.broadcasted_iota(jnp.int32, (nseq, nkeys), 1)
    dist = jnp.where(c < WINDOW, WINDOW + t - c, t - (c - WINDOW))
    valid = (dist >= 0) & (dist <= WINDOW) & (c < WINDOW + nseq)
    distf = dist.astype(_F32)
    bias = jnp.concatenate([jnp.where(valid, -_slope(h) * distf, NEG_INF) for h in range(N_HEADS)], axis=0)

    lane = lax.broadcasted_iota(jnp.int32, (rows, 2 * HEAD_DIM), 1)
    lo = lane < HEAD_DIM
    qsc = proj[:, :ATTN_WIDTH] * Q_SCALE
    q_lo, q_hi = [], []
    for p in range(GROUP):
        slab = qsc[:, p * 128:(p + 1) * 128]
        q_lo.append(jnp.where(lo, slab, 0.0).reshape(nbatch, nseq, 128))
        q_hi.append(jnp.where(lo, 0.0, slab).reshape(nbatch, nseq, 128))
    qm = jnp.concatenate(q_lo + q_hi, axis=1).astype(_BF)
    sc = jnp.einsum('bqd,bkd->bqk', qm, kc_scr[...], preferred_element_type=_F32) + bias[None]
    pr = _softmax_with_sink(sc, sinkcol_ref[...][None]).astype(_BF)
    o = jnp.einsum('bqk,bkd->bqd', pr, vc_scr[...], preferred_element_type=_F32)
    attn = []
    for p in range(GROUP):
        o_lo = o[:, p * nseq:(p + 1) * nseq, :].reshape(rows, 128)
        o_hi = o[:, (p + GROUP) * nseq:(p + GROUP + 1) * nseq, :].reshape(rows, 128)
        attn.append(jnp.where(lo, o_lo, o_hi).astype(_BF))

    u_new = proj[:, U_OFF:].reshape(nbatch, nseq, POOL_WIDTH)
    u_ext = jnp.concatenate([st_ref[...], u_new], axis=1)
    pool_ref[...] = u_ext[:, nseq:, :]
    tpos = past_len + lax.broadcasted_iota(jnp.int32, (nseq, POOL_GROUP_WIDTH), 0)
    pooled = []
    for g, w in enumerate(POOL_WINDOWS):
        c0 = g * POOL_GROUP_WIDTH
        ug = u_ext[:, :, c0:c0 + POOL_GROUP_WIDTH]
        sw = _window_sums(ug, 1)[w][:, POOL_PAD:, :]
        cnt = jnp.minimum(tpos + 1, w).astype(_F32)
        d = (sw / cnt[None] - ug[:, POOL_PAD:, :]).reshape(rows, POOL_GROUP_WIDTH).astype(_BF)
        y = _dot(d, wpool_ref[g]) * pscale_ref[:, c0:c0 + POOL_GROUP_WIDTH]
        pooled.append(y.astype(_BF))

    cat = jnp.concatenate(attn + pooled, axis=1)
    x1 = x + _dot(cat, wout_ref[...])
    x1_ref[...] = x1
    qc_ref[...] = _dot(_rms(x1, gcross_ref[...]).astype(_BF), wcq_ref[...]) * CQ_SCALE


def _sample_mixer(x, gmix, win, sinkcol, ck, cv, st, wpool, pscale, wout, gcross, wcq, nbatch, nseq, past_len):
    R, D = x.shape
    rows = nbatch * nseq
    const = lambda *shape: pl.BlockSpec(shape, lambda i: (0,) * len(shape))
    return pl.pallas_call(
        functools.partial(_sample_mixer_kernel, nbatch=nbatch, nseq=nseq, past_len=past_len),
        grid=(R // rows,),
        in_specs=[
            pl.BlockSpec((rows, D), lambda i: (i, 0)),
            const(1, D),
            const(D, IN_WIDTH),
            const(N_HEADS * nseq, 1),
            pl.BlockSpec((nbatch, WINDOW, KV_WIDTH), lambda i: (i, 0, 0)),
            pl.BlockSpec((nbatch, WINDOW, KV_WIDTH), lambda i: (i, 0, 0)),
            pl.BlockSpec((nbatch, POOL_PAD, POOL_WIDTH), lambda i: (i, 0, 0)),
            const(len(POOL_WINDOWS), POOL_GROUP_WIDTH, POOL_GROUP_WIDTH),
            const(1, POOL_WIDTH),
            const(D, D),
            const(1, D),
            const(D, D),
        ],
        out_specs=[
            pl.BlockSpec((rows, D), lambda i: (i, 0)),
            pl.BlockSpec((rows, D), lambda i: (i, 0)),
            pl.BlockSpec((nbatch, WINDOW, KV_WIDTH), lambda i: (i, 0, 0)),
            pl.BlockSpec((nbatch, WINDOW, KV_WIDTH), lambda i: (i, 0, 0)),
            pl.BlockSpec((nbatch, POOL_PAD, POOL_WIDTH), lambda i: (i, 0, 0)),
        ],
        out_shape=[
            jax.ShapeDtypeStruct((R, D), _F32),
            jax.ShapeDtypeStruct((R, D), _F32),
            jax.ShapeDtypeStruct((R // nseq, WINDOW, KV_WIDTH), _F32),
            jax.ShapeDtypeStruct((R // nseq, WINDOW, KV_WIDTH), _F32),
            jax.ShapeDtypeStruct((R // nseq, POOL_PAD, POOL_WIDTH), _F32),
        ],
        scratch_shapes=[
            pltpu.VMEM((nbatch, 2 * WINDOW, KV_WIDTH), _BF),
            pltpu.VMEM((nbatch, 2 * WINDOW, KV_WIDTH), _BF),
        ],
        compiler_params=pltpu.CompilerParams(
            dimension_semantics=("arbitrary",), vmem_limit_bytes=VMEM_LIMIT_BYTES),
        name="sample_mixer",
    )(x, gmix, win, sinkcol, ck, cv, st, wpool, pscale, wout, gcross, wcq)


def _mem_cache_rows(cache):
    nb = cache.shape[0]
    c = cache.reshape(nb, N_MEM, N_CROSS_HEADS, CROSS_HEAD_DIM // 128, 128)
    return c.transpose(0, 1, 3, 2, 4).reshape(nb, N_MEM * MEM_ROWS, 128)


def _load_mem_head(ref, b, h):
    halves = [ref[b, pl.ds(half * N_CROSS_HEADS + h, N_MEM, stride=MEM_ROWS), :]
              for half in range(CROSS_HEAD_DIM // 128)]
    return jnp.concatenate(halves, axis=1)


def _sample_memattn_kernel(qc_ref, mk_ref, mv_ref, o_ref, *, nbatch, nseq):
    rows = nbatch * nseq
    q = qc_ref[...].reshape(nbatch, nseq, D_MODEL)
    outs = []
    for h in range(N_CROSS_HEADS):
        c0 = h * CROSS_HEAD_DIM
        qh = q[:, :, c0:c0 + CROSS_HEAD_DIM].astype(_BF)
        kh = jnp.stack([_load_mem_head(mk_ref, b, h).astype(_BF) for b in range(nbatch)])
        vh = jnp.stack([_load_mem_head(mv_ref, b, h).astype(_BF) for b in range(nbatch)])
        sc = jnp.einsum('bqd,bkd->bqk', qh, kh, preferred_element_type=_F32)
        pr = _softmax(sc).astype(_BF)
        outs.append(jnp.einsum('bqk,bkd->bqd', pr, vh, preferred_element_type=_F32))
    o_ref[...] = jnp.concatenate(outs, axis=2).reshape(rows, D_MODEL)


def _sample_memattn(qc, mk, mv, nbatch, nseq):
    R, D = qc.shape
    rows = nbatch * nseq
    return pl.pallas_call(
        functools.partial(_sample_memattn_kernel, nbatch=nbatch, nseq=nseq),
        grid=(R // rows,),
        in_specs=[
            pl.BlockSpec((rows, D), lambda i: (i, 0)),
            pl.BlockSpec((nbatch, N_MEM * MEM_ROWS, 128), lambda i: (i, 0, 0)),
            pl.BlockSpec((nbatch, N_MEM * MEM_ROWS, 128), lambda i: (i, 0, 0)),
        ],
        out_specs=pl.BlockSpec((rows, D), lambda i: (i, 0)),
        out_shape=jax.ShapeDtypeStruct((R, D), _F32),
        compiler_params=pltpu.CompilerParams(
            dimension_semantics=("arbitrary",), vmem_limit_bytes=VMEM_LIMIT_BYTES),
        name="sample_memattn",
    )(qc, mk, mv)


def _sample_tail_kernel(x1_ref, o_ref, wco_ref, gffn_ref, wup_ref, wdown_ref, gfinal_ref, y_ref, *, tile, sub):
    gffn, gfinal = gffn_ref[...], gfinal_ref[...]

    def stream(r0):
        x2 = x1_ref[r0:r0 + sub, :] + _dot(o_ref[r0:r0 + sub, :].astype(_BF), wco_ref[...])
        yield

        def store(y):
            y_ref[r0:r0 + sub, :] = y

        yield from _ffn_final_stages(x2, gffn, wup_ref, wdown_ref, gfinal, store)

    _interleave([stream(r0) for r0 in range(0, tile, sub)], TAIL_SKEW)


def _sample_tail(x1, o, wco, gffn, wup, wdown, gfinal, tile, sub):
    R, D = x1.shape
    const = lambda *shape: _single(shape, lambda i: (0,) * len(shape))
    return pl.pallas_call(
        functools.partial(_sample_tail_kernel, tile=tile, sub=sub),
        grid=(R // tile,),
        in_specs=[
            pl.BlockSpec((tile, D), lambda i: (i, 0)),
            pl.BlockSpec((tile, D), lambda i: (i, 0)),
            const(D, D),
            const(1, D),
            const(D, D_FF),
            const(D_FF, D),
            const(1, D),
        ],
        out_specs=pl.BlockSpec((tile, D), lambda i: (i, 0)),
        out_shape=jax.ShapeDtypeStruct((R, D), _F32),
        compiler_params=pltpu.CompilerParams(
            dimension_semantics=("arbitrary",), vmem_limit_bytes=VMEM_LIMIT_BYTES),
        name="sample_tail",
    )(x1, o, wco, gffn, wup, wdown, gfinal)


PROMPT_TILE = 512
TAIL_TILE = 1024
TAIL_SUB = 256
TAIL_SKEW = 5
SAMPLE_MIXER_BATCH = 16
SAMPLE_MEMATTN_BATCH = 4


def kernel(x_prompt, x_sample, cache_win_k, cache_win_v, state_pool, cache_mem_k, cache_mem_v, mem_prompt,
           g_mix, w_in, attn_sinks, w_pool, pool_scale, w_out, g_cross, g_mem, w_cq, w_ck, w_cv, w_co,
           g_ffn, w_up, w_down, g_final):
    depth = g_mix.shape[0]
    assert depth == 1, "one layer per step"
    B, S, D = x_prompt.shape
    DB, T, _ = x_sample.shape
    past_len = PAST_LEN
    l = 0

    win = jnp.concatenate([_pair_heads(w_in[l][:, :ATTN_WIDTH], 1), w_in[l][:, ATTN_WIDTH:]], axis=1).astype(_BF)
    wout = jnp.concatenate([_pair_heads(w_out[l][:ATTN_WIDTH, :], 0), w_out[l][ATTN_WIDTH:, :]], axis=0).astype(_BF)
    wpool = w_pool[l].astype(_BF)
    gmix, gcross, gmem, gffn = (g[l].reshape(1, D) for g in (g_mix, g_cross, g_mem, g_ffn))
    gfinal = g_final.reshape(1, D)
    pscale = pool_scale[l].reshape(1, POOL_WIDTH)
    sinks = attn_sinks[l]

    x1p, klast, vlast, plast, wcq, wco, wup, wdown = _prompt_mixer(
        x_prompt, gmix, win, sinks, wpool, pscale, wout, (w_cq[l], w_co[l], w_up[l], w_down[l]), PROMPT_TILE)
    mem_k, mem_v, mem_kt, mem_vb = _mem_kv(mem_prompt, gmem, w_ck[l], w_cv[l])
    y_prompt = _prompt_tail(x1p, gcross, wcq, mem_kt, mem_vb, wco, gffn, wup, wdown, gfinal, TAIL_TILE, TAIL_SUB)

    xs = x_sample.reshape(DB * T, D)
    ck = cache_win_k[l].reshape(DB, WINDOW, KV_WIDTH)
    cv = cache_win_v[l].reshape(DB, WINDOW, KV_WIDTH)
    st = jnp.pad(state_pool[l], ((0, 0), (POOL_PAD - POOL_HIST, 0), (0, 0)))
    sinkcol = jnp.repeat(sinks, T).reshape(N_HEADS * T, 1)
    x1s, qc, wk_s, wv_s, pool_s = _sample_mixer(xs, gmix, win, sinkcol, ck, cv, st, wpool, pscale, wout,
                                                gcross, wcq, SAMPLE_MIXER_BATCH, T, past_len)
    mk = _mem_cache_rows(cache_mem_k[l])
    mv = _mem_cache_rows(cache_mem_v[l])
    o_s = _sample_memattn(qc, mk, mv, SAMPLE_MEMATTN_BATCH, T)
    y_sample = _sample_tail(x1s, o_s, wco, gffn, wup, wdown, gfinal, TAIL_TILE, TAIL_SUB).reshape(DB, T, D)

    return (
        y_prompt,
        y_sample,
        klast.reshape(1, B, WINDOW, N_KV_HEADS, HEAD_DIM),
        vlast.reshape(1, B, WINDOW, N_KV_HEADS, HEAD_DIM),
        plast[:, POOL_PAD - POOL_HIST:, :][None],
        _mem_cache_unrows(mem_k)[None],
        _mem_cache_unrows(mem_v)[None],
        wk_s.reshape(1, DB, WINDOW, N_KV_HEADS, HEAD_DIM),
        wv_s.reshape(1, DB, WINDOW, N_KV_HEADS, HEAD_DIM),
        pool_s[:, POOL_PAD - POOL_HIST:, :][None],
    )
```

```python
import functools

import jax
import jax.numpy as jnp
from jax import lax
from jax.experimental import pallas as pl
from jax.experimental.pallas import tpu as pltpu

D_MODEL = 1024
PAST_LEN = 16384
HEAD_DIM = 64
N_HEADS = 8
N_KV_HEADS = 2
GROUP = N_HEADS // N_KV_HEADS
ATTN_WIDTH = N_HEADS * HEAD_DIM
KV_WIDTH = N_KV_HEADS * HEAD_DIM
WINDOW = 128
BLOCK = WINDOW
POOL_WIDTH = D_MODEL - ATTN_WIDTH
POOL_WINDOWS = (2, 4, 8, 16)
POOL_GROUP_WIDTH = 128
POOL_HIST = 15
POOL_PAD = 16
IN_WIDTH = ATTN_WIDTH + 2 * KV_WIDTH + POOL_WIDTH
N_MEM = 256
N_CROSS_HEADS = 4
CROSS_HEAD_DIM = 256
MEM_ROWS = N_CROSS_HEADS * (CROSS_HEAD_DIM // 128)
D_FF = 4 * D_MODEL
FF_CHUNK = 1024
RMS_EPS = 1e-5
NEG_INF = -1e30
Q_SCALE = HEAD_DIM ** -0.5
CQ_SCALE = CROSS_HEAD_DIM ** -0.5
K_OFF = ATTN_WIDTH
V_OFF = ATTN_WIDTH + KV_WIDTH
U_OFF = ATTN_WIDTH + 2 * KV_WIDTH


def _pair_heads(w, axis):
    shape = w.shape
    split = shape[:axis] + (N_KV_HEADS, GROUP, HEAD_DIM) + shape[axis + 1:]
    return jnp.swapaxes(w.reshape(split), axis, axis + 1).reshape(shape)


VMEM_LIMIT_BYTES = 56 * 1024 * 1024

_BF = jnp.bfloat16
_F32 = jnp.float32


def _slope(h):
    return 2.0 ** (-8.0 * (h + 1) / N_HEADS)


def _dot(a, b):
    return jnp.dot(a, b, preferred_element_type=_F32)


def _rms(x, g):
    ms = jnp.mean(x * x, axis=-1, keepdims=True)
    return x * lax.rsqrt(ms + RMS_EPS) * g


def _softmax_with_sink(s, sink):
    m = jnp.maximum(jnp.max(s, axis=-1, keepdims=True), sink)
    e = jnp.exp(s - m)
    den = jnp.sum(e, axis=-1, keepdims=True) + jnp.exp(sink - m)
    return e * (1.0 / den)


def _softmax(s):
    m = jnp.max(s, axis=-1, keepdims=True)
    e = jnp.exp(s - m)
    return e * (1.0 / jnp.sum(e, axis=-1, keepdims=True))


def _window_sums(u_ext, axis):
    out = {}
    s = u_ext
    w = 1
    while w < max(POOL_WINDOWS):
        s = s + pltpu.roll(s, w, axis)
        w *= 2
        out[w] = s
    return out


def _interleave(streams, skew):
    pending = list(streams)
    live = []
    rnd = 0
    while live or pending:
        if pending and rnd % skew == 0:
            live.append(pending.pop(0))
        for g in list(live):
            try:
                next(g)
            except StopIteration:
                live.remove(g)
        rnd += 1


def _ffn_final_stages(x2, gffn, wup_ref, wdown_ref, gfinal, store):
    hn = _rms(x2, gffn).astype(_BF)
    yield
    acc = x2
    for c in range(D_FF // FF_CHUNK):
        hc = _dot(hn, wup_ref[:, c * FF_CHUNK:(c + 1) * FF_CHUNK])
        yield
        hc = jnp.maximum(hc, 0.0)
        hc = (hc * hc).astype(_BF)
        acc = acc + _dot(hc, wdown_ref[c * FF_CHUNK:(c + 1) * FF_CHUNK, :])
        yield
    store(_rms(acc, gfinal))


def _prompt_mixer_kernel(x_ref, gmix_ref, win_ref, sink_ref, wpool_ref, pscale_ref, wout_ref,
                         wcq_ref, wco_ref, wup_ref, wdown_ref, qc_ref, mk_ref, mv_ref,
                         x1_ref, klast_ref, vlast_ref, plast_ref, wcq_bf_ref, wco_bf_ref, wup_bf_ref, wdown_bf_ref,
                         os_ref, kt_scr, v_scr, u_scr, bias_scr, *, tile, nseq):
    b = pl.program_id(0)
    s = pl.program_id(1)
    nb = tile // BLOCK

    for src, dst in ((wcq_ref, wcq_bf_ref), (wco_ref, wco_bf_ref), (wup_ref, wup_bf_ref), (wdown_ref, wdown_bf_ref)):
        dst[...] = src[...].astype(_BF)

    @pl.when((b == 0) & (s == 0))
    def _():
        qi = lax.broadcasted_iota(jnp.int32, (BLOCK, 2 * BLOCK), 0)
        kc = lax.broadcasted_iota(jnp.int32, (BLOCK, 2 * BLOCK), 1)
        dist = qi + BLOCK - kc
        valid = (dist >= 0) & (dist <= WINDOW)
        valid_first = valid & (kc >= BLOCK)
        distf = dist.astype(_F32)
        for h in range(N_HEADS):
            ali = -_slope(h) * distf
            bias_scr[0, h] = jnp.where(valid, ali, NEG_INF)
            bias_scr[1, h] = jnp.where(valid_first, ali, NEG_INF)

    @pl.when(s == 0)
    def _():
        kt_scr[...] = jnp.zeros((KV_WIDTH, BLOCK), _BF)
        v_scr[...] = jnp.zeros((BLOCK, KV_WIDTH), _BF)
        u_scr[...] = jnp.zeros((POOL_PAD, POOL_WIDTH), _F32)

    lane = lax.broadcasted_iota(jnp.int32, (BLOCK, 2 * HEAD_DIM), 1)
    lo = lane < HEAD_DIM
    row = lax.broadcasted_iota(jnp.int32, (BLOCK, POOL_GROUP_WIDTH), 0)
    gmix = gmix_ref[...]

    def project(j):
        x = x_ref[0, j * BLOCK:(j + 1) * BLOCK, :]
        proj = _dot(_rms(x, gmix).astype(_BF), win_ref[...])
        k = proj[:, K_OFF:K_OFF + KV_WIDTH]
        v = proj[:, V_OFF:V_OFF + KV_WIDTH]
        u = proj[:, U_OFF:]
        if j == nb - 1:
            klast_ref[0] = k
            vlast_ref[0] = v
            plast_ref[0] = u[BLOCK - POOL_PAD:]
        return dict(x=x, q=proj[:, :ATTN_WIDTH] * Q_SCALE, kt=k.T.astype(_BF), v=v.astype(_BF), u=u)

    def pool(j, u_hist, u):
        pos = s * tile + j * BLOCK + row
        ys = []
        for g, w in enumerate(POOL_WINDOWS):
            c0 = g * POOL_GROUP_WIDTH
            u_ext = jnp.concatenate([u_hist[:, c0:c0 + POOL_GROUP_WIDTH], u[:, c0:c0 + POOL_GROUP_WIDTH]], axis=0)
            sw = _window_sums(u_ext, 0)[w][POOL_PAD:]
            cnt = jnp.minimum(pos + 1, w).astype(_F32)
            d = (sw / cnt - u_ext[POOL_PAD:]).astype(_BF)
            y = _dot(d, wpool_ref[g]) * pscale_ref[:, c0:c0 + POOL_GROUP_WIDTH]
            ys.append(y.astype(_BF))
        return ys

    def wave_scores(j, blk, kt_prev, pairs):
        first = ((s == 0) & (j == 0)).astype(jnp.int32) if j == 0 else 0
        kt2 = jnp.concatenate([kt_prev, blk["kt"]], axis=1)
        scores = []
        for p in pairs:
            slab = blk["q"][:, p * 128:(p + 1) * 128]
            for half, h in ((0, p), (1, p + GROUP)):
                qm = jnp.where(lo if half == 0 else ~lo, slab, 0.0).astype(_BF)
                scores.append(_dot(qm, kt2) + bias_scr[first, h])
        return scores

    def wave_values(blk, v_prev, pairs, scores):
        v2 = jnp.concatenate([v_prev, blk["v"]], axis=0)
        slabs = []
        for i, p in enumerate(pairs):
            outs = []
            for half, h in ((0, p), (1, p + GROUP)):
                pr = _softmax_with_sink(scores[2 * i + half], sink_ref[h])
                outs.append(_dot(pr.astype(_BF), v2))
            slabs.append(jnp.where(lo, outs[0], outs[1]).astype(_BF))
        return slabs

    def output(j, blk, slabs):
        cat = jnp.concatenate(slabs, axis=1)
        x1_ref[0, j * BLOCK:(j + 1) * BLOCK, :] = blk["x"] + _dot(cat, wout_ref[...])

    heads = [slice(h * CROSS_HEAD_DIM, (h + 1) * CROSS_HEAD_DIM) for h in range(N_CROSS_HEADS)]

    def mem_scores(i):
        q = qc_ref[i * nseq:(i + 1) * nseq, :].astype(_BF)
        return [lax.dot_general(q[:, c], _load_mem_head(mk_ref, i, h).astype(_BF), (((1,), (1,)), ((), ())),
                                preferred_element_type=_F32) for h, c in enumerate(heads)]

    def mem_values(i, scores):
        outs = [_dot(_softmax(sc).astype(_BF), _load_mem_head(mv_ref, i, h).astype(_BF))
                for h, sc in enumerate(scores)]
        os_ref[i * nseq:(i + 1) * nseq, :] = jnp.concatenate(outs, axis=1)

    kt_prev, v_prev, u_hist = kt_scr[...], v_scr[...], u_scr[...]
    blk = project(0)
    done = None
    for j in range(nb):
        sc0 = wave_scores(j, blk, kt_prev, (0, 1))
        nxt = project(j + 1) if j + 1 < nb else None
        msc = mem_scores(j)
        sc1 = wave_scores(j, blk, kt_prev, (2, 3))
        pooled = pool(j, u_hist, blk["u"])
        at0 = wave_values(blk, v_prev, (0, 1), sc0)
        if done is not None:
            output(*done)
        mem_values(j, msc)
        at1 = wave_values(blk, v_prev, (2, 3), sc1)
        done = (j, blk, at0 + at1 + pooled)
        kt_prev, v_prev, u_hist = blk["kt"], blk["v"], blk["u"][BLOCK - POOL_PAD:]
        blk = nxt
    output(*done)
    kt_scr[...] = kt_prev
    v_scr[...] = v_prev
    u_scr[...] = u_hist


def _prompt_mixer(x, gmix, win, sinks, wpool, pscale, wout, tail_weights, qc, mk, mv, tile):
    B, S, D = x.shape
    ns = S // tile
    nsteps = B * ns
    nmem = tile // BLOCK
    nseq = qc.shape[0] // mk.shape[0]
    assert mk.shape[0] == nsteps * nmem
    const = lambda *shape: pl.BlockSpec(shape, lambda b, s: (0,) * len(shape))
    chunk = lambda w: pl.BlockSpec((w.shape[0] // nsteps, w.shape[1]), lambda b, s: (b * ns + s, 0))
    assert all(w.shape[0] % (16 * nsteps) == 0 for w in tail_weights)
    mem_rows = pl.BlockSpec((nmem * nseq, D), lambda b, s: (b * ns + s, 0))
    mem_cache = pl.BlockSpec((nmem, N_MEM * MEM_ROWS, 128), lambda b, s: (b * ns + s, 0, 0))
    return pl.pallas_call(
        functools.partial(_prompt_mixer_kernel, tile=tile, nseq=nseq),
        grid=(B, ns),
        in_specs=[
            pl.BlockSpec((1, tile, D), lambda b, s: (b, s, 0)),
            const(1, D),
            const(D, IN_WIDTH),
            pl.BlockSpec(memory_space=pltpu.SMEM),
            const(len(POOL_WINDOWS), POOL_GROUP_WIDTH, POOL_GROUP_WIDTH),
            const(1, POOL_WIDTH),
            const(D, D),
        ] + [chunk(w) for w in tail_weights] + [mem_rows, mem_cache, mem_cache],
        out_specs=[
            pl.BlockSpec((1, tile, D), lambda b, s: (b, s, 0)),
            pl.BlockSpec((1, BLOCK, KV_WIDTH), lambda b, s: (b, 0, 0)),
            pl.BlockSpec((1, BLOCK, KV_WIDTH), lambda b, s: (b, 0, 0)),
            pl.BlockSpec((1, POOL_PAD, POOL_WIDTH), lambda b, s: (b, 0, 0)),
        ] + [chunk(w) for w in tail_weights] + [mem_rows],
        out_shape=[
            jax.ShapeDtypeStruct((B, S, D), _F32),
            jax.ShapeDtypeStruct((B, BLOCK, KV_WIDTH), _F32),
            jax.ShapeDtypeStruct((B, BLOCK, KV_WIDTH), _F32),
            jax.ShapeDtypeStruct((B, POOL_PAD, POOL_WIDTH), _F32),
        ] + [jax.ShapeDtypeStruct(w.shape, _BF) for w in tail_weights] + [jax.ShapeDtypeStruct(qc.shape, _F32)],
        scratch_shapes=[
            pltpu.VMEM((KV_WIDTH, BLOCK), _BF),
            pltpu.VMEM((BLOCK, KV_WIDTH), _BF),
            pltpu.VMEM((POOL_PAD, POOL_WIDTH), _F32),
            pltpu.VMEM((2, N_HEADS, BLOCK, 2 * BLOCK), _F32),
        ],
        compiler_params=pltpu.CompilerParams(
            dimension_semantics=("arbitrary", "arbitrary"), vmem_limit_bytes=VMEM_LIMIT_BYTES),
        name="prompt_mixer",
    )(x, gmix, win, sinks, wpool, pscale, wout, *tail_weights, qc, mk, mv)


def _mem_kv_kernel(mem_ref, gmem_ref, wck_ref, wcv_ref, k_ref, v_ref, kt_ref, vb_ref, wck_scr, wcv_scr):
    @pl.when(pl.program_id(0) == 0)
    def _():
        wck_scr[...] = wck_ref[...].astype(_BF)
        wcv_scr[...] = wcv_ref[...].astype(_BF)

    hm = _rms(mem_ref[0], gmem_ref[...]).astype(_BF)
    k = _dot(hm, wck_scr[...])
    v = _dot(hm, wcv_scr[...])
    vb_ref[0] = v.astype(_BF)
    for h in range(N_CROSS_HEADS):
        kt_ref[0, h] = k[:, h * CROSS_HEAD_DIM:(h + 1) * CROSS_HEAD_DIM].T.astype(_BF)
        for half in range(CROSS_HEAD_DIM // 128):
            c0 = h * CROSS_HEAD_DIM + half * 128
            rows = pl.ds(half * N_CROSS_HEADS + h, N_MEM, stride=MEM_ROWS)
            k_ref[0, rows, :] = k[:, c0:c0 + 128]
            v_ref[0, rows, :] = v[:, c0:c0 + 128]


def _mem_kv(mem, gmem, wck, wcv):
    B, M, D = mem.shape
    const = lambda *shape: pl.BlockSpec(shape, lambda b: (0,) * len(shape))
    return pl.pallas_call(
        _mem_kv_kernel,
        grid=(B,),
        in_specs=[pl.BlockSpec((1, M, D), lambda b: (b, 0, 0)), const(1, D), const(D, D), const(D, D)],
        out_specs=[
            pl.BlockSpec((1, M * MEM_ROWS, 128), lambda b: (b, 0, 0)),
            pl.BlockSpec((1, M * MEM_ROWS, 128), lambda b: (b, 0, 0)),
            pl.BlockSpec((1, N_CROSS_HEADS, CROSS_HEAD_DIM, M), lambda b: (b, 0, 0, 0)),
            pl.BlockSpec((1, M, D), lambda b: (b, 0, 0)),
        ],
        out_shape=[
            jax.ShapeDtypeStruct((B, M * MEM_ROWS, 128), _F32),
            jax.ShapeDtypeStruct((B, M * MEM_ROWS, 128), _F32),
            jax.ShapeDtypeStruct((B, N_CROSS_HEADS, CROSS_HEAD_DIM, M), _BF),
            jax.ShapeDtypeStruct((B, M, D), _BF),
        ],
        scratch_shapes=[pltpu.VMEM((D, D), _BF), pltpu.VMEM((D, D), _BF)],
        compiler_params=pltpu.CompilerParams(
            dimension_semantics=("arbitrary",), vmem_limit_bytes=VMEM_LIMIT_BYTES),
        name="prompt_mem_kv",
    )(mem, gmem, wck, wcv)


def _mem_cache_unrows(rows):
    nb = rows.shape[0]
    c = rows.reshape(nb, N_MEM, CROSS_HEAD_DIM // 128, N_CROSS_HEADS, 128)
    return c.transpose(0, 1, 3, 2, 4).reshape(nb, N_MEM, N_CROSS_HEADS, CROSS_HEAD_DIM)


def _prompt_tail_kernel(x1_ref, gcross_ref, wcq_ref, kt_ref, vb_ref, wco_ref, gffn_ref, wup_ref, wdown_ref,
                        gfinal_ref, y_ref, *, tile, sub):
    gcross, gffn, gfinal = gcross_ref[...], gffn_ref[...], gfinal_ref[...]

    def stream(r0):
        x1 = x1_ref[0, r0:r0 + sub, :]
        hn = _rms(x1, gcross).astype(_BF)
        yield
        q = (_dot(hn, wcq_ref[...]) * CQ_SCALE).astype(_BF)
        yield
        heads = [slice(h * CROSS_HEAD_DIM, (h + 1) * CROSS_HEAD_DIM) for h in range(N_CROSS_HEADS)]
        scores = [_dot(q[:, c], kt_ref[0, h]) for h, c in enumerate(heads)]
        yield
        outs = [_dot(_softmax(sc).astype(_BF), vb_ref[0, :, c]).astype(_BF) for sc, c in zip(scores, heads)]
        yield
        x2 = x1 + _dot(jnp.concatenate(outs, axis=1), wco_ref[...])
        yield

        def store(y):
            y_ref[0, r0:r0 + sub, :] = y

        yield from _ffn_final_stages(x2, gffn, wup_ref, wdown_ref, gfinal, store)

    _interleave([stream(r0) for r0 in range(0, tile, sub)], TAIL_SKEW)


def _single(shape, index_map):
    return pl.BlockSpec(shape, index_map, pipeline_mode=pl.Buffered(1))


def _prompt_tail(x1, gcross, wcq, kt, vb, wco, gffn, wup, wdown, gfinal, tile, sub):
    B, S, D = x1.shape
    const = lambda *shape: _single(shape, lambda b, s: (0,) * len(shape))
    return pl.pallas_call(
        functools.partial(_prompt_tail_kernel, tile=tile, sub=sub),
        grid=(B, S // tile),
        in_specs=[
            pl.BlockSpec((1, tile, D), lambda b, s: (b, s, 0)),
            const(1, D),
            const(D, D),
            pl.BlockSpec((1, N_CROSS_HEADS, CROSS_HEAD_DIM, N_MEM), lambda b, s: (b, 0, 0, 0)),
            pl.BlockSpec((1, N_MEM, D), lambda b, s: (b, 0, 0)),
            const(D, D),
            const(1, D),
            const(D, D_FF),
            const(D_FF, D),
            const(1, D),
        ],
        out_specs=pl.BlockSpec((1, tile, D), lambda b, s: (b, s, 0)),
        out_shape=jax.ShapeDtypeStruct((B, S, D), _F32),
        compiler_params=pltpu.CompilerParams(
            dimension_semantics=("arbitrary", "arbitrary"), vmem_limit_bytes=VMEM_LIMIT_BYTES),
        name="prompt_tail",
    )(x1, gcross, wcq, kt, vb, wco, gffn, wup, wdown, gfinal)


def _sample_mixer_kernel(x_ref, gmix_ref, win_ref, sinkcol_ref, ck_ref, cv_ref, st_ref, wpool_ref, pscale_ref,
                         wout_ref, gcross_ref, wcq_ref,
                         x1_ref, qc_ref, wk_ref, wv_ref, pool_ref,
                         kc_scr, vc_scr, wcq_scr, *, nbatch, nseq, past_len):
    rows = nbatch * nseq

    @pl.when(pl.program_id(0) == 0)
    def _():
        wcq_scr[...] = wcq_ref[...].astype(_BF)

    nkeys = 2 * WINDOW
    x = x_ref[...]
    xn = _rms(x, gmix_ref[...]).astype(_BF)
    proj = _dot(xn, win_ref[...])

    k_new = proj[:, K_OFF:K_OFF + KV_WIDTH].reshape(nbatch, nseq, KV_WIDTH)
    v_new = proj[:, V_OFF:V_OFF + KV_WIDTH].reshape(nbatch, nseq, KV_WIDTH)
    ck = ck_ref[...]
    cv = cv_ref[...]
    wk_ref[:, :WINDOW - nseq, :] = ck[:, nseq:, :]
    wk_ref[:, WINDOW - nseq:, :] = k_new
    wv_ref[:, :WINDOW - nseq, :] = cv[:, nseq:, :]
    wv_ref[:, WINDOW - nseq:, :] = v_new

    zpad = jnp.zeros((nbatch, nkeys - WINDOW, KV_WIDTH), _BF)
    kc_scr[:, WINDOW:, :] = zpad
    vc_scr[:, WINDOW:, :] = zpad
    kc_scr[:, :WINDOW, :] = ck.astype(_BF)
    vc_scr[:, :WINDOW, :] = cv.astype(_BF)
    kc_scr[:, WINDOW:WINDOW + nseq, :] = k_new.astype(_BF)
    vc_scr[:, WINDOW:WINDOW + nseq, :] = v_new.astype(_BF)

    t = lax.broadcasted_iota(jnp.int32, (nseq, nkeys), 0)
    c = lax.broadcasted_iota(jnp.int32, (nseq, nkeys), 1)
    dist = jnp.where(c < WINDOW, WINDOW + t - c, t - (c - WINDOW))
    valid = (dist >= 0) & (dist <= WINDOW) & (c < WINDOW + nseq)
    distf = dist.astype(_F32)
    bias = jnp.concatenate([jnp.where(valid, -_slope(h) * distf, NEG_INF) for h in range(N_HEADS)], axis=0)

    lane = lax.broadcasted_iota(jnp.int32, (rows, 2 * HEAD_DIM), 1)
    lo = lane < HEAD_DIM
    qsc = proj[:, :ATTN_WIDTH] * Q_SCALE
    q_lo, q_hi = [], []
    for p in range(GROUP):
        slab = qsc[:, p * 128:(p + 1) * 128]
        q_lo.append(jnp.where(lo, slab, 0.0).reshape(nbatch, nseq, 128))
        q_hi.append(jnp.where(lo, 0.0, slab).reshape(nbatch, nseq, 128))
    qm = jnp.concatenate(q_lo + q_hi, axis=1).astype(_BF)
    sc = jnp.einsum('bqd,bkd->bqk', qm, kc_scr[...], preferred_element_type=_F32) + bias[None]
    pr = _softmax_with_sink(sc, sinkcol_ref[...][None]).astype(_BF)
    o = jnp.einsum('bqk,bkd->bqd', pr, vc_scr[...], preferred_element_type=_F32)
    attn = []
    for p in range(GROUP):
        o_lo = o[:, p * nseq:(p + 1) * nseq, :].reshape(rows, 128)
        o_hi = o[:, (p + GROUP) * nseq:(p + GROUP + 1) * nseq, :].reshape(rows, 128)
        attn.append(jnp.where(lo, o_lo, o_hi).astype(_BF))

    u_new = proj[:, U_OFF:].reshape(nbatch, nseq, POOL_WIDTH)
    u_ext = jnp.concatenate([st_ref[...], u_new], axis=1)
    pool_ref[...] = u_ext[:, nseq:, :]
    tpos = past_len + lax.broadcasted_iota(jnp.int32, (nseq, POOL_GROUP_WIDTH), 0)
    pooled = []
    for g, w in enumerate(POOL_WINDOWS):
        c0 = g * POOL_GROUP_WIDTH
        ug = u_ext[:, :, c0:c0 + POOL_GROUP_WIDTH]
        sw = _window_sums(ug, 1)[w][:, POOL_PAD:, :]
        cnt = jnp.minimum(tpos + 1, w).astype(_F32)
        d = (sw / cnt[None] - ug[:, POOL_PAD:, :]).reshape(rows, POOL_GROUP_WIDTH).astype(_BF)
        y = _dot(d, wpool_ref[g]) * pscale_ref[:, c0:c0 + POOL_GROUP_WIDTH]
        pooled.append(y.astype(_BF))

    cat = jnp.concatenate(attn + pooled, axis=1)
    x1 = x + _dot(cat, wout_ref[...])
    x1_ref[...] = x1
    qc_ref[...] = _dot(_rms(x1, gcross_ref[...]).astype(_BF), wcq_scr[...]) * CQ_SCALE


def _sample_mixer(x, gmix, win, sinkcol, ck, cv, st, wpool, pscale, wout, gcross, wcq, nbatch, nseq, past_len):
    R, D = x.shape
    rows = nbatch * nseq
    const = lambda *shape: pl.BlockSpec(shape, lambda i: (0,) * len(shape))
    return pl.pallas_call(
        functools.partial(_sample_mixer_kernel, nbatch=nbatch, nseq=nseq, past_len=past_len),
        grid=(R // rows,),
        in_specs=[
            pl.BlockSpec((rows, D), lambda i: (i, 0)),
            const(1, D),
            const(D, IN_WIDTH),
            const(N_HEADS * nseq, 1),
            pl.BlockSpec((nbatch, WINDOW, KV_WIDTH), lambda i: (i, 0, 0)),
            pl.BlockSpec((nbatch, WINDOW, KV_WIDTH), lambda i: (i, 0, 0)),
            pl.BlockSpec((nbatch, POOL_PAD, POOL_WIDTH), lambda i: (i, 0, 0)),
            const(len(POOL_WINDOWS), POOL_GROUP_WIDTH, POOL_GROUP_WIDTH),
            const(1, POOL_WIDTH),
            const(D, D),
            const(1, D),
            const(D, D),
        ],
        out_specs=[
            pl.BlockSpec((rows, D), lambda i: (i, 0)),
            pl.BlockSpec((rows, D), lambda i: (i, 0)),
            pl.BlockSpec((nbatch, WINDOW, KV_WIDTH), lambda i: (i, 0, 0)),
            pl.BlockSpec((nbatch, WINDOW, KV_WIDTH), lambda i: (i, 0, 0)),
            pl.BlockSpec((nbatch, POOL_PAD, POOL_WIDTH), lambda i: (i, 0, 0)),
        ],
        out_shape=[
            jax.ShapeDtypeStruct((R, D), _F32),
            jax.ShapeDtypeStruct((R, D), _F32),
            jax.ShapeDtypeStruct((R // nseq, WINDOW, KV_WIDTH), _F32),
            jax.ShapeDtypeStruct((R // nseq, WINDOW, KV_WIDTH), _F32),
            jax.ShapeDtypeStruct((R // nseq, POOL_PAD, POOL_WIDTH), _F32),
        ],
        scratch_shapes=[
            pltpu.VMEM((nbatch, 2 * WINDOW, KV_WIDTH), _BF),
            pltpu.VMEM((nbatch, 2 * WINDOW, KV_WIDTH), _BF),
            pltpu.VMEM((D, D), _BF),
        ],
        compiler_params=pltpu.CompilerParams(
            dimension_semantics=("arbitrary",), vmem_limit_bytes=VMEM_LIMIT_BYTES),
        name="sample_mixer",
    )(x, gmix, win, sinkcol, ck, cv, st, wpool, pscale, wout, gcross, wcq)


def _mem_cache_rows(cache):
    nb = cache.shape[0]
    c = cache.reshape(nb, N_MEM, N_CROSS_HEADS, CROSS_HEAD_DIM // 128, 128)
    return c.transpose(0, 1, 3, 2, 4).reshape(nb, N_MEM * MEM_ROWS, 128)


def _load_mem_head(ref, b, h):
    halves = [ref[b, pl.ds(half * N_CROSS_HEADS + h, N_MEM, stride=MEM_ROWS), :]
              for half in range(CROSS_HEAD_DIM // 128)]
    return jnp.concatenate(halves, axis=1)


def _sample_tail_kernel(x1_ref, o_ref, wco_ref, gffn_ref, wup_ref, wdown_ref, gfinal_ref, y_ref, *, tile, sub):
    gffn, gfinal = gffn_ref[...], gfinal_ref[...]

    def stream(r0):
        x2 = x1_ref[r0:r0 + sub, :] + _dot(o_ref[r0:r0 + sub, :].astype(_BF), wco_ref[...])
        yield

        def store(y):
            y_ref[r0:r0 + sub, :] = y

        yield from _ffn_final_stages(x2, gffn, wup_ref, wdown_ref, gfinal, store)

    _interleave([stream(r0) for r0 in range(0, tile, sub)], TAIL_SKEW)


def _sample_tail(x1, o, wco, gffn, wup, wdown, gfinal, tile, sub):
    R, D = x1.shape
    const = lambda *shape: _single(shape, lambda i: (0,) * len(shape))
    return pl.pallas_call(
        functools.partial(_sample_tail_kernel, tile=tile, sub=sub),
        grid=(R // tile,),
        in_specs=[
            pl.BlockSpec((tile, D), lambda i: (i, 0)),
            pl.BlockSpec((tile, D), lambda i: (i, 0)),
            const(D, D),
            const(1, D),
            const(D, D_FF),
            const(D_FF, D),
            const(1, D),
        ],
        out_specs=pl.BlockSpec((tile, D), lambda i: (i, 0)),
        out_shape=jax.ShapeDtypeStruct((R, D), _F32),
        compiler_params=pltpu.CompilerParams(
            dimension_semantics=("arbitrary",), vmem_limit_bytes=VMEM_LIMIT_BYTES),
        name="sample_tail",
    )(x1, o, wco, gffn, wup, wdown, gfinal)


PROMPT_TILE = 512
TAIL_TILE = 1024
TAIL_SUB = 256
TAIL_SKEW = 5
SAMPLE_MIXER_BATCH = 16


def kernel(x_prompt, x_sample, cache_win_k, cache_win_v, state_pool, cache_mem_k, cache_mem_v, mem_prompt,
           g_mix, w_in, attn_sinks, w_pool, pool_scale, w_out, g_cross, g_mem, w_cq, w_ck, w_cv, w_co,
           g_ffn, w_up, w_down, g_final):
    depth = g_mix.shape[0]
    assert depth == 1, "one layer per step"
    B, S, D = x_prompt.shape
    DB, T, _ = x_sample.shape
    past_len = PAST_LEN
    l = 0

    win = jnp.concatenate([_pair_heads(w_in[l][:, :ATTN_WIDTH], 1), w_in[l][:, ATTN_WIDTH:]], axis=1).astype(_BF)
    wout = jnp.concatenate([_pair_heads(w_out[l][:ATTN_WIDTH, :], 0), w_out[l][ATTN_WIDTH:, :]], axis=0).astype(_BF)
    wpool = w_pool[l].astype(_BF)
    gmix, gcross, gmem, gffn = (g[l].reshape(1, D) for g in (g_mix, g_cross, g_mem, g_ffn))
    gfinal = g_final.reshape(1, D)
    pscale = pool_scale[l].reshape(1, POOL_WIDTH)
    sinks = attn_sinks[l]

    xs = x_sample.reshape(DB * T, D)
    ck = cache_win_k[l].reshape(DB, WINDOW, KV_WIDTH)
    cv = cache_win_v[l].reshape(DB, WINDOW, KV_WIDTH)
    st = jnp.pad(state_pool[l], ((0, 0), (POOL_PAD - POOL_HIST, 0), (0, 0)))
    sinkcol = jnp.repeat(sinks, T).reshape(N_HEADS * T, 1)
    x1s, qc, wk_s, wv_s, pool_s = _sample_mixer(xs, gmix, win, sinkcol, ck, cv, st, wpool, pscale, wout,
                                                gcross, w_cq[l], SAMPLE_MIXER_BATCH, T, past_len)

    mk = _mem_cache_rows(cache_mem_k[l])
    mv = _mem_cache_rows(cache_mem_v[l])
    x1p, klast, vlast, plast, wcq, wco, wup, wdown, o_s = _prompt_mixer(
        x_prompt, gmix, win, sinks, wpool, pscale, wout, (w_cq[l], w_co[l], w_up[l], w_down[l]), qc, mk, mv,
        PROMPT_TILE)
    mem_k, mem_v, mem_kt, mem_vb = _mem_kv(mem_prompt, gmem, w_ck[l], w_cv[l])
    y_prompt = _prompt_tail(x1p, gcross, wcq, mem_kt, mem_vb, wco, gffn, wup, wdown, gfinal, TAIL_TILE, TAIL_SUB)

    y_sample = _sample_tail(x1s, o_s, wco, gffn, wup, wdown, gfinal, TAIL_TILE, TAIL_SUB).reshape(DB, T, D)

    return (
        y_prompt,
        y_sample,
        klast.reshape(1, B, WINDOW, N_KV_HEADS, HEAD_DIM),
        vlast.reshape(1, B, WINDOW, N_KV_HEADS, HEAD_DIM),
        plast[:, POOL_PAD - POOL_HIST:, :][None],
        _mem_cache_unrows(mem_k)[None],
        _mem_cache_unrows(mem_v)[None],
        wk_s.reshape(1, DB, WINDOW, N_KV_HEADS, HEAD_DIM),
        wv_s.reshape(1, DB, WINDOW, N_KV_HEADS, HEAD_DIM),
        pool_s[:, POOL_PAD - POOL_HIST:, :][None],
    )
```

```python
import functools

import jax
import jax.numpy as jnp
from jax import lax
from jax.experimental import pallas as pl
from jax.experimental.pallas import tpu as pltpu

D_MODEL = 1024
PAST_LEN = 16384
HEAD_DIM = 64
N_HEADS = 8
N_KV_HEADS = 2
GROUP = N_HEADS // N_KV_HEADS
ATTN_WIDTH = N_HEADS * HEAD_DIM
KV_WIDTH = N_KV_HEADS * HEAD_DIM
WINDOW = 128
BLOCK = WINDOW
POOL_WIDTH = D_MODEL - ATTN_WIDTH
POOL_WINDOWS = (2, 4, 8, 16)
POOL_GROUP_WIDTH = 128
POOL_HIST = 15
POOL_PAD = 16
IN_WIDTH = ATTN_WIDTH + 2 * KV_WIDTH + POOL_WIDTH
N_MEM = 256
N_CROSS_HEADS = 4
CROSS_HEAD_DIM = 256
MEM_ROWS = N_CROSS_HEADS * (CROSS_HEAD_DIM // 128)
D_FF = 4 * D_MODEL
FF_CHUNK = 1024
RMS_EPS = 1e-5
NEG_INF = -1e30
Q_SCALE = HEAD_DIM ** -0.5
CQ_SCALE = CROSS_HEAD_DIM ** -0.5
K_OFF = ATTN_WIDTH
V_OFF = ATTN_WIDTH + KV_WIDTH
U_OFF = ATTN_WIDTH + 2 * KV_WIDTH


def _pair_heads(w, axis):
    shape = w.shape
    split = shape[:axis] + (N_KV_HEADS, GROUP, HEAD_DIM) + shape[axis + 1:]
    return jnp.swapaxes(w.reshape(split), axis, axis + 1).reshape(shape)


VMEM_LIMIT_BYTES = 56 * 1024 * 1024

_BF = jnp.bfloat16
_F32 = jnp.float32


def _slope(h):
    return 2.0 ** (-8.0 * (h + 1) / N_HEADS)


def _dot(a, b):
    return jnp.dot(a, b, preferred_element_type=_F32)


def _rms(x, g):
    ms = jnp.mean(x * x, axis=-1, keepdims=True)
    return x * lax.rsqrt(ms + RMS_EPS) * g


def _softmax_with_sink(s, sink):
    m = jnp.maximum(jnp.max(s, axis=-1, keepdims=True), sink)
    e = jnp.exp(s - m)
    den = jnp.sum(e, axis=-1, keepdims=True) + jnp.exp(sink - m)
    return e * (1.0 / den)


def _softmax(s):
    m = jnp.max(s, axis=-1, keepdims=True)
    e = jnp.exp(s - m)
    return e * (1.0 / jnp.sum(e, axis=-1, keepdims=True))


def _window_sums(u_ext, axis):
    out = {}
    s = u_ext
    w = 1
    while w < max(POOL_WINDOWS):
        s = s + pltpu.roll(s, w, axis)
        w *= 2
        out[w] = s
    return out


def _interleave(streams, skew):
    pending = list(streams)
    live = []
    rnd = 0
    while live or pending:
        if pending and rnd % skew == 0:
            live.append(pending.pop(0))
        for g in list(live):
            try:
                next(g)
            except StopIteration:
                live.remove(g)
        rnd += 1


def _ffn_final_stages(x2, gffn, wup_ref, wdown_ref, gfinal, store):
    hn = _rms(x2, gffn).astype(_BF)
    yield
    acc = x2
    for c in range(D_FF // FF_CHUNK):
        hc = _dot(hn, wup_ref[:, c * FF_CHUNK:(c + 1) * FF_CHUNK])
        yield
        hc = jnp.maximum(hc, 0.0)
        hc = (hc * hc).astype(_BF)
        acc = acc + _dot(hc, wdown_ref[c * FF_CHUNK:(c + 1) * FF_CHUNK, :])
        yield
    store(_rms(acc, gfinal))


def _prompt_mixer_kernel(x_ref, gmix_ref, win_ref, sink_ref, wpool_ref, pscale_ref, wout_ref,
                         wcq_ref, wco_ref, wup_ref, wdown_ref, qc_ref, mk_ref, mv_ref,
                         x1_ref, klast_ref, vlast_ref, plast_ref, wcq_bf_ref, wco_bf_ref, wup_bf_ref, wdown_bf_ref,
                         os_ref, kt_scr, v_scr, u_scr, bias_scr, *, tile, nseq):
    b = pl.program_id(0)
    s = pl.program_id(1)
    nb = tile // BLOCK

    for src, dst in ((wcq_ref, wcq_bf_ref), (wco_ref, wco_bf_ref), (wup_ref, wup_bf_ref), (wdown_ref, wdown_bf_ref)):
        dst[...] = src[...].astype(_BF)

    @pl.when((b == 0) & (s == 0))
    def _():
        qi = lax.broadcasted_iota(jnp.int32, (BLOCK, 2 * BLOCK), 0)
        kc = lax.broadcasted_iota(jnp.int32, (BLOCK, 2 * BLOCK), 1)
        dist = qi + BLOCK - kc
        valid = (dist >= 0) & (dist <= WINDOW)
        valid_first = valid & (kc >= BLOCK)
        distf = dist.astype(_F32)
        for h in range(N_HEADS):
            ali = -_slope(h) * distf
            bias_scr[0, h] = jnp.where(valid, ali, NEG_INF)
            bias_scr[1, h] = jnp.where(valid_first, ali, NEG_INF)

    @pl.when(s == 0)
    def _():
        kt_scr[...] = jnp.zeros((KV_WIDTH, BLOCK), _BF)
        v_scr[...] = jnp.zeros((BLOCK, KV_WIDTH), _BF)
        u_scr[...] = jnp.zeros((POOL_PAD, POOL_WIDTH), _F32)

    lane = lax.broadcasted_iota(jnp.int32, (BLOCK, 2 * HEAD_DIM), 1)
    lo = lane < HEAD_DIM
    row = lax.broadcasted_iota(jnp.int32, (BLOCK, POOL_GROUP_WIDTH), 0)
    gmix = gmix_ref[...]

    def project(j):
        x = x_ref[0, j * BLOCK:(j + 1) * BLOCK, :]
        proj = _dot(_rms(x, gmix).astype(_BF), win_ref[...])
        k = proj[:, K_OFF:K_OFF + KV_WIDTH]
        v = proj[:, V_OFF:V_OFF + KV_WIDTH]
        u = proj[:, U_OFF:]
        if j == nb - 1:
            klast_ref[0] = k
            vlast_ref[0] = v
            plast_ref[0] = u[BLOCK - POOL_PAD:]
        return dict(x=x, q=proj[:, :ATTN_WIDTH] * Q_SCALE, kt=k.T.astype(_BF), v=v.astype(_BF), u=u)

    def pool(j, u_hist, u):
        pos = s * tile + j * BLOCK + row
        ys = []
        for g, w in enumerate(POOL_WINDOWS):
            c0 = g * POOL_GROUP_WIDTH
            u_ext = jnp.concatenate([u_hist[:, c0:c0 + POOL_GROUP_WIDTH], u[:, c0:c0 + POOL_GROUP_WIDTH]], axis=0)
            sw = _window_sums(u_ext, 0)[w][POOL_PAD:]
            cnt = jnp.minimum(pos + 1, w).astype(_F32)
            d = (sw / cnt - u_ext[POOL_PAD:]).astype(_BF)
            y = _dot(d, wpool_ref[g]) * pscale_ref[:, c0:c0 + POOL_GROUP_WIDTH]
            ys.append(y.astype(_BF))
        return ys

    def wave_scores(j, blk, kt_prev, pairs):
        first = ((s == 0) & (j == 0)).astype(jnp.int32) if j == 0 else 0
        kt2 = jnp.concatenate([kt_prev, blk["kt"]], axis=1)
        scores = []
        for p in pairs:
            slab = blk["q"][:, p * 128:(p + 1) * 128]
            for half, h in ((0, p), (1, p + GROUP)):
                qm = jnp.where(lo if half == 0 else ~lo, slab, 0.0).astype(_BF)
                scores.append(_dot(qm, kt2) + bias_scr[first, h])
        return scores

    def wave_values(blk, v_prev, pairs, scores):
        v2 = jnp.concatenate([v_prev, blk["v"]], axis=0)
        slabs = []
        for i, p in enumerate(pairs):
            outs = []
            for half, h in ((0, p), (1, p + GROUP)):
                pr = _softmax_with_sink(scores[2 * i + half], sink_ref[h])
                outs.append(_dot(pr.astype(_BF), v2))
            slabs.append(jnp.where(lo, outs[0], outs[1]).astype(_BF))
        return slabs

    def output(j, blk, slabs):
        cat = jnp.concatenate(slabs, axis=1)
        x1_ref[0, j * BLOCK:(j + 1) * BLOCK, :] = blk["x"] + _dot(cat, wout_ref[...])

    heads = [slice(h * CROSS_HEAD_DIM, (h + 1) * CROSS_HEAD_DIM) for h in range(N_CROSS_HEADS)]

    def mem_scores(i):
        q = qc_ref[i * nseq:(i + 1) * nseq, :].astype(_BF)
        return [lax.dot_general(q[:, c], _load_mem_head(mk_ref, i, h).astype(_BF), (((1,), (1,)), ((), ())),
                                preferred_element_type=_F32) for h, c in enumerate(heads)]

    def mem_values(i, scores):
        outs = [_dot(_softmax(sc).astype(_BF), _load_mem_head(mv_ref, i, h).astype(_BF))
                for h, sc in enumerate(scores)]
        os_ref[i * nseq:(i + 1) * nseq, :] = jnp.concatenate(outs, axis=1)

    kt_prev, v_prev, u_hist = kt_scr[...], v_scr[...], u_scr[...]
    blk = project(0)
    done = None
    for j in range(nb):
        sc0 = wave_scores(j, blk, kt_prev, (0, 1))
        nxt = project(j + 1) if j + 1 < nb else None
        msc = mem_scores(j)
        sc1 = wave_scores(j, blk, kt_prev, (2, 3))
        pooled = pool(j, u_hist, blk["u"])
        at0 = wave_values(blk, v_prev, (0, 1), sc0)
        if done is not None:
            output(*done)
        mem_values(j, msc)
        at1 = wave_values(blk, v_prev, (2, 3), sc1)
        done = (j, blk, at0 + at1 + pooled)
        kt_prev, v_prev, u_hist = blk["kt"], blk["v"], blk["u"][BLOCK - POOL_PAD:]
        blk = nxt
    output(*done)
    kt_scr[...] = kt_prev
    v_scr[...] = v_prev
    u_scr[...] = u_hist


def _prompt_mixer(x, gmix, win, sinks, wpool, pscale, wout, tail_weights, qc, mk, mv, tile):
    B, S, D = x.shape
    ns = S // tile
    nsteps = B * ns
    nmem = tile // BLOCK
    nseq = qc.shape[0] // mk.shape[0]
    assert mk.shape[0] == nsteps * nmem
    const = lambda *shape: pl.BlockSpec(shape, lambda b, s: (0,) * len(shape))
    chunk = lambda w: pl.BlockSpec((w.shape[0] // nsteps, w.shape[1]), lambda b, s: (b * ns + s, 0))
    assert all(w.shape[0] % (16 * nsteps) == 0 for w in tail_weights)
    mem_rows = pl.BlockSpec((nmem * nseq, D), lambda b, s: (b * ns + s, 0))
    mem_cache = pl.BlockSpec((nmem, N_MEM * MEM_ROWS, 128), lambda b, s: (b * ns + s, 0, 0))
    return pl.pallas_call(
        functools.partial(_prompt_mixer_kernel, tile=tile, nseq=nseq),
        grid=(B, ns),
        in_specs=[
            pl.BlockSpec((1, tile, D), lambda b, s: (b, s, 0)),
            const(1, D),
            const(D, IN_WIDTH),
            pl.BlockSpec(memory_space=pltpu.SMEM),
            const(len(POOL_WINDOWS), POOL_GROUP_WIDTH, POOL_GROUP_WIDTH),
            const(1, POOL_WIDTH),
            const(D, D),
        ] + [chunk(w) for w in tail_weights] + [mem_rows, mem_cache, mem_cache],
        out_specs=[
            pl.BlockSpec((1, tile, D), lambda b, s: (b, s, 0)),
            pl.BlockSpec((1, BLOCK, KV_WIDTH), lambda b, s: (b, 0, 0)),
            pl.BlockSpec((1, BLOCK, KV_WIDTH), lambda b, s: (b, 0, 0)),
            pl.BlockSpec((1, POOL_PAD, POOL_WIDTH), lambda b, s: (b, 0, 0)),
        ] + [chunk(w) for w in tail_weights] + [mem_rows],
        out_shape=[
            jax.ShapeDtypeStruct((B, S, D), _F32),
            jax.ShapeDtypeStruct((B, BLOCK, KV_WIDTH), _F32),
            jax.ShapeDtypeStruct((B, BLOCK, KV_WIDTH), _F32),
            jax.ShapeDtypeStruct((B, POOL_PAD, POOL_WIDTH), _F32),
        ] + [jax.ShapeDtypeStruct(w.shape, _BF) for w in tail_weights] + [jax.ShapeDtypeStruct(qc.shape, _F32)],
        scratch_shapes=[
            pltpu.VMEM((KV_WIDTH, BLOCK), _BF),
            pltpu.VMEM((BLOCK, KV_WIDTH), _BF),
            pltpu.VMEM((POOL_PAD, POOL_WIDTH), _F32),
            pltpu.VMEM((2, N_HEADS, BLOCK, 2 * BLOCK), _F32),
        ],
        compiler_params=pltpu.CompilerParams(
            dimension_semantics=("arbitrary", "arbitrary"), vmem_limit_bytes=VMEM_LIMIT_BYTES),
        name="prompt_mixer",
    )(x, gmix, win, sinks, wpool, pscale, wout, *tail_weights, qc, mk, mv)


def _mem_kv_kernel(mem_ref, gmem_ref, wck_ref, wcv_ref, k_ref, v_ref, kt_ref, vb_ref, wck_scr, wcv_scr):
    @pl.when(pl.program_id(0) == 0)
    def _():
        wck_scr[...] = wck_ref[...].astype(_BF)
        wcv_scr[...] = wcv_ref[...].astype(_BF)

    hm = _rms(mem_ref[0], gmem_ref[...]).astype(_BF)
    k = _dot(hm, wck_scr[...])
    v = _dot(hm, wcv_scr[...])
    vb_ref[0] = v.astype(_BF)
    for h in range(N_CROSS_HEADS):
        kt_ref[0, h] = k[:, h * CROSS_HEAD_DIM:(h + 1) * CROSS_HEAD_DIM].T.astype(_BF)
        for half in range(CROSS_HEAD_DIM // 128):
            c0 = h * CROSS_HEAD_DIM + half * 128
            rows = pl.ds(half * N_CROSS_HEADS + h, N_MEM, stride=MEM_ROWS)
            k_ref[0, rows, :] = k[:, c0:c0 + 128]
            v_ref[0, rows, :] = v[:, c0:c0 + 128]


def _mem_kv(mem, gmem, wck, wcv):
    B, M, D = mem.shape
    const = lambda *shape: pl.BlockSpec(shape, lambda b: (0,) * len(shape))
    return pl.pallas_call(
        _mem_kv_kernel,
        grid=(B,),
        in_specs=[pl.BlockSpec((1, M, D), lambda b: (b, 0, 0)), const(1, D), const(D, D), const(D, D)],
        out_specs=[
            pl.BlockSpec((1, M * MEM_ROWS, 128), lambda b: (b, 0, 0)),
            pl.BlockSpec((1, M * MEM_ROWS, 128), lambda b: (b, 0, 0)),
            pl.BlockSpec((1, N_CROSS_HEADS, CROSS_HEAD_DIM, M), lambda b: (b, 0, 0, 0)),
            pl.BlockSpec((1, M, D), lambda b: (b, 0, 0)),
        ],
        out_shape=[
            jax.ShapeDtypeStruct((B, M * MEM_ROWS, 128), _F32),
            jax.ShapeDtypeStruct((B, M * MEM_ROWS, 128), _F32),
            jax.ShapeDtypeStruct((B, N_CROSS_HEADS, CROSS_HEAD_DIM, M), _BF),
            jax.ShapeDtypeStruct((B, M, D), _BF),
        ],
        scratch_shapes=[pltpu.VMEM((D, D), _BF), pltpu.VMEM((D, D), _BF)],
        compiler_params=pltpu.CompilerParams(
            dimension_semantics=("arbitrary",), vmem_limit_bytes=VMEM_LIMIT_BYTES),
        name="prompt_mem_kv",
    )(mem, gmem, wck, wcv)


def _mem_cache_unrows(rows):
    nb = rows.shape[0]
    c = rows.reshape(nb, N_MEM, CROSS_HEAD_DIM // 128, N_CROSS_HEADS, 128)
    return c.transpose(0, 1, 3, 2, 4).reshape(nb, N_MEM, N_CROSS_HEADS, CROSS_HEAD_DIM)


def _prompt_tail_kernel(x1_ref, gcross_ref, wcq_ref, kt_ref, vb_ref, wco_ref, gffn_ref, wup_ref, wdown_ref,
                        gfinal_ref, y_ref, *, tile, sub):
    gcross, gffn, gfinal = gcross_ref[...], gffn_ref[...], gfinal_ref[...]

    def stream(r0):
        x1 = x1_ref[0, r0:r0 + sub, :]
        hn = _rms(x1, gcross).astype(_BF)
        yield
        q = (_dot(hn, wcq_ref[...]) * CQ_SCALE).astype(_BF)
        yield
        heads = [slice(h * CROSS_HEAD_DIM, (h + 1) * CROSS_HEAD_DIM) for h in range(N_CROSS_HEADS)]
        scores = [_dot(q[:, c], kt_ref[0, h]) for h, c in enumerate(heads)]
        yield
        outs = [_dot(_softmax(sc).astype(_BF), vb_ref[0, :, c]).astype(_BF) for sc, c in zip(scores, heads)]
        yield
        x2 = x1 + _dot(jnp.concatenate(outs, axis=1), wco_ref[...])
        yield

        def store(y):
            y_ref[0, r0:r0 + sub, :] = y

        yield from _ffn_final_stages(x2, gffn, wup_ref, wdown_ref, gfinal, store)

    _interleave([stream(r0) for r0 in range(0, tile, sub)], TAIL_SKEW)


def _single(shape, index_map):
    return pl.BlockSpec(shape, index_map, pipeline_mode=pl.Buffered(1))


def _prompt_tail(x1, gcross, wcq, kt, vb, wco, gffn, wup, wdown, gfinal, tile, sub):
    B, S, D = x1.shape
    const = lambda *shape: _single(shape, lambda b, s: (0,) * len(shape))
    return pl.pallas_call(
        functools.partial(_prompt_tail_kernel, tile=tile, sub=sub),
        grid=(B, S // tile),
        in_specs=[
            pl.BlockSpec((1, tile, D), lambda b, s: (b, s, 0)),
            const(1, D),
            const(D, D),
            pl.BlockSpec((1, N_CROSS_HEADS, CROSS_HEAD_DIM, N_MEM), lambda b, s: (b, 0, 0, 0)),
            pl.BlockSpec((1, N_MEM, D), lambda b, s: (b, 0, 0)),
            const(D, D),
            const(1, D),
            const(D, D_FF),
            const(D_FF, D),
            const(1, D),
        ],
        out_specs=pl.BlockSpec((1, tile, D), lambda b, s: (b, s, 0)),
        out_shape=jax.ShapeDtypeStruct((B, S, D), _F32),
        compiler_params=pltpu.CompilerParams(
            dimension_semantics=("arbitrary", "arbitrary"), vmem_limit_bytes=VMEM_LIMIT_BYTES),
        name="prompt_tail",
    )(x1, gcross, wcq, kt, vb, wco, gffn, wup, wdown, gfinal)


def _sample_mixer_kernel(x_ref, gmix_ref, win_ref, sinkcol_ref, ck_ref, cv_ref, st_ref, wpool_ref, pscale_ref,
                         wout_ref, gcross_ref, wcq_ref,
                         x1_ref, qc_ref, wk_ref, wv_ref, pool_ref,
                         kc_scr, vc_scr, wcq_scr, *, nbatch, sub, nseq, past_len):
    @pl.when(pl.program_id(0) == 0)
    def _():
        wcq_scr[...] = wcq_ref[...].astype(_BF)

    rows = sub * nseq
    nkeys = 2 * WINDOW
    gmix, gcross = gmix_ref[...], gcross_ref[...]
    sinkcol = sinkcol_ref[...][None]

    t = lax.broadcasted_iota(jnp.int32, (nseq, nkeys), 0)
    c = lax.broadcasted_iota(jnp.int32, (nseq, nkeys), 1)
    dist = jnp.where(c < WINDOW, WINDOW + t - c, t - (c - WINDOW))
    valid = (dist >= 0) & (dist <= WINDOW) & (c < WINDOW + nseq)
    distf = dist.astype(_F32)
    bias = jnp.concatenate([jnp.where(valid, -_slope(h) * distf, NEG_INF) for h in range(N_HEADS)], axis=0)[None]
    lane = lax.broadcasted_iota(jnp.int32, (rows, 2 * HEAD_DIM), 1)
    lo = lane < HEAD_DIM
    tpos = past_len + lax.broadcasted_iota(jnp.int32, (nseq, POOL_GROUP_WIDTH), 0)
    zpad = jnp.zeros((sub, nkeys - WINDOW, KV_WIDTH), _BF)

    def stream(b0):
        bs = slice(b0, b0 + sub)
        rs = slice(b0 * nseq, (b0 + sub) * nseq)
        x = x_ref[rs, :]
        proj = _dot(_rms(x, gmix).astype(_BF), win_ref[...])
        yield
        k_new = proj[:, K_OFF:K_OFF + KV_WIDTH].reshape(sub, nseq, KV_WIDTH)
        v_new = proj[:, V_OFF:V_OFF + KV_WIDTH].reshape(sub, nseq, KV_WIDTH)
        ck = ck_ref[bs]
        cv = cv_ref[bs]
        wk_ref[bs, :WINDOW - nseq, :] = ck[:, nseq:, :]
        wk_ref[bs, WINDOW - nseq:, :] = k_new
        wv_ref[bs, :WINDOW - nseq, :] = cv[:, nseq:, :]
        wv_ref[bs, WINDOW - nseq:, :] = v_new
        kc_scr[bs, WINDOW:, :] = zpad
        vc_scr[bs, WINDOW:, :] = zpad
        kc_scr[bs, :WINDOW, :] = ck.astype(_BF)
        vc_scr[bs, :WINDOW, :] = cv.astype(_BF)
        kc_scr[bs, WINDOW:WINDOW + nseq, :] = k_new.astype(_BF)
        vc_scr[bs, WINDOW:WINDOW + nseq, :] = v_new.astype(_BF)
        qsc = proj[:, :ATTN_WIDTH] * Q_SCALE
        q_lo, q_hi = [], []
        for p in range(GROUP):
            slab = qsc[:, p * 128:(p + 1) * 128]
            q_lo.append(jnp.where(lo, slab, 0.0).reshape(sub, nseq, 128))
            q_hi.append(jnp.where(lo, 0.0, slab).reshape(sub, nseq, 128))
        qm = jnp.concatenate(q_lo + q_hi, axis=1).astype(_BF)
        sc = jnp.einsum('bqd,bkd->bqk', qm, kc_scr[bs], preferred_element_type=_F32) + bias
        yield
        pr = _softmax_with_sink(sc, sinkcol).astype(_BF)
        o = jnp.einsum('bqk,bkd->bqd', pr, vc_scr[bs], preferred_element_type=_F32)
        yield
        attn = []
        for p in range(GROUP):
            o_lo = o[:, p * nseq:(p + 1) * nseq, :].reshape(rows, 128)
            o_hi = o[:, (p + GROUP) * nseq:(p + GROUP + 1) * nseq, :].reshape(rows, 128)
            attn.append(jnp.where(lo, o_lo, o_hi).astype(_BF))
        u_new = proj[:, U_OFF:].reshape(sub, nseq, POOL_WIDTH)
        u_ext = jnp.concatenate([st_ref[bs], u_new], axis=1)
        pool_ref[bs] = u_ext[:, nseq:, :]
        pooled = []
        for g, w in enumerate(POOL_WINDOWS):
            c0 = g * POOL_GROUP_WIDTH
            ug = u_ext[:, :, c0:c0 + POOL_GROUP_WIDTH]
            sw = _window_sums(ug, 1)[w][:, POOL_PAD:, :]
            cnt = jnp.minimum(tpos + 1, w).astype(_F32)
            d = (sw / cnt[None] - ug[:, POOL_PAD:, :]).reshape(rows, POOL_GROUP_WIDTH).astype(_BF)
            y = _dot(d, wpool_ref[g]) * pscale_ref[:, c0:c0 + POOL_GROUP_WIDTH]
            pooled.append(y.astype(_BF))
        yield
        x1 = x + _dot(jnp.concatenate(attn + pooled, axis=1), wout_ref[...])
        x1_ref[rs, :] = x1
        yield
        qc_ref[rs, :] = _dot(_rms(x1, gcross).astype(_BF), wcq_scr[...]) * CQ_SCALE

    _interleave([stream(b0) for b0 in range(0, nbatch, sub)], SAMPLE_MIXER_SKEW)


def _sample_mixer(x, gmix, win, sinkcol, ck, cv, st, wpool, pscale, wout, gcross, wcq, nbatch, sub, nseq,
                  past_len):
    R, D = x.shape
    rows = nbatch * nseq
    const = lambda *shape: pl.BlockSpec(shape, lambda i: (0,) * len(shape))
    return pl.pallas_call(
        functools.partial(_sample_mixer_kernel, nbatch=nbatch, sub=sub, nseq=nseq, past_len=past_len),
        grid=(R // rows,),
        in_specs=[
            pl.BlockSpec((rows, D), lambda i: (i, 0)),
            const(1, D),
            const(D, IN_WIDTH),
            const(N_HEADS * nseq, 1),
            pl.BlockSpec((nbatch, WINDOW, KV_WIDTH), lambda i: (i, 0, 0)),
            pl.BlockSpec((nbatch, WINDOW, KV_WIDTH), lambda i: (i, 0, 0)),
            pl.BlockSpec((nbatch, POOL_PAD, POOL_WIDTH), lambda i: (i, 0, 0)),
            const(len(POOL_WINDOWS), POOL_GROUP_WIDTH, POOL_GROUP_WIDTH),
            const(1, POOL_WIDTH),
            const(D, D),
            const(1, D),
            const(D, D),
        ],
        out_specs=[
            pl.BlockSpec((rows, D), lambda i: (i, 0)),
            pl.BlockSpec((rows, D), lambda i: (i, 0)),
            pl.BlockSpec((nbatch, WINDOW, KV_WIDTH), lambda i: (i, 0, 0)),
            pl.BlockSpec((nbatch, WINDOW, KV_WIDTH), lambda i: (i, 0, 0)),
            pl.BlockSpec((nbatch, POOL_PAD, POOL_WIDTH), lambda i: (i, 0, 0)),
        ],
        out_shape=[
            jax.ShapeDtypeStruct((R, D), _F32),
            jax.ShapeDtypeStruct((R, D), _F32),
            jax.ShapeDtypeStruct((R // nseq, WINDOW, KV_WIDTH), _F32),
            jax.ShapeDtypeStruct((R // nseq, WINDOW, KV_WIDTH), _F32),
            jax.ShapeDtypeStruct((R // nseq, POOL_PAD, POOL_WIDTH), _F32),
        ],
        scratch_shapes=[
            pltpu.VMEM((nbatch, 2 * WINDOW, KV_WIDTH), _BF),
            pltpu.VMEM((nbatch, 2 * WINDOW, KV_WIDTH), _BF),
            pltpu.VMEM((D, D), _BF),
        ],
        compiler_params=pltpu.CompilerParams(
            dimension_semantics=("arbitrary",), vmem_limit_bytes=VMEM_LIMIT_BYTES),
        name="sample_mixer",
    )(x, gmix, win, sinkcol, ck, cv, st, wpool, pscale, wout, gcross, wcq)


def _mem_cache_rows(cache):
    nb = cache.shape[0]
    c = cache.reshape(nb, N_MEM, N_CROSS_HEADS, CROSS_HEAD_DIM // 128, 128)
    return c.transpose(0, 1, 3, 2, 4).reshape(nb, N_MEM * MEM_ROWS, 128)


def _load_mem_head(ref, b, h):
    halves = [ref[b, pl.ds(half * N_CROSS_HEADS + h, N_MEM, stride=MEM_ROWS), :]
              for half in range(CROSS_HEAD_DIM // 128)]
    return jnp.concatenate(halves, axis=1)


def _sample_tail_kernel(x1_ref, o_ref, wco_ref, gffn_ref, wup_ref, wdown_ref, gfinal_ref, y_ref, *, tile, sub):
    gffn, gfinal = gffn_ref[...], gfinal_ref[...]

    def stream(r0):
        x2 = x1_ref[r0:r0 + sub, :] + _dot(o_ref[r0:r0 + sub, :].astype(_BF), wco_ref[...])
        yield

        def store(y):
            y_ref[r0:r0 + sub, :] = y

        yield from _ffn_final_stages(x2, gffn, wup_ref, wdown_ref, gfinal, store)

    _interleave([stream(r0) for r0 in range(0, tile, sub)], TAIL_SKEW)


def _sample_tail(x1, o, wco, gffn, wup, wdown, gfinal, tile, sub):
    R, D = x1.shape
    const = lambda *shape: _single(shape, lambda i: (0,) * len(shape))
    return pl.pallas_call(
        functools.partial(_sample_tail_kernel, tile=tile, sub=sub),
        grid=(R // tile,),
        in_specs=[
            pl.BlockSpec((tile, D), lambda i: (i, 0)),
            pl.BlockSpec((tile, D), lambda i: (i, 0)),
            const(D, D),
            const(1, D),
            const(D, D_FF),
            const(D_FF, D),
            const(1, D),
        ],
        out_specs=pl.BlockSpec((tile, D), lambda i: (i, 0)),
        out_shape=jax.ShapeDtypeStruct((R, D), _F32),
        compiler_params=pltpu.CompilerParams(
            dimension_semantics=("arbitrary",), vmem_limit_bytes=VMEM_LIMIT_BYTES),
        name="sample_tail",
    )(x1, o, wco, gffn, wup, wdown, gfinal)


PROMPT_TILE = 512
TAIL_TILE = 1024
TAIL_SUB = 256
TAIL_SKEW = 5
SAMPLE_MIXER_BATCH = 32
SAMPLE_MIXER_SUB = 16
SAMPLE_MIXER_SKEW = 3


def kernel(x_prompt, x_sample, cache_win_k, cache_win_v, state_pool, cache_mem_k, cache_mem_v, mem_prompt,
           g_mix, w_in, attn_sinks, w_pool, pool_scale, w_out, g_cross, g_mem, w_cq, w_ck, w_cv, w_co,
           g_ffn, w_up, w_down, g_final):
    depth = g_mix.shape[0]
    assert depth == 1, "one layer per step"
    B, S, D = x_prompt.shape
    DB, T, _ = x_sample.shape
    past_len = PAST_LEN
    l = 0

    win = jnp.concatenate([_pair_heads(w_in[l][:, :ATTN_WIDTH], 1), w_in[l][:, ATTN_WIDTH:]], axis=1).astype(_BF)
    wout = jnp.concatenate([_pair_heads(w_out[l][:ATTN_WIDTH, :], 0), w_out[l][ATTN_WIDTH:, :]], axis=0).astype(_BF)
    wpool = w_pool[l].astype(_BF)
    gmix, gcross, gmem, gffn = (g[l].reshape(1, D) for g in (g_mix, g_cross, g_mem, g_ffn))
    gfinal = g_final.reshape(1, D)
    pscale = pool_scale[l].reshape(1, POOL_WIDTH)
    sinks = attn_sinks[l]

    xs = x_sample.reshape(DB * T, D)
    ck = cache_win_k[l].reshape(DB, WINDOW, KV_WIDTH)
    cv = cache_win_v[l].reshape(DB, WINDOW, KV_WIDTH)
    st = jnp.pad(state_pool[l], ((0, 0), (POOL_PAD - POOL_HIST, 0), (0, 0)))
    sinkcol = jnp.repeat(sinks, T).reshape(N_HEADS * T, 1)
    x1s, qc, wk_s, wv_s, pool_s = _sample_mixer(xs, gmix, win, sinkcol, ck, cv, st, wpool, pscale, wout,
                                                gcross, w_cq[l], SAMPLE_MIXER_BATCH, SAMPLE_MIXER_SUB, T, past_len)

    mk = _mem_cache_rows(cache_mem_k[l])
    mv = _mem_cache_rows(cache_mem_v[l])
    x1p, klast, vlast, plast, wcq, wco, wup, wdown, o_s = _prompt_mixer(
        x_prompt, gmix, win, sinks, wpool, pscale, wout, (w_cq[l], w_co[l], w_up[l], w_down[l]), qc, mk, mv,
        PROMPT_TILE)
    mem_k, mem_v, mem_kt, mem_vb = _mem_kv(mem_prompt, gmem, w_ck[l], w_cv[l])
    y_prompt = _prompt_tail(x1p, gcross, wcq, mem_kt, mem_vb, wco, gffn, wup, wdown, gfinal, TAIL_TILE, TAIL_SUB)

    y_sample = _sample_tail(x1s, o_s, wco, gffn, wup, wdown, gfinal, TAIL_TILE, TAIL_SUB).reshape(DB, T, D)

    return (
        y_prompt,
        y_sample,
        klast.reshape(1, B, WINDOW, N_KV_HEADS, HEAD_DIM),
        vlast.reshape(1, B, WINDOW, N_KV_HEADS, HEAD_DIM),
        plast[:, POOL_PAD - POOL_HIST:, :][None],
        _mem_cache_unrows(mem_k)[None],
        _mem_cache_unrows(mem_v)[None],
        wk_s.reshape(1, DB, WINDOW, N_KV_HEADS, HEAD_DIM),
        wv_s.reshape(1, DB, WINDOW, N_KV_HEADS, HEAD_DIM),
        pool_s[:, POOL_PAD - POOL_HIST:, :][None],
    )
```

```python
import functools

import jax
import jax.numpy as jnp
from jax import lax
from jax.experimental import pallas as pl
from jax.experimental.pallas import tpu as pltpu

D_MODEL = 1024
PAST_LEN = 16384
HEAD_DIM = 64
N_HEADS = 8
N_KV_HEADS = 2
GROUP = N_HEADS // N_KV_HEADS
ATTN_WIDTH = N_HEADS * HEAD_DIM
KV_WIDTH = N_KV_HEADS * HEAD_DIM
WINDOW = 128
BLOCK = WINDOW
POOL_WIDTH = D_MODEL - ATTN_WIDTH
POOL_WINDOWS = (2, 4, 8, 16)
POOL_GROUP_WIDTH = 128
POOL_HIST = 15
POOL_PAD = 16
IN_WIDTH = ATTN_WIDTH + 2 * KV_WIDTH + POOL_WIDTH
N_MEM = 256
N_CROSS_HEADS = 4
CROSS_HEAD_DIM = 256
MEM_ROWS = N_CROSS_HEADS * (CROSS_HEAD_DIM // 128)
D_FF = 4 * D_MODEL
FF_CHUNK = 1024
RMS_EPS = 1e-5
NEG_INF = -1e30
Q_SCALE = HEAD_DIM ** -0.5
CQ_SCALE = CROSS_HEAD_DIM ** -0.5
K_OFF = ATTN_WIDTH
V_OFF = ATTN_WIDTH + KV_WIDTH
U_OFF = ATTN_WIDTH + 2 * KV_WIDTH


def _pair_heads(w, axis):
    shape = w.shape
    split = shape[:axis] + (N_KV_HEADS, GROUP, HEAD_DIM) + shape[axis + 1:]
    return jnp.swapaxes(w.reshape(split), axis, axis + 1).reshape(shape)


VMEM_LIMIT_BYTES = 56 * 1024 * 1024

_BF = jnp.bfloat16
_F32 = jnp.float32


def _slope(h):
    return 2.0 ** (-8.0 * (h + 1) / N_HEADS)


def _dot(a, b):
    return jnp.dot(a, b, preferred_element_type=_F32)


def _rms(x, g):
    ms = jnp.mean(x * x, axis=-1, keepdims=True)
    return x * lax.rsqrt(ms + RMS_EPS) * g


def _softmax_with_sink(s, sink):
    m = jnp.maximum(jnp.max(s, axis=-1, keepdims=True), sink)
    e = jnp.exp(s - m)
    den = jnp.sum(e, axis=-1, keepdims=True) + jnp.exp(sink - m)
    return e * (1.0 / den)


def _softmax(s):
    m = jnp.max(s, axis=-1, keepdims=True)
    e = jnp.exp(s - m)
    return e * (1.0 / jnp.sum(e, axis=-1, keepdims=True))


def _window_sums(u_ext, axis):
    out = {}
    s = u_ext
    w = 1
    while w < max(POOL_WINDOWS):
        s = s + pltpu.roll(s, w, axis)
        w *= 2
        out[w] = s
    return out


def _interleave(streams, skew):
    pending = list(streams)
    live = []
    rnd = 0
    while live or pending:
        if pending and rnd % skew == 0:
            live.append(pending.pop(0))
        for g in list(live):
            try:
                next(g)
            except StopIteration:
                live.remove(g)
        rnd += 1


def _ffn_final_stages(x2, gffn, wup_ref, wdown_ref, gfinal, store):
    hn = _rms(x2, gffn).astype(_BF)
    yield
    acc = x2
    for c in range(D_FF // FF_CHUNK):
        hc = _dot(hn, wup_ref[:, c * FF_CHUNK:(c + 1) * FF_CHUNK])
        yield
        hc = jnp.maximum(hc, 0.0)
        hc = (hc * hc).astype(_BF)
        acc = acc + _dot(hc, wdown_ref[c * FF_CHUNK:(c + 1) * FF_CHUNK, :])
        yield
    store(_rms(acc, gfinal))


def _prompt_mixer_kernel(x_ref, gmix_ref, win_ref, sink_ref, wpool_ref, pscale_ref, wout_ref,
                         wcq_ref, wco_ref, wup_ref, wdown_ref, qc_ref, mk_ref, mv_ref,
                         x1_ref, klast_ref, vlast_ref, plast_ref, wcq_bf_ref, wco_bf_ref, wup_bf_ref, wdown_bf_ref,
                         os_ref, kt_scr, v_scr, u_scr, bias_scr, *, tile, nseq):
    b = pl.program_id(0)
    s = pl.program_id(1)
    nb = tile // BLOCK

    for src, dst in ((wcq_ref, wcq_bf_ref), (wco_ref, wco_bf_ref), (wup_ref, wup_bf_ref), (wdown_ref, wdown_bf_ref)):
        dst[...] = src[...].astype(_BF)

    @pl.when((b == 0) & (s == 0))
    def _():
        qi = lax.broadcasted_iota(jnp.int32, (BLOCK, 2 * BLOCK), 0)
        kc = lax.broadcasted_iota(jnp.int32, (BLOCK, 2 * BLOCK), 1)
        dist = qi + BLOCK - kc
        valid = (dist >= 0) & (dist <= WINDOW)
        valid_first = valid & (kc >= BLOCK)
        distf = dist.astype(_F32)
        for h in range(N_HEADS):
            ali = -_slope(h) * distf
            bias_scr[0, h] = jnp.where(valid, ali, NEG_INF)
            bias_scr[1, h] = jnp.where(valid_first, ali, NEG_INF)

    @pl.when(s == 0)
    def _():
        kt_scr[...] = jnp.zeros((KV_WIDTH, BLOCK), _BF)
        v_scr[...] = jnp.zeros((BLOCK, KV_WIDTH), _BF)
        u_scr[...] = jnp.zeros((POOL_PAD, POOL_WIDTH), _F32)

    lane = lax.broadcasted_iota(jnp.int32, (BLOCK, 2 * HEAD_DIM), 1)
    lo = lane < HEAD_DIM
    row = lax.broadcasted_iota(jnp.int32, (BLOCK, POOL_GROUP_WIDTH), 0)
    gmix = gmix_ref[...]

    def project(j):
        x = x_ref[0, j * BLOCK:(j + 1) * BLOCK, :]
        proj = _dot(_rms(x, gmix).astype(_BF), win_ref[...])
        k = proj[:, K_OFF:K_OFF + KV_WIDTH]
        v = proj[:, V_OFF:V_OFF + KV_WIDTH]
        u = proj[:, U_OFF:]
        kt = k.T
        if j == nb - 1:
            klast_ref[0] = kt
            vlast_ref[0] = v.T
            plast_ref[0] = u[BLOCK - POOL_PAD:]
        return dict(x=x, q=proj[:, :ATTN_WIDTH] * Q_SCALE, kt=kt.astype(_BF), v=v.astype(_BF), u=u)

    def pool(j, u_hist, u):
        pos = s * tile + j * BLOCK + row
        ys = []
        for g, w in enumerate(POOL_WINDOWS):
            c0 = g * POOL_GROUP_WIDTH
            u_ext = jnp.concatenate([u_hist[:, c0:c0 + POOL_GROUP_WIDTH], u[:, c0:c0 + POOL_GROUP_WIDTH]], axis=0)
            sw = _window_sums(u_ext, 0)[w][POOL_PAD:]
            cnt = jnp.minimum(pos + 1, w).astype(_F32)
            d = (sw / cnt - u_ext[POOL_PAD:]).astype(_BF)
            y = _dot(d, wpool_ref[g]) * pscale_ref[:, c0:c0 + POOL_GROUP_WIDTH]
            ys.append(y.astype(_BF))
        return ys

    def wave_scores(j, blk, kt_prev, pairs):
        first = ((s == 0) & (j == 0)).astype(jnp.int32) if j == 0 else 0
        kt2 = jnp.concatenate([kt_prev, blk["kt"]], axis=1)
        scores = []
        for p in pairs:
            slab = blk["q"][:, p * 128:(p + 1) * 128]
            for half, h in ((0, p), (1, p + GROUP)):
                qm = jnp.where(lo if half == 0 else ~lo, slab, 0.0).astype(_BF)
                scores.append(_dot(qm, kt2) + bias_scr[first, h])
        return scores

    def wave_values(blk, v_prev, pairs, scores):
        v2 = jnp.concatenate([v_prev, blk["v"]], axis=0)
        slabs = []
        for i, p in enumerate(pairs):
            outs = []
            for half, h in ((0, p), (1, p + GROUP)):
                pr = _softmax_with_sink(scores[2 * i + half], sink_ref[h])
                outs.append(_dot(pr.astype(_BF), v2))
            slabs.append(jnp.where(lo, outs[0], outs[1]).astype(_BF))
        return slabs

    def output(j, blk, slabs):
        cat = jnp.concatenate(slabs, axis=1)
        x1_ref[0, j * BLOCK:(j + 1) * BLOCK, :] = blk["x"] + _dot(cat, wout_ref[...])

    heads = [slice(h * CROSS_HEAD_DIM, (h + 1) * CROSS_HEAD_DIM) for h in range(N_CROSS_HEADS)]

    def mem_scores(i):
        q = qc_ref[i * nseq:(i + 1) * nseq, :].astype(_BF)
        return [lax.dot_general(q[:, c], _load_mem_head(mk_ref, i, h).astype(_BF), (((1,), (1,)), ((), ())),
                                preferred_element_type=_F32) for h, c in enumerate(heads)]

    def mem_values(i, scores):
        outs = [_dot(_softmax(sc).astype(_BF), _load_mem_head(mv_ref, i, h).astype(_BF))
                for h, sc in enumerate(scores)]
        os_ref[i * nseq:(i + 1) * nseq, :] = jnp.concatenate(outs, axis=1)

    kt_prev, v_prev, u_hist = kt_scr[...], v_scr[...], u_scr[...]
    blk = project(0)
    done = None
    for j in range(nb):
        sc0 = wave_scores(j, blk, kt_prev, (0, 1))
        nxt = project(j + 1) if j + 1 < nb else None
        msc = mem_scores(j)
        sc1 = wave_scores(j, blk, kt_prev, (2, 3))
        pooled = pool(j, u_hist, blk["u"])
        at0 = wave_values(blk, v_prev, (0, 1), sc0)
        if done is not None:
            output(*done)
        mem_values(j, msc)
        at1 = wave_values(blk, v_prev, (2, 3), sc1)
        done = (j, blk, at0 + at1 + pooled)
        kt_prev, v_prev, u_hist = blk["kt"], blk["v"], blk["u"][BLOCK - POOL_PAD:]
        blk = nxt
    output(*done)
    kt_scr[...] = kt_prev
    v_scr[...] = v_prev
    u_scr[...] = u_hist


def _prompt_mixer(x, gmix, win, sinks, wpool, pscale, wout, tail_weights, qc, mk, mv, tile):
    B, S, D = x.shape
    ns = S // tile
    nsteps = B * ns
    nmem = tile // BLOCK
    nseq = qc.shape[0] // mk.shape[0]
    assert mk.shape[0] == nsteps * nmem
    const = lambda *shape: pl.BlockSpec(shape, lambda b, s: (0,) * len(shape))
    chunk = lambda w: pl.BlockSpec((w.shape[0] // nsteps, w.shape[1]), lambda b, s: (b * ns + s, 0))
    assert all(w.shape[0] % (16 * nsteps) == 0 for w in tail_weights)
    mem_rows = pl.BlockSpec((nmem * nseq, D), lambda b, s: (b * ns + s, 0))
    mem_cache = pl.BlockSpec((nmem, N_MEM * MEM_ROWS, 128), lambda b, s: (b * ns + s, 0, 0))
    return pl.pallas_call(
        functools.partial(_prompt_mixer_kernel, tile=tile, nseq=nseq),
        grid=(B, ns),
        in_specs=[
            pl.BlockSpec((1, tile, D), lambda b, s: (b, s, 0)),
            const(1, D),
            const(D, IN_WIDTH),
            pl.BlockSpec(memory_space=pltpu.SMEM),
            const(len(POOL_WINDOWS), POOL_GROUP_WIDTH, POOL_GROUP_WIDTH),
            const(1, POOL_WIDTH),
            const(D, D),
        ] + [chunk(w) for w in tail_weights] + [mem_rows, mem_cache, mem_cache],
        out_specs=[
            pl.BlockSpec((1, tile, D), lambda b, s: (b, s, 0)),
            pl.BlockSpec((1, BLOCK, KV_WIDTH), lambda b, s: (b, 0, 0)),
            pl.BlockSpec((1, BLOCK, KV_WIDTH), lambda b, s: (b, 0, 0)),
            pl.BlockSpec((1, POOL_PAD, POOL_WIDTH), lambda b, s: (b, 0, 0)),
        ] + [chunk(w) for w in tail_weights] + [mem_rows],
        out_shape=[
            jax.ShapeDtypeStruct((B, S, D), _F32),
            jax.ShapeDtypeStruct((B, BLOCK, KV_WIDTH), _F32),
            jax.ShapeDtypeStruct((B, BLOCK, KV_WIDTH), _F32),
            jax.ShapeDtypeStruct((B, POOL_PAD, POOL_WIDTH), _F32),
        ] + [jax.ShapeDtypeStruct(w.shape, _BF) for w in tail_weights] + [jax.ShapeDtypeStruct(qc.shape, _F32)],
        scratch_shapes=[
            pltpu.VMEM((KV_WIDTH, BLOCK), _BF),
            pltpu.VMEM((BLOCK, KV_WIDTH), _BF),
            pltpu.VMEM((POOL_PAD, POOL_WIDTH), _F32),
            pltpu.VMEM((2, N_HEADS, BLOCK, 2 * BLOCK), _F32),
        ],
        compiler_params=pltpu.CompilerParams(
            dimension_semantics=("arbitrary", "arbitrary"), vmem_limit_bytes=VMEM_LIMIT_BYTES),
        name="prompt_mixer",
    )(x, gmix, win, sinks, wpool, pscale, wout, *tail_weights, qc, mk, mv)


def _mem_kv_kernel(mem_ref, gmem_ref, wck_ref, wcv_ref, k_ref, v_ref, kt_ref, vb_ref, wck_scr, wcv_scr):
    @pl.when(pl.program_id(0) == 0)
    def _():
        wck_scr[...] = wck_ref[...].astype(_BF)
        wcv_scr[...] = wcv_ref[...].astype(_BF)

    hm = _rms(mem_ref[0], gmem_ref[...]).astype(_BF)
    k = _dot(hm, wck_scr[...])
    v = _dot(hm, wcv_scr[...])
    vb_ref[0] = v.astype(_BF)
    for h in range(N_CROSS_HEADS):
        kt_ref[0, h] = k[:, h * CROSS_HEAD_DIM:(h + 1) * CROSS_HEAD_DIM].T.astype(_BF)
        for half in range(CROSS_HEAD_DIM // 128):
            c0 = h * CROSS_HEAD_DIM + half * 128
            rows = pl.ds(half * N_CROSS_HEADS + h, N_MEM, stride=MEM_ROWS)
            k_ref[0, rows, :] = k[:, c0:c0 + 128]
            v_ref[0, rows, :] = v[:, c0:c0 + 128]


def _mem_kv(mem, gmem, wck, wcv):
    B, M, D = mem.shape
    const = lambda *shape: pl.BlockSpec(shape, lambda b: (0,) * len(shape))
    return pl.pallas_call(
        _mem_kv_kernel,
        grid=(B,),
        in_specs=[pl.BlockSpec((1, M, D), lambda b: (b, 0, 0)), const(1, D), const(D, D), const(D, D)],
        out_specs=[
            pl.BlockSpec((1, M * MEM_ROWS, 128), lambda b: (b, 0, 0)),
            pl.BlockSpec((1, M * MEM_ROWS, 128), lambda b: (b, 0, 0)),
            pl.BlockSpec((1, N_CROSS_HEADS, CROSS_HEAD_DIM, M), lambda b: (b, 0, 0, 0)),
            pl.BlockSpec((1, M, D), lambda b: (b, 0, 0)),
        ],
        out_shape=[
            jax.ShapeDtypeStruct((B, M * MEM_ROWS, 128), _F32),
            jax.ShapeDtypeStruct((B, M * MEM_ROWS, 128), _F32),
            jax.ShapeDtypeStruct((B, N_CROSS_HEADS, CROSS_HEAD_DIM, M), _BF),
            jax.ShapeDtypeStruct((B, M, D), _BF),
        ],
        scratch_shapes=[pltpu.VMEM((D, D), _BF), pltpu.VMEM((D, D), _BF)],
        compiler_params=pltpu.CompilerParams(
            dimension_semantics=("arbitrary",), vmem_limit_bytes=VMEM_LIMIT_BYTES),
        name="prompt_mem_kv",
    )(mem, gmem, wck, wcv)


def _mem_cache_unrows(rows):
    nb = rows.shape[0]
    c = rows.reshape(nb, N_MEM, CROSS_HEAD_DIM // 128, N_CROSS_HEADS, 128)
    return c.transpose(0, 1, 3, 2, 4).reshape(nb, N_MEM, N_CROSS_HEADS, CROSS_HEAD_DIM)


def _prompt_tail_kernel(x1_ref, gcross_ref, wcq_ref, kt_ref, vb_ref, wco_ref, gffn_ref, wup_ref, wdown_ref,
                        gfinal_ref, y_ref, *, tile, sub):
    gcross, gffn, gfinal = gcross_ref[...], gffn_ref[...], gfinal_ref[...]

    def stream(r0):
        x1 = x1_ref[0, r0:r0 + sub, :]
        hn = _rms(x1, gcross).astype(_BF)
        yield
        q = (_dot(hn, wcq_ref[...]) * CQ_SCALE).astype(_BF)
        yield
        heads = [slice(h * CROSS_HEAD_DIM, (h + 1) * CROSS_HEAD_DIM) for h in range(N_CROSS_HEADS)]
        scores = [_dot(q[:, c], kt_ref[0, h]) for h, c in enumerate(heads)]
        yield
        outs = [_dot(_softmax(sc).astype(_BF), vb_ref[0, :, c]).astype(_BF) for sc, c in zip(scores, heads)]
        yield
        x2 = x1 + _dot(jnp.concatenate(outs, axis=1), wco_ref[...])
        yield

        def store(y):
            y_ref[0, r0:r0 + sub, :] = y

        yield from _ffn_final_stages(x2, gffn, wup_ref, wdown_ref, gfinal, store)

    _interleave([stream(r0) for r0 in range(0, tile, sub)], TAIL_SKEW)


def _single(shape, index_map):
    return pl.BlockSpec(shape, index_map, pipeline_mode=pl.Buffered(1))


def _prompt_tail(x1, gcross, wcq, kt, vb, wco, gffn, wup, wdown, gfinal, tile, sub):
    B, S, D = x1.shape
    const = lambda *shape: _single(shape, lambda b, s: (0,) * len(shape))
    return pl.pallas_call(
        functools.partial(_prompt_tail_kernel, tile=tile, sub=sub),
        grid=(B, S // tile),
        in_specs=[
            pl.BlockSpec((1, tile, D), lambda b, s: (b, s, 0)),
            const(1, D),
            const(D, D),
            pl.BlockSpec((1, N_CROSS_HEADS, CROSS_HEAD_DIM, N_MEM), lambda b, s: (b, 0, 0, 0)),
            pl.BlockSpec((1, N_MEM, D), lambda b, s: (b, 0, 0)),
            const(D, D),
            const(1, D),
            const(D, D_FF),
            const(D_FF, D),
            const(1, D),
        ],
        out_specs=pl.BlockSpec((1, tile, D), lambda b, s: (b, s, 0)),
        out_shape=jax.ShapeDtypeStruct((B, S, D), _F32),
        compiler_params=pltpu.CompilerParams(
            dimension_semantics=("arbitrary", "arbitrary"), vmem_limit_bytes=VMEM_LIMIT_BYTES),
        name="prompt_tail",
    )(x1, gcross, wcq, kt, vb, wco, gffn, wup, wdown, gfinal)


def _sample_mixer_kernel(x_ref, gmix_ref, win_ref, sinkcol_ref, ck_ref, cv_ref, st_ref, wpool_ref, pscale_ref,
                         wout_ref, gcross_ref, wcq_ref,
                         x1_ref, qc_ref, wk_ref, wv_ref, pool_ref,
                         kc_scr, vc_scr, wcq_scr, u_scr, *, nbatch, sub, nseq, past_len):
    @pl.when(pl.program_id(0) == 0)
    def _():
        wcq_scr[...] = wcq_ref[...].astype(_BF)

    rows = sub * nseq
    nkeys = 2 * WINDOW
    gmix, gcross = gmix_ref[...], gcross_ref[...]
    sinkcol = sinkcol_ref[...][None]

    t = lax.broadcasted_iota(jnp.int32, (nseq, nkeys), 0)
    c = lax.broadcasted_iota(jnp.int32, (nseq, nkeys), 1)
    dist = jnp.where(c < WINDOW, WINDOW + t - c, t - (c - WINDOW))
    valid = (dist >= 0) & (dist <= WINDOW) & (c < WINDOW + nseq)
    distf = dist.astype(_F32)
    bias = jnp.concatenate([jnp.where(valid, -_slope(h) * distf, NEG_INF) for h in range(N_HEADS)], axis=0)[None]
    lane = lax.broadcasted_iota(jnp.int32, (rows, 2 * HEAD_DIM), 1)
    lo = lane < HEAD_DIM
    tpos = past_len + lax.broadcasted_iota(jnp.int32, (nseq, POOL_GROUP_WIDTH), 0)
    zpad = jnp.zeros((sub, nkeys - WINDOW, KV_WIDTH), _BF)

    def stream(b0):
        bs = slice(b0, b0 + sub)
        rs = slice(b0 * nseq, (b0 + sub) * nseq)
        x = x_ref[rs, :]
        proj = _dot(_rms(x, gmix).astype(_BF), win_ref[...])
        yield
        k_new = proj[:, K_OFF:K_OFF + KV_WIDTH].reshape(sub, nseq, KV_WIDTH)
        v_new = proj[:, V_OFF:V_OFF + KV_WIDTH].reshape(sub, nseq, KV_WIDTH)
        ck = jnp.swapaxes(ck_ref[bs], 1, 2)
        cv = jnp.swapaxes(cv_ref[bs], 1, 2)
        wk_ref[bs, :WINDOW - nseq, :] = ck[:, nseq:, :]
        wk_ref[bs, WINDOW - nseq:, :] = k_new
        wv_ref[bs, :WINDOW - nseq, :] = cv[:, nseq:, :]
        wv_ref[bs, WINDOW - nseq:, :] = v_new
        kc_scr[bs, WINDOW:, :] = zpad
        vc_scr[bs, WINDOW:, :] = zpad
        kc_scr[bs, :WINDOW, :] = ck.astype(_BF)
        vc_scr[bs, :WINDOW, :] = cv.astype(_BF)
        kc_scr[bs, WINDOW:WINDOW + nseq, :] = k_new.astype(_BF)
        vc_scr[bs, WINDOW:WINDOW + nseq, :] = v_new.astype(_BF)
        qsc = proj[:, :ATTN_WIDTH] * Q_SCALE
        q_lo, q_hi = [], []
        for p in range(GROUP):
            slab = qsc[:, p * 128:(p + 1) * 128]
            q_lo.append(jnp.where(lo, slab, 0.0).reshape(sub, nseq, 128))
            q_hi.append(jnp.where(lo, 0.0, slab).reshape(sub, nseq, 128))
        qm = jnp.concatenate(q_lo + q_hi, axis=1).astype(_BF)
        sc = jnp.einsum('bqd,bkd->bqk', qm, kc_scr[bs], preferred_element_type=_F32) + bias
        yield
        pr = _softmax_with_sink(sc, sinkcol).astype(_BF)
        o = jnp.einsum('bqk,bkd->bqd', pr, vc_scr[bs], preferred_element_type=_F32)
        yield
        attn = []
        for p in range(GROUP):
            o_lo = o[:, p * nseq:(p + 1) * nseq, :].reshape(rows, 128)
            o_hi = o[:, (p + GROUP) * nseq:(p + GROUP + 1) * nseq, :].reshape(rows, 128)
            attn.append(jnp.where(lo, o_lo, o_hi).astype(_BF))
        ext = POOL_PAD + nseq
        seq_rows = lambda r: pl.ds(b0 * ext + r, sub, stride=ext)
        pooled = []
        for g, w in enumerate(POOL_WINDOWS):
            c0 = g * POOL_GROUP_WIDTH
            u_scr[g, seq_rows(0), :] = jnp.zeros((sub, POOL_GROUP_WIDTH), _F32)
            for r in range(POOL_HIST):
                u_scr[g, seq_rows(POOL_PAD - POOL_HIST + r), :] = st_ref[r, bs, c0:c0 + POOL_GROUP_WIDTH]
            for i in range(sub):
                u_scr[g, (b0 + i) * ext + POOL_PAD:(b0 + i + 1) * ext, :] = (
                    proj[i * nseq:(i + 1) * nseq, U_OFF + c0:U_OFF + c0 + POOL_GROUP_WIDTH])
            for r in range(POOL_HIST):
                pool_ref[r, bs, c0:c0 + POOL_GROUP_WIDTH] = u_scr[g, seq_rows(ext - POOL_HIST + r), :]
            ug = u_scr[g, b0 * ext:(b0 + sub) * ext, :].reshape(sub, ext, POOL_GROUP_WIDTH)
            sw = _window_sums(ug, 1)[w][:, POOL_PAD:, :]
            cnt = jnp.minimum(tpos + 1, w).astype(_F32)
            d = (sw / cnt[None] - ug[:, POOL_PAD:, :]).reshape(rows, POOL_GROUP_WIDTH).astype(_BF)
            y = _dot(d, wpool_ref[g]) * pscale_ref[:, c0:c0 + POOL_GROUP_WIDTH]
            pooled.append(y.astype(_BF))
        yield
        x1 = x + _dot(jnp.concatenate(attn + pooled, axis=1), wout_ref[...])
        x1_ref[rs, :] = x1
        yield
        qc_ref[rs, :] = _dot(_rms(x1, gcross).astype(_BF), wcq_scr[...]) * CQ_SCALE

    _interleave([stream(b0) for b0 in range(0, nbatch, sub)], SAMPLE_MIXER_SKEW)


def _sample_mixer(x, gmix, win, sinkcol, ck, cv, st, wpool, pscale, wout, gcross, wcq, nbatch, sub, nseq,
                  past_len):
    R, D = x.shape
    rows = nbatch * nseq
    const = lambda *shape: pl.BlockSpec(shape, lambda i: (0,) * len(shape))
    return pl.pallas_call(
        functools.partial(_sample_mixer_kernel, nbatch=nbatch, sub=sub, nseq=nseq, past_len=past_len),
        grid=(R // rows,),
        in_specs=[
            pl.BlockSpec((rows, D), lambda i: (i, 0)),
            const(1, D),
            const(D, IN_WIDTH),
            const(N_HEADS * nseq, 1),
            pl.BlockSpec((nbatch, WINDOW, KV_WIDTH), lambda i: (i, 0, 0)),
            pl.BlockSpec((nbatch, WINDOW, KV_WIDTH), lambda i: (i, 0, 0)),
            pl.BlockSpec((POOL_HIST, nbatch, POOL_WIDTH), lambda i: (0, i, 0)),
            const(len(POOL_WINDOWS), POOL_GROUP_WIDTH, POOL_GROUP_WIDTH),
            const(1, POOL_WIDTH),
            const(D, D),
            const(1, D),
            const(D, D),
        ],
        out_specs=[
            pl.BlockSpec((rows, D), lambda i: (i, 0)),
            pl.BlockSpec((rows, D), lambda i: (i, 0)),
            pl.BlockSpec((nbatch, WINDOW, KV_WIDTH), lambda i: (i, 0, 0)),
            pl.BlockSpec((nbatch, WINDOW, KV_WIDTH), lambda i: (i, 0, 0)),
            pl.BlockSpec((POOL_HIST, nbatch, POOL_WIDTH), lambda i: (0, i, 0)),
        ],
        out_shape=[
            jax.ShapeDtypeStruct((R, D), _F32),
            jax.ShapeDtypeStruct((R, D), _F32),
            jax.ShapeDtypeStruct((R // nseq, WINDOW, KV_WIDTH), _F32),
            jax.ShapeDtypeStruct((R // nseq, WINDOW, KV_WIDTH), _F32),
            jax.ShapeDtypeStruct((POOL_HIST, R // nseq, POOL_WIDTH), _F32),
        ],
        scratch_shapes=[
            pltpu.VMEM((nbatch, 2 * WINDOW, KV_WIDTH), _BF),
            pltpu.VMEM((nbatch, 2 * WINDOW, KV_WIDTH), _BF),
            pltpu.VMEM((D, D), _BF),
            pltpu.VMEM((len(POOL_WINDOWS), nbatch * (POOL_PAD + nseq), POOL_GROUP_WIDTH), _F32),
        ],
        compiler_params=pltpu.CompilerParams(
            dimension_semantics=("arbitrary",), vmem_limit_bytes=VMEM_LIMIT_BYTES),
        name="sample_mixer",
    )(x, gmix, win, sinkcol, ck, cv, st, wpool, pscale, wout, gcross, wcq)


def _mem_cache_rows(cache):
    nb = cache.shape[0]
    c = cache.reshape(nb, N_MEM, N_CROSS_HEADS, CROSS_HEAD_DIM // 128, 128)
    return c.transpose(0, 1, 3, 2, 4).reshape(nb, N_MEM * MEM_ROWS, 128)


def _load_mem_head(ref, b, h):
    halves = [ref[b, pl.ds(half * N_CROSS_HEADS + h, N_MEM, stride=MEM_ROWS), :]
              for half in range(CROSS_HEAD_DIM // 128)]
    return jnp.concatenate(halves, axis=1)


def _sample_tail_kernel(x1_ref, o_ref, wco_ref, gffn_ref, wup_ref, wdown_ref, gfinal_ref, y_ref, *, tile, sub):
    gffn, gfinal = gffn_ref[...], gfinal_ref[...]

    def stream(r0):
        x2 = x1_ref[r0:r0 + sub, :] + _dot(o_ref[r0:r0 + sub, :].astype(_BF), wco_ref[...])
        yield

        def store(y):
            y_ref[r0:r0 + sub, :] = y

        yield from _ffn_final_stages(x2, gffn, wup_ref, wdown_ref, gfinal, store)

    _interleave([stream(r0) for r0 in range(0, tile, sub)], TAIL_SKEW)


def _sample_tail(x1, o, wco, gffn, wup, wdown, gfinal, tile, sub):
    R, D = x1.shape
    const = lambda *shape: _single(shape, lambda i: (0,) * len(shape))
    return pl.pallas_call(
        functools.partial(_sample_tail_kernel, tile=tile, sub=sub),
        grid=(R // tile,),
        in_specs=[
            pl.BlockSpec((tile, D), lambda i: (i, 0)),
            pl.BlockSpec((tile, D), lambda i: (i, 0)),
            const(D, D),
            const(1, D),
            const(D, D_FF),
            const(D_FF, D),
            const(1, D),
        ],
        out_specs=pl.BlockSpec((tile, D), lambda i: (i, 0)),
        out_shape=jax.ShapeDtypeStruct((R, D), _F32),
        compiler_params=pltpu.CompilerParams(
            dimension_semantics=("arbitrary",), vmem_limit_bytes=VMEM_LIMIT_BYTES),
        name="sample_tail",
    )(x1, o, wco, gffn, wup, wdown, gfinal)


PROMPT_TILE = 512
TAIL_TILE = 1024
TAIL_SUB = 256
TAIL_SKEW = 5
SAMPLE_MIXER_BATCH = 32
SAMPLE_MIXER_SUB = 16
SAMPLE_MIXER_SKEW = 3


def kernel(x_prompt, x_sample, cache_win_k, cache_win_v, state_pool, cache_mem_k, cache_mem_v, mem_prompt,
           g_mix, w_in, attn_sinks, w_pool, pool_scale, w_out, g_cross, g_mem, w_cq, w_ck, w_cv, w_co,
           g_ffn, w_up, w_down, g_final):
    depth = g_mix.shape[0]
    assert depth == 1, "one layer per step"
    B, S, D = x_prompt.shape
    DB, T, _ = x_sample.shape
    past_len = PAST_LEN
    l = 0

    win = jnp.concatenate([_pair_heads(w_in[l][:, :ATTN_WIDTH], 1), w_in[l][:, ATTN_WIDTH:]], axis=1).astype(_BF)
    wout = jnp.concatenate([_pair_heads(w_out[l][:ATTN_WIDTH, :], 0), w_out[l][ATTN_WIDTH:, :]], axis=0).astype(_BF)
    wpool = w_pool[l].astype(_BF)
    gmix, gcross, gmem, gffn = (g[l].reshape(1, D) for g in (g_mix, g_cross, g_mem, g_ffn))
    gfinal = g_final.reshape(1, D)
    pscale = pool_scale[l].reshape(1, POOL_WIDTH)
    sinks = attn_sinks[l]

    xs = x_sample.reshape(DB * T, D)
    ck = cache_win_k[l].transpose(0, 2, 3, 1).reshape(DB, KV_WIDTH, WINDOW)
    cv = cache_win_v[l].transpose(0, 2, 3, 1).reshape(DB, KV_WIDTH, WINDOW)
    st = state_pool[l].transpose(1, 0, 2)
    sinkcol = jnp.repeat(sinks, T).reshape(N_HEADS * T, 1)
    x1s, qc, wk_s, wv_s, pool_s = _sample_mixer(xs, gmix, win, sinkcol, ck, cv, st, wpool, pscale, wout,
                                                gcross, w_cq[l], SAMPLE_MIXER_BATCH, SAMPLE_MIXER_SUB, T, past_len)

    mk = _mem_cache_rows(cache_mem_k[l])
    mv = _mem_cache_rows(cache_mem_v[l])
    x1p, klast, vlast, plast, wcq, wco, wup, wdown, o_s = _prompt_mixer(
        x_prompt, gmix, win, sinks, wpool, pscale, wout, (w_cq[l], w_co[l], w_up[l], w_down[l]), qc, mk, mv,
        PROMPT_TILE)
    mem_k, mem_v, mem_kt, mem_vb = _mem_kv(mem_prompt, gmem, w_ck[l], w_cv[l])
    y_prompt = _prompt_tail(x1p, gcross, wcq, mem_kt, mem_vb, wco, gffn, wup, wdown, gfinal, TAIL_TILE, TAIL_SUB)

    y_sample = _sample_tail(x1s, o_s, wco, gffn, wup, wdown, gfinal, TAIL_TILE, TAIL_SUB).reshape(DB, T, D)

    return (
        y_prompt,
        y_sample,
        klast.reshape(B, N_KV_HEADS, HEAD_DIM, WINDOW).transpose(0, 3, 1, 2)[None],
        vlast.reshape(B, N_KV_HEADS, HEAD_DIM, WINDOW).transpose(0, 3, 1, 2)[None],
        plast[:, POOL_PAD - POOL_HIST:, :][None],
        _mem_cache_unrows(mem_k)[None],
        _mem_cache_unrows(mem_v)[None],
        wk_s.reshape(1, DB, WINDOW, N_KV_HEADS, HEAD_DIM),
        wv_s.reshape(1, DB, WINDOW, N_KV_HEADS, HEAD_DIM),
        pool_s.transpose(1, 0, 2)[None],
    )
```

```python
import functools

import jax
import jax.numpy as jnp
from jax import lax
from jax.experimental import pallas as pl
from jax.experimental.pallas import tpu as pltpu

D_MODEL = 1024
PAST_LEN = 16384
HEAD_DIM = 64
N_HEADS = 8
N_KV_HEADS = 2
GROUP = N_HEADS // N_KV_HEADS
ATTN_WIDTH = N_HEADS * HEAD_DIM
KV_WIDTH = N_KV_HEADS * HEAD_DIM
WINDOW = 128
BLOCK = WINDOW
POOL_WIDTH = D_MODEL - ATTN_WIDTH
POOL_WINDOWS = (2, 4, 8, 16)
POOL_GROUP_WIDTH = 128
POOL_HIST = 15
POOL_PAD = 16
IN_WIDTH = ATTN_WIDTH + 2 * KV_WIDTH + POOL_WIDTH
N_MEM = 256
N_CROSS_HEADS = 4
CROSS_HEAD_DIM = 256
MEM_ROWS = N_CROSS_HEADS * (CROSS_HEAD_DIM // 128)
D_FF = 4 * D_MODEL
FF_CHUNK = 1024
RMS_EPS = 1e-5
NEG_INF = -1e30
Q_SCALE = HEAD_DIM ** -0.5
CQ_SCALE = CROSS_HEAD_DIM ** -0.5
K_OFF = ATTN_WIDTH
V_OFF = ATTN_WIDTH + KV_WIDTH
U_OFF = ATTN_WIDTH + 2 * KV_WIDTH


def _pair_heads(w, axis):
    shape = w.shape
    split = shape[:axis] + (N_KV_HEADS, GROUP, HEAD_DIM) + shape[axis + 1:]
    return jnp.swapaxes(w.reshape(split), axis, axis + 1).reshape(shape)


VMEM_LIMIT_BYTES = 56 * 1024 * 1024

_BF = jnp.bfloat16
_F32 = jnp.float32


def _slope(h):
    return 2.0 ** (-8.0 * (h + 1) / N_HEADS)


def _dot(a, b):
    return jnp.dot(a, b, preferred_element_type=_F32)


def _rms(x, g):
    ms = jnp.mean(x * x, axis=-1, keepdims=True)
    return x * lax.rsqrt(ms + RMS_EPS) * g


def _softmax_with_sink(s, sink):
    m = jnp.maximum(jnp.max(s, axis=-1, keepdims=True), sink)
    e = jnp.exp(s - m)
    den = jnp.sum(e, axis=-1, keepdims=True) + jnp.exp(sink - m)
    return e * (1.0 / den)


def _softmax_with_sink_parts(s, sink):
    m = jnp.maximum(jnp.max(s, axis=-1, keepdims=True), sink)
    e = jnp.exp(s - m)
    den = jnp.sum(e, axis=-1, keepdims=True) + jnp.exp(sink - m)
    return e, 1.0 / den


def _softmax(s):
    m = jnp.max(s, axis=-1, keepdims=True)
    e = jnp.exp(s - m)
    return e * (1.0 / jnp.sum(e, axis=-1, keepdims=True))


def _window_sums(u_ext, axis):
    out = {}
    s = u_ext
    w = 1
    while w < max(POOL_WINDOWS):
        s = s + pltpu.roll(s, w, axis)
        w *= 2
        out[w] = s
    return out


def _interleave(streams, skew):
    pending = list(streams)
    live = []
    rnd = 0
    while live or pending:
        if pending and rnd % skew == 0:
            live.append(pending.pop(0))
        for g in list(live):
            try:
                next(g)
            except StopIteration:
                live.remove(g)
        rnd += 1


def _ffn_final_stages(x2, gffn, wup_ref, wdown_ref, gfinal, store):
    hn = _rms(x2, gffn).astype(_BF)
    yield
    acc = x2
    for c in range(D_FF // FF_CHUNK):
        hc = _dot(hn, wup_ref[:, c * FF_CHUNK:(c + 1) * FF_CHUNK])
        yield
        hc = jnp.maximum(hc, 0.0)
        hc = (hc * hc).astype(_BF)
        acc = acc + _dot(hc, wdown_ref[c * FF_CHUNK:(c + 1) * FF_CHUNK, :])
        yield
    store(_rms(acc, gfinal))


def _prompt_mixer_kernel(x_ref, gmix_ref, win_ref, sink_ref, wpool_ref, pscale_ref, wout_ref,
                         wcq_ref, wco_ref, wup_ref, wdown_ref, qc_ref, mk_ref, mv_ref,
                         x1_ref, klast_ref, vlast_ref, plast_ref, wcq_bf_ref, wco_bf_ref, wup_bf_ref, wdown_bf_ref,
                         os_ref, kt_scr, v_scr, u_scr, bias_scr, inv_scr, *, tile, nseq):
    b = pl.program_id(0)
    s = pl.program_id(1)
    nb = tile // BLOCK

    for src, dst in ((wcq_ref, wcq_bf_ref), (wco_ref, wco_bf_ref), (wup_ref, wup_bf_ref), (wdown_ref, wdown_bf_ref)):
        dst[...] = src[...].astype(_BF)

    @pl.when((b == 0) & (s == 0))
    def _():
        qi = lax.broadcasted_iota(jnp.int32, (BLOCK, 2 * BLOCK), 0)
        kc = lax.broadcasted_iota(jnp.int32, (BLOCK, 2 * BLOCK), 1)
        dist = qi + BLOCK - kc
        valid = (dist >= 0) & (dist <= WINDOW)
        valid_first = valid & (kc >= BLOCK)
        distf = dist.astype(_F32)
        for h in range(N_HEADS):
            ali = -_slope(h) * distf
            bias_scr[0, h] = jnp.where(valid, ali, NEG_INF)
            bias_scr[1, h] = jnp.where(valid_first, ali, NEG_INF)
        prow = lax.broadcasted_iota(jnp.int32, (BLOCK, POOL_GROUP_WIDTH), 0)
        for g, w in enumerate(POOL_WINDOWS):
            inv_scr[0, g] = jnp.full((BLOCK, POOL_GROUP_WIDTH), 1.0 / w, _F32)
            inv_scr[1, g] = 1.0 / jnp.minimum(prow + 1, w).astype(_F32)

    @pl.when(s == 0)
    def _():
        kt_scr[...] = jnp.zeros((KV_WIDTH, BLOCK), _BF)
        v_scr[...] = jnp.zeros((BLOCK, KV_WIDTH), _BF)
        u_scr[...] = jnp.zeros((POOL_PAD, POOL_WIDTH), _F32)

    lane = lax.broadcasted_iota(jnp.int32, (BLOCK, 2 * HEAD_DIM), 1)
    lo = lane < HEAD_DIM
    gmix = gmix_ref[...]
    zeros_kt = jnp.zeros((HEAD_DIM, 2 * BLOCK), _BF)

    def project(j):
        x = x_ref[0, j * BLOCK:(j + 1) * BLOCK, :]
        proj = _dot(_rms(x, gmix).astype(_BF), win_ref[...])
        k = proj[:, K_OFF:K_OFF + KV_WIDTH]
        v = proj[:, V_OFF:V_OFF + KV_WIDTH]
        u = proj[:, U_OFF:]
        kt = k.T
        if j == nb - 1:
            klast_ref[0] = kt
            vlast_ref[0] = v.T
            plast_ref[0] = u[BLOCK - POOL_PAD:]
        return dict(x=x, q=(proj[:, :ATTN_WIDTH] * Q_SCALE).astype(_BF), kt=kt.astype(_BF), v=v.astype(_BF), u=u)

    def pool(j, u_hist, u):
        first = ((s == 0) & (j == 0)).astype(jnp.int32) if j == 0 else 0
        ys = []
        for g, w in enumerate(POOL_WINDOWS):
            c0 = g * POOL_GROUP_WIDTH
            u_ext = jnp.concatenate([u_hist[:, c0:c0 + POOL_GROUP_WIDTH], u[:, c0:c0 + POOL_GROUP_WIDTH]], axis=0)
            sw = _window_sums(u_ext, 0)[w][POOL_PAD:]
            d = (sw * inv_scr[first, g] - u_ext[POOL_PAD:]).astype(_BF)
            y = _dot(d, wpool_ref[g]) * pscale_ref[:, c0:c0 + POOL_GROUP_WIDTH]
            ys.append(y.astype(_BF))
        return ys

    def kv_operands(kt_prev, v_prev, blk):
        kt2 = jnp.concatenate([kt_prev, blk["kt"]], axis=1)
        v2 = jnp.concatenate([v_prev, blk["v"]], axis=0)
        kt_pair = jnp.concatenate([jnp.concatenate([kt2[:HEAD_DIM], zeros_kt], axis=0),
                                   jnp.concatenate([zeros_kt, kt2[HEAD_DIM:]], axis=0)], axis=1)
        lane2 = lax.broadcasted_iota(jnp.int32, v2.shape, 1)
        zero = jnp.zeros_like(v2)
        v_pair = jnp.concatenate([jnp.where(lane2 < HEAD_DIM, v2, zero), jnp.where(lane2 < HEAD_DIM, zero, v2)],
                                 axis=0)
        return kt_pair, v_pair

    def wave_scores(j, blk, kt_pair, pairs):
        first = ((s == 0) & (j == 0)).astype(jnp.int32) if j == 0 else 0
        scores = []
        for p in pairs:
            sc = _dot(blk["q"][:, p * 128:(p + 1) * 128], kt_pair)
            scores.append(sc[:, :2 * BLOCK] + bias_scr[first, p])
            scores.append(sc[:, 2 * BLOCK:] + bias_scr[first, p + GROUP])
        return scores

    def wave_values(v_pair, pairs, scores):
        slabs = []
        for i, p in enumerate(pairs):
            es, invs = [], []
            for half, h in ((0, p), (1, p + GROUP)):
                e, inv = _softmax_with_sink_parts(scores[2 * i + half], sink_ref[h])
                es.append(e.astype(_BF))
                invs.append(inv)
            o = _dot(jnp.concatenate(es, axis=1), v_pair)
            slabs.append((o * jnp.where(lo, invs[0], invs[1])).astype(_BF))
        return slabs

    def output(j, blk, slabs):
        cat = jnp.concatenate(slabs, axis=1)
        x1_ref[0, j * BLOCK:(j + 1) * BLOCK, :] = blk["x"] + _dot(cat, wout_ref[...])

    heads = [slice(h * CROSS_HEAD_DIM, (h + 1) * CROSS_HEAD_DIM) for h in range(N_CROSS_HEADS)]

    def mem_scores(i):
        q = qc_ref[i * nseq:(i + 1) * nseq, :].astype(_BF)
        return [lax.dot_general(q[:, c], _load_mem_head(mk_ref, i, h).astype(_BF), (((1,), (1,)), ((), ())),
                                preferred_element_type=_F32) for h, c in enumerate(heads)]

    def mem_values(i, scores):
        outs = [_dot(_softmax(sc).astype(_BF), _load_mem_head(mv_ref, i, h).astype(_BF))
                for h, sc in enumerate(scores)]
        os_ref[i * nseq:(i + 1) * nseq, :] = jnp.concatenate(outs, axis=1)

    kt_prev, v_prev, u_hist = kt_scr[...], v_scr[...], u_scr[...]
    blk = project(0)
    done = None
    for j in range(nb):
        kt_pair, v_pair = kv_operands(kt_prev, v_prev, blk)
        sc0 = wave_scores(j, blk, kt_pair, (0, 1))
        nxt = project(j + 1) if j + 1 < nb else None
        msc = mem_scores(j)
        sc1 = wave_scores(j, blk, kt_pair, (2, 3))
        pooled = pool(j, u_hist, blk["u"])
        at0 = wave_values(v_pair, (0, 1), sc0)
        if done is not None:
            output(*done)
        mem_values(j, msc)
        at1 = wave_values(v_pair, (2, 3), sc1)
        done = (j, blk, at0 + at1 + pooled)
        kt_prev, v_prev, u_hist = blk["kt"], blk["v"], blk["u"][BLOCK - POOL_PAD:]
        blk = nxt
    output(*done)
    kt_scr[...] = kt_prev
    v_scr[...] = v_prev
    u_scr[...] = u_hist


def _prompt_mixer(x, gmix, win, sinks, wpool, pscale, wout, tail_weights, qc, mk, mv, tile):
    B, S, D = x.shape
    ns = S // tile
    nsteps = B * ns
    nmem = tile // BLOCK
    nseq = qc.shape[0] // mk.shape[0]
    assert mk.shape[0] == nsteps * nmem
    const = lambda *shape: pl.BlockSpec(shape, lambda b, s: (0,) * len(shape))
    chunk = lambda w: pl.BlockSpec((w.shape[0] // nsteps, w.shape[1]), lambda b, s: (b * ns + s, 0))
    assert all(w.shape[0] % (16 * nsteps) == 0 for w in tail_weights)
    mem_rows = pl.BlockSpec((nmem * nseq, D), lambda b, s: (b * ns + s, 0))
    mem_cache = pl.BlockSpec((nmem, N_MEM * MEM_ROWS, 128), lambda b, s: (b * ns + s, 0, 0))
    return pl.pallas_call(
        functools.partial(_prompt_mixer_kernel, tile=tile, nseq=nseq),
        grid=(B, ns),
        in_specs=[
            pl.BlockSpec((1, tile, D), lambda b, s: (b, s, 0)),
            const(1, D),
            const(D, IN_WIDTH),
            pl.BlockSpec(memory_space=pltpu.SMEM),
            const(len(POOL_WINDOWS), POOL_GROUP_WIDTH, POOL_GROUP_WIDTH),
            const(1, POOL_WIDTH),
            const(D, D),
        ] + [chunk(w) for w in tail_weights] + [mem_rows, mem_cache, mem_cache],
        out_specs=[
            pl.BlockSpec((1, tile, D), lambda b, s: (b, s, 0)),
            pl.BlockSpec((1, BLOCK, KV_WIDTH), lambda b, s: (b, 0, 0)),
            pl.BlockSpec((1, BLOCK, KV_WIDTH), lambda b, s: (b, 0, 0)),
            pl.BlockSpec((1, POOL_PAD, POOL_WIDTH), lambda b, s: (b, 0, 0)),
        ] + [chunk(w) for w in tail_weights] + [mem_rows],
        out_shape=[
            jax.ShapeDtypeStruct((B, S, D), _F32),
            jax.ShapeDtypeStruct((B, BLOCK, KV_WIDTH), _F32),
            jax.ShapeDtypeStruct((B, BLOCK, KV_WIDTH), _F32),
            jax.ShapeDtypeStruct((B, POOL_PAD, POOL_WIDTH), _F32),
        ] + [jax.ShapeDtypeStruct(w.shape, _BF) for w in tail_weights] + [jax.ShapeDtypeStruct(qc.shape, _F32)],
        scratch_shapes=[
            pltpu.VMEM((KV_WIDTH, BLOCK), _BF),
            pltpu.VMEM((BLOCK, KV_WIDTH), _BF),
            pltpu.VMEM((POOL_PAD, POOL_WIDTH), _F32),
            pltpu.VMEM((2, N_HEADS, BLOCK, 2 * BLOCK), _F32),
            pltpu.VMEM((2, len(POOL_WINDOWS), BLOCK, POOL_GROUP_WIDTH), _F32),
        ],
        compiler_params=pltpu.CompilerParams(
            dimension_semantics=("arbitrary", "arbitrary"), vmem_limit_bytes=VMEM_LIMIT_BYTES),
        name="prompt_mixer",
    )(x, gmix, win, sinks, wpool, pscale, wout, *tail_weights, qc, mk, mv)


def _mem_kv_kernel(mem_ref, gmem_ref, wck_ref, wcv_ref, k_ref, v_ref, kt_ref, vb_ref, wck_scr, wcv_scr):
    @pl.when(pl.program_id(0) == 0)
    def _():
        wck_scr[...] = wck_ref[...].astype(_BF)
        wcv_scr[...] = wcv_ref[...].astype(_BF)

    hm = _rms(mem_ref[0], gmem_ref[...]).astype(_BF)
    k = _dot(hm, wck_scr[...])
    v = _dot(hm, wcv_scr[...])
    vb_ref[0] = v.astype(_BF)
    for h in range(N_CROSS_HEADS):
        kt_ref[0, h] = k[:, h * CROSS_HEAD_DIM:(h + 1) * CROSS_HEAD_DIM].T.astype(_BF)
        for half in range(CROSS_HEAD_DIM // 128):
            c0 = h * CROSS_HEAD_DIM + half * 128
            rows = pl.ds(half * N_CROSS_HEADS + h, N_MEM, stride=MEM_ROWS)
            k_ref[0, rows, :] = k[:, c0:c0 + 128]
            v_ref[0, rows, :] = v[:, c0:c0 + 128]


def _mem_kv(mem, gmem, wck, wcv):
    B, M, D = mem.shape
    const = lambda *shape: pl.BlockSpec(shape, lambda b: (0,) * len(shape))
    return pl.pallas_call(
        _mem_kv_kernel,
        grid=(B,),
        in_specs=[pl.BlockSpec((1, M, D), lambda b: (b, 0, 0)), const(1, D), const(D, D), const(D, D)],
        out_specs=[
            pl.BlockSpec((1, M * MEM_ROWS, 128), lambda b: (b, 0, 0)),
            pl.BlockSpec((1, M * MEM_ROWS, 128), lambda b: (b, 0, 0)),
            pl.BlockSpec((1, N_CROSS_HEADS, CROSS_HEAD_DIM, M), lambda b: (b, 0, 0, 0)),
            pl.BlockSpec((1, M, D), lambda b: (b, 0, 0)),
        ],
        out_shape=[
            jax.ShapeDtypeStruct((B, M * MEM_ROWS, 128), _F32),
            jax.ShapeDtypeStruct((B, M * MEM_ROWS, 128), _F32),
            jax.ShapeDtypeStruct((B, N_CROSS_HEADS, CROSS_HEAD_DIM, M), _BF),
            jax.ShapeDtypeStruct((B, M, D), _BF),
        ],
        scratch_shapes=[pltpu.VMEM((D, D), _BF), pltpu.VMEM((D, D), _BF)],
        compiler_params=pltpu.CompilerParams(
            dimension_semantics=("arbitrary",), vmem_limit_bytes=VMEM_LIMIT_BYTES),
        name="prompt_mem_kv",
    )(mem, gmem, wck, wcv)


def _mem_cache_unrows(rows):
    nb = rows.shape[0]
    c = rows.reshape(nb, N_MEM, CROSS_HEAD_DIM // 128, N_CROSS_HEADS, 128)
    return c.transpose(0, 1, 3, 2, 4).reshape(nb, N_MEM, N_CROSS_HEADS, CROSS_HEAD_DIM)


def _prompt_tail_kernel(x1_ref, gcross_ref, wcq_ref, kt_ref, vb_ref, wco_ref, gffn_ref, wup_ref, wdown_ref,
                        gfinal_ref, y_ref, *, tile, sub):
    gcross, gffn, gfinal = gcross_ref[...], gffn_ref[...], gfinal_ref[...]

    def stream(r0):
        x1 = x1_ref[0, r0:r0 + sub, :]
        hn = _rms(x1, gcross).astype(_BF)
        yield
        q = (_dot(hn, wcq_ref[...]) * CQ_SCALE).astype(_BF)
        yield
        heads = [slice(h * CROSS_HEAD_DIM, (h + 1) * CROSS_HEAD_DIM) for h in range(N_CROSS_HEADS)]
        scores = [_dot(q[:, c], kt_ref[0, h]) for h, c in enumerate(heads)]
        yield
        outs = [_dot(_softmax(sc).astype(_BF), vb_ref[0, :, c]).astype(_BF) for sc, c in zip(scores, heads)]
        yield
        x2 = x1 + _dot(jnp.concatenate(outs, axis=1), wco_ref[...])
        yield

        def store(y):
            y_ref[0, r0:r0 + sub, :] = y

        yield from _ffn_final_stages(x2, gffn, wup_ref, wdown_ref, gfinal, store)

    _interleave([stream(r0) for r0 in range(0, tile, sub)], TAIL_SKEW)


def _single(shape, index_map):
    return pl.BlockSpec(shape, index_map, pipeline_mode=pl.Buffered(1))


def _prompt_tail(x1, gcross, wcq, kt, vb, wco, gffn, wup, wdown, gfinal, tile, sub):
    B, S, D = x1.shape
    const = lambda *shape: _single(shape, lambda b, s: (0,) * len(shape))
    return pl.pallas_call(
        functools.partial(_prompt_tail_kernel, tile=tile, sub=sub),
        grid=(B, S // tile),
        in_specs=[
            pl.BlockSpec((1, tile, D), lambda b, s: (b, s, 0)),
            const(1, D),
            const(D, D),
            pl.BlockSpec((1, N_CROSS_HEADS, CROSS_HEAD_DIM, N_MEM), lambda b, s: (b, 0, 0, 0)),
            pl.BlockSpec((1, N_MEM, D), lambda b, s: (b, 0, 0)),
            const(D, D),
            const(1, D),
            const(D, D_FF),
            const(D_FF, D),
            const(1, D),
        ],
        out_specs=pl.BlockSpec((1, tile, D), lambda b, s: (b, s, 0)),
        out_shape=jax.ShapeDtypeStruct((B, S, D), _F32),
        compiler_params=pltpu.CompilerParams(
            dimension_semantics=("arbitrary", "arbitrary"), vmem_limit_bytes=VMEM_LIMIT_BYTES),
        name="prompt_tail",
    )(x1, gcross, wcq, kt, vb, wco, gffn, wup, wdown, gfinal)


def _sample_mixer_kernel(x_ref, gmix_ref, win_ref, sinkcol_ref, ck_ref, cv_ref, st_ref, wpool_ref, pscale_ref,
                         wout_ref, gcross_ref, wcq_ref,
                         x1_ref, qc_ref, wk_ref, wv_ref, pool_ref,
                         kc_scr, vc_scr, wcq_scr, u_scr, *, nbatch, sub, nseq, past_len):
    @pl.when(pl.program_id(0) == 0)
    def _():
        wcq_scr[...] = wcq_ref[...].astype(_BF)

    rows = sub * nseq
    nkeys = 2 * WINDOW
    gmix, gcross = gmix_ref[...], gcross_ref[...]
    sinkcol = sinkcol_ref[...][None]

    t = lax.broadcasted_iota(jnp.int32, (nseq, nkeys), 0)
    c = lax.broadcasted_iota(jnp.int32, (nseq, nkeys), 1)
    dist = jnp.where(c < WINDOW, WINDOW + t - c, t - (c - WINDOW))
    valid = (dist >= 0) & (dist <= WINDOW) & (c < WINDOW + nseq)
    distf = dist.astype(_F32)
    bias = jnp.concatenate([jnp.where(valid, -_slope(h) * distf, NEG_INF) for h in range(N_HEADS)], axis=0)[None]
    lane = lax.broadcasted_iota(jnp.int32, (rows, 2 * HEAD_DIM), 1)
    lo = lane < HEAD_DIM
    tpos = past_len + lax.broadcasted_iota(jnp.int32, (nseq, POOL_GROUP_WIDTH), 0)
    zpad = jnp.zeros((sub, nkeys - WINDOW, KV_WIDTH), _BF)

    def stream(b0):
        bs = slice(b0, b0 + sub)
        rs = slice(b0 * nseq, (b0 + sub) * nseq)
        x = x_ref[rs, :]
        proj = _dot(_rms(x, gmix).astype(_BF), win_ref[...])
        yield
        k_new = proj[:, K_OFF:K_OFF + KV_WIDTH].reshape(sub, nseq, KV_WIDTH)
        v_new = proj[:, V_OFF:V_OFF + KV_WIDTH].reshape(sub, nseq, KV_WIDTH)
        ck = jnp.swapaxes(ck_ref[bs], 1, 2)
        cv = jnp.swapaxes(cv_ref[bs], 1, 2)
        wk_ref[bs, :WINDOW - nseq, :] = ck[:, nseq:, :]
        wk_ref[bs, WINDOW - nseq:, :] = k_new
        wv_ref[bs, :WINDOW - nseq, :] = cv[:, nseq:, :]
        wv_ref[bs, WINDOW - nseq:, :] = v_new
        kc_scr[bs, WINDOW:, :] = zpad
        vc_scr[bs, WINDOW:, :] = zpad
        kc_scr[bs, :WINDOW, :] = ck.astype(_BF)
        vc_scr[bs, :WINDOW, :] = cv.astype(_BF)
        kc_scr[bs, WINDOW:WINDOW + nseq, :] = k_new.astype(_BF)
        vc_scr[bs, WINDOW:WINDOW + nseq, :] = v_new.astype(_BF)
        qsc = proj[:, :ATTN_WIDTH] * Q_SCALE
        q_lo, q_hi = [], []
        for p in range(GROUP):
            slab = qsc[:, p * 128:(p + 1) * 128]
            q_lo.append(jnp.where(lo, slab, 0.0).reshape(sub, nseq, 128))
            q_hi.append(jnp.where(lo, 0.0, slab).reshape(sub, nseq, 128))
        qm = jnp.concatenate(q_lo + q_hi, axis=1).astype(_BF)
        sc = jnp.einsum('bqd,bkd->bqk', qm, kc_scr[bs], preferred_element_type=_F32) + bias
        yield
        pr = _softmax_with_sink(sc, sinkcol).astype(_BF)
        o = jnp.einsum('bqk,bkd->bqd', pr, vc_scr[bs], preferred_element_type=_F32)
        yield
        attn = []
        for p in range(GROUP):
            o_lo = o[:, p * nseq:(p + 1) * nseq, :].reshape(rows, 128)
            o_hi = o[:, (p + GROUP) * nseq:(p + GROUP + 1) * nseq, :].reshape(rows, 128)
            attn.append(jnp.where(lo, o_lo, o_hi).astype(_BF))
        ext = POOL_PAD + nseq
        seq_rows = lambda r: pl.ds(b0 * ext + r, sub, stride=ext)
        pooled = []
        for g, w in enumerate(POOL_WINDOWS):
            c0 = g * POOL_GROUP_WIDTH
            u_scr[g, seq_rows(0), :] = jnp.zeros((sub, POOL_GROUP_WIDTH), _F32)
            for r in range(POOL_HIST):
                u_scr[g, seq_rows(POOL_PAD - POOL_HIST + r), :] = st_ref[r, bs, c0:c0 + POOL_GROUP_WIDTH]
            for i in range(sub):
                u_scr[g, (b0 + i) * ext + POOL_PAD:(b0 + i + 1) * ext, :] = (
                    proj[i * nseq:(i + 1) * nseq, U_OFF + c0:U_OFF + c0 + POOL_GROUP_WIDTH])
            for r in range(POOL_HIST):
                pool_ref[r, bs, c0:c0 + POOL_GROUP_WIDTH] = u_scr[g, seq_rows(ext - POOL_HIST + r), :]
            ug = u_scr[g, b0 * ext:(b0 + sub) * ext, :].reshape(sub, ext, POOL_GROUP_WIDTH)
            sw = _window_sums(ug, 1)[w][:, POOL_PAD:, :]
            cnt = jnp.minimum(tpos + 1, w).astype(_F32)
            d = (sw / cnt[None] - ug[:, POOL_PAD:, :]).reshape(rows, POOL_GROUP_WIDTH).astype(_BF)
            y = _dot(d, wpool_ref[g]) * pscale_ref[:, c0:c0 + POOL_GROUP_WIDTH]
            pooled.append(y.astype(_BF))
        yield
        x1 = x + _dot(jnp.concatenate(attn + pooled, axis=1), wout_ref[...])
        x1_ref[rs, :] = x1
        yield
        qc_ref[rs, :] = _dot(_rms(x1, gcross).astype(_BF), wcq_scr[...]) * CQ_SCALE

    _interleave([stream(b0) for b0 in range(0, nbatch, sub)], SAMPLE_MIXER_SKEW)


def _sample_mixer(x, gmix, win, sinkcol, ck, cv, st, wpool, pscale, wout, gcross, wcq, nbatch, sub, nseq,
                  past_len):
    R, D = x.shape
    rows = nbatch * nseq
    const = lambda *shape: pl.BlockSpec(shape, lambda i: (0,) * len(shape))
    return pl.pallas_call(
        functools.partial(_sample_mixer_kernel, nbatch=nbatch, sub=sub, nseq=nseq, past_len=past_len),
        grid=(R // rows,),
        in_specs=[
            pl.BlockSpec((rows, D), lambda i: (i, 0)),
            const(1, D),
            const(D, IN_WIDTH),
            const(N_HEADS * nseq, 1),
            pl.BlockSpec((nbatch, WINDOW, KV_WIDTH), lambda i: (i, 0, 0)),
            pl.BlockSpec((nbatch, WINDOW, KV_WIDTH), lambda i: (i, 0, 0)),
            pl.BlockSpec((POOL_HIST, nbatch, POOL_WIDTH), lambda i: (0, i, 0)),
            const(len(POOL_WINDOWS), POOL_GROUP_WIDTH, POOL_GROUP_WIDTH),
            const(1, POOL_WIDTH),
            const(D, D),
            const(1, D),
            const(D, D),
        ],
        out_specs=[
            pl.BlockSpec((rows, D), lambda i: (i, 0)),
            pl.BlockSpec((rows, D), lambda i: (i, 0)),
            pl.BlockSpec((nbatch, WINDOW, KV_WIDTH), lambda i: (i, 0, 0)),
            pl.BlockSpec((nbatch, WINDOW, KV_WIDTH), lambda i: (i, 0, 0)),
            pl.BlockSpec((POOL_HIST, nbatch, POOL_WIDTH), lambda i: (0, i, 0)),
        ],
        out_shape=[
            jax.ShapeDtypeStruct((R, D), _F32),
            jax.ShapeDtypeStruct((R, D), _F32),
            jax.ShapeDtypeStruct((R // nseq, WINDOW, KV_WIDTH), _F32),
            jax.ShapeDtypeStruct((R // nseq, WINDOW, KV_WIDTH), _F32),
            jax.ShapeDtypeStruct((POOL_HIST, R // nseq, POOL_WIDTH), _F32),
        ],
        scratch_shapes=[
            pltpu.VMEM((nbatch, 2 * WINDOW, KV_WIDTH), _BF),
            pltpu.VMEM((nbatch, 2 * WINDOW, KV_WIDTH), _BF),
            pltpu.VMEM((D, D), _BF),
            pltpu.VMEM((len(POOL_WINDOWS), nbatch * (POOL_PAD + nseq), POOL_GROUP_WIDTH), _F32),
        ],
        compiler_params=pltpu.CompilerParams(
            dimension_semantics=("arbitrary",), vmem_limit_bytes=VMEM_LIMIT_BYTES),
        name="sample_mixer",
    )(x, gmix, win, sinkcol, ck, cv, st, wpool, pscale, wout, gcross, wcq)


def _mem_cache_rows(cache):
    nb = cache.shape[0]
    c = cache.reshape(nb, N_MEM, N_CROSS_HEADS, CROSS_HEAD_DIM // 128, 128)
    return c.transpose(0, 1, 3, 2, 4).reshape(nb, N_MEM * MEM_ROWS, 128)


def _load_mem_head(ref, b, h):
    halves = [ref[b, pl.ds(half * N_CROSS_HEADS + h, N_MEM, stride=MEM_ROWS), :]
              for half in range(CROSS_HEAD_DIM // 128)]
    return jnp.concatenate(halves, axis=1)


def _sample_tail_kernel(x1_ref, o_ref, wco_ref, gffn_ref, wup_ref, wdown_ref, gfinal_ref, y_ref, *, tile, sub):
    gffn, gfinal = gffn_ref[...], gfinal_ref[...]

    def stream(r0):
        x2 = x1_ref[r0:r0 + sub, :] + _dot(o_ref[r0:r0 + sub, :].astype(_BF), wco_ref[...])
        yield

        def store(y):
            y_ref[r0:r0 + sub, :] = y

        yield from _ffn_final_stages(x2, gffn, wup_ref, wdown_ref, gfinal, store)

    _interleave([stream(r0) for r0 in range(0, tile, sub)], TAIL_SKEW)


def _sample_tail(x1, o, wco, gffn, wup, wdown, gfinal, tile, sub):
    R, D = x1.shape
    const = lambda *shape: _single(shape, lambda i: (0,) * len(shape))
    return pl.pallas_call(
        functools.partial(_sample_tail_kernel, tile=tile, sub=sub),
        grid=(R // tile,),
        in_specs=[
            pl.BlockSpec((tile, D), lambda i: (i, 0)),
            pl.BlockSpec((tile, D), lambda i: (i, 0)),
            const(D, D),
            const(1, D),
            const(D, D_FF),
            const(D_FF, D),
            const(1, D),
        ],
        out_specs=pl.BlockSpec((tile, D), lambda i: (i, 0)),
        out_shape=jax.ShapeDtypeStruct((R, D), _F32),
        compiler_params=pltpu.CompilerParams(
            dimension_semantics=("arbitrary",), vmem_limit_bytes=VMEM_LIMIT_BYTES),
        name="sample_tail",
    )(x1, o, wco, gffn, wup, wdown, gfinal)


PROMPT_TILE = 512
TAIL_TILE = 1024
TAIL_SUB = 256
TAIL_SKEW = 5
SAMPLE_MIXER_BATCH = 32
SAMPLE_MIXER_SUB = 16
SAMPLE_MIXER_SKEW = 3


def kernel(x_prompt, x_sample, cache_win_k, cache_win_v, state_pool, cache_mem_k, cache_mem_v, mem_prompt,
           g_mix, w_in, attn_sinks, w_pool, pool_scale, w_out, g_cross, g_mem, w_cq, w_ck, w_cv, w_co,
           g_ffn, w_up, w_down, g_final):
    depth = g_mix.shape[0]
    assert depth == 1, "one layer per step"
    B, S, D = x_prompt.shape
    DB, T, _ = x_sample.shape
    past_len = PAST_LEN
    l = 0

    win = jnp.concatenate([_pair_heads(w_in[l][:, :ATTN_WIDTH], 1), w_in[l][:, ATTN_WIDTH:]], axis=1).astype(_BF)
    wout = jnp.concatenate([_pair_heads(w_out[l][:ATTN_WIDTH, :], 0), w_out[l][ATTN_WIDTH:, :]], axis=0).astype(_BF)
    wpool = w_pool[l].astype(_BF)
    gmix, gcross, gmem, gffn = (g[l].reshape(1, D) for g in (g_mix, g_cross, g_mem, g_ffn))
    gfinal = g_final.reshape(1, D)
    pscale = pool_scale[l].reshape(1, POOL_WIDTH)
    sinks = attn_sinks[l]

    xs = x_sample.reshape(DB * T, D)
    ck = cache_win_k[l].transpose(0, 2, 3, 1).reshape(DB, KV_WIDTH, WINDOW)
    cv = cache_win_v[l].transpose(0, 2, 3, 1).reshape(DB, KV_WIDTH, WINDOW)
    st = state_pool[l].transpose(1, 0, 2)
    sinkcol = jnp.repeat(sinks, T).reshape(N_HEADS * T, 1)
    x1s, qc, wk_s, wv_s, pool_s = _sample_mixer(xs, gmix, win, sinkcol, ck, cv, st, wpool, pscale, wout,
                                                gcross, w_cq[l], SAMPLE_MIXER_BATCH, SAMPLE_MIXER_SUB, T, past_len)

    mk = _mem_cache_rows(cache_mem_k[l])
    mv = _mem_cache_rows(cache_mem_v[l])
    x1p, klast, vlast, plast, wcq, wco, wup, wdown, o_s = _prompt_mixer(
        x_prompt, gmix, win, sinks, wpool, pscale, wout, (w_cq[l], w_co[l], w_up[l], w_down[l]), qc, mk, mv,
        PROMPT_TILE)
    mem_k, mem_v, mem_kt, mem_vb = _mem_kv(mem_prompt, gmem, w_ck[l], w_cv[l])
    y_prompt = _prompt_tail(x1p, gcross, wcq, mem_kt, mem_vb, wco, gffn, wup, wdown, gfinal, TAIL_TILE, TAIL_SUB)

    y_sample = _sample_tail(x1s, o_s, wco, gffn, wup, wdown, gfinal, TAIL_TILE, TAIL_SUB).reshape(DB, T, D)

    return (
        y_prompt,
        y_sample,
        klast.reshape(B, N_KV_HEADS, HEAD_DIM, WINDOW).transpose(0, 3, 1, 2)[None],
        vlast.reshape(B, N_KV_HEADS, HEAD_DIM, WINDOW).transpose(0, 3, 1, 2)[None],
        plast[:, POOL_PAD - POOL_HIST:, :][None],
        _mem_cache_unrows(mem_k)[None],
        _mem_cache_unrows(mem_v)[None],
        wk_s.reshape(1, DB, WINDOW, N_KV_HEADS, HEAD_DIM),
        wv_s.reshape(1, DB, WINDOW, N_KV_HEADS, HEAD_DIM),
        pool_s.transpose(1, 0, 2)[None],
    )
```

```python
import functools

import jax
import jax.numpy as jnp
from jax import lax
from jax.experimental import pallas as pl
from jax.experimental.pallas import tpu as pltpu

D_MODEL = 1024
PAST_LEN = 16384
HEAD_DIM = 64
N_HEADS = 8
N_KV_HEADS = 2
GROUP = N_HEADS // N_KV_HEADS
ATTN_WIDTH = N_HEADS * HEAD_DIM
KV_WIDTH = N_KV_HEADS * HEAD_DIM
WINDOW = 128
BLOCK = WINDOW
POOL_WIDTH = D_MODEL - ATTN_WIDTH
POOL_WINDOWS = (2, 4, 8, 16)
POOL_GROUP_WIDTH = 128
POOL_HIST = 15
POOL_PAD = 16
IN_WIDTH = ATTN_WIDTH + 2 * KV_WIDTH + POOL_WIDTH
N_MEM = 256
N_CROSS_HEADS = 4
CROSS_HEAD_DIM = 256
MEM_ROWS = N_CROSS_HEADS * (CROSS_HEAD_DIM // 128)
D_FF = 4 * D_MODEL
FF_CHUNK = 1024
RMS_EPS = 1e-5
NEG_INF = -1e30
Q_SCALE = HEAD_DIM ** -0.5
CQ_SCALE = CROSS_HEAD_DIM ** -0.5
K_OFF = ATTN_WIDTH
V_OFF = ATTN_WIDTH + KV_WIDTH
U_OFF = ATTN_WIDTH + 2 * KV_WIDTH


def _pair_heads(w, axis):
    shape = w.shape
    split = shape[:axis] + (N_KV_HEADS, GROUP, HEAD_DIM) + shape[axis + 1:]
    return jnp.swapaxes(w.reshape(split), axis, axis + 1).reshape(shape)


VMEM_LIMIT_BYTES = 56 * 1024 * 1024

_BF = jnp.bfloat16
_F32 = jnp.float32


def _slope(h):
    return 2.0 ** (-8.0 * (h + 1) / N_HEADS)


def _dot(a, b):
    return jnp.dot(a, b, preferred_element_type=_F32)


def _rms(x, g):
    ms = jnp.mean(x * x, axis=-1, keepdims=True)
    return x * lax.rsqrt(ms + RMS_EPS) * g


def _softmax_with_sink(s, sink):
    m = jnp.maximum(jnp.max(s, axis=-1, keepdims=True), sink)
    e = jnp.exp(s - m)
    den = jnp.sum(e, axis=-1, keepdims=True) + jnp.exp(sink - m)
    return e * (1.0 / den)


def _softmax_with_sink_parts(s, sink):
    m = jnp.maximum(jnp.max(s, axis=-1, keepdims=True), sink)
    e = jnp.exp(s - m)
    den = jnp.sum(e, axis=-1, keepdims=True) + jnp.exp(sink - m)
    return e, 1.0 / den


def _softmax(s):
    m = jnp.max(s, axis=-1, keepdims=True)
    e = jnp.exp(s - m)
    return e * (1.0 / jnp.sum(e, axis=-1, keepdims=True))


def _window_sums(u_ext, axis):
    out = {}
    s = u_ext
    w = 1
    while w < max(POOL_WINDOWS):
        s = s + pltpu.roll(s, w, axis)
        w *= 2
        out[w] = s
    return out


def _interleave(streams, skew):
    pending = list(streams)
    live = []
    rnd = 0
    while live or pending:
        if pending and rnd % skew == 0:
            live.append(pending.pop(0))
        for g in list(live):
            try:
                next(g)
            except StopIteration:
                live.remove(g)
        rnd += 1


def _ffn_final_stages(x2, gffn, wup_ref, wdown_ref, gfinal, store):
    hn = _rms(x2, gffn).astype(_BF)
    yield
    acc = x2
    for c in range(D_FF // FF_CHUNK):
        hc = _dot(hn, wup_ref[:, c * FF_CHUNK:(c + 1) * FF_CHUNK])
        yield
        hc = jnp.maximum(hc, 0.0)
        hc = (hc * hc).astype(_BF)
        acc = acc + _dot(hc, wdown_ref[c * FF_CHUNK:(c + 1) * FF_CHUNK, :])
        yield
    store(_rms(acc, gfinal))


def _prompt_mixer_kernel(x_ref, gmix_ref, win_ref, sink_ref, wpool_ref, pscale_ref, wout_ref, gcross_ref, wcq_ref,
                         wco_ref, wup_ref, wdown_ref, qc_ref, mk_ref, mv_ref,
                         x1_ref, qcp_ref, klast_ref, vlast_ref, plast_ref, wco_bf_ref, wup_bf_ref, wdown_bf_ref,
                         os_ref, kt_scr, v_scr, u_scr, bias_scr, inv_scr, *, tile, nseq):
    b = pl.program_id(0)
    s = pl.program_id(1)
    nb = tile // BLOCK

    for src, dst in ((wco_ref, wco_bf_ref), (wup_ref, wup_bf_ref), (wdown_ref, wdown_bf_ref)):
        dst[...] = src[...].astype(_BF)

    @pl.when((b == 0) & (s == 0))
    def _():
        qi = lax.broadcasted_iota(jnp.int32, (BLOCK, 2 * BLOCK), 0)
        kc = lax.broadcasted_iota(jnp.int32, (BLOCK, 2 * BLOCK), 1)
        dist = qi + BLOCK - kc
        valid = (dist >= 0) & (dist <= WINDOW)
        valid_first = valid & (kc >= BLOCK)
        distf = dist.astype(_F32)
        for h in range(N_HEADS):
            ali = -_slope(h) * distf
            bias_scr[0, h] = jnp.where(valid, ali, NEG_INF)
            bias_scr[1, h] = jnp.where(valid_first, ali, NEG_INF)
        prow = lax.broadcasted_iota(jnp.int32, (BLOCK, POOL_GROUP_WIDTH), 0)
        for g, w in enumerate(POOL_WINDOWS):
            inv_scr[0, g] = jnp.full((BLOCK, POOL_GROUP_WIDTH), 1.0 / w, _F32)
            inv_scr[1, g] = 1.0 / jnp.minimum(prow + 1, w).astype(_F32)

    @pl.when(s == 0)
    def _():
        kt_scr[...] = jnp.zeros((KV_WIDTH, BLOCK), _BF)
        v_scr[...] = jnp.zeros((BLOCK, KV_WIDTH), _BF)
        u_scr[...] = jnp.zeros((POOL_PAD, POOL_WIDTH), _F32)

    lane = lax.broadcasted_iota(jnp.int32, (BLOCK, 2 * HEAD_DIM), 1)
    lo = lane < HEAD_DIM
    gmix, gcross = gmix_ref[...], gcross_ref[...]
    zeros_kt = jnp.zeros((HEAD_DIM, 2 * BLOCK), _BF)

    def project(j):
        x = x_ref[0, j * BLOCK:(j + 1) * BLOCK, :]
        proj = _dot(_rms(x, gmix).astype(_BF), win_ref[...])
        k = proj[:, K_OFF:K_OFF + KV_WIDTH]
        v = proj[:, V_OFF:V_OFF + KV_WIDTH]
        u = proj[:, U_OFF:]
        kt = k.T
        if j == nb - 1:
            klast_ref[0] = kt
            vlast_ref[0] = v.T
            plast_ref[0] = u[BLOCK - POOL_PAD:]
        return dict(x=x, q=(proj[:, :ATTN_WIDTH] * Q_SCALE).astype(_BF), kt=kt.astype(_BF), v=v.astype(_BF), u=u)

    def pool(j, u_hist, u):
        first = ((s == 0) & (j == 0)).astype(jnp.int32) if j == 0 else 0
        ys = []
        for g, w in enumerate(POOL_WINDOWS):
            c0 = g * POOL_GROUP_WIDTH
            u_ext = jnp.concatenate([u_hist[:, c0:c0 + POOL_GROUP_WIDTH], u[:, c0:c0 + POOL_GROUP_WIDTH]], axis=0)
            sw = _window_sums(u_ext, 0)[w][POOL_PAD:]
            d = (sw * inv_scr[first, g] - u_ext[POOL_PAD:]).astype(_BF)
            y = _dot(d, wpool_ref[g]) * pscale_ref[:, c0:c0 + POOL_GROUP_WIDTH]
            ys.append(y.astype(_BF))
        return ys

    def kv_operands(kt_prev, v_prev, blk):
        kt2 = jnp.concatenate([kt_prev, blk["kt"]], axis=1)
        v2 = jnp.concatenate([v_prev, blk["v"]], axis=0)
        kt_pair = jnp.concatenate([jnp.concatenate([kt2[:HEAD_DIM], zeros_kt], axis=0),
                                   jnp.concatenate([zeros_kt, kt2[HEAD_DIM:]], axis=0)], axis=1)
        lane2 = lax.broadcasted_iota(jnp.int32, v2.shape, 1)
        zero = jnp.zeros_like(v2)
        v_pair = jnp.concatenate([jnp.where(lane2 < HEAD_DIM, v2, zero), jnp.where(lane2 < HEAD_DIM, zero, v2)],
                                 axis=0)
        return kt_pair, v_pair

    def wave_scores(j, blk, kt_pair, pairs):
        first = ((s == 0) & (j == 0)).astype(jnp.int32) if j == 0 else 0
        scores = []
        for p in pairs:
            sc = _dot(blk["q"][:, p * 128:(p + 1) * 128], kt_pair)
            scores.append(sc[:, :2 * BLOCK] + bias_scr[first, p])
            scores.append(sc[:, 2 * BLOCK:] + bias_scr[first, p + GROUP])
        return scores

    def wave_values(v_pair, pairs, scores):
        slabs = []
        for i, p in enumerate(pairs):
            es, invs = [], []
            for half, h in ((0, p), (1, p + GROUP)):
                e, inv = _softmax_with_sink_parts(scores[2 * i + half], sink_ref[h])
                es.append(e.astype(_BF))
                invs.append(inv)
            o = _dot(jnp.concatenate(es, axis=1), v_pair)
            slabs.append((o * jnp.where(lo, invs[0], invs[1])).astype(_BF))
        return slabs

    def output(j, blk, slabs):
        cat = jnp.concatenate(slabs, axis=1)
        x1 = blk["x"] + _dot(cat, wout_ref[...])
        x1_ref[0, j * BLOCK:(j + 1) * BLOCK, :] = x1
        return j, _rms(x1, gcross).astype(_BF)

    def cross_query(j, hn):
        qcp_ref[0, j * BLOCK:(j + 1) * BLOCK, :] = (_dot(hn, wcq_ref[...]) * CQ_SCALE).astype(_BF)

    heads = [slice(h * CROSS_HEAD_DIM, (h + 1) * CROSS_HEAD_DIM) for h in range(N_CROSS_HEADS)]

    def mem_scores(i):
        q = qc_ref[i * nseq:(i + 1) * nseq, :].astype(_BF)
        return [lax.dot_general(q[:, c], _load_mem_head(mk_ref, i, h).astype(_BF), (((1,), (1,)), ((), ())),
                                preferred_element_type=_F32) for h, c in enumerate(heads)]

    def mem_values(i, scores):
        outs = [_dot(_softmax(sc).astype(_BF), _load_mem_head(mv_ref, i, h).astype(_BF))
                for h, sc in enumerate(scores)]
        os_ref[i * nseq:(i + 1) * nseq, :] = jnp.concatenate(outs, axis=1)

    kt_prev, v_prev, u_hist = kt_scr[...], v_scr[...], u_scr[...]
    blk = project(0)
    done = None
    for j in range(nb):
        kt_pair, v_pair = kv_operands(kt_prev, v_prev, blk)
        sc0 = wave_scores(j, blk, kt_pair, (0, 1))
        nxt = project(j + 1) if j + 1 < nb else None
        msc = mem_scores(j)
        sc1 = wave_scores(j, blk, kt_pair, (2, 3))
        pooled = pool(j, u_hist, blk["u"])
        at0 = wave_values(v_pair, (0, 1), sc0)
        cq = output(*done) if done is not None else None
        mem_values(j, msc)
        at1 = wave_values(v_pair, (2, 3), sc1)
        if cq is not None:
            cross_query(*cq)
        done = (j, blk, at0 + at1 + pooled)
        kt_prev, v_prev, u_hist = blk["kt"], blk["v"], blk["u"][BLOCK - POOL_PAD:]
        blk = nxt
    cross_query(*output(*done))
    kt_scr[...] = kt_prev
    v_scr[...] = v_prev
    u_scr[...] = u_hist


def _prompt_mixer(x, gmix, win, sinks, wpool, pscale, wout, gcross, wcq, tail_weights, qc, mk, mv, tile):
    B, S, D = x.shape
    ns = S // tile
    nsteps = B * ns
    nmem = tile // BLOCK
    nseq = qc.shape[0] // mk.shape[0]
    assert mk.shape[0] == nsteps * nmem
    const = lambda *shape: pl.BlockSpec(shape, lambda b, s: (0,) * len(shape))
    chunk = lambda w: pl.BlockSpec((w.shape[0] // nsteps, w.shape[1]), lambda b, s: (b * ns + s, 0))
    assert all(w.shape[0] % (16 * nsteps) == 0 for w in tail_weights)
    mem_rows = pl.BlockSpec((nmem * nseq, D), lambda b, s: (b * ns + s, 0))
    mem_cache = pl.BlockSpec((nmem, N_MEM * MEM_ROWS, 128), lambda b, s: (b * ns + s, 0, 0))
    return pl.pallas_call(
        functools.partial(_prompt_mixer_kernel, tile=tile, nseq=nseq),
        grid=(B, ns),
        in_specs=[
            pl.BlockSpec((1, tile, D), lambda b, s: (b, s, 0)),
            const(1, D),
            const(D, IN_WIDTH),
            pl.BlockSpec(memory_space=pltpu.SMEM),
            const(len(POOL_WINDOWS), POOL_GROUP_WIDTH, POOL_GROUP_WIDTH),
            const(1, POOL_WIDTH),
            const(D, D),
            const(1, D),
            const(D, D),
        ] + [chunk(w) for w in tail_weights] + [mem_rows, mem_cache, mem_cache],
        out_specs=[
            pl.BlockSpec((1, tile, D), lambda b, s: (b, s, 0)),
            pl.BlockSpec((1, tile, D), lambda b, s: (b, s, 0)),
            pl.BlockSpec((1, BLOCK, KV_WIDTH), lambda b, s: (b, 0, 0)),
            pl.BlockSpec((1, BLOCK, KV_WIDTH), lambda b, s: (b, 0, 0)),
            pl.BlockSpec((1, POOL_PAD, POOL_WIDTH), lambda b, s: (b, 0, 0)),
        ] + [chunk(w) for w in tail_weights] + [mem_rows],
        out_shape=[
            jax.ShapeDtypeStruct((B, S, D), _F32),
            jax.ShapeDtypeStruct((B, S, D), _BF),
            jax.ShapeDtypeStruct((B, BLOCK, KV_WIDTH), _F32),
            jax.ShapeDtypeStruct((B, BLOCK, KV_WIDTH), _F32),
            jax.ShapeDtypeStruct((B, POOL_PAD, POOL_WIDTH), _F32),
        ] + [jax.ShapeDtypeStruct(w.shape, _BF) for w in tail_weights] + [jax.ShapeDtypeStruct(qc.shape, _F32)],
        scratch_shapes=[
            pltpu.VMEM((KV_WIDTH, BLOCK), _BF),
            pltpu.VMEM((BLOCK, KV_WIDTH), _BF),
            pltpu.VMEM((POOL_PAD, POOL_WIDTH), _F32),
            pltpu.VMEM((2, N_HEADS, BLOCK, 2 * BLOCK), _F32),
            pltpu.VMEM((2, len(POOL_WINDOWS), BLOCK, POOL_GROUP_WIDTH), _F32),
        ],
        compiler_params=pltpu.CompilerParams(
            dimension_semantics=("arbitrary", "arbitrary"), vmem_limit_bytes=VMEM_LIMIT_BYTES),
        name="prompt_mixer",
    )(x, gmix, win, sinks, wpool, pscale, wout, gcross, wcq, *tail_weights, qc, mk, mv)


def _mem_kv_kernel(mem_ref, gmem_ref, wck_ref, wcv_ref, k_ref, v_ref, kt_ref, vb_ref, wck_scr, wcv_scr):
    @pl.when(pl.program_id(0) == 0)
    def _():
        wck_scr[...] = wck_ref[...].astype(_BF)
        wcv_scr[...] = wcv_ref[...].astype(_BF)

    hm = _rms(mem_ref[0], gmem_ref[...]).astype(_BF)
    k = _dot(hm, wck_scr[...])
    v = _dot(hm, wcv_scr[...])
    vb_ref[0] = v.astype(_BF)
    for h in range(N_CROSS_HEADS):
        kt_ref[0, h] = k[:, h * CROSS_HEAD_DIM:(h + 1) * CROSS_HEAD_DIM].T.astype(_BF)
        for half in range(CROSS_HEAD_DIM // 128):
            c0 = h * CROSS_HEAD_DIM + half * 128
            rows = pl.ds(half * N_CROSS_HEADS + h, N_MEM, stride=MEM_ROWS)
            k_ref[0, rows, :] = k[:, c0:c0 + 128]
            v_ref[0, rows, :] = v[:, c0:c0 + 128]


def _mem_kv(mem, gmem, wck, wcv):
    B, M, D = mem.shape
    const = lambda *shape: pl.BlockSpec(shape, lambda b: (0,) * len(shape))
    return pl.pallas_call(
        _mem_kv_kernel,
        grid=(B,),
        in_specs=[pl.BlockSpec((1, M, D), lambda b: (b, 0, 0)), const(1, D), const(D, D), const(D, D)],
        out_specs=[
            pl.BlockSpec((1, M * MEM_ROWS, 128), lambda b: (b, 0, 0)),
            pl.BlockSpec((1, M * MEM_ROWS, 128), lambda b: (b, 0, 0)),
            pl.BlockSpec((1, N_CROSS_HEADS, CROSS_HEAD_DIM, M), lambda b: (b, 0, 0, 0)),
            pl.BlockSpec((1, M, D), lambda b: (b, 0, 0)),
        ],
        out_shape=[
            jax.ShapeDtypeStruct((B, M * MEM_ROWS, 128), _F32),
            jax.ShapeDtypeStruct((B, M * MEM_ROWS, 128), _F32),
            jax.ShapeDtypeStruct((B, N_CROSS_HEADS, CROSS_HEAD_DIM, M), _BF),
            jax.ShapeDtypeStruct((B, M, D), _BF),
        ],
        scratch_shapes=[pltpu.VMEM((D, D), _BF), pltpu.VMEM((D, D), _BF)],
        compiler_params=pltpu.CompilerParams(
            dimension_semantics=("arbitrary",), vmem_limit_bytes=VMEM_LIMIT_BYTES),
        name="prompt_mem_kv",
    )(mem, gmem, wck, wcv)


def _mem_cache_unrows(rows):
    nb = rows.shape[0]
    c = rows.reshape(nb, N_MEM, CROSS_HEAD_DIM // 128, N_CROSS_HEADS, 128)
    return c.transpose(0, 1, 3, 2, 4).reshape(nb, N_MEM, N_CROSS_HEADS, CROSS_HEAD_DIM)


def _prompt_tail_kernel(x1_ref, q_ref, kt_ref, vb_ref, wco_ref, gffn_ref, wup_ref, wdown_ref,
                        gfinal_ref, y_ref, *, tile, sub):
    gffn, gfinal = gffn_ref[...], gfinal_ref[...]

    def stream(r0):
        x1 = x1_ref[0, r0:r0 + sub, :]
        q = q_ref[0, r0:r0 + sub, :]
        heads = [slice(h * CROSS_HEAD_DIM, (h + 1) * CROSS_HEAD_DIM) for h in range(N_CROSS_HEADS)]
        scores = [_dot(q[:, c], kt_ref[0, h]) for h, c in enumerate(heads)]
        yield
        outs = [_dot(_softmax(sc).astype(_BF), vb_ref[0, :, c]).astype(_BF) for sc, c in zip(scores, heads)]
        yield
        x2 = x1 + _dot(jnp.concatenate(outs, axis=1), wco_ref[...])
        yield

        def store(y):
            y_ref[0, r0:r0 + sub, :] = y

        yield from _ffn_final_stages(x2, gffn, wup_ref, wdown_ref, gfinal, store)

    _interleave([stream(r0) for r0 in range(0, tile, sub)], TAIL_SKEW)


def _single(shape, index_map):
    return pl.BlockSpec(shape, index_map, pipeline_mode=pl.Buffered(1))


def _prompt_tail(x1, q, kt, vb, wco, gffn, wup, wdown, gfinal, tile, sub):
    B, S, D = x1.shape
    const = lambda *shape: _single(shape, lambda b, s: (0,) * len(shape))
    return pl.pallas_call(
        functools.partial(_prompt_tail_kernel, tile=tile, sub=sub),
        grid=(B, S // tile),
        in_specs=[
            pl.BlockSpec((1, tile, D), lambda b, s: (b, s, 0)),
            pl.BlockSpec((1, tile, D), lambda b, s: (b, s, 0)),
            pl.BlockSpec((1, N_CROSS_HEADS, CROSS_HEAD_DIM, N_MEM), lambda b, s: (b, 0, 0, 0)),
            pl.BlockSpec((1, N_MEM, D), lambda b, s: (b, 0, 0)),
            const(D, D),
            const(1, D),
            const(D, D_FF),
            const(D_FF, D),
            const(1, D),
        ],
        out_specs=pl.BlockSpec((1, tile, D), lambda b, s: (b, s, 0)),
        out_shape=jax.ShapeDtypeStruct((B, S, D), _F32),
        compiler_params=pltpu.CompilerParams(
            dimension_semantics=("arbitrary", "arbitrary"), vmem_limit_bytes=VMEM_LIMIT_BYTES),
        name="prompt_tail",
    )(x1, q, kt, vb, wco, gffn, wup, wdown, gfinal)


def _sample_mixer_kernel(x_ref, gmix_ref, win_ref, sinkcol_ref, ck_ref, cv_ref, st_ref, wpool_ref, pscale_ref,
                         wout_ref, gcross_ref, wcq_ref,
                         x1_ref, qc_ref, wk_ref, wv_ref, pool_ref, wcq_bf_ref,
                         kc_scr, vc_scr, u_scr, *, nbatch, sub, nseq, past_len):
    @pl.when(pl.program_id(0) == 0)
    def _():
        wcq_bf_ref[...] = wcq_ref[...].astype(_BF)

    rows = sub * nseq
    nkeys = 2 * WINDOW
    gmix, gcross = gmix_ref[...], gcross_ref[...]
    sinkcol = sinkcol_ref[...][None]

    t = lax.broadcasted_iota(jnp.int32, (nseq, nkeys), 0)
    c = lax.broadcasted_iota(jnp.int32, (nseq, nkeys), 1)
    dist = jnp.where(c < WINDOW, WINDOW + t - c, t - (c - WINDOW))
    valid = (dist >= 0) & (dist <= WINDOW) & (c < WINDOW + nseq)
    distf = dist.astype(_F32)
    bias = jnp.concatenate([jnp.where(valid, -_slope(h) * distf, NEG_INF) for h in range(N_HEADS)], axis=0)[None]
    lane = lax.broadcasted_iota(jnp.int32, (rows, 2 * HEAD_DIM), 1)
    lo = lane < HEAD_DIM
    tpos = past_len + lax.broadcasted_iota(jnp.int32, (nseq, POOL_GROUP_WIDTH), 0)
    zpad = jnp.zeros((sub, nkeys - WINDOW, KV_WIDTH), _BF)

    def stream(b0):
        bs = slice(b0, b0 + sub)
        rs = slice(b0 * nseq, (b0 + sub) * nseq)
        x = x_ref[rs, :]
        proj = _dot(_rms(x, gmix).astype(_BF), win_ref[...])
        yield
        k_new = proj[:, K_OFF:K_OFF + KV_WIDTH].reshape(sub, nseq, KV_WIDTH)
        v_new = proj[:, V_OFF:V_OFF + KV_WIDTH].reshape(sub, nseq, KV_WIDTH)
        ck = jnp.swapaxes(ck_ref[bs], 1, 2)
        cv = jnp.swapaxes(cv_ref[bs], 1, 2)
        wk_ref[bs, :WINDOW - nseq, :] = ck[:, nseq:, :]
        wk_ref[bs, WINDOW - nseq:, :] = k_new
        wv_ref[bs, :WINDOW - nseq, :] = cv[:, nseq:, :]
        wv_ref[bs, WINDOW - nseq:, :] = v_new
        kc_scr[bs, WINDOW:, :] = zpad
        vc_scr[bs, WINDOW:, :] = zpad
        kc_scr[bs, :WINDOW, :] = ck.astype(_BF)
        vc_scr[bs, :WINDOW, :] = cv.astype(_BF)
        kc_scr[bs, WINDOW:WINDOW + nseq, :] = k_new.astype(_BF)
        vc_scr[bs, WINDOW:WINDOW + nseq, :] = v_new.astype(_BF)
        qsc = proj[:, :ATTN_WIDTH] * Q_SCALE
        q_lo, q_hi = [], []
        for p in range(GROUP):
            slab = qsc[:, p * 128:(p + 1) * 128]
            q_lo.append(jnp.where(lo, slab, 0.0).reshape(sub, nseq, 128))
            q_hi.append(jnp.where(lo, 0.0, slab).reshape(sub, nseq, 128))
        qm = jnp.concatenate(q_lo + q_hi, axis=1).astype(_BF)
        sc = jnp.einsum('bqd,bkd->bqk', qm, kc_scr[bs], preferred_element_type=_F32) + bias
        yield
        pr = _softmax_with_sink(sc, sinkcol).astype(_BF)
        o = jnp.einsum('bqk,bkd->bqd', pr, vc_scr[bs], preferred_element_type=_F32)
        yield
        attn = []
        for p in range(GROUP):
            o_lo = o[:, p * nseq:(p + 1) * nseq, :].reshape(rows, 128)
            o_hi = o[:, (p + GROUP) * nseq:(p + GROUP + 1) * nseq, :].reshape(rows, 128)
            attn.append(jnp.where(lo, o_lo, o_hi).astype(_BF))
        ext = POOL_PAD + nseq
        seq_rows = lambda r: pl.ds(b0 * ext + r, sub, stride=ext)
        pooled = []
        for g, w in enumerate(POOL_WINDOWS):
            c0 = g * POOL_GROUP_WIDTH
            u_scr[g, seq_rows(0), :] = jnp.zeros((sub, POOL_GROUP_WIDTH), _F32)
            for r in range(POOL_HIST):
                u_scr[g, seq_rows(POOL_PAD - POOL_HIST + r), :] = st_ref[r, bs, c0:c0 + POOL_GROUP_WIDTH]
            for i in range(sub):
                u_scr[g, (b0 + i) * ext + POOL_PAD:(b0 + i + 1) * ext, :] = (
                    proj[i * nseq:(i + 1) * nseq, U_OFF + c0:U_OFF + c0 + POOL_GROUP_WIDTH])
            for r in range(POOL_HIST):
                pool_ref[r, bs, c0:c0 + POOL_GROUP_WIDTH] = u_scr[g, seq_rows(ext - POOL_HIST + r), :]
            ug = u_scr[g, b0 * ext:(b0 + sub) * ext, :].reshape(sub, ext, POOL_GROUP_WIDTH)
            sw = _window_sums(ug, 1)[w][:, POOL_PAD:, :]
            cnt = jnp.minimum(tpos + 1, w).astype(_F32)
            d = (sw / cnt[None] - ug[:, POOL_PAD:, :]).reshape(rows, POOL_GROUP_WIDTH).astype(_BF)
            y = _dot(d, wpool_ref[g]) * pscale_ref[:, c0:c0 + POOL_GROUP_WIDTH]
            pooled.append(y.astype(_BF))
        yield
        x1 = x + _dot(jnp.concatenate(attn + pooled, axis=1), wout_ref[...])
        x1_ref[rs, :] = x1
        yield
        qc_ref[rs, :] = _dot(_rms(x1, gcross).astype(_BF), wcq_bf_ref[...]) * CQ_SCALE

    _interleave([stream(b0) for b0 in range(0, nbatch, sub)], SAMPLE_MIXER_SKEW)


def _sample_mixer(x, gmix, win, sinkcol, ck, cv, st, wpool, pscale, wout, gcross, wcq, nbatch, sub, nseq,
                  past_len):
    R, D = x.shape
    rows = nbatch * nseq
    const = lambda *shape: pl.BlockSpec(shape, lambda i: (0,) * len(shape))
    return pl.pallas_call(
        functools.partial(_sample_mixer_kernel, nbatch=nbatch, sub=sub, nseq=nseq, past_len=past_len),
        grid=(R // rows,),
        in_specs=[
            pl.BlockSpec((rows, D), lambda i: (i, 0)),
            const(1, D),
            const(D, IN_WIDTH),
            const(N_HEADS * nseq, 1),
            pl.BlockSpec((nbatch, WINDOW, KV_WIDTH), lambda i: (i, 0, 0)),
            pl.BlockSpec((nbatch, WINDOW, KV_WIDTH), lambda i: (i, 0, 0)),
            pl.BlockSpec((POOL_HIST, nbatch, POOL_WIDTH), lambda i: (0, i, 0)),
            const(len(POOL_WINDOWS), POOL_GROUP_WIDTH, POOL_GROUP_WIDTH),
            const(1, POOL_WIDTH),
            const(D, D),
            const(1, D),
            const(D, D),
        ],
        out_specs=[
            pl.BlockSpec((rows, D), lambda i: (i, 0)),
            pl.BlockSpec((rows, D), lambda i: (i, 0)),
            pl.BlockSpec((nbatch, WINDOW, KV_WIDTH), lambda i: (i, 0, 0)),
            pl.BlockSpec((nbatch, WINDOW, KV_WIDTH), lambda i: (i, 0, 0)),
            pl.BlockSpec((POOL_HIST, nbatch, POOL_WIDTH), lambda i: (0, i, 0)),
            const(D, D),
        ],
        out_shape=[
            jax.ShapeDtypeStruct((R, D), _F32),
            jax.ShapeDtypeStruct((R, D), _F32),
            jax.ShapeDtypeStruct((R // nseq, WINDOW, KV_WIDTH), _F32),
            jax.ShapeDtypeStruct((R // nseq, WINDOW, KV_WIDTH), _F32),
            jax.ShapeDtypeStruct((POOL_HIST, R // nseq, POOL_WIDTH), _F32),
            jax.ShapeDtypeStruct((D, D), _BF),
        ],
        scratch_shapes=[
            pltpu.VMEM((nbatch, 2 * WINDOW, KV_WIDTH), _BF),
            pltpu.VMEM((nbatch, 2 * WINDOW, KV_WIDTH), _BF),
            pltpu.VMEM((len(POOL_WINDOWS), nbatch * (POOL_PAD + nseq), POOL_GROUP_WIDTH), _F32),
        ],
        compiler_params=pltpu.CompilerParams(
            dimension_semantics=("arbitrary",), vmem_limit_bytes=VMEM_LIMIT_BYTES),
        name="sample_mixer",
    )(x, gmix, win, sinkcol, ck, cv, st, wpool, pscale, wout, gcross, wcq)


def _mem_cache_rows(cache):
    nb = cache.shape[0]
    c = cache.reshape(nb, N_MEM, N_CROSS_HEADS, CROSS_HEAD_DIM // 128, 128)
    return c.transpose(0, 1, 3, 2, 4).reshape(nb, N_MEM * MEM_ROWS, 128)


def _load_mem_head(ref, b, h):
    halves = [ref[b, pl.ds(half * N_CROSS_HEADS + h, N_MEM, stride=MEM_ROWS), :]
              for half in range(CROSS_HEAD_DIM // 128)]
    return jnp.concatenate(halves, axis=1)


def _sample_tail_kernel(x1_ref, o_ref, wco_ref, gffn_ref, wup_ref, wdown_ref, gfinal_ref, y_ref, *, tile, sub):
    gffn, gfinal = gffn_ref[...], gfinal_ref[...]

    def stream(r0):
        x2 = x1_ref[r0:r0 + sub, :] + _dot(o_ref[r0:r0 + sub, :].astype(_BF), wco_ref[...])
        yield

        def store(y):
            y_ref[r0:r0 + sub, :] = y

        yield from _ffn_final_stages(x2, gffn, wup_ref, wdown_ref, gfinal, store)

    _interleave([stream(r0) for r0 in range(0, tile, sub)], TAIL_SKEW)


def _sample_tail(x1, o, wco, gffn, wup, wdown, gfinal, tile, sub):
    R, D = x1.shape
    const = lambda *shape: _single(shape, lambda i: (0,) * len(shape))
    return pl.pallas_call(
        functools.partial(_sample_tail_kernel, tile=tile, sub=sub),
        grid=(R // tile,),
        in_specs=[
            pl.BlockSpec((tile, D), lambda i: (i, 0)),
            pl.BlockSpec((tile, D), lambda i: (i, 0)),
            const(D, D),
            const(1, D),
            const(D, D_FF),
            const(D_FF, D),
            const(1, D),
        ],
        out_specs=pl.BlockSpec((tile, D), lambda i: (i, 0)),
        out_shape=jax.ShapeDtypeStruct((R, D), _F32),
        compiler_params=pltpu.CompilerParams(
            dimension_semantics=("arbitrary",), vmem_limit_bytes=VMEM_LIMIT_BYTES),
        name="sample_tail",
    )(x1, o, wco, gffn, wup, wdown, gfinal)


PROMPT_TILE = 512
TAIL_TILE = 1024
TAIL_SUB = 256
TAIL_SKEW = 4
SAMPLE_MIXER_BATCH = 32
SAMPLE_MIXER_SUB = 16
SAMPLE_MIXER_SKEW = 3


def kernel(x_prompt, x_sample, cache_win_k, cache_win_v, state_pool, cache_mem_k, cache_mem_v, mem_prompt,
           g_mix, w_in, attn_sinks, w_pool, pool_scale, w_out, g_cross, g_mem, w_cq, w_ck, w_cv, w_co,
           g_ffn, w_up, w_down, g_final):
    depth = g_mix.shape[0]
    assert depth == 1, "one layer per step"
    B, S, D = x_prompt.shape
    DB, T, _ = x_sample.shape
    past_len = PAST_LEN
    l = 0

    win = jnp.concatenate([_pair_heads(w_in[l][:, :ATTN_WIDTH], 1), w_in[l][:, ATTN_WIDTH:]], axis=1).astype(_BF)
    wout = jnp.concatenate([_pair_heads(w_out[l][:ATTN_WIDTH, :], 0), w_out[l][ATTN_WIDTH:, :]], axis=0).astype(_BF)
    wpool = w_pool[l].astype(_BF)
    gmix, gcross, gmem, gffn = (g[l].reshape(1, D) for g in (g_mix, g_cross, g_mem, g_ffn))
    gfinal = g_final.reshape(1, D)
    pscale = pool_scale[l].reshape(1, POOL_WIDTH)
    sinks = attn_sinks[l]

    xs = x_sample.reshape(DB * T, D)
    ck = cache_win_k[l].transpose(0, 2, 3, 1).reshape(DB, KV_WIDTH, WINDOW)
    cv = cache_win_v[l].transpose(0, 2, 3, 1).reshape(DB, KV_WIDTH, WINDOW)
    st = state_pool[l].transpose(1, 0, 2)
    sinkcol = jnp.repeat(sinks, T).reshape(N_HEADS * T, 1)
    x1s, qc, wk_s, wv_s, pool_s, wcq = _sample_mixer(xs, gmix, win, sinkcol, ck, cv, st, wpool, pscale, wout,
                                                gcross, w_cq[l], SAMPLE_MIXER_BATCH, SAMPLE_MIXER_SUB, T, past_len)

    mk = _mem_cache_rows(cache_mem_k[l])
    mv = _mem_cache_rows(cache_mem_v[l])
    x1p, qcp, klast, vlast, plast, wco, wup, wdown, o_s = _prompt_mixer(
        x_prompt, gmix, win, sinks, wpool, pscale, wout, gcross, wcq, (w_co[l], w_up[l], w_down[l]), qc, mk, mv,
        PROMPT_TILE)
    mem_k, mem_v, mem_kt, mem_vb = _mem_kv(mem_prompt, gmem, w_ck[l], w_cv[l])
    y_prompt = _prompt_tail(x1p, qcp, mem_kt, mem_vb, wco, gffn, wup, wdown, gfinal, TAIL_TILE, TAIL_SUB)

    y_sample = _sample_tail(x1s, o_s, wco, gffn, wup, wdown, gfinal, TAIL_TILE, TAIL_SUB).reshape(DB, T, D)

    return (
        y_prompt,
        y_sample,
        klast.reshape(B, N_KV_HEADS, HEAD_DIM, WINDOW).transpose(0, 3, 1, 2)[None],
        vlast.reshape(B, N_KV_HEADS, HEAD_DIM, WINDOW).transpose(0, 3, 1, 2)[None],
        plast[:, POOL_PAD - POOL_HIST:, :][None],
        _mem_cache_unrows(mem_k)[None],
        _mem_cache_unrows(mem_v)[None],
        wk_s.reshape(1, DB, WINDOW, N_KV_HEADS, HEAD_DIM),
        wv_s.reshape(1, DB, WINDOW, N_KV_HEADS, HEAD_DIM),
        pool_s.transpose(1, 0, 2)[None],
    )
```

```python
import functools

import jax
import jax.numpy as jnp
from jax import lax
from jax.experimental import pallas as pl
from jax.experimental.pallas import tpu as pltpu

D_MODEL = 1024
PAST_LEN = 16384
HEAD_DIM = 64
N_HEADS = 8
N_KV_HEADS = 2
GROUP = N_HEADS // N_KV_HEADS
ATTN_WIDTH = N_HEADS * HEAD_DIM
KV_WIDTH = N_KV_HEADS * HEAD_DIM
WINDOW = 128
BLOCK = WINDOW
POOL_WIDTH = D_MODEL - ATTN_WIDTH
POOL_WINDOWS = (2, 4, 8, 16)
POOL_GROUP_WIDTH = 128
POOL_HIST = 15
POOL_PAD = 16
IN_WIDTH = ATTN_WIDTH + 2 * KV_WIDTH + POOL_WIDTH
N_MEM = 256
N_CROSS_HEADS = 4
CROSS_HEAD_DIM = 256
MEM_ROWS = N_CROSS_HEADS * (CROSS_HEAD_DIM // 128)
D_FF = 4 * D_MODEL
FF_CHUNK = 1024
RMS_EPS = 1e-5
NEG_INF = -1e30
Q_SCALE = HEAD_DIM ** -0.5
CQ_SCALE = CROSS_HEAD_DIM ** -0.5
K_OFF = ATTN_WIDTH
V_OFF = ATTN_WIDTH + KV_WIDTH
U_OFF = ATTN_WIDTH + 2 * KV_WIDTH


def _pair_heads(w, axis):
    shape = w.shape
    split = shape[:axis] + (N_KV_HEADS, GROUP, HEAD_DIM) + shape[axis + 1:]
    return jnp.swapaxes(w.reshape(split), axis, axis + 1).reshape(shape)


WOUT_SPLIT = 768

VMEM_LIMIT_BYTES = 56 * 1024 * 1024

_BF = jnp.bfloat16
_F32 = jnp.float32


def _slope(h):
    return 2.0 ** (-8.0 * (h + 1) / N_HEADS)


def _dot(a, b):
    return jnp.dot(a, b, preferred_element_type=_F32)


def _rms(x, g):
    ms = jnp.mean(x * x, axis=-1, keepdims=True)
    return x * lax.rsqrt(ms + RMS_EPS) * g


def _softmax_with_sink(s, sink):
    m = jnp.maximum(jnp.max(s, axis=-1, keepdims=True), sink)
    e = jnp.exp(s - m)
    den = jnp.sum(e, axis=-1, keepdims=True) + jnp.exp(sink - m)
    return e * (1.0 / den)


def _softmax_with_sink_parts(s, sink):
    m = jnp.maximum(jnp.max(s, axis=-1, keepdims=True), sink)
    e = jnp.exp(s - m)
    den = jnp.sum(e, axis=-1, keepdims=True) + jnp.exp(sink - m)
    return e, 1.0 / den


def _softmax(s):
    m = jnp.max(s, axis=-1, keepdims=True)
    e = jnp.exp(s - m)
    return e * (1.0 / jnp.sum(e, axis=-1, keepdims=True))


def _window_sums(u_ext, axis):
    out = {}
    s = u_ext
    w = 1
    while w < max(POOL_WINDOWS):
        s = s + pltpu.roll(s, w, axis)
        w *= 2
        out[w] = s
    return out


def _interleave(streams, skew):
    pending = list(streams)
    live = []
    rnd = 0
    while live or pending:
        if pending and rnd % skew == 0:
            live.append(pending.pop(0))
        for g in list(live):
            try:
                next(g)
            except StopIteration:
                live.remove(g)
        rnd += 1


def _ffn_final_stages(x2, gffn, wup_ref, wdown_ref, gfinal, store):
    hn = _rms(x2, gffn).astype(_BF)
    yield
    acc = x2
    for c in range(D_FF // FF_CHUNK):
        hc = _dot(hn, wup_ref[:, c * FF_CHUNK:(c + 1) * FF_CHUNK])
        yield
        hc = jnp.maximum(hc, 0.0)
        hc = (hc * hc).astype(_BF)
        acc = acc + _dot(hc, wdown_ref[c * FF_CHUNK:(c + 1) * FF_CHUNK, :])
        yield
    store(_rms(acc, gfinal))


def _prompt_mixer_kernel(x_ref, gmix_ref, win_ref, sink_ref, wpool_ref, pscale_ref, wouta_ref, woutb_ref,
                         wco_ref, wup_ref, wdown_ref, qc_ref, mk_ref, mv_ref,
                         x1_ref, klast_ref, vlast_ref, plast_ref, wco_bf_ref, wup_bf_ref, wdown_bf_ref,
                         os_ref, kt_scr, v_scr, u_scr, bias_scr, inv_scr, *, tile, nseq):
    b = pl.program_id(0)
    s = pl.program_id(1)
    nb = tile // BLOCK

    for src, dst in ((wco_ref, wco_bf_ref), (wup_ref, wup_bf_ref), (wdown_ref, wdown_bf_ref)):
        dst[...] = src[...].astype(_BF)

    @pl.when((b == 0) & (s == 0))
    def _():
        qi = lax.broadcasted_iota(jnp.int32, (BLOCK, 2 * BLOCK), 0)
        kc = lax.broadcasted_iota(jnp.int32, (BLOCK, 2 * BLOCK), 1)
        dist = qi + BLOCK - kc
        valid = (dist >= 0) & (dist <= WINDOW)
        valid_first = valid & (kc >= BLOCK)
        distf = dist.astype(_F32)
        for h in range(N_HEADS):
            ali = -_slope(h) * distf
            bias_scr[0, h] = jnp.where(valid, ali, NEG_INF)
            bias_scr[1, h] = jnp.where(valid_first, ali, NEG_INF)
        prow = lax.broadcasted_iota(jnp.int32, (BLOCK, POOL_GROUP_WIDTH), 0)
        for g, w in enumerate(POOL_WINDOWS):
            inv_scr[0, g] = jnp.full((BLOCK, POOL_GROUP_WIDTH), 1.0 / w, _F32)
            inv_scr[1, g] = 1.0 / jnp.minimum(prow + 1, w).astype(_F32)

    @pl.when(s == 0)
    def _():
        kt_scr[...] = jnp.zeros((KV_WIDTH, BLOCK), _BF)
        v_scr[...] = jnp.zeros((BLOCK, KV_WIDTH), _BF)
        u_scr[...] = jnp.zeros((POOL_PAD, POOL_WIDTH), _F32)

    lane = lax.broadcasted_iota(jnp.int32, (BLOCK, 2 * HEAD_DIM), 1)
    lo = lane < HEAD_DIM
    gmix = gmix_ref[...]
    zeros_kt = jnp.zeros((HEAD_DIM, 2 * BLOCK), _BF)

    def project(j):
        x = x_ref[0, j * BLOCK:(j + 1) * BLOCK, :]
        proj = _dot(_rms(x, gmix).astype(_BF), win_ref[...])
        k = proj[:, K_OFF:K_OFF + KV_WIDTH]
        v = proj[:, V_OFF:V_OFF + KV_WIDTH]
        u = proj[:, U_OFF:]
        kt = k.T
        if j == nb - 1:
            klast_ref[0] = kt
            vlast_ref[0] = v.T
            plast_ref[0] = u[BLOCK - POOL_PAD:]
        return dict(x=x, q=(proj[:, :ATTN_WIDTH] * Q_SCALE).astype(_BF), kt=kt.astype(_BF), v=v.astype(_BF), u=u)

    def pool(j, u_hist, u):
        first = ((s == 0) & (j == 0)).astype(jnp.int32) if j == 0 else 0
        ys = []
        for g, w in enumerate(POOL_WINDOWS):
            c0 = g * POOL_GROUP_WIDTH
            u_ext = jnp.concatenate([u_hist[:, c0:c0 + POOL_GROUP_WIDTH], u[:, c0:c0 + POOL_GROUP_WIDTH]], axis=0)
            sw = _window_sums(u_ext, 0)[w][POOL_PAD:]
            d = (sw * inv_scr[first, g] - u_ext[POOL_PAD:]).astype(_BF)
            y = _dot(d, wpool_ref[g]) * pscale_ref[:, c0:c0 + POOL_GROUP_WIDTH]
            ys.append(y.astype(_BF))
        return ys

    def kv_operands(kt_prev, v_prev, blk):
        kt2 = jnp.concatenate([kt_prev, blk["kt"]], axis=1)
        v2 = jnp.concatenate([v_prev, blk["v"]], axis=0)
        kt_pair = jnp.concatenate([jnp.concatenate([kt2[:HEAD_DIM], zeros_kt], axis=0),
                                   jnp.concatenate([zeros_kt, kt2[HEAD_DIM:]], axis=0)], axis=1)
        lane2 = lax.broadcasted_iota(jnp.int32, v2.shape, 1)
        zero = jnp.zeros_like(v2)
        v_pair = jnp.concatenate([jnp.where(lane2 < HEAD_DIM, v2, zero), jnp.where(lane2 < HEAD_DIM, zero, v2)],
                                 axis=0)
        return kt_pair, v_pair

    def wave_scores(j, blk, kt_pair, pairs):
        first = ((s == 0) & (j == 0)).astype(jnp.int32) if j == 0 else 0
        scores = []
        for p in pairs:
            sc = _dot(blk["q"][:, p * 128:(p + 1) * 128], kt_pair)
            scores.append(sc[:, :2 * BLOCK] + bias_scr[first, p])
            scores.append(sc[:, 2 * BLOCK:] + bias_scr[first, p + GROUP])
        return scores

    def wave_values(v_pair, pairs, scores):
        slabs = []
        for i, p in enumerate(pairs):
            es, invs = [], []
            for half, h in ((0, p), (1, p + GROUP)):
                e, inv = _softmax_with_sink_parts(scores[2 * i + half], sink_ref[h])
                es.append(e.astype(_BF))
                invs.append(inv)
            o = _dot(jnp.concatenate(es, axis=1), v_pair)
            slabs.append((o * jnp.where(lo, invs[0], invs[1])).astype(_BF))
        return slabs

    def output(j, blk, slabs):
        cat = jnp.concatenate(slabs, axis=1)
        y = jnp.concatenate([_dot(cat, wouta_ref[...]), _dot(cat, woutb_ref[...])], axis=1)
        x1_ref[0, j * BLOCK:(j + 1) * BLOCK, :] = blk["x"] + y

    heads = [slice(h * CROSS_HEAD_DIM, (h + 1) * CROSS_HEAD_DIM) for h in range(N_CROSS_HEADS)]

    def mem_scores(i):
        q = qc_ref[i * nseq:(i + 1) * nseq, :].astype(_BF)
        return [lax.dot_general(q[:, c], _load_mem_head(mk_ref, i, h).astype(_BF), (((1,), (1,)), ((), ())),
                                preferred_element_type=_F32) for h, c in enumerate(heads)]

    def mem_values(i, scores):
        outs = [_dot(_softmax(sc).astype(_BF), _load_mem_head(mv_ref, i, h).astype(_BF))
                for h, sc in enumerate(scores)]
        os_ref[i * nseq:(i + 1) * nseq, :] = jnp.concatenate(outs, axis=1)

    kt_prev, v_prev, u_hist = kt_scr[...], v_scr[...], u_scr[...]
    blk = project(0)
    done = None
    for j in range(nb):
        kt_pair, v_pair = kv_operands(kt_prev, v_prev, blk)
        sc0 = wave_scores(j, blk, kt_pair, (0, 1))
        nxt = project(j + 1) if j + 1 < nb else None
        msc = mem_scores(j)
        sc1 = wave_scores(j, blk, kt_pair, (2, 3))
        pooled = pool(j, u_hist, blk["u"])
        at0 = wave_values(v_pair, (0, 1), sc0)
        if done is not None:
            output(*done)
        mem_values(j, msc)
        at1 = wave_values(v_pair, (2, 3), sc1)
        done = (j, blk, at0 + at1 + pooled)
        kt_prev, v_prev, u_hist = blk["kt"], blk["v"], blk["u"][BLOCK - POOL_PAD:]
        blk = nxt
    output(*done)
    kt_scr[...] = kt_prev
    v_scr[...] = v_prev
    u_scr[...] = u_hist


def _prompt_mixer(x, gmix, win, sinks, wpool, pscale, wout, tail_weights, qc, mk, mv, tile):
    B, S, D = x.shape
    ns = S // tile
    nsteps = B * ns
    nmem = tile // BLOCK
    nseq = qc.shape[0] // mk.shape[0]
    assert mk.shape[0] == nsteps * nmem
    const = lambda *shape: pl.BlockSpec(shape, lambda b, s: (0,) * len(shape))
    chunk = lambda w: pl.BlockSpec((w.shape[0] // nsteps, w.shape[1]), lambda b, s: (b * ns + s, 0))
    assert all(w.shape[0] % (16 * nsteps) == 0 for w in tail_weights)
    mem_rows = pl.BlockSpec((nmem * nseq, D), lambda b, s: (b * ns + s, 0))
    mem_cache = pl.BlockSpec((nmem, N_MEM * MEM_ROWS, 128), lambda b, s: (b * ns + s, 0, 0))
    return pl.pallas_call(
        functools.partial(_prompt_mixer_kernel, tile=tile, nseq=nseq),
        grid=(B, ns),
        in_specs=[
            pl.BlockSpec((1, tile, D), lambda b, s: (b, s, 0)),
            const(1, D),
            const(D, IN_WIDTH),
            pl.BlockSpec(memory_space=pltpu.SMEM),
            const(len(POOL_WINDOWS), POOL_GROUP_WIDTH, POOL_GROUP_WIDTH),
            const(1, POOL_WIDTH),
            const(D, WOUT_SPLIT),
            const(D, D - WOUT_SPLIT),
        ] + [chunk(w) for w in tail_weights] + [mem_rows, mem_cache, mem_cache],
        out_specs=[
            pl.BlockSpec((1, tile, D), lambda b, s: (b, s, 0)),
            pl.BlockSpec((1, BLOCK, KV_WIDTH), lambda b, s: (b, 0, 0)),
            pl.BlockSpec((1, BLOCK, KV_WIDTH), lambda b, s: (b, 0, 0)),
            pl.BlockSpec((1, POOL_PAD, POOL_WIDTH), lambda b, s: (b, 0, 0)),
        ] + [chunk(w) for w in tail_weights] + [mem_rows],
        out_shape=[
            jax.ShapeDtypeStruct((B, S, D), _F32),
            jax.ShapeDtypeStruct((B, BLOCK, KV_WIDTH), _F32),
            jax.ShapeDtypeStruct((B, BLOCK, KV_WIDTH), _F32),
            jax.ShapeDtypeStruct((B, POOL_PAD, POOL_WIDTH), _F32),
        ] + [jax.ShapeDtypeStruct(w.shape, _BF) for w in tail_weights] + [jax.ShapeDtypeStruct(qc.shape, _F32)],
        scratch_shapes=[
            pltpu.VMEM((KV_WIDTH, BLOCK), _BF),
            pltpu.VMEM((BLOCK, KV_WIDTH), _BF),
            pltpu.VMEM((POOL_PAD, POOL_WIDTH), _F32),
            pltpu.VMEM((2, N_HEADS, BLOCK, 2 * BLOCK), _F32),
            pltpu.VMEM((2, len(POOL_WINDOWS), BLOCK, POOL_GROUP_WIDTH), _F32),
        ],
        compiler_params=pltpu.CompilerParams(
            dimension_semantics=("arbitrary", "arbitrary"), vmem_limit_bytes=VMEM_LIMIT_BYTES),
        name="prompt_mixer",
    )(x, gmix, win, sinks, wpool, pscale, wout[:, :WOUT_SPLIT], wout[:, WOUT_SPLIT:], *tail_weights, qc, mk, mv)


def _mem_kv_kernel(mem_ref, gmem_ref, wck_ref, wcv_ref, k_ref, v_ref, kt_ref, vb_ref, wck_scr, wcv_scr):
    @pl.when(pl.program_id(0) == 0)
    def _():
        wck_scr[...] = wck_ref[...].astype(_BF)
        wcv_scr[...] = wcv_ref[...].astype(_BF)

    hm = _rms(mem_ref[0], gmem_ref[...]).astype(_BF)
    k = _dot(hm, wck_scr[...])
    v = _dot(hm, wcv_scr[...])
    vb_ref[0] = v.astype(_BF)
    for h in range(N_CROSS_HEADS):
        kt_ref[0, h] = k[:, h * CROSS_HEAD_DIM:(h + 1) * CROSS_HEAD_DIM].T.astype(_BF)
        for half in range(CROSS_HEAD_DIM // 128):
            c0 = h * CROSS_HEAD_DIM + half * 128
            rows = pl.ds(half * N_CROSS_HEADS + h, N_MEM, stride=MEM_ROWS)
            k_ref[0, rows, :] = k[:, c0:c0 + 128]
            v_ref[0, rows, :] = v[:, c0:c0 + 128]


def _mem_kv(mem, gmem, wck, wcv):
    B, M, D = mem.shape
    const = lambda *shape: pl.BlockSpec(shape, lambda b: (0,) * len(shape))
    return pl.pallas_call(
        _mem_kv_kernel,
        grid=(B,),
        in_specs=[pl.BlockSpec((1, M, D), lambda b: (b, 0, 0)), const(1, D), const(D, D), const(D, D)],
        out_specs=[
            pl.BlockSpec((1, M * MEM_ROWS, 128), lambda b: (b, 0, 0)),
            pl.BlockSpec((1, M * MEM_ROWS, 128), lambda b: (b, 0, 0)),
            pl.BlockSpec((1, N_CROSS_HEADS, CROSS_HEAD_DIM, M), lambda b: (b, 0, 0, 0)),
            pl.BlockSpec((1, M, D), lambda b: (b, 0, 0)),
        ],
        out_shape=[
            jax.ShapeDtypeStruct((B, M * MEM_ROWS, 128), _F32),
            jax.ShapeDtypeStruct((B, M * MEM_ROWS, 128), _F32),
            jax.ShapeDtypeStruct((B, N_CROSS_HEADS, CROSS_HEAD_DIM, M), _BF),
            jax.ShapeDtypeStruct((B, M, D), _BF),
        ],
        scratch_shapes=[pltpu.VMEM((D, D), _BF), pltpu.VMEM((D, D), _BF)],
        compiler_params=pltpu.CompilerParams(
            dimension_semantics=("arbitrary",), vmem_limit_bytes=VMEM_LIMIT_BYTES),
        name="prompt_mem_kv",
    )(mem, gmem, wck, wcv)


def _mem_cache_unrows(rows):
    nb = rows.shape[0]
    c = rows.reshape(nb, N_MEM, CROSS_HEAD_DIM // 128, N_CROSS_HEADS, 128)
    return c.transpose(0, 1, 3, 2, 4).reshape(nb, N_MEM, N_CROSS_HEADS, CROSS_HEAD_DIM)


def _prompt_tail_kernel(x1_ref, gcross_ref, wcq_ref, kt_ref, vb_ref, wco_ref, gffn_ref, wup_ref, wdown_ref,
                        gfinal_ref, y_ref, *, tile, sub):
    gcross, gffn, gfinal = gcross_ref[...], gffn_ref[...], gfinal_ref[...]

    def stream(r0):
        x1 = x1_ref[0, r0:r0 + sub, :]
        hn = _rms(x1, gcross).astype(_BF)
        yield
        q = (_dot(hn, wcq_ref[...]) * CQ_SCALE).astype(_BF)
        yield
        heads = [slice(h * CROSS_HEAD_DIM, (h + 1) * CROSS_HEAD_DIM) for h in range(N_CROSS_HEADS)]
        scores = [_dot(q[:, c], kt_ref[0, h]) for h, c in enumerate(heads)]
        yield
        outs = [_dot(_softmax(sc).astype(_BF), vb_ref[0, :, c]).astype(_BF) for sc, c in zip(scores, heads)]
        yield
        x2 = x1 + _dot(jnp.concatenate(outs, axis=1), wco_ref[...])
        yield

        def store(y):
            y_ref[0, r0:r0 + sub, :] = y

        yield from _ffn_final_stages(x2, gffn, wup_ref, wdown_ref, gfinal, store)

    _interleave([stream(r0) for r0 in range(0, tile, sub)], TAIL_SKEW)


def _single(shape, index_map):
    return pl.BlockSpec(shape, index_map, pipeline_mode=pl.Buffered(1))


def _prompt_tail(x1, gcross, wcq, kt, vb, wco, gffn, wup, wdown, gfinal, tile, sub):
    B, S, D = x1.shape
    const = lambda *shape: _single(shape, lambda b, s: (0,) * len(shape))
    return pl.pallas_call(
        functools.partial(_prompt_tail_kernel, tile=tile, sub=sub),
        grid=(B, S // tile),
        in_specs=[
            pl.BlockSpec((1, tile, D), lambda b, s: (b, s, 0)),
            const(1, D),
            const(D, D),
            pl.BlockSpec((1, N_CROSS_HEADS, CROSS_HEAD_DIM, N_MEM), lambda b, s: (b, 0, 0, 0)),
            pl.BlockSpec((1, N_MEM, D), lambda b, s: (b, 0, 0)),
            const(D, D),
            const(1, D),
            const(D, D_FF),
            const(D_FF, D),
            const(1, D),
        ],
        out_specs=pl.BlockSpec((1, tile, D), lambda b, s: (b, s, 0)),
        out_shape=jax.ShapeDtypeStruct((B, S, D), _F32),
        compiler_params=pltpu.CompilerParams(
            dimension_semantics=("arbitrary", "arbitrary"), vmem_limit_bytes=VMEM_LIMIT_BYTES),
        name="prompt_tail",
    )(x1, gcross, wcq, kt, vb, wco, gffn, wup, wdown, gfinal)


def _sample_mixer_kernel(x_ref, gmix_ref, win_ref, sinkcol_ref, ck_ref, cv_ref, st_ref, wpool_ref, pscale_ref,
                         wout_ref, gcross_ref, wcq_ref,
                         x1_ref, qc_ref, wk_ref, wv_ref, pool_ref, wcq_bf_ref,
                         kc_scr, vc_scr, u_scr, *, nbatch, sub, nseq, past_len):
    @pl.when(pl.program_id(0) == 0)
    def _():
        wcq_bf_ref[...] = wcq_ref[...].astype(_BF)

    rows = sub * nseq
    nkeys = 2 * WINDOW
    gmix, gcross = gmix_ref[...], gcross_ref[...]
    sinkcol = sinkcol_ref[...][None]

    t = lax.broadcasted_iota(jnp.int32, (nseq, nkeys), 0)
    c = lax.broadcasted_iota(jnp.int32, (nseq, nkeys), 1)
    dist = jnp.where(c < WINDOW, WINDOW + t - c, t - (c - WINDOW))
    valid = (dist >= 0) & (dist <= WINDOW) & (c < WINDOW + nseq)
    distf = dist.astype(_F32)
    bias = jnp.concatenate([jnp.where(valid, -_slope(h) * distf, NEG_INF) for h in range(N_HEADS)], axis=0)[None]
    lane = lax.broadcasted_iota(jnp.int32, (rows, 2 * HEAD_DIM), 1)
    lo = lane < HEAD_DIM
    tpos = past_len + lax.broadcasted_iota(jnp.int32, (nseq, POOL_GROUP_WIDTH), 0)
    zpad = jnp.zeros((sub, nkeys - WINDOW, KV_WIDTH), _BF)

    def stream(b0):
        bs = slice(b0, b0 + sub)
        rs = slice(b0 * nseq, (b0 + sub) * nseq)
        x = x_ref[rs, :]
        proj = _dot(_rms(x, gmix).astype(_BF), win_ref[...])
        yield
        k_new = proj[:, K_OFF:K_OFF + KV_WIDTH].reshape(sub, nseq, KV_WIDTH)
        v_new = proj[:, V_OFF:V_OFF + KV_WIDTH].reshape(sub, nseq, KV_WIDTH)
        ck = jnp.swapaxes(ck_ref[bs], 1, 2)
        cv = jnp.swapaxes(cv_ref[bs], 1, 2)
        wk_ref[bs, :WINDOW - nseq, :] = ck[:, nseq:, :]
        wk_ref[bs, WINDOW - nseq:, :] = k_new
        wv_ref[bs, :WINDOW - nseq, :] = cv[:, nseq:, :]
        wv_ref[bs, WINDOW - nseq:, :] = v_new
        kc_scr[bs, WINDOW:, :] = zpad
        vc_scr[bs, WINDOW:, :] = zpad
        kc_scr[bs, :WINDOW, :] = ck.astype(_BF)
        vc_scr[bs, :WINDOW, :] = cv.astype(_BF)
        kc_scr[bs, WINDOW:WINDOW + nseq, :] = k_new.astype(_BF)
        vc_scr[bs, WINDOW:WINDOW + nseq, :] = v_new.astype(_BF)
        qsc = proj[:, :ATTN_WIDTH] * Q_SCALE
        q_lo, q_hi = [], []
        for p in range(GROUP):
            slab = qsc[:, p * 128:(p + 1) * 128]
            q_lo.append(jnp.where(lo, slab, 0.0).reshape(sub, nseq, 128))
            q_hi.append(jnp.where(lo, 0.0, slab).reshape(sub, nseq, 128))
        qm = jnp.concatenate(q_lo + q_hi, axis=1).astype(_BF)
        sc = jnp.einsum('bqd,bkd->bqk', qm, kc_scr[bs], preferred_element_type=_F32) + bias
        yield
        pr = _softmax_with_sink(sc, sinkcol).astype(_BF)
        o = jnp.einsum('bqk,bkd->bqd', pr, vc_scr[bs], preferred_element_type=_F32)
        yield
        attn = []
        for p in range(GROUP):
            o_lo = o[:, p * nseq:(p + 1) * nseq, :].reshape(rows, 128)
            o_hi = o[:, (p + GROUP) * nseq:(p + GROUP + 1) * nseq, :].reshape(rows, 128)
            attn.append(jnp.where(lo, o_lo, o_hi).astype(_BF))
        ext = POOL_PAD + nseq
        seq_rows = lambda r: pl.ds(b0 * ext + r, sub, stride=ext)
        pooled = []
        for g, w in enumerate(POOL_WINDOWS):
            c0 = g * POOL_GROUP_WIDTH
            u_scr[g, seq_rows(0), :] = jnp.zeros((sub, POOL_GROUP_WIDTH), _F32)
            for r in range(POOL_HIST):
                u_scr[g, seq_rows(POOL_PAD - POOL_HIST + r), :] = st_ref[r, bs, c0:c0 + POOL_GROUP_WIDTH]
            for i in range(sub):
                u_scr[g, (b0 + i) * ext + POOL_PAD:(b0 + i + 1) * ext, :] = (
                    proj[i * nseq:(i + 1) * nseq, U_OFF + c0:U_OFF + c0 + POOL_GROUP_WIDTH])
            for r in range(POOL_HIST):
                pool_ref[r, bs, c0:c0 + POOL_GROUP_WIDTH] = u_scr[g, seq_rows(ext - POOL_HIST + r), :]
            ug = u_scr[g, b0 * ext:(b0 + sub) * ext, :].reshape(sub, ext, POOL_GROUP_WIDTH)
            sw = _window_sums(ug, 1)[w][:, POOL_PAD:, :]
            cnt = jnp.minimum(tpos + 1, w).astype(_F32)
            d = (sw / cnt[None] - ug[:, POOL_PAD:, :]).reshape(rows, POOL_GROUP_WIDTH).astype(_BF)
            y = _dot(d, wpool_ref[g]) * pscale_ref[:, c0:c0 + POOL_GROUP_WIDTH]
            pooled.append(y.astype(_BF))
        yield
        x1 = x + _dot(jnp.concatenate(attn + pooled, axis=1), wout_ref[...])
        x1_ref[rs, :] = x1
        yield
        qc_ref[rs, :] = _dot(_rms(x1, gcross).astype(_BF), wcq_bf_ref[...]) * CQ_SCALE

    _interleave([stream(b0) for b0 in range(0, nbatch, sub)], SAMPLE_MIXER_SKEW)


def _sample_mixer(x, gmix, win, sinkcol, ck, cv, st, wpool, pscale, wout, gcross, wcq, nbatch, sub, nseq,
                  past_len):
    R, D = x.shape
    rows = nbatch * nseq
    const = lambda *shape: pl.BlockSpec(shape, lambda i: (0,) * len(shape))
    return pl.pallas_call(
        functools.partial(_sample_mixer_kernel, nbatch=nbatch, sub=sub, nseq=nseq, past_len=past_len),
        grid=(R // rows,),
        in_specs=[
            pl.BlockSpec((rows, D), lambda i: (i, 0)),
            const(1, D),
            const(D, IN_WIDTH),
            const(N_HEADS * nseq, 1),
            pl.BlockSpec((nbatch, WINDOW, KV_WIDTH), lambda i: (i, 0, 0)),
            pl.BlockSpec((nbatch, WINDOW, KV_WIDTH), lambda i: (i, 0, 0)),
            pl.BlockSpec((POOL_HIST, nbatch, POOL_WIDTH), lambda i: (0, i, 0)),
            const(len(POOL_WINDOWS), POOL_GROUP_WIDTH, POOL_GROUP_WIDTH),
            const(1, POOL_WIDTH),
            const(D, D),
            const(1, D),
            const(D, D),
        ],
        out_specs=[
            pl.BlockSpec((rows, D), lambda i: (i, 0)),
            pl.BlockSpec((rows, D), lambda i: (i, 0)),
            pl.BlockSpec((nbatch, WINDOW, KV_WIDTH), lambda i: (i, 0, 0)),
            pl.BlockSpec((nbatch, WINDOW, KV_WIDTH), lambda i: (i, 0, 0)),
            pl.BlockSpec((POOL_HIST, nbatch, POOL_WIDTH), lambda i: (0, i, 0)),
            const(D, D),
        ],
        out_shape=[
            jax.ShapeDtypeStruct((R, D), _F32),
            jax.ShapeDtypeStruct((R, D), _F32),
            jax.ShapeDtypeStruct((R // nseq, WINDOW, KV_WIDTH), _F32),
            jax.ShapeDtypeStruct((R // nseq, WINDOW, KV_WIDTH), _F32),
            jax.ShapeDtypeStruct((POOL_HIST, R // nseq, POOL_WIDTH), _F32),
            jax.ShapeDtypeStruct((D, D), _BF),
        ],
        scratch_shapes=[
            pltpu.VMEM((nbatch, 2 * WINDOW, KV_WIDTH), _BF),
            pltpu.VMEM((nbatch, 2 * WINDOW, KV_WIDTH), _BF),
            pltpu.VMEM((len(POOL_WINDOWS), nbatch * (POOL_PAD + nseq), POOL_GROUP_WIDTH), _F32),
        ],
        compiler_params=pltpu.CompilerParams(
            dimension_semantics=("arbitrary",), vmem_limit_bytes=VMEM_LIMIT_BYTES),
        name="sample_mixer",
    )(x, gmix, win, sinkcol, ck, cv, st, wpool, pscale, wout, gcross, wcq)


def _mem_cache_rows(cache):
    nb = cache.shape[0]
    c = cache.reshape(nb, N_MEM, N_CROSS_HEADS, CROSS_HEAD_DIM // 128, 128)
    return c.transpose(0, 1, 3, 2, 4).reshape(nb, N_MEM * MEM_ROWS, 128)


def _load_mem_head(ref, b, h):
    halves = [ref[b, pl.ds(half * N_CROSS_HEADS + h, N_MEM, stride=MEM_ROWS), :]
              for half in range(CROSS_HEAD_DIM // 128)]
    return jnp.concatenate(halves, axis=1)


def _sample_tail_kernel(x1_ref, o_ref, wco_ref, gffn_ref, wup_ref, wdown_ref, gfinal_ref, y_ref, *, tile, sub):
    gffn, gfinal = gffn_ref[...], gfinal_ref[...]

    def stream(r0):
        x2 = x1_ref[r0:r0 + sub, :] + _dot(o_ref[r0:r0 + sub, :].astype(_BF), wco_ref[...])
        yield

        def store(y):
            y_ref[r0:r0 + sub, :] = y

        yield from _ffn_final_stages(x2, gffn, wup_ref, wdown_ref, gfinal, store)

    _interleave([stream(r0) for r0 in range(0, tile, sub)], TAIL_SKEW)


def _sample_tail(x1, o, wco, gffn, wup, wdown, gfinal, tile, sub):
    R, D = x1.shape
    const = lambda *shape: _single(shape, lambda i: (0,) * len(shape))
    return pl.pallas_call(
        functools.partial(_sample_tail_kernel, tile=tile, sub=sub),
        grid=(R // tile,),
        in_specs=[
            pl.BlockSpec((tile, D), lambda i: (i, 0)),
            pl.BlockSpec((tile, D), lambda i: (i, 0)),
            const(D, D),
            const(1, D),
            const(D, D_FF),
            const(D_FF, D),
            const(1, D),
        ],
        out_specs=pl.BlockSpec((tile, D), lambda i: (i, 0)),
        out_shape=jax.ShapeDtypeStruct((R, D), _F32),
        compiler_params=pltpu.CompilerParams(
            dimension_semantics=("arbitrary",), vmem_limit_bytes=VMEM_LIMIT_BYTES),
        name="sample_tail",
    )(x1, o, wco, gffn, wup, wdown, gfinal)


PROMPT_TILE = 512
TAIL_TILE = 1024
TAIL_SUB = 256
TAIL_SKEW = 5
SAMPLE_MIXER_BATCH = 32
SAMPLE_MIXER_SUB = 16
SAMPLE_MIXER_SKEW = 3


def kernel(x_prompt, x_sample, cache_win_k, cache_win_v, state_pool, cache_mem_k, cache_mem_v, mem_prompt,
           g_mix, w_in, attn_sinks, w_pool, pool_scale, w_out, g_cross, g_mem, w_cq, w_ck, w_cv, w_co,
           g_ffn, w_up, w_down, g_final):
    depth = g_mix.shape[0]
    assert depth == 1, "one layer per step"
    B, S, D = x_prompt.shape
    DB, T, _ = x_sample.shape
    past_len = PAST_LEN
    l = 0

    win = jnp.concatenate([_pair_heads(w_in[l][:, :ATTN_WIDTH], 1), w_in[l][:, ATTN_WIDTH:]], axis=1).astype(_BF)
    wout = jnp.concatenate([_pair_heads(w_out[l][:ATTN_WIDTH, :], 0), w_out[l][ATTN_WIDTH:, :]], axis=0).astype(_BF)
    wpool = w_pool[l].astype(_BF)
    gmix, gcross, gmem, gffn = (g[l].reshape(1, D) for g in (g_mix, g_cross, g_mem, g_ffn))
    gfinal = g_final.reshape(1, D)
    pscale = pool_scale[l].reshape(1, POOL_WIDTH)
    sinks = attn_sinks[l]

    xs = x_sample.reshape(DB * T, D)
    ck = cache_win_k[l].transpose(0, 2, 3, 1).reshape(DB, KV_WIDTH, WINDOW)
    cv = cache_win_v[l].transpose(0, 2, 3, 1).reshape(DB, KV_WIDTH, WINDOW)
    st = state_pool[l].transpose(1, 0, 2)
    sinkcol = jnp.repeat(sinks, T).reshape(N_HEADS * T, 1)
    x1s, qc, wk_s, wv_s, pool_s, wcq = _sample_mixer(xs, gmix, win, sinkcol, ck, cv, st, wpool, pscale, wout,
                                                gcross, w_cq[l], SAMPLE_MIXER_BATCH, SAMPLE_MIXER_SUB, T, past_len)

    mk = _mem_cache_rows(cache_mem_k[l])
    mv = _mem_cache_rows(cache_mem_v[l])
    x1p, klast, vlast, plast, wco, wup, wdown, o_s = _prompt_mixer(
        x_prompt, gmix, win, sinks, wpool, pscale, wout, (w_co[l], w_up[l], w_down[l]), qc, mk, mv, PROMPT_TILE)
    mem_k, mem_v, mem_kt, mem_vb = _mem_kv(mem_prompt, gmem, w_ck[l], w_cv[l])
    y_prompt = _prompt_tail(x1p, gcross, wcq, mem_kt, mem_vb, wco, gffn, wup, wdown, gfinal, TAIL_TILE, TAIL_SUB)

    y_sample = _sample_tail(x1s, o_s, wco, gffn, wup, wdown, gfinal, TAIL_TILE, TAIL_SUB).reshape(DB, T, D)

    return (
        y_prompt,
        y_sample,
        klast.reshape(B, N_KV_HEADS, HEAD_DIM, WINDOW).transpose(0, 3, 1, 2)[None],
        vlast.reshape(B, N_KV_HEADS, HEAD_DIM, WINDOW).transpose(0, 3, 1, 2)[None],
        plast[:, POOL_PAD - POOL_HIST:, :][None],
        _mem_cache_unrows(mem_k)[None],
        _mem_cache_unrows(mem_v)[None],
        wk_s.reshape(1, DB, WINDOW, N_KV_HEADS, HEAD_DIM),
        wv_s.reshape(1, DB, WINDOW, N_KV_HEADS, HEAD_DIM),
        pool_s.transpose(1, 0, 2)[None],
    )
```

```python
import functools

import jax
import jax.numpy as jnp
from jax import lax
from jax.experimental import pallas as pl
from jax.experimental.pallas import tpu as pltpu

D_MODEL = 1024
PAST_LEN = 16384
HEAD_DIM = 64
N_HEADS = 8
N_KV_HEADS = 2
GROUP = N_HEADS // N_KV_HEADS
ATTN_WIDTH = N_HEADS * HEAD_DIM
KV_WIDTH = N_KV_HEADS * HEAD_DIM
WINDOW = 128
BLOCK = WINDOW
POOL_WIDTH = D_MODEL - ATTN_WIDTH
POOL_WINDOWS = (2, 4, 8, 16)
POOL_GROUP_WIDTH = 128
POOL_HIST = 15
POOL_PAD = 16
IN_WIDTH = ATTN_WIDTH + 2 * KV_WIDTH + POOL_WIDTH
N_MEM = 256
N_CROSS_HEADS = 4
CROSS_HEAD_DIM = 256
MEM_ROWS = N_CROSS_HEADS * (CROSS_HEAD_DIM // 128)
D_FF = 4 * D_MODEL
FF_CHUNK = 1024
RMS_EPS = 1e-5
NEG_INF = -1e30
Q_SCALE = HEAD_DIM ** -0.5
CQ_SCALE = CROSS_HEAD_DIM ** -0.5
K_OFF = ATTN_WIDTH
V_OFF = ATTN_WIDTH + KV_WIDTH
U_OFF = ATTN_WIDTH + 2 * KV_WIDTH


def _pair_heads(w, axis):
    shape = w.shape
    split = shape[:axis] + (N_KV_HEADS, GROUP, HEAD_DIM) + shape[axis + 1:]
    return jnp.swapaxes(w.reshape(split), axis, axis + 1).reshape(shape)


WOUT_SPLIT = 768

VMEM_LIMIT_BYTES = 56 * 1024 * 1024

_BF = jnp.bfloat16
_F32 = jnp.float32


def _slope(h):
    return 2.0 ** (-8.0 * (h + 1) / N_HEADS)


def _dot(a, b):
    return jnp.dot(a, b, preferred_element_type=_F32)


def _dot_panels(a, panel_refs):
    return jnp.concatenate([_dot(a, ref[...]) for ref in panel_refs], axis=1)


def _rms(x, g):
    ms = jnp.mean(x * x, axis=-1, keepdims=True)
    return x * lax.rsqrt(ms + RMS_EPS) * g


def _softmax_with_sink(s, sink):
    m = jnp.maximum(jnp.max(s, axis=-1, keepdims=True), sink)
    e = jnp.exp(s - m)
    den = jnp.sum(e, axis=-1, keepdims=True) + jnp.exp(sink - m)
    return e * (1.0 / den)


def _softmax_with_sink_parts(s, sink):
    m = jnp.maximum(jnp.max(s, axis=-1, keepdims=True), sink)
    e = jnp.exp(s - m)
    den = jnp.sum(e, axis=-1, keepdims=True) + jnp.exp(sink - m)
    return e, 1.0 / den


def _softmax(s):
    m = jnp.max(s, axis=-1, keepdims=True)
    e = jnp.exp(s - m)
    return e * (1.0 / jnp.sum(e, axis=-1, keepdims=True))


def _window_sums(u_ext, axis):
    out = {}
    s = u_ext
    w = 1
    while w < max(POOL_WINDOWS):
        s = s + pltpu.roll(s, w, axis)
        w *= 2
        out[w] = s
    return out


def _interleave(streams, skew):
    pending = list(streams)
    live = []
    rnd = 0
    while live or pending:
        if pending and rnd % skew == 0:
            live.append(pending.pop(0))
        for g in list(live):
            try:
                next(g)
            except StopIteration:
                live.remove(g)
        rnd += 1


def _ffn_final_stages(x2, gffn, wup_ref, wdown_ref, gfinal, store):
    hn = _rms(x2, gffn).astype(_BF)
    yield
    acc = x2
    for c in range(D_FF // FF_CHUNK):
        hc = _dot(hn, wup_ref[:, c * FF_CHUNK:(c + 1) * FF_CHUNK])
        yield
        hc = jnp.maximum(hc, 0.0)
        hc = (hc * hc).astype(_BF)
        acc = acc + _dot(hc, wdown_ref[c * FF_CHUNK:(c + 1) * FF_CHUNK, :])
        yield
    store(_rms(acc, gfinal))


def _prompt_mixer_kernel(x_ref, gmix_ref, win_ref, sink_ref, wpool_ref, pscale_ref, wouta_ref, woutb_ref,
                         wco_ref, wup_ref, wdown_ref, qc_ref, mk_ref, mv_ref,
                         x1_ref, klast_ref, vlast_ref, plast_ref, wco_bf_ref, wup_bf_ref, wdown_bf_ref,
                         os_ref, kt_scr, v_scr, u_scr, bias_scr, inv_scr, *, tile, nseq):
    b = pl.program_id(0)
    s = pl.program_id(1)
    nb = tile // BLOCK

    for src, dst in ((wco_ref, wco_bf_ref), (wup_ref, wup_bf_ref), (wdown_ref, wdown_bf_ref)):
        dst[...] = src[...].astype(_BF)

    @pl.when((b == 0) & (s == 0))
    def _():
        qi = lax.broadcasted_iota(jnp.int32, (BLOCK, 2 * BLOCK), 0)
        kc = lax.broadcasted_iota(jnp.int32, (BLOCK, 2 * BLOCK), 1)
        dist = qi + BLOCK - kc
        valid = (dist >= 0) & (dist <= WINDOW)
        valid_first = valid & (kc >= BLOCK)
        distf = dist.astype(_F32)
        for h in range(N_HEADS):
            ali = -_slope(h) * distf
            bias_scr[0, h] = jnp.where(valid, ali, NEG_INF)
            bias_scr[1, h] = jnp.where(valid_first, ali, NEG_INF)
        prow = lax.broadcasted_iota(jnp.int32, (BLOCK, POOL_GROUP_WIDTH), 0)
        for g, w in enumerate(POOL_WINDOWS):
            inv_scr[0, g] = jnp.full((BLOCK, POOL_GROUP_WIDTH), 1.0 / w, _F32)
            inv_scr[1, g] = 1.0 / jnp.minimum(prow + 1, w).astype(_F32)

    @pl.when(s == 0)
    def _():
        kt_scr[...] = jnp.zeros((KV_WIDTH, BLOCK), _BF)
        v_scr[...] = jnp.zeros((BLOCK, KV_WIDTH), _BF)
        u_scr[...] = jnp.zeros((POOL_PAD, POOL_WIDTH), _F32)

    lane = lax.broadcasted_iota(jnp.int32, (BLOCK, 2 * HEAD_DIM), 1)
    lo = lane < HEAD_DIM
    gmix = gmix_ref[...]
    zeros_kt = jnp.zeros((HEAD_DIM, 2 * BLOCK), _BF)

    def project(j):
        x = x_ref[0, j * BLOCK:(j + 1) * BLOCK, :]
        proj = _dot(_rms(x, gmix).astype(_BF), win_ref[...])
        k = proj[:, K_OFF:K_OFF + KV_WIDTH]
        v = proj[:, V_OFF:V_OFF + KV_WIDTH]
        u = proj[:, U_OFF:]
        kt = k.T
        if j == nb - 1:
            klast_ref[0] = kt
            vlast_ref[0] = v.T
            plast_ref[0] = u[BLOCK - POOL_PAD:]
        return dict(x=x, q=(proj[:, :ATTN_WIDTH] * Q_SCALE).astype(_BF), kt=kt.astype(_BF), v=v.astype(_BF), u=u)

    def pool(j, u_hist, u):
        first = ((s == 0) & (j == 0)).astype(jnp.int32) if j == 0 else 0
        ys = []
        for g, w in enumerate(POOL_WINDOWS):
            c0 = g * POOL_GROUP_WIDTH
            u_ext = jnp.concatenate([u_hist[:, c0:c0 + POOL_GROUP_WIDTH], u[:, c0:c0 + POOL_GROUP_WIDTH]], axis=0)
            sw = _window_sums(u_ext, 0)[w][POOL_PAD:]
            d = (sw * inv_scr[first, g] - u_ext[POOL_PAD:]).astype(_BF)
            y = _dot(d, wpool_ref[g]) * pscale_ref[:, c0:c0 + POOL_GROUP_WIDTH]
            ys.append(y.astype(_BF))
        return ys

    def kv_operands(kt_prev, v_prev, blk):
        kt2 = jnp.concatenate([kt_prev, blk["kt"]], axis=1)
        v2 = jnp.concatenate([v_prev, blk["v"]], axis=0)
        kt_pair = jnp.concatenate([jnp.concatenate([kt2[:HEAD_DIM], zeros_kt], axis=0),
                                   jnp.concatenate([zeros_kt, kt2[HEAD_DIM:]], axis=0)], axis=1)
        lane2 = lax.broadcasted_iota(jnp.int32, v2.shape, 1)
        zero = jnp.zeros_like(v2)
        v_pair = jnp.concatenate([jnp.where(lane2 < HEAD_DIM, v2, zero), jnp.where(lane2 < HEAD_DIM, zero, v2)],
                                 axis=0)
        return kt_pair, v_pair

    def wave_scores(j, blk, kt_pair, pairs):
        first = ((s == 0) & (j == 0)).astype(jnp.int32) if j == 0 else 0
        scores = []
        for p in pairs:
            sc = _dot(blk["q"][:, p * 128:(p + 1) * 128], kt_pair)
            scores.append(sc[:, :2 * BLOCK] + bias_scr[first, p])
            scores.append(sc[:, 2 * BLOCK:] + bias_scr[first, p + GROUP])
        return scores

    def wave_values(v_pair, pairs, scores):
        slabs = []
        for i, p in enumerate(pairs):
            es, invs = [], []
            for half, h in ((0, p), (1, p + GROUP)):
                e, inv = _softmax_with_sink_parts(scores[2 * i + half], sink_ref[h])
                es.append(e.astype(_BF))
                invs.append(inv)
            o = _dot(jnp.concatenate(es, axis=1), v_pair)
            slabs.append((o * jnp.where(lo, invs[0], invs[1])).astype(_BF))
        return slabs

    def output(j, blk, slabs):
        cat = jnp.concatenate(slabs, axis=1)
        x1_ref[0, j * BLOCK:(j + 1) * BLOCK, :] = blk["x"] + _dot_panels(cat, (wouta_ref, woutb_ref))

    heads = [slice(h * CROSS_HEAD_DIM, (h + 1) * CROSS_HEAD_DIM) for h in range(N_CROSS_HEADS)]

    def mem_scores(i):
        q = qc_ref[i * nseq:(i + 1) * nseq, :].astype(_BF)
        return [lax.dot_general(q[:, c], _load_mem_head(mk_ref, i, h).astype(_BF), (((1,), (1,)), ((), ())),
                                preferred_element_type=_F32) for h, c in enumerate(heads)]

    def mem_values(i, scores):
        outs = [_dot(_softmax(sc).astype(_BF), _load_mem_head(mv_ref, i, h).astype(_BF))
                for h, sc in enumerate(scores)]
        os_ref[i * nseq:(i + 1) * nseq, :] = jnp.concatenate(outs, axis=1)

    kt_prev, v_prev, u_hist = kt_scr[...], v_scr[...], u_scr[...]
    blk = project(0)
    done = None
    for j in range(nb):
        kt_pair, v_pair = kv_operands(kt_prev, v_prev, blk)
        sc0 = wave_scores(j, blk, kt_pair, (0, 1))
        nxt = project(j + 1) if j + 1 < nb else None
        msc = mem_scores(j)
        sc1 = wave_scores(j, blk, kt_pair, (2, 3))
        pooled = pool(j, u_hist, blk["u"])
        at0 = wave_values(v_pair, (0, 1), sc0)
        if done is not None:
            output(*done)
        mem_values(j, msc)
        at1 = wave_values(v_pair, (2, 3), sc1)
        done = (j, blk, at0 + at1 + pooled)
        kt_prev, v_prev, u_hist = blk["kt"], blk["v"], blk["u"][BLOCK - POOL_PAD:]
        blk = nxt
    output(*done)
    kt_scr[...] = kt_prev
    v_scr[...] = v_prev
    u_scr[...] = u_hist


def _prompt_mixer(x, gmix, win, sinks, wpool, pscale, wout, tail_weights, qc, mk, mv, tile):
    B, S, D = x.shape
    ns = S // tile
    nsteps = B * ns
    nmem = tile // BLOCK
    nseq = qc.shape[0] // mk.shape[0]
    assert mk.shape[0] == nsteps * nmem
    const = lambda *shape: pl.BlockSpec(shape, lambda b, s: (0,) * len(shape))
    chunk = lambda w: pl.BlockSpec((w.shape[0] // nsteps, w.shape[1]), lambda b, s: (b * ns + s, 0))
    assert all(w.shape[0] % (16 * nsteps) == 0 for w in tail_weights)
    mem_rows = pl.BlockSpec((nmem * nseq, D), lambda b, s: (b * ns + s, 0))
    mem_cache = pl.BlockSpec((nmem, N_MEM * MEM_ROWS, 128), lambda b, s: (b * ns + s, 0, 0))
    return pl.pallas_call(
        functools.partial(_prompt_mixer_kernel, tile=tile, nseq=nseq),
        grid=(B, ns),
        in_specs=[
            pl.BlockSpec((1, tile, D), lambda b, s: (b, s, 0)),
            const(1, D),
            const(D, IN_WIDTH),
            pl.BlockSpec(memory_space=pltpu.SMEM),
            const(len(POOL_WINDOWS), POOL_GROUP_WIDTH, POOL_GROUP_WIDTH),
            const(1, POOL_WIDTH),
            const(D, WOUT_SPLIT),
            const(D, D - WOUT_SPLIT),
        ] + [chunk(w) for w in tail_weights] + [mem_rows, mem_cache, mem_cache],
        out_specs=[
            pl.BlockSpec((1, tile, D), lambda b, s: (b, s, 0)),
            pl.BlockSpec((1, BLOCK, KV_WIDTH), lambda b, s: (b, 0, 0)),
            pl.BlockSpec((1, BLOCK, KV_WIDTH), lambda b, s: (b, 0, 0)),
            pl.BlockSpec((1, POOL_PAD, POOL_WIDTH), lambda b, s: (b, 0, 0)),
        ] + [chunk(w) for w in tail_weights] + [mem_rows],
        out_shape=[
            jax.ShapeDtypeStruct((B, S, D), _F32),
            jax.ShapeDtypeStruct((B, BLOCK, KV_WIDTH), _F32),
            jax.ShapeDtypeStruct((B, BLOCK, KV_WIDTH), _F32),
            jax.ShapeDtypeStruct((B, POOL_PAD, POOL_WIDTH), _F32),
        ] + [jax.ShapeDtypeStruct(w.shape, _BF) for w in tail_weights] + [jax.ShapeDtypeStruct(qc.shape, _F32)],
        scratch_shapes=[
            pltpu.VMEM((KV_WIDTH, BLOCK), _BF),
            pltpu.VMEM((BLOCK, KV_WIDTH), _BF),
            pltpu.VMEM((POOL_PAD, POOL_WIDTH), _F32),
            pltpu.VMEM((2, N_HEADS, BLOCK, 2 * BLOCK), _F32),
            pltpu.VMEM((2, len(POOL_WINDOWS), BLOCK, POOL_GROUP_WIDTH), _F32),
        ],
        compiler_params=pltpu.CompilerParams(
            dimension_semantics=("arbitrary", "arbitrary"), vmem_limit_bytes=VMEM_LIMIT_BYTES),
        name="prompt_mixer",
    )(x, gmix, win, sinks, wpool, pscale, *wout, *tail_weights, qc, mk, mv)


def _mem_kv_kernel(mem_ref, gmem_ref, wck_ref, wcv_ref, k_ref, v_ref, kt_ref, vb_ref, wck_scr, wcv_scr):
    @pl.when(pl.program_id(0) == 0)
    def _():
        wck_scr[...] = wck_ref[...].astype(_BF)
        wcv_scr[...] = wcv_ref[...].astype(_BF)

    hm = _rms(mem_ref[0], gmem_ref[...]).astype(_BF)
    k = _dot(hm, wck_scr[...])
    v = _dot(hm, wcv_scr[...])
    vb_ref[0] = v.astype(_BF)
    for h in range(N_CROSS_HEADS):
        kt_ref[0, h] = k[:, h * CROSS_HEAD_DIM:(h + 1) * CROSS_HEAD_DIM].T.astype(_BF)
        for half in range(CROSS_HEAD_DIM // 128):
            c0 = h * CROSS_HEAD_DIM + half * 128
            rows = pl.ds(half * N_CROSS_HEADS + h, N_MEM, stride=MEM_ROWS)
            k_ref[0, rows, :] = k[:, c0:c0 + 128]
            v_ref[0, rows, :] = v[:, c0:c0 + 128]


def _mem_kv(mem, gmem, wck, wcv):
    B, M, D = mem.shape
    const = lambda *shape: pl.BlockSpec(shape, lambda b: (0,) * len(shape))
    return pl.pallas_call(
        _mem_kv_kernel,
        grid=(B,),
        in_specs=[pl.BlockSpec((1, M, D), lambda b: (b, 0, 0)), const(1, D), const(D, D), const(D, D)],
        out_specs=[
            pl.BlockSpec((1, M * MEM_ROWS, 128), lambda b: (b, 0, 0)),
            pl.BlockSpec((1, M * MEM_ROWS, 128), lambda b: (b, 0, 0)),
            pl.BlockSpec((1, N_CROSS_HEADS, CROSS_HEAD_DIM, M), lambda b: (b, 0, 0, 0)),
            pl.BlockSpec((1, M, D), lambda b: (b, 0, 0)),
        ],
        out_shape=[
            jax.ShapeDtypeStruct((B, M * MEM_ROWS, 128), _F32),
            jax.ShapeDtypeStruct((B, M * MEM_ROWS, 128), _F32),
            jax.ShapeDtypeStruct((B, N_CROSS_HEADS, CROSS_HEAD_DIM, M), _BF),
            jax.ShapeDtypeStruct((B, M, D), _BF),
        ],
        scratch_shapes=[pltpu.VMEM((D, D), _BF), pltpu.VMEM((D, D), _BF)],
        compiler_params=pltpu.CompilerParams(
            dimension_semantics=("arbitrary",), vmem_limit_bytes=VMEM_LIMIT_BYTES),
        name="prompt_mem_kv",
    )(mem, gmem, wck, wcv)


def _mem_cache_unrows(rows):
    nb = rows.shape[0]
    c = rows.reshape(nb, N_MEM, CROSS_HEAD_DIM // 128, N_CROSS_HEADS, 128)
    return c.transpose(0, 1, 3, 2, 4).reshape(nb, N_MEM, N_CROSS_HEADS, CROSS_HEAD_DIM)


def _prompt_tail_kernel(x1_ref, gcross_ref, wcq_ref, kt_ref, vb_ref, wco_ref, gffn_ref, wup_ref, wdown_ref,
                        gfinal_ref, y_ref, *, tile, sub):
    gcross, gffn, gfinal = gcross_ref[...], gffn_ref[...], gfinal_ref[...]

    def stream(r0):
        x1 = x1_ref[0, r0:r0 + sub, :]
        hn = _rms(x1, gcross).astype(_BF)
        yield
        q = (_dot(hn, wcq_ref[...]) * CQ_SCALE).astype(_BF)
        yield
        heads = [slice(h * CROSS_HEAD_DIM, (h + 1) * CROSS_HEAD_DIM) for h in range(N_CROSS_HEADS)]
        scores = [_dot(q[:, c], kt_ref[0, h]) for h, c in enumerate(heads)]
        yield
        outs = [_dot(_softmax(sc).astype(_BF), vb_ref[0, :, c]).astype(_BF) for sc, c in zip(scores, heads)]
        yield
        x2 = x1 + _dot(jnp.concatenate(outs, axis=1), wco_ref[...])
        yield

        def store(y):
            y_ref[0, r0:r0 + sub, :] = y

        yield from _ffn_final_stages(x2, gffn, wup_ref, wdown_ref, gfinal, store)

    _interleave([stream(r0) for r0 in range(0, tile, sub)], TAIL_SKEW)


def _single(shape, index_map):
    return pl.BlockSpec(shape, index_map, pipeline_mode=pl.Buffered(1))


def _prompt_tail(x1, gcross, wcq, kt, vb, wco, gffn, wup, wdown, gfinal, tile, sub):
    B, S, D = x1.shape
    const = lambda *shape: _single(shape, lambda b, s: (0,) * len(shape))
    return pl.pallas_call(
        functools.partial(_prompt_tail_kernel, tile=tile, sub=sub),
        grid=(B, S // tile),
        in_specs=[
            pl.BlockSpec((1, tile, D), lambda b, s: (b, s, 0)),
            const(1, D),
            const(D, D),
            pl.BlockSpec((1, N_CROSS_HEADS, CROSS_HEAD_DIM, N_MEM), lambda b, s: (b, 0, 0, 0)),
            pl.BlockSpec((1, N_MEM, D), lambda b, s: (b, 0, 0)),
            const(D, D),
            const(1, D),
            const(D, D_FF),
            const(D_FF, D),
            const(1, D),
        ],
        out_specs=pl.BlockSpec((1, tile, D), lambda b, s: (b, s, 0)),
        out_shape=jax.ShapeDtypeStruct((B, S, D), _F32),
        compiler_params=pltpu.CompilerParams(
            dimension_semantics=("arbitrary", "arbitrary"), vmem_limit_bytes=VMEM_LIMIT_BYTES),
        name="prompt_tail",
    )(x1, gcross, wcq, kt, vb, wco, gffn, wup, wdown, gfinal)


def _sample_mixer_kernel(x_ref, gmix_ref, win_ref, sinkcol_ref, ck_ref, cv_ref, st_ref, wpool_ref, pscale_ref,
                         wouta_ref, woutb_ref, gcross_ref, wcq_ref,
                         x1_ref, qc_ref, wk_ref, wv_ref, pool_ref, wcq_bf_ref,
                         kc_scr, vc_scr, u_scr, *, nbatch, sub, nseq, past_len):
    @pl.when(pl.program_id(0) == 0)
    def _():
        wcq_bf_ref[...] = wcq_ref[...].astype(_BF)

    rows = sub * nseq
    nkeys = 2 * WINDOW
    gmix, gcross = gmix_ref[...], gcross_ref[...]
    sinkcol = sinkcol_ref[...][None]

    t = lax.broadcasted_iota(jnp.int32, (nseq, nkeys), 0)
    c = lax.broadcasted_iota(jnp.int32, (nseq, nkeys), 1)
    dist = jnp.where(c < WINDOW, WINDOW + t - c, t - (c - WINDOW))
    valid = (dist >= 0) & (dist <= WINDOW) & (c < WINDOW + nseq)
    distf = dist.astype(_F32)
    bias = jnp.concatenate([jnp.where(valid, -_slope(h) * distf, NEG_INF) for h in range(N_HEADS)], axis=0)[None]
    lane = lax.broadcasted_iota(jnp.int32, (rows, 2 * HEAD_DIM), 1)
    lo = lane < HEAD_DIM
    tpos = past_len + lax.broadcasted_iota(jnp.int32, (nseq, POOL_GROUP_WIDTH), 0)
    zpad = jnp.zeros((sub, nkeys - WINDOW, KV_WIDTH), _BF)

    def stream(b0):
        bs = slice(b0, b0 + sub)
        rs = slice(b0 * nseq, (b0 + sub) * nseq)
        x = x_ref[rs, :]
        proj = _dot(_rms(x, gmix).astype(_BF), win_ref[...])
        yield
        k_new = proj[:, K_OFF:K_OFF + KV_WIDTH].reshape(sub, nseq, KV_WIDTH)
        v_new = proj[:, V_OFF:V_OFF + KV_WIDTH].reshape(sub, nseq, KV_WIDTH)
        ck = jnp.swapaxes(ck_ref[bs], 1, 2)
        cv = jnp.swapaxes(cv_ref[bs], 1, 2)
        wk_ref[bs, :WINDOW - nseq, :] = ck[:, nseq:, :]
        wk_ref[bs, WINDOW - nseq:, :] = k_new
        wv_ref[bs, :WINDOW - nseq, :] = cv[:, nseq:, :]
        wv_ref[bs, WINDOW - nseq:, :] = v_new
        kc_scr[bs, WINDOW:, :] = zpad
        vc_scr[bs, WINDOW:, :] = zpad
        kc_scr[bs, :WINDOW, :] = ck.astype(_BF)
        vc_scr[bs, :WINDOW, :] = cv.astype(_BF)
        kc_scr[bs, WINDOW:WINDOW + nseq, :] = k_new.astype(_BF)
        vc_scr[bs, WINDOW:WINDOW + nseq, :] = v_new.astype(_BF)
        qsc = proj[:, :ATTN_WIDTH] * Q_SCALE
        q_lo, q_hi = [], []
        for p in range(GROUP):
            slab = qsc[:, p * 128:(p + 1) * 128]
            q_lo.append(jnp.where(lo, slab, 0.0).reshape(sub, nseq, 128))
            q_hi.append(jnp.where(lo, 0.0, slab).reshape(sub, nseq, 128))
        qm = jnp.concatenate(q_lo + q_hi, axis=1).astype(_BF)
        sc = jnp.einsum('bqd,bkd->bqk', qm, kc_scr[bs], preferred_element_type=_F32) + bias
        yield
        pr = _softmax_with_sink(sc, sinkcol).astype(_BF)
        o = jnp.einsum('bqk,bkd->bqd', pr, vc_scr[bs], preferred_element_type=_F32)
        yield
        attn = []
        for p in range(GROUP):
            o_lo = o[:, p * nseq:(p + 1) * nseq, :].reshape(rows, 128)
            o_hi = o[:, (p + GROUP) * nseq:(p + GROUP + 1) * nseq, :].reshape(rows, 128)
            attn.append(jnp.where(lo, o_lo, o_hi).astype(_BF))
        ext = POOL_PAD + nseq
        seq_rows = lambda r: pl.ds(b0 * ext + r, sub, stride=ext)
        pooled = []
        for g, w in enumerate(POOL_WINDOWS):
            c0 = g * POOL_GROUP_WIDTH
            u_scr[g, seq_rows(0), :] = jnp.zeros((sub, POOL_GROUP_WIDTH), _F32)
            for r in range(POOL_HIST):
                u_scr[g, seq_rows(POOL_PAD - POOL_HIST + r), :] = st_ref[r, bs, c0:c0 + POOL_GROUP_WIDTH]
            for i in range(sub):
                u_scr[g, (b0 + i) * ext + POOL_PAD:(b0 + i + 1) * ext, :] = (
                    proj[i * nseq:(i + 1) * nseq, U_OFF + c0:U_OFF + c0 + POOL_GROUP_WIDTH])
            for r in range(POOL_HIST):
                pool_ref[r, bs, c0:c0 + POOL_GROUP_WIDTH] = u_scr[g, seq_rows(ext - POOL_HIST + r), :]
            ug = u_scr[g, b0 * ext:(b0 + sub) * ext, :].reshape(sub, ext, POOL_GROUP_WIDTH)
            sw = _window_sums(ug, 1)[w][:, POOL_PAD:, :]
            cnt = jnp.minimum(tpos + 1, w).astype(_F32)
            d = (sw / cnt[None] - ug[:, POOL_PAD:, :]).reshape(rows, POOL_GROUP_WIDTH).astype(_BF)
            y = _dot(d, wpool_ref[g]) * pscale_ref[:, c0:c0 + POOL_GROUP_WIDTH]
            pooled.append(y.astype(_BF))
        yield
        x1 = x + _dot_panels(jnp.concatenate(attn + pooled, axis=1), (wouta_ref, woutb_ref))
        x1_ref[rs, :] = x1
        yield
        qc_ref[rs, :] = _dot(_rms(x1, gcross).astype(_BF), wcq_bf_ref[...]) * CQ_SCALE

    _interleave([stream(b0) for b0 in range(0, nbatch, sub)], SAMPLE_MIXER_SKEW)


def _sample_mixer(x, gmix, win, sinkcol, ck, cv, st, wpool, pscale, wout, gcross, wcq, nbatch, sub, nseq,
                  past_len):
    R, D = x.shape
    rows = nbatch * nseq
    const = lambda *shape: pl.BlockSpec(shape, lambda i: (0,) * len(shape))
    return pl.pallas_call(
        functools.partial(_sample_mixer_kernel, nbatch=nbatch, sub=sub, nseq=nseq, past_len=past_len),
        grid=(R // rows,),
        in_specs=[
            pl.BlockSpec((rows, D), lambda i: (i, 0)),
            const(1, D),
            const(D, IN_WIDTH),
            const(N_HEADS * nseq, 1),
            pl.BlockSpec((nbatch, WINDOW, KV_WIDTH), lambda i: (i, 0, 0)),
            pl.BlockSpec((nbatch, WINDOW, KV_WIDTH), lambda i: (i, 0, 0)),
            pl.BlockSpec((POOL_HIST, nbatch, POOL_WIDTH), lambda i: (0, i, 0)),
            const(len(POOL_WINDOWS), POOL_GROUP_WIDTH, POOL_GROUP_WIDTH),
            const(1, POOL_WIDTH),
            const(D, WOUT_SPLIT),
            const(D, D - WOUT_SPLIT),
            const(1, D),
            const(D, D),
        ],
        out_specs=[
            pl.BlockSpec((rows, D), lambda i: (i, 0)),
            pl.BlockSpec((rows, D), lambda i: (i, 0)),
            pl.BlockSpec((nbatch, WINDOW, KV_WIDTH), lambda i: (i, 0, 0)),
            pl.BlockSpec((nbatch, WINDOW, KV_WIDTH), lambda i: (i, 0, 0)),
            pl.BlockSpec((POOL_HIST, nbatch, POOL_WIDTH), lambda i: (0, i, 0)),
            const(D, D),
        ],
        out_shape=[
            jax.ShapeDtypeStruct((R, D), _F32),
            jax.ShapeDtypeStruct((R, D), _F32),
            jax.ShapeDtypeStruct((R // nseq, WINDOW, KV_WIDTH), _F32),
            jax.ShapeDtypeStruct((R // nseq, WINDOW, KV_WIDTH), _F32),
            jax.ShapeDtypeStruct((POOL_HIST, R // nseq, POOL_WIDTH), _F32),
            jax.ShapeDtypeStruct((D, D), _BF),
        ],
        scratch_shapes=[
            pltpu.VMEM((nbatch, 2 * WINDOW, KV_WIDTH), _BF),
            pltpu.VMEM((nbatch, 2 * WINDOW, KV_WIDTH), _BF),
            pltpu.VMEM((len(POOL_WINDOWS), nbatch * (POOL_PAD + nseq), POOL_GROUP_WIDTH), _F32),
        ],
        compiler_params=pltpu.CompilerParams(
            dimension_semantics=("arbitrary",), vmem_limit_bytes=VMEM_LIMIT_BYTES),
        name="sample_mixer",
    )(x, gmix, win, sinkcol, ck, cv, st, wpool, pscale, *wout, gcross, wcq)


def _mem_cache_rows(cache):
    nb = cache.shape[0]
    c = cache.reshape(nb, N_MEM, N_CROSS_HEADS, CROSS_HEAD_DIM // 128, 128)
    return c.transpose(0, 1, 3, 2, 4).reshape(nb, N_MEM * MEM_ROWS, 128)


def _load_mem_head(ref, b, h):
    halves = [ref[b, pl.ds(half * N_CROSS_HEADS + h, N_MEM, stride=MEM_ROWS), :]
              for half in range(CROSS_HEAD_DIM // 128)]
    return jnp.concatenate(halves, axis=1)


def _sample_tail_kernel(x1_ref, o_ref, wco_ref, gffn_ref, wup_ref, wdown_ref, gfinal_ref, y_ref, *, tile, sub):
    gffn, gfinal = gffn_ref[...], gfinal_ref[...]

    def stream(r0):
        x2 = x1_ref[r0:r0 + sub, :] + _dot(o_ref[r0:r0 + sub, :].astype(_BF), wco_ref[...])
        yield

        def store(y):
            y_ref[r0:r0 + sub, :] = y

        yield from _ffn_final_stages(x2, gffn, wup_ref, wdown_ref, gfinal, store)

    _interleave([stream(r0) for r0 in range(0, tile, sub)], TAIL_SKEW)


def _sample_tail(x1, o, wco, gffn, wup, wdown, gfinal, tile, sub):
    R, D = x1.shape
    const = lambda *shape: _single(shape, lambda i: (0,) * len(shape))
    return pl.pallas_call(
        functools.partial(_sample_tail_kernel, tile=tile, sub=sub),
        grid=(R // tile,),
        in_specs=[
            pl.BlockSpec((tile, D), lambda i: (i, 0)),
            pl.BlockSpec((tile, D), lambda i: (i, 0)),
            const(D, D),
            const(1, D),
            const(D, D_FF),
            const(D_FF, D),
            const(1, D),
        ],
        out_specs=pl.BlockSpec((tile, D), lambda i: (i, 0)),
        out_shape=jax.ShapeDtypeStruct((R, D), _F32),
        compiler_params=pltpu.CompilerParams(
            dimension_semantics=("arbitrary",), vmem_limit_bytes=VMEM_LIMIT_BYTES),
        name="sample_tail",
    )(x1, o, wco, gffn, wup, wdown, gfinal)


PROMPT_TILE = 512
TAIL_TILE = 1024
TAIL_SUB = 256
TAIL_SKEW = 5
SAMPLE_MIXER_BATCH = 32
SAMPLE_MIXER_SUB = 16
SAMPLE_MIXER_SKEW = 3


def kernel(x_prompt, x_sample, cache_win_k, cache_win_v, state_pool, cache_mem_k, cache_mem_v, mem_prompt,
           g_mix, w_in, attn_sinks, w_pool, pool_scale, w_out, g_cross, g_mem, w_cq, w_ck, w_cv, w_co,
           g_ffn, w_up, w_down, g_final):
    depth = g_mix.shape[0]
    assert depth == 1, "one layer per step"
    B, S, D = x_prompt.shape
    DB, T, _ = x_sample.shape
    past_len = PAST_LEN
    l = 0

    win = jnp.concatenate([_pair_heads(w_in[l][:, :ATTN_WIDTH], 1), w_in[l][:, ATTN_WIDTH:]], axis=1).astype(_BF)
    wout = jnp.concatenate([_pair_heads(w_out[l][:ATTN_WIDTH, :], 0), w_out[l][ATTN_WIDTH:, :]], axis=0)
    wout = (wout[:, :WOUT_SPLIT].astype(_BF), wout[:, WOUT_SPLIT:].astype(_BF))
    wpool = w_pool[l].astype(_BF)
    gmix, gcross, gmem, gffn = (g[l].reshape(1, D) for g in (g_mix, g_cross, g_mem, g_ffn))
    gfinal = g_final.reshape(1, D)
    pscale = pool_scale[l].reshape(1, POOL_WIDTH)
    sinks = attn_sinks[l]

    xs = x_sample.reshape(DB * T, D)
    ck = cache_win_k[l].transpose(0, 2, 3, 1).reshape(DB, KV_WIDTH, WINDOW)
    cv = cache_win_v[l].transpose(0, 2, 3, 1).reshape(DB, KV_WIDTH, WINDOW)
    st = state_pool[l].transpose(1, 0, 2)
    sinkcol = jnp.repeat(sinks, T).reshape(N_HEADS * T, 1)
    x1s, qc, wk_s, wv_s, pool_s, wcq = _sample_mixer(xs, gmix, win, sinkcol, ck, cv, st, wpool, pscale, wout,
                                                gcross, w_cq[l], SAMPLE_MIXER_BATCH, SAMPLE_MIXER_SUB, T, past_len)

    mk = _mem_cache_rows(cache_mem_k[l])
    mv = _mem_cache_rows(cache_mem_v[l])
    x1p, klast, vlast, plast, wco, wup, wdown, o_s = _prompt_mixer(
        x_prompt, gmix, win, sinks, wpool, pscale, wout, (w_co[l], w_up[l], w_down[l]), qc, mk, mv, PROMPT_TILE)
    mem_k, mem_v, mem_kt, mem_vb = _mem_kv(mem_prompt, gmem, w_ck[l], w_cv[l])
    y_prompt = _prompt_tail(x1p, gcross, wcq, mem_kt, mem_vb, wco, gffn, wup, wdown, gfinal, TAIL_TILE, TAIL_SUB)

    y_sample = _sample_tail(x1s, o_s, wco, gffn, wup, wdown, gfinal, TAIL_TILE, TAIL_SUB).reshape(DB, T, D)

    return (
        y_prompt,
        y_sample,
        klast.reshape(B, N_KV_HEADS, HEAD_DIM, WINDOW).transpose(0, 3, 1, 2)[None],
        vlast.reshape(B, N_KV_HEADS, HEAD_DIM, WINDOW).transpose(0, 3, 1, 2)[None],
        plast[:, POOL_PAD - POOL_HIST:, :][None],
        _mem_cache_unrows(mem_k)[None],
        _mem_cache_unrows(mem_v)[None],
        wk_s.reshape(1, DB, WINDOW, N_KV_HEADS, HEAD_DIM),
        wv_s.reshape(1, DB, WINDOW, N_KV_HEADS, HEAD_DIM),
        pool_s.transpose(1, 0, 2)[None],
    )
```

```python
import functools

import jax
import jax.numpy as jnp
from jax import lax
from jax.experimental import pallas as pl
from jax.experimental.pallas import tpu as pltpu

D_MODEL = 1024
PAST_LEN = 16384
HEAD_DIM = 64
N_HEADS = 8
N_KV_HEADS = 2
GROUP = N_HEADS // N_KV_HEADS
ATTN_WIDTH = N_HEADS * HEAD_DIM
KV_WIDTH = N_KV_HEADS * HEAD_DIM
WINDOW = 128
BLOCK = WINDOW
POOL_WIDTH = D_MODEL - ATTN_WIDTH
POOL_WINDOWS = (2, 4, 8, 16)
POOL_GROUP_WIDTH = 128
POOL_HIST = 15
POOL_PAD = 16
IN_WIDTH = ATTN_WIDTH + 2 * KV_WIDTH + POOL_WIDTH
N_MEM = 256
N_CROSS_HEADS = 4
CROSS_HEAD_DIM = 256
MEM_ROWS = N_CROSS_HEADS * (CROSS_HEAD_DIM // 128)
D_FF = 4 * D_MODEL
FF_CHUNK = 1024
RMS_EPS = 1e-5
NEG_INF = -1e30
Q_SCALE = HEAD_DIM ** -0.5
CQ_SCALE = CROSS_HEAD_DIM ** -0.5
K_OFF = ATTN_WIDTH
V_OFF = ATTN_WIDTH + KV_WIDTH
U_OFF = ATTN_WIDTH + 2 * KV_WIDTH


def _pair_heads(w, axis):
    shape = w.shape
    split = shape[:axis] + (N_KV_HEADS, GROUP, HEAD_DIM) + shape[axis + 1:]
    return jnp.swapaxes(w.reshape(split), axis, axis + 1).reshape(shape)


WOUT_SPLIT = 768

VMEM_LIMIT_BYTES = 56 * 1024 * 1024

_BF = jnp.bfloat16
_F32 = jnp.float32


def _slope(h):
    return 2.0 ** (-8.0 * (h + 1) / N_HEADS)


def _dot(a, b):
    return jnp.dot(a, b, preferred_element_type=_F32)


def _dot_panels(a, panel_refs):
    return jnp.concatenate([_dot(a, ref[...]) for ref in panel_refs], axis=1)


def _rms(x, g):
    ms = jnp.mean(x * x, axis=-1, keepdims=True)
    return x * lax.rsqrt(ms + RMS_EPS) * g


def _softmax_with_sink(s, sink):
    m = jnp.maximum(jnp.max(s, axis=-1, keepdims=True), sink)
    e = jnp.exp(s - m)
    den = jnp.sum(e, axis=-1, keepdims=True) + jnp.exp(sink - m)
    return e * (1.0 / den)


def _softmax_with_sink_parts(s, sink):
    m = jnp.maximum(jnp.max(s, axis=-1, keepdims=True), sink)
    e = jnp.exp(s - m)
    den = jnp.sum(e, axis=-1, keepdims=True) + jnp.exp(sink - m)
    return e, 1.0 / den


def _softmax(s):
    m = jnp.max(s, axis=-1, keepdims=True)
    e = jnp.exp(s - m)
    return e * (1.0 / jnp.sum(e, axis=-1, keepdims=True))


def _window_sums(u_ext, axis):
    out = {}
    s = u_ext
    w = 1
    while w < max(POOL_WINDOWS):
        s = s + pltpu.roll(s, w, axis)
        w *= 2
        out[w] = s
    return out


def _interleave(streams, skew):
    pending = list(streams)
    live = []
    rnd = 0
    while live or pending:
        if pending and rnd % skew == 0:
            live.append(pending.pop(0))
        for g in list(live):
            try:
                next(g)
            except StopIteration:
                live.remove(g)
        rnd += 1


def _ffn_final_stages(x2, gffn, wup_ref, wdown_ref, gfinal, store):
    hn = _rms(x2, gffn).astype(_BF)
    yield
    acc = x2
    for c in range(D_FF // FF_CHUNK):
        hc = _dot(hn, wup_ref[:, c * FF_CHUNK:(c + 1) * FF_CHUNK])
        yield
        hc = jnp.maximum(hc, 0.0)
        hc = (hc * hc).astype(_BF)
        acc = acc + _dot(hc, wdown_ref[c * FF_CHUNK:(c + 1) * FF_CHUNK, :])
        yield
    store(_rms(acc, gfinal))


def _prompt_mixer_kernel(x_ref, gmix_ref, win_ref, sink_ref, wpool_ref, pscale_ref, wouta_ref, woutb_ref,
                         wco_ref, wup_ref, wdown_ref, qc_ref, mk_ref, mv_ref,
                         x1_ref, klast_ref, vlast_ref, plast_ref, wco_bf_ref, wup_bf_ref, wdown_bf_ref,
                         os_ref, kt_scr, v_scr, u_scr, bias_scr, inv_scr, *, tile, nseq):
    b = pl.program_id(0)
    s = pl.program_id(1)
    nb = tile // BLOCK

    @pl.when((b == 0) & (s == 0))
    def _():
        qi = lax.broadcasted_iota(jnp.int32, (BLOCK, 2 * BLOCK), 0)
        kc = lax.broadcasted_iota(jnp.int32, (BLOCK, 2 * BLOCK), 1)
        dist = qi + BLOCK - kc
        valid = (dist >= 0) & (dist <= WINDOW)
        valid_first = valid & (kc >= BLOCK)
        distf = dist.astype(_F32)
        for h in range(N_HEADS):
            ali = -_slope(h) * distf
            bias_scr[0, h] = jnp.where(valid, ali, NEG_INF)
            bias_scr[1, h] = jnp.where(valid_first, ali, NEG_INF)
        prow = lax.broadcasted_iota(jnp.int32, (BLOCK, POOL_GROUP_WIDTH), 0)
        for g, w in enumerate(POOL_WINDOWS):
            inv_scr[0, g] = jnp.full((BLOCK, POOL_GROUP_WIDTH), 1.0 / w, _F32)
            inv_scr[1, g] = 1.0 / jnp.minimum(prow + 1, w).astype(_F32)

    @pl.when(s == 0)
    def _():
        kt_scr[...] = jnp.zeros((KV_WIDTH, BLOCK), _BF)
        v_scr[...] = jnp.zeros((BLOCK, KV_WIDTH), _BF)
        u_scr[...] = jnp.zeros((POOL_PAD, POOL_WIDTH), _F32)

    lane = lax.broadcasted_iota(jnp.int32, (BLOCK, 2 * HEAD_DIM), 1)
    lo = lane < HEAD_DIM
    gmix = gmix_ref[...]
    zeros_kt = jnp.zeros((HEAD_DIM, 2 * BLOCK), _BF)

    def project(j):
        x = x_ref[0, j * BLOCK:(j + 1) * BLOCK, :]
        proj = _dot(_rms(x, gmix).astype(_BF), win_ref[...])
        k = proj[:, K_OFF:K_OFF + KV_WIDTH]
        v = proj[:, V_OFF:V_OFF + KV_WIDTH]
        u = proj[:, U_OFF:]
        kt = k.T
        if j == nb - 1:
            klast_ref[0] = kt
            vlast_ref[0] = v.T
            plast_ref[0] = u[BLOCK - POOL_PAD:]
        return dict(x=x, q=(proj[:, :ATTN_WIDTH] * Q_SCALE).astype(_BF), kt=kt.astype(_BF), v=v.astype(_BF), u=u)

    def pool(j, u_hist, u):
        first = ((s == 0) & (j == 0)).astype(jnp.int32) if j == 0 else 0
        ys = []
        for g, w in enumerate(POOL_WINDOWS):
            c0 = g * POOL_GROUP_WIDTH
            u_ext = jnp.concatenate([u_hist[:, c0:c0 + POOL_GROUP_WIDTH], u[:, c0:c0 + POOL_GROUP_WIDTH]], axis=0)
            sw = _window_sums(u_ext, 0)[w][POOL_PAD:]
            d = (sw * inv_scr[first, g] - u_ext[POOL_PAD:]).astype(_BF)
            y = _dot(d, wpool_ref[g]) * pscale_ref[:, c0:c0 + POOL_GROUP_WIDTH]
            ys.append(y.astype(_BF))
        return ys

    def kv_operands(kt_prev, v_prev, blk):
        kt2 = jnp.concatenate([kt_prev, blk["kt"]], axis=1)
        v2 = jnp.concatenate([v_prev, blk["v"]], axis=0)
        kt_pair = jnp.concatenate([jnp.concatenate([kt2[:HEAD_DIM], zeros_kt], axis=0),
                                   jnp.concatenate([zeros_kt, kt2[HEAD_DIM:]], axis=0)], axis=1)
        lane2 = lax.broadcasted_iota(jnp.int32, v2.shape, 1)
        zero = jnp.zeros_like(v2)
        v_pair = jnp.concatenate([jnp.where(lane2 < HEAD_DIM, v2, zero), jnp.where(lane2 < HEAD_DIM, zero, v2)],
                                 axis=0)
        return kt_pair, v_pair

    def wave_scores(j, blk, kt_pair, pairs):
        first = ((s == 0) & (j == 0)).astype(jnp.int32) if j == 0 else 0
        scores = []
        for p in pairs:
            sc = _dot(blk["q"][:, p * 128:(p + 1) * 128], kt_pair)
            scores.append(sc[:, :2 * BLOCK] + bias_scr[first, p])
            scores.append(sc[:, 2 * BLOCK:] + bias_scr[first, p + GROUP])
        return scores

    def wave_values(v_pair, pairs, scores):
        slabs = []
        for i, p in enumerate(pairs):
            es, invs = [], []
            for half, h in ((0, p), (1, p + GROUP)):
                e, inv = _softmax_with_sink_parts(scores[2 * i + half], sink_ref[h])
                es.append(e.astype(_BF))
                invs.append(inv)
            o = _dot(jnp.concatenate(es, axis=1), v_pair)
            slabs.append((o * jnp.where(lo, invs[0], invs[1])).astype(_BF))
        return slabs

    def output(j, blk, slabs):
        cat = jnp.concatenate(slabs, axis=1)
        x1_ref[0, j * BLOCK:(j + 1) * BLOCK, :] = blk["x"] + _dot_panels(cat, (wouta_ref, woutb_ref))

    heads = [slice(h * CROSS_HEAD_DIM, (h + 1) * CROSS_HEAD_DIM) for h in range(N_CROSS_HEADS)]

    def mem_scores(i):
        q = qc_ref[i * nseq:(i + 1) * nseq, :].astype(_BF)
        return [lax.dot_general(q[:, c], _load_mem_head(mk_ref, i, h).astype(_BF), (((1,), (1,)), ((), ())),
                                preferred_element_type=_F32) for h, c in enumerate(heads)]

    def mem_values(i, scores):
        outs = [_dot(_softmax(sc).astype(_BF), _load_mem_head(mv_ref, i, h).astype(_BF))
                for h, sc in enumerate(scores)]
        os_ref[i * nseq:(i + 1) * nseq, :] = jnp.concatenate(outs, axis=1)

    kt_prev, v_prev, u_hist = kt_scr[...], v_scr[...], u_scr[...]
    blk = project(0)
    done = None
    for j in range(nb):
        kt_pair, v_pair = kv_operands(kt_prev, v_prev, blk)
        sc0 = wave_scores(j, blk, kt_pair, (0, 1))
        nxt = project(j + 1) if j + 1 < nb else None
        msc = mem_scores(j)
        sc1 = wave_scores(j, blk, kt_pair, (2, 3))
        pooled = pool(j, u_hist, blk["u"])
        at0 = wave_values(v_pair, (0, 1), sc0)
        if done is not None:
            output(*done)
        mem_values(j, msc)
        at1 = wave_values(v_pair, (2, 3), sc1)
        done = (j, blk, at0 + at1 + pooled)
        kt_prev, v_prev, u_hist = blk["kt"], blk["v"], blk["u"][BLOCK - POOL_PAD:]
        blk = nxt
    output(*done)
    kt_scr[...] = kt_prev
    v_scr[...] = v_prev
    u_scr[...] = u_hist

    for src, dst in ((wco_ref, wco_bf_ref), (wup_ref, wup_bf_ref), (wdown_ref, wdown_bf_ref)):
        dst[...] = src[...].astype(_BF)


def _prompt_mixer(x, gmix, win, sinks, wpool, pscale, wout, tail_weights, qc, mk, mv, tile):
    B, S, D = x.shape
    ns = S // tile
    nsteps = B * ns
    nmem = tile // BLOCK
    nseq = qc.shape[0] // mk.shape[0]
    assert mk.shape[0] == nsteps * nmem
    const = lambda *shape: pl.BlockSpec(shape, lambda b, s: (0,) * len(shape))
    chunk = lambda w: pl.BlockSpec((w.shape[0] // nsteps, w.shape[1]), lambda b, s: (b * ns + s, 0))
    assert all(w.shape[0] % (16 * nsteps) == 0 for w in tail_weights)
    mem_rows = pl.BlockSpec((nmem * nseq, D), lambda b, s: (b * ns + s, 0))
    mem_cache = pl.BlockSpec((nmem, N_MEM * MEM_ROWS, 128), lambda b, s: (b * ns + s, 0, 0))
    return pl.pallas_call(
        functools.partial(_prompt_mixer_kernel, tile=tile, nseq=nseq),
        grid=(B, ns),
        in_specs=[
            pl.BlockSpec((1, tile, D), lambda b, s: (b, s, 0)),
            const(1, D),
            const(D, IN_WIDTH),
            pl.BlockSpec(memory_space=pltpu.SMEM),
            const(len(POOL_WINDOWS), POOL_GROUP_WIDTH, POOL_GROUP_WIDTH),
            const(1, POOL_WIDTH),
            const(D, WOUT_SPLIT),
            const(D, D - WOUT_SPLIT),
        ] + [chunk(w) for w in tail_weights] + [mem_rows, mem_cache, mem_cache],
        out_specs=[
            pl.BlockSpec((1, tile, D), lambda b, s: (b, s, 0)),
            pl.BlockSpec((1, BLOCK, KV_WIDTH), lambda b, s: (b, 0, 0)),
            pl.BlockSpec((1, BLOCK, KV_WIDTH), lambda b, s: (b, 0, 0)),
            pl.BlockSpec((1, POOL_PAD, POOL_WIDTH), lambda b, s: (b, 0, 0)),
        ] + [chunk(w) for w in tail_weights] + [mem_rows],
        out_shape=[
            jax.ShapeDtypeStruct((B, S, D), _F32),
            jax.ShapeDtypeStruct((B, BLOCK, KV_WIDTH), _F32),
            jax.ShapeDtypeStruct((B, BLOCK, KV_WIDTH), _F32),
            jax.ShapeDtypeStruct((B, POOL_PAD, POOL_WIDTH), _F32),
        ] + [jax.ShapeDtypeStruct(w.shape, _BF) for w in tail_weights] + [jax.ShapeDtypeStruct(qc.shape, _F32)],
        scratch_shapes=[
            pltpu.VMEM((KV_WIDTH, BLOCK), _BF),
            pltpu.VMEM((BLOCK, KV_WIDTH), _BF),
            pltpu.VMEM((POOL_PAD, POOL_WIDTH), _F32),
            pltpu.VMEM((2, N_HEADS, BLOCK, 2 * BLOCK), _F32),
            pltpu.VMEM((2, len(POOL_WINDOWS), BLOCK, POOL_GROUP_WIDTH), _F32),
        ],
        compiler_params=pltpu.CompilerParams(
            dimension_semantics=("arbitrary", "arbitrary"), vmem_limit_bytes=VMEM_LIMIT_BYTES),
        name="prompt_mixer",
    )(x, gmix, win, sinks, wpool, pscale, *wout, *tail_weights, qc, mk, mv)


def _mem_kv_kernel(mem_ref, gmem_ref, wck_ref, wcv_ref, k_ref, v_ref, kt_ref, vb_ref, wck_scr, wcv_scr):
    @pl.when(pl.program_id(0) == 0)
    def _():
        wck_scr[...] = wck_ref[...].astype(_BF)
        wcv_scr[...] = wcv_ref[...].astype(_BF)

    hm = _rms(mem_ref[0], gmem_ref[...]).astype(_BF)
    k = _dot(hm, wck_scr[...])
    v = _dot(hm, wcv_scr[...])
    vb_ref[0] = v.astype(_BF)
    for h in range(N_CROSS_HEADS):
        kt_ref[0, h] = k[:, h * CROSS_HEAD_DIM:(h + 1) * CROSS_HEAD_DIM].T.astype(_BF)
        for half in range(CROSS_HEAD_DIM // 128):
            c0 = h * CROSS_HEAD_DIM + half * 128
            rows = pl.ds(half * N_CROSS_HEADS + h, N_MEM, stride=MEM_ROWS)
            k_ref[0, rows, :] = k[:, c0:c0 + 128]
            v_ref[0, rows, :] = v[:, c0:c0 + 128]


def _mem_kv(mem, gmem, wck, wcv):
    B, M, D = mem.shape
    const = lambda *shape: pl.BlockSpec(shape, lambda b: (0,) * len(shape))
    return pl.pallas_call(
        _mem_kv_kernel,
        grid=(B,),
        in_specs=[pl.BlockSpec((1, M, D), lambda b: (b, 0, 0)), const(1, D), const(D, D), const(D, D)],
        out_specs=[
            pl.BlockSpec((1, M * MEM_ROWS, 128), lambda b: (b, 0, 0)),
            pl.BlockSpec((1, M * MEM_ROWS, 128), lambda b: (b, 0, 0)),
            pl.BlockSpec((1, N_CROSS_HEADS, CROSS_HEAD_DIM, M), lambda b: (b, 0, 0, 0)),
            pl.BlockSpec((1, M, D), lambda b: (b, 0, 0)),
        ],
        out_shape=[
            jax.ShapeDtypeStruct((B, M * MEM_ROWS, 128), _F32),
            jax.ShapeDtypeStruct((B, M * MEM_ROWS, 128), _F32),
            jax.ShapeDtypeStruct((B, N_CROSS_HEADS, CROSS_HEAD_DIM, M), _BF),
            jax.ShapeDtypeStruct((B, M, D), _BF),
        ],
        scratch_shapes=[pltpu.VMEM((D, D), _BF), pltpu.VMEM((D, D), _BF)],
        compiler_params=pltpu.CompilerParams(
            dimension_semantics=("arbitrary",), vmem_limit_bytes=VMEM_LIMIT_BYTES),
        name="prompt_mem_kv",
    )(mem, gmem, wck, wcv)


def _mem_cache_unrows(rows):
    nb = rows.shape[0]
    c = rows.reshape(nb, N_MEM, CROSS_HEAD_DIM // 128, N_CROSS_HEADS, 128)
    return c.transpose(0, 1, 3, 2, 4).reshape(nb, N_MEM, N_CROSS_HEADS, CROSS_HEAD_DIM)


def _prompt_tail_kernel(x1_ref, gcross_ref, wcq_ref, kt_ref, vb_ref, wco_ref, gffn_ref, wup_ref, wdown_ref,
                        gfinal_ref, y_ref, *, tile, sub):
    gcross, gffn, gfinal = gcross_ref[...], gffn_ref[...], gfinal_ref[...]

    def stream(r0):
        x1 = x1_ref[0, r0:r0 + sub, :]
        hn = _rms(x1, gcross).astype(_BF)
        yield
        q = (_dot(hn, wcq_ref[...]) * CQ_SCALE).astype(_BF)
        yield
        heads = [slice(h * CROSS_HEAD_DIM, (h + 1) * CROSS_HEAD_DIM) for h in range(N_CROSS_HEADS)]
        scores = [_dot(q[:, c], kt_ref[0, h]) for h, c in enumerate(heads)]
        yield
        outs = [_dot(_softmax(sc).astype(_BF), vb_ref[0, :, c]).astype(_BF) for sc, c in zip(scores, heads)]
        yield
        x2 = x1 + _dot(jnp.concatenate(outs, axis=1), wco_ref[...])
        yield

        def store(y):
            y_ref[0, r0:r0 + sub, :] = y

        yield from _ffn_final_stages(x2, gffn, wup_ref, wdown_ref, gfinal, store)

    _interleave([stream(r0) for r0 in range(0, tile, sub)], TAIL_SKEW)


def _single(shape, index_map):
    return pl.BlockSpec(shape, index_map, pipeline_mode=pl.Buffered(1))


def _prompt_tail(x1, gcross, wcq, kt, vb, wco, gffn, wup, wdown, gfinal, tile, sub):
    B, S, D = x1.shape
    const = lambda *shape: _single(shape, lambda b, s: (0,) * len(shape))
    return pl.pallas_call(
        functools.partial(_prompt_tail_kernel, tile=tile, sub=sub),
        grid=(B, S // tile),
        in_specs=[
            pl.BlockSpec((1, tile, D), lambda b, s: (b, s, 0)),
            const(1, D),
            const(D, D),
            pl.BlockSpec((1, N_CROSS_HEADS, CROSS_HEAD_DIM, N_MEM), lambda b, s: (b, 0, 0, 0)),
            pl.BlockSpec((1, N_MEM, D), lambda b, s: (b, 0, 0)),
            const(D, D),
            const(1, D),
            const(D, D_FF),
            const(D_FF, D),
            const(1, D),
        ],
        out_specs=pl.BlockSpec((1, tile, D), lambda b, s: (b, s, 0)),
        out_shape=jax.ShapeDtypeStruct((B, S, D), _F32),
        compiler_params=pltpu.CompilerParams(
            dimension_semantics=("arbitrary", "arbitrary"), vmem_limit_bytes=VMEM_LIMIT_BYTES),
        name="prompt_tail",
    )(x1, gcross, wcq, kt, vb, wco, gffn, wup, wdown, gfinal)


def _sample_mixer_kernel(x_ref, gmix_ref, win_ref, sinkcol_ref, ck_ref, cv_ref, st_ref, wpool_ref, pscale_ref,
                         wouta_ref, woutb_ref, gcross_ref, wcq_ref,
                         x1_ref, qc_ref, wk_ref, wv_ref, pool_ref, wcq_bf_ref,
                         kc_scr, vc_scr, u_scr, *, nbatch, sub, nseq, past_len):
    @pl.when(pl.program_id(0) == 0)
    def _():
        wcq_bf_ref[...] = wcq_ref[...].astype(_BF)

    rows = sub * nseq
    nkeys = 2 * WINDOW
    gmix, gcross = gmix_ref[...], gcross_ref[...]
    sinkcol = sinkcol_ref[...][None]

    t = lax.broadcasted_iota(jnp.int32, (nseq, nkeys), 0)
    c = lax.broadcasted_iota(jnp.int32, (nseq, nkeys), 1)
    dist = jnp.where(c < WINDOW, WINDOW + t - c, t - (c - WINDOW))
    valid = (dist >= 0) & (dist <= WINDOW) & (c < WINDOW + nseq)
    distf = dist.astype(_F32)
    bias = jnp.concatenate([jnp.where(valid, -_slope(h) * distf, NEG_INF) for h in range(N_HEADS)], axis=0)[None]
    lane = lax.broadcasted_iota(jnp.int32, (rows, 2 * HEAD_DIM), 1)
    lo = lane < HEAD_DIM
    tpos = past_len + lax.broadcasted_iota(jnp.int32, (nseq, POOL_GROUP_WIDTH), 0)
    zpad = jnp.zeros((sub, nkeys - WINDOW, KV_WIDTH), _BF)

    def stream(b0):
        bs = slice(b0, b0 + sub)
        rs = slice(b0 * nseq, (b0 + sub) * nseq)
        x = x_ref[rs, :]
        proj = _dot(_rms(x, gmix).astype(_BF), win_ref[...])
        yield
        k_new = proj[:, K_OFF:K_OFF + KV_WIDTH].reshape(sub, nseq, KV_WIDTH)
        v_new = proj[:, V_OFF:V_OFF + KV_WIDTH].reshape(sub, nseq, KV_WIDTH)
        ck = jnp.swapaxes(ck_ref[bs], 1, 2)
        cv = jnp.swapaxes(cv_ref[bs], 1, 2)
        wk_ref[bs, :WINDOW - nseq, :] = ck[:, nseq:, :]
        wk_ref[bs, WINDOW - nseq:, :] = k_new
        wv_ref[bs, :WINDOW - nseq, :] = cv[:, nseq:, :]
        wv_ref[bs, WINDOW - nseq:, :] = v_new
        kc_scr[bs, WINDOW:, :] = zpad
        vc_scr[bs, WINDOW:, :] = zpad
        kc_scr[bs, :WINDOW, :] = ck.astype(_BF)
        vc_scr[bs, :WINDOW, :] = cv.astype(_BF)
        kc_scr[bs, WINDOW:WINDOW + nseq, :] = k_new.astype(_BF)
        vc_scr[bs, WINDOW:WINDOW + nseq, :] = v_new.astype(_BF)
        qsc = proj[:, :ATTN_WIDTH] * Q_SCALE
        q_lo, q_hi = [], []
        for p in range(GROUP):
            slab = qsc[:, p * 128:(p + 1) * 128]
            q_lo.append(jnp.where(lo, slab, 0.0).reshape(sub, nseq, 128))
            q_hi.append(jnp.where(lo, 0.0, slab).reshape(sub, nseq, 128))
        qm = jnp.concatenate(q_lo + q_hi, axis=1).astype(_BF)
        sc = jnp.einsum('bqd,bkd->bqk', qm, kc_scr[bs], preferred_element_type=_F32) + bias
        yield
        pr = _softmax_with_sink(sc, sinkcol).astype(_BF)
        o = jnp.einsum('bqk,bkd->bqd', pr, vc_scr[bs], preferred_element_type=_F32)
        yield
        attn = []
        for p in range(GROUP):
            o_lo = o[:, p * nseq:(p + 1) * nseq, :].reshape(rows, 128)
            o_hi = o[:, (p + GROUP) * nseq:(p + GROUP + 1) * nseq, :].reshape(rows, 128)
            attn.append(jnp.where(lo, o_lo, o_hi).astype(_BF))
        ext = POOL_PAD + nseq
        seq_rows = lambda r: pl.ds(b0 * ext + r, sub, stride=ext)
        pooled = []
        for g, w in enumerate(POOL_WINDOWS):
            c0 = g * POOL_GROUP_WIDTH
            u_scr[g, seq_rows(0), :] = jnp.zeros((sub, POOL_GROUP_WIDTH), _F32)
            for r in range(POOL_HIST):
                u_scr[g, seq_rows(POOL_PAD - POOL_HIST + r), :] = st_ref[r, bs, c0:c0 + POOL_GROUP_WIDTH]
            for i in range(sub):
                u_scr[g, (b0 + i) * ext + POOL_PAD:(b0 + i + 1) * ext, :] = (
                    proj[i * nseq:(i + 1) * nseq, U_OFF + c0:U_OFF + c0 + POOL_GROUP_WIDTH])
            for r in range(POOL_HIST):
                pool_ref[r, bs, c0:c0 + POOL_GROUP_WIDTH] = u_scr[g, seq_rows(ext - POOL_HIST + r), :]
            ug = u_scr[g, b0 * ext:(b0 + sub) * ext, :].reshape(sub, ext, POOL_GROUP_WIDTH)
            sw = _window_sums(ug, 1)[w][:, POOL_PAD:, :]
            cnt = jnp.minimum(tpos + 1, w).astype(_F32)
            d = (sw / cnt[None] - ug[:, POOL_PAD:, :]).reshape(rows, POOL_GROUP_WIDTH).astype(_BF)
            y = _dot(d, wpool_ref[g]) * pscale_ref[:, c0:c0 + POOL_GROUP_WIDTH]
            pooled.append(y.astype(_BF))
        yield
        x1 = x + _dot_panels(jnp.concatenate(attn + pooled, axis=1), (wouta_ref, woutb_ref))
        x1_ref[rs, :] = x1
        yield
        qc_ref[rs, :] = _dot(_rms(x1, gcross).astype(_BF), wcq_bf_ref[...]) * CQ_SCALE

    _interleave([stream(b0) for b0 in range(0, nbatch, sub)], SAMPLE_MIXER_SKEW)


def _sample_mixer(x, gmix, win, sinkcol, ck, cv, st, wpool, pscale, wout, gcross, wcq, nbatch, sub, nseq,
                  past_len):
    R, D = x.shape
    rows = nbatch * nseq
    const = lambda *shape: pl.BlockSpec(shape, lambda i: (0,) * len(shape))
    return pl.pallas_call(
        functools.partial(_sample_mixer_kernel, nbatch=nbatch, sub=sub, nseq=nseq, past_len=past_len),
        grid=(R // rows,),
        in_specs=[
            pl.BlockSpec((rows, D), lambda i: (i, 0)),
            const(1, D),
            const(D, IN_WIDTH),
            const(N_HEADS * nseq, 1),
            pl.BlockSpec((nbatch, WINDOW, KV_WIDTH), lambda i: (i, 0, 0)),
            pl.BlockSpec((nbatch, WINDOW, KV_WIDTH), lambda i: (i, 0, 0)),
            pl.BlockSpec((POOL_HIST, nbatch, POOL_WIDTH), lambda i: (0, i, 0)),
            const(len(POOL_WINDOWS), POOL_GROUP_WIDTH, POOL_GROUP_WIDTH),
            const(1, POOL_WIDTH),
            const(D, WOUT_SPLIT),
            const(D, D - WOUT_SPLIT),
            const(1, D),
            const(D, D),
        ],
        out_specs=[
            pl.BlockSpec((rows, D), lambda i: (i, 0)),
            pl.BlockSpec((rows, D), lambda i: (i, 0)),
            pl.BlockSpec((nbatch, WINDOW, KV_WIDTH), lambda i: (i, 0, 0)),
            pl.BlockSpec((nbatch, WINDOW, KV_WIDTH), lambda i: (i, 0, 0)),
            pl.BlockSpec((POOL_HIST, nbatch, POOL_WIDTH), lambda i: (0, i, 0)),
            const(D, D),
        ],
        out_shape=[
            jax.ShapeDtypeStruct((R, D), _F32),
            jax.ShapeDtypeStruct((R, D), _F32),
            jax.ShapeDtypeStruct((R // nseq, WINDOW, KV_WIDTH), _F32),
            jax.ShapeDtypeStruct((R // nseq, WINDOW, KV_WIDTH), _F32),
            jax.ShapeDtypeStruct((POOL_HIST, R // nseq, POOL_WIDTH), _F32),
            jax.ShapeDtypeStruct((D, D), _BF),
        ],
        scratch_shapes=[
            pltpu.VMEM((nbatch, 2 * WINDOW, KV_WIDTH), _BF),
            pltpu.VMEM((nbatch, 2 * WINDOW, KV_WIDTH), _BF),
            pltpu.VMEM((len(POOL_WINDOWS), nbatch * (POOL_PAD + nseq), POOL_GROUP_WIDTH), _F32),
        ],
        compiler_params=pltpu.CompilerParams(
            dimension_semantics=("arbitrary",), vmem_limit_bytes=VMEM_LIMIT_BYTES),
        name="sample_mixer",
    )(x, gmix, win, sinkcol, ck, cv, st, wpool, pscale, *wout, gcross, wcq)


def _mem_cache_rows(cache):
    nb = cache.shape[0]
    c = cache.reshape(nb, N_MEM, N_CROSS_HEADS, CROSS_HEAD_DIM // 128, 128)
    return c.transpose(0, 1, 3, 2, 4).reshape(nb, N_MEM * MEM_ROWS, 128)


def _load_mem_head(ref, b, h):
    halves = [ref[b, pl.ds(half * N_CROSS_HEADS + h, N_MEM, stride=MEM_ROWS), :]
              for half in range(CROSS_HEAD_DIM // 128)]
    return jnp.concatenate(halves, axis=1)


def _sample_tail_kernel(x1_ref, o_ref, wco_ref, gffn_ref, wup_ref, wdown_ref, gfinal_ref, y_ref, *, tile, sub):
    gffn, gfinal = gffn_ref[...], gfinal_ref[...]

    def stream(r0):
        x2 = x1_ref[r0:r0 + sub, :] + _dot(o_ref[r0:r0 + sub, :].astype(_BF), wco_ref[...])
        yield

        def store(y):
            y_ref[r0:r0 + sub, :] = y

        yield from _ffn_final_stages(x2, gffn, wup_ref, wdown_ref, gfinal, store)

    _interleave([stream(r0) for r0 in range(0, tile, sub)], TAIL_SKEW)


def _sample_tail(x1, o, wco, gffn, wup, wdown, gfinal, tile, sub):
    R, D = x1.shape
    const = lambda *shape: _single(shape, lambda i: (0,) * len(shape))
    return pl.pallas_call(
        functools.partial(_sample_tail_kernel, tile=tile, sub=sub),
        grid=(R // tile,),
        in_specs=[
            pl.BlockSpec((tile, D), lambda i: (i, 0)),
            pl.BlockSpec((tile, D), lambda i: (i, 0)),
            const(D, D),
            const(1, D),
            const(D, D_FF),
            const(D_FF, D),
            const(1, D),
        ],
        out_specs=pl.BlockSpec((tile, D), lambda i: (i, 0)),
        out_shape=jax.ShapeDtypeStruct((R, D), _F32),
        compiler_params=pltpu.CompilerParams(
            dimension_semantics=("arbitrary",), vmem_limit_bytes=VMEM_LIMIT_BYTES),
        name="sample_tail",
    )(x1, o, wco, gffn, wup, wdown, gfinal)


PROMPT_TILE = 512
TAIL_TILE = 1024
TAIL_SUB = 256
TAIL_SKEW = 5
SAMPLE_MIXER_BATCH = 32
SAMPLE_MIXER_SUB = 16
SAMPLE_MIXER_SKEW = 3


def kernel(x_prompt, x_sample, cache_win_k, cache_win_v, state_pool, cache_mem_k, cache_mem_v, mem_prompt,
           g_mix, w_in, attn_sinks, w_pool, pool_scale, w_out, g_cross, g_mem, w_cq, w_ck, w_cv, w_co,
           g_ffn, w_up, w_down, g_final):
    depth = g_mix.shape[0]
    assert depth == 1, "one layer per step"
    B, S, D = x_prompt.shape
    DB, T, _ = x_sample.shape
    past_len = PAST_LEN
    l = 0

    win = jnp.concatenate([_pair_heads(w_in[l][:, :ATTN_WIDTH], 1), w_in[l][:, ATTN_WIDTH:]], axis=1).astype(_BF)
    wout = jnp.concatenate([_pair_heads(w_out[l][:ATTN_WIDTH, :], 0), w_out[l][ATTN_WIDTH:, :]], axis=0)
    wout = (wout[:, :WOUT_SPLIT].astype(_BF), wout[:, WOUT_SPLIT:].astype(_BF))
    wpool = w_pool[l].astype(_BF)
    gmix, gcross, gmem, gffn = (g[l].reshape(1, D) for g in (g_mix, g_cross, g_mem, g_ffn))
    gfinal = g_final.reshape(1, D)
    pscale = pool_scale[l].reshape(1, POOL_WIDTH)
    sinks = attn_sinks[l]

    xs = x_sample.reshape(DB * T, D)
    ck = cache_win_k[l].transpose(0, 2, 3, 1).reshape(DB, KV_WIDTH, WINDOW)
    cv = cache_win_v[l].transpose(0, 2, 3, 1).reshape(DB, KV_WIDTH, WINDOW)
    st = state_pool[l].transpose(1, 0, 2)
    sinkcol = jnp.repeat(sinks, T).reshape(N_HEADS * T, 1)
    x1s, qc, wk_s, wv_s, pool_s, wcq = _sample_mixer(xs, gmix, win, sinkcol, ck, cv, st, wpool, pscale, wout,
                                                gcross, w_cq[l], SAMPLE_MIXER_BATCH, SAMPLE_MIXER_SUB, T, past_len)

    mk = _mem_cache_rows(cache_mem_k[l])
    mv = _mem_cache_rows(cache_mem_v[l])
    x1p, klast, vlast, plast, wco, wup, wdown, o_s = _prompt_mixer(
        x_prompt, gmix, win, sinks, wpool, pscale, wout, (w_co[l], w_up[l], w_down[l]), qc, mk, mv, PROMPT_TILE)
    mem_k, mem_v, mem_kt, mem_vb = _mem_kv(mem_prompt, gmem, w_ck[l], w_cv[l])
    y_prompt = _prompt_tail(x1p, gcross, wcq, mem_kt, mem_vb, wco, gffn, wup, wdown, gfinal, TAIL_TILE, TAIL_SUB)

    y_sample = _sample_tail(x1s, o_s, wco, gffn, wup, wdown, gfinal, TAIL_TILE, TAIL_SUB).reshape(DB, T, D)

    return (
        y_prompt,
        y_sample,
        klast.reshape(B, N_KV_HEADS, HEAD_DIM, WINDOW).transpose(0, 3, 1, 2)[None],
        vlast.reshape(B, N_KV_HEADS, HEAD_DIM, WINDOW).transpose(0, 3, 1, 2)[None],
        plast[:, POOL_PAD - POOL_HIST:, :][None],
        _mem_cache_unrows(mem_k)[None],
        _mem_cache_unrows(mem_v)[None],
        wk_s.reshape(1, DB, WINDOW, N_KV_HEADS, HEAD_DIM),
        wv_s.reshape(1, DB, WINDOW, N_KV_HEADS, HEAD_DIM),
        pool_s.transpose(1, 0, 2)[None],
    )
```

```python
import functools

import jax
import jax.numpy as jnp
from jax import lax
from jax.experimental import pallas as pl
from jax.experimental.pallas import tpu as pltpu

D_MODEL = 1024
PAST_LEN = 16384
HEAD_DIM = 64
N_HEADS = 8
N_KV_HEADS = 2
GROUP = N_HEADS // N_KV_HEADS
ATTN_WIDTH = N_HEADS * HEAD_DIM
KV_WIDTH = N_KV_HEADS * HEAD_DIM
WINDOW = 128
BLOCK = WINDOW
POOL_WIDTH = D_MODEL - ATTN_WIDTH
POOL_WINDOWS = (2, 4, 8, 16)
POOL_GROUP_WIDTH = 128
POOL_HIST = 15
POOL_PAD = 16
IN_WIDTH = ATTN_WIDTH + 2 * KV_WIDTH + POOL_WIDTH
N_MEM = 256
N_CROSS_HEADS = 4
CROSS_HEAD_DIM = 256
MEM_ROWS = N_CROSS_HEADS * (CROSS_HEAD_DIM // 128)
D_FF = 4 * D_MODEL
FF_CHUNK = 1024
RMS_EPS = 1e-5
NEG_INF = -1e30
Q_SCALE = HEAD_DIM ** -0.5
CQ_SCALE = CROSS_HEAD_DIM ** -0.5
K_OFF = ATTN_WIDTH
V_OFF = ATTN_WIDTH + KV_WIDTH
U_OFF = ATTN_WIDTH + 2 * KV_WIDTH


def _pair_heads(w, axis):
    shape = w.shape
    split = shape[:axis] + (N_KV_HEADS, GROUP, HEAD_DIM) + shape[axis + 1:]
    return jnp.swapaxes(w.reshape(split), axis, axis + 1).reshape(shape)


WOUT_SPLIT = 768

VMEM_LIMIT_BYTES = 56 * 1024 * 1024

_BF = jnp.bfloat16
_F32 = jnp.float32


def _slope(h):
    return 2.0 ** (-8.0 * (h + 1) / N_HEADS)


def _dot(a, b):
    return jnp.dot(a, b, preferred_element_type=_F32)


def _dot_panels(a, panel_refs):
    return jnp.concatenate([_dot(a, ref[...]) for ref in panel_refs], axis=1)


def _rms(x, g):
    ms = jnp.mean(x * x, axis=-1, keepdims=True)
    return x * lax.rsqrt(ms + RMS_EPS) * g


def _softmax_with_sink(s, sink):
    m = jnp.maximum(jnp.max(s, axis=-1, keepdims=True), sink)
    e = jnp.exp(s - m)
    den = jnp.sum(e, axis=-1, keepdims=True) + jnp.exp(sink - m)
    return e * (1.0 / den)


def _softmax_with_sink_numerators(s, sink):
    m = jnp.maximum(jnp.max(s, axis=-1, keepdims=True), sink)
    return jnp.exp(s - m), jnp.exp(sink - m)


def _softmax(s):
    m = jnp.max(s, axis=-1, keepdims=True)
    e = jnp.exp(s - m)
    return e * (1.0 / jnp.sum(e, axis=-1, keepdims=True))


def _window_sums(u_ext, axis):
    out = {}
    s = u_ext
    w = 1
    while w < max(POOL_WINDOWS):
        s = s + pltpu.roll(s, w, axis)
        w *= 2
        out[w] = s
    return out


def _interleave(streams, skew):
    pending = list(streams)
    live = []
    rnd = 0
    while live or pending:
        if pending and rnd % skew == 0:
            live.append(pending.pop(0))
        for g in list(live):
            try:
                next(g)
            except StopIteration:
                live.remove(g)
        rnd += 1


def _ffn_final_stages(x2, gffn, wup_ref, wdown_ref, gfinal, store):
    hn = _rms(x2, gffn).astype(_BF)
    yield
    acc = x2
    for c in range(D_FF // FF_CHUNK):
        hc = _dot(hn, wup_ref[:, c * FF_CHUNK:(c + 1) * FF_CHUNK])
        yield
        hc = jnp.maximum(hc, 0.0)
        hc = (hc * hc).astype(_BF)
        acc = acc + _dot(hc, wdown_ref[c * FF_CHUNK:(c + 1) * FF_CHUNK, :])
        yield
    store(_rms(acc, gfinal))


def _prompt_mixer_kernel(x_ref, gmix_ref, win_ref, sink_ref, wpool_ref, pscale_ref, wouta_ref, woutb_ref,
                         wco_ref, wup_ref, wdown_ref, qc_ref, mk_ref, mv_ref,
                         x1_ref, klast_ref, vlast_ref, plast_ref, wco_bf_ref, wup_bf_ref, wdown_bf_ref,
                         os_ref, kt_scr, v_scr, u_scr, bias_scr, inv_scr, *, tile, nseq):
    b = pl.program_id(0)
    s = pl.program_id(1)
    nb = tile // BLOCK

    @pl.when((b == 0) & (s == 0))
    def _():
        qi = lax.broadcasted_iota(jnp.int32, (BLOCK, 2 * BLOCK), 0)
        kc = lax.broadcasted_iota(jnp.int32, (BLOCK, 2 * BLOCK), 1)
        dist = qi + BLOCK - kc
        valid = (dist >= 0) & (dist <= WINDOW)
        valid_first = valid & (kc >= BLOCK)
        distf = dist.astype(_F32)
        for h in range(N_HEADS):
            ali = -_slope(h) * distf
            bias_scr[0, h] = jnp.where(valid, ali, NEG_INF)
            bias_scr[1, h] = jnp.where(valid_first, ali, NEG_INF)
        prow = lax.broadcasted_iota(jnp.int32, (BLOCK, POOL_GROUP_WIDTH), 0)
        for g, w in enumerate(POOL_WINDOWS):
            inv_scr[0, g] = jnp.full((BLOCK, POOL_GROUP_WIDTH), 1.0 / w, _F32)
            inv_scr[1, g] = 1.0 / jnp.minimum(prow + 1, w).astype(_F32)

    @pl.when(s == 0)
    def _():
        kt_scr[...] = jnp.zeros((KV_WIDTH, BLOCK), _BF)
        v_scr[...] = jnp.zeros((BLOCK, KV_WIDTH), _BF)
        u_scr[...] = jnp.zeros((POOL_PAD, POOL_WIDTH), _F32)

    lane = lax.broadcasted_iota(jnp.int32, (BLOCK, 2 * HEAD_DIM), 1)
    lo = lane < HEAD_DIM
    gmix = gmix_ref[...]
    zeros_kt = jnp.zeros((HEAD_DIM, 2 * BLOCK), _BF)
    key_row = lax.broadcasted_iota(jnp.int32, (4 * BLOCK, 2 * HEAD_DIM), 0)
    key_lane = lax.broadcasted_iota(jnp.int32, (4 * BLOCK, 2 * HEAD_DIM), 1)
    ones_pair = ((key_row < 2 * BLOCK) == (key_lane < HEAD_DIM)).astype(_F32).astype(_BF)

    def project(j):
        x = x_ref[0, j * BLOCK:(j + 1) * BLOCK, :]
        proj = _dot(_rms(x, gmix).astype(_BF), win_ref[...])
        k = proj[:, K_OFF:K_OFF + KV_WIDTH]
        v = proj[:, V_OFF:V_OFF + KV_WIDTH]
        u = proj[:, U_OFF:]
        kt = k.T
        if j == nb - 1:
            klast_ref[0] = kt
            vlast_ref[0] = v.T
            plast_ref[0] = u[BLOCK - POOL_PAD:]
        return dict(x=x, q=(proj[:, :ATTN_WIDTH] * Q_SCALE).astype(_BF), kt=kt.astype(_BF), v=v.astype(_BF), u=u)

    def pool(j, u_hist, u):
        first = ((s == 0) & (j == 0)).astype(jnp.int32) if j == 0 else 0
        ys = []
        for g, w in enumerate(POOL_WINDOWS):
            c0 = g * POOL_GROUP_WIDTH
            u_ext = jnp.concatenate([u_hist[:, c0:c0 + POOL_GROUP_WIDTH], u[:, c0:c0 + POOL_GROUP_WIDTH]], axis=0)
            sw = _window_sums(u_ext, 0)[w][POOL_PAD:]
            d = (sw * inv_scr[first, g] - u_ext[POOL_PAD:]).astype(_BF)
            y = _dot(d, wpool_ref[g]) * pscale_ref[:, c0:c0 + POOL_GROUP_WIDTH]
            ys.append(y.astype(_BF))
        return ys

    def kv_operands(kt_prev, v_prev, blk):
        kt2 = jnp.concatenate([kt_prev, blk["kt"]], axis=1)
        v2 = jnp.concatenate([v_prev, blk["v"]], axis=0)
        kt_pair = jnp.concatenate([jnp.concatenate([kt2[:HEAD_DIM], zeros_kt], axis=0),
                                   jnp.concatenate([zeros_kt, kt2[HEAD_DIM:]], axis=0)], axis=1)
        lane2 = lax.broadcasted_iota(jnp.int32, v2.shape, 1)
        zero = jnp.zeros_like(v2)
        v_pair = jnp.concatenate([jnp.where(lane2 < HEAD_DIM, v2, zero), jnp.where(lane2 < HEAD_DIM, zero, v2)],
                                 axis=0)
        return kt_pair, jnp.concatenate([v_pair, ones_pair], axis=1)

    def wave_scores(j, blk, kt_pair, pairs):
        first = ((s == 0) & (j == 0)).astype(jnp.int32) if j == 0 else 0
        scores = []
        for p in pairs:
            sc = _dot(blk["q"][:, p * 128:(p + 1) * 128], kt_pair)
            scores.append(sc[:, :2 * BLOCK] + bias_scr[first, p])
            scores.append(sc[:, 2 * BLOCK:] + bias_scr[first, p + GROUP])
        return scores

    def wave_values(v_pair, pairs, scores):
        slabs = []
        for i, p in enumerate(pairs):
            es, sink_es = [], []
            for half, h in ((0, p), (1, p + GROUP)):
                e, sink_e = _softmax_with_sink_numerators(scores[2 * i + half], sink_ref[h])
                es.append(e.astype(_BF))
                sink_es.append(sink_e)
            o = _dot(jnp.concatenate(es, axis=1), v_pair)
            den = o[:, 2 * HEAD_DIM:] + jnp.where(lo, sink_es[0], sink_es[1])
            slabs.append((o[:, :2 * HEAD_DIM] * (1.0 / den)).astype(_BF))
        return slabs

    def output(j, blk, slabs):
        cat = jnp.concatenate(slabs, axis=1)
        x1_ref[0, j * BLOCK:(j + 1) * BLOCK, :] = blk["x"] + _dot_panels(cat, (wouta_ref, woutb_ref))

    heads = [slice(h * CROSS_HEAD_DIM, (h + 1) * CROSS_HEAD_DIM) for h in range(N_CROSS_HEADS)]

    def mem_scores(i):
        q = qc_ref[i * nseq:(i + 1) * nseq, :].astype(_BF)
        return [lax.dot_general(q[:, c], _load_mem_head(mk_ref, i, h).astype(_BF), (((1,), (1,)), ((), ())),
                                preferred_element_type=_F32) for h, c in enumerate(heads)]

    def mem_values(i, scores):
        outs = [_dot(_softmax(sc).astype(_BF), _load_mem_head(mv_ref, i, h).astype(_BF))
                for h, sc in enumerate(scores)]
        os_ref[i * nseq:(i + 1) * nseq, :] = jnp.concatenate(outs, axis=1)

    kt_prev, v_prev, u_hist = kt_scr[...], v_scr[...], u_scr[...]
    blk = project(0)
    done = None
    for j in range(nb):
        kt_pair, v_pair = kv_operands(kt_prev, v_prev, blk)
        sc0 = wave_scores(j, blk, kt_pair, (0, 1))
        nxt = project(j + 1) if j + 1 < nb else None
        msc = mem_scores(j)
        sc1 = wave_scores(j, blk, kt_pair, (2, 3))
        pooled = pool(j, u_hist, blk["u"])
        at0 = wave_values(v_pair, (0, 1), sc0)
        if done is not None:
            output(*done)
        mem_values(j, msc)
        at1 = wave_values(v_pair, (2, 3), sc1)
        done = (j, blk, at0 + at1 + pooled)
        kt_prev, v_prev, u_hist = blk["kt"], blk["v"], blk["u"][BLOCK - POOL_PAD:]
        blk = nxt
    output(*done)
    kt_scr[...] = kt_prev
    v_scr[...] = v_prev
    u_scr[...] = u_hist

    for src, dst in ((wco_ref, wco_bf_ref), (wup_ref, wup_bf_ref), (wdown_ref, wdown_bf_ref)):
        dst[...] = src[...].astype(_BF)


def _prompt_mixer(x, gmix, win, sinks, wpool, pscale, wout, tail_weights, qc, mk, mv, tile):
    B, S, D = x.shape
    ns = S // tile
    nsteps = B * ns
    nmem = tile // BLOCK
    nseq = qc.shape[0] // mk.shape[0]
    assert mk.shape[0] == nsteps * nmem
    const = lambda *shape: pl.BlockSpec(shape, lambda b, s: (0,) * len(shape))
    chunk = lambda w: pl.BlockSpec((w.shape[0] // nsteps, w.shape[1]), lambda b, s: (b * ns + s, 0))
    assert all(w.shape[0] % (16 * nsteps) == 0 for w in tail_weights)
    mem_rows = pl.BlockSpec((nmem * nseq, D), lambda b, s: (b * ns + s, 0))
    mem_cache = pl.BlockSpec((nmem, N_MEM * MEM_ROWS, 128), lambda b, s: (b * ns + s, 0, 0))
    return pl.pallas_call(
        functools.partial(_prompt_mixer_kernel, tile=tile, nseq=nseq),
        grid=(B, ns),
        in_specs=[
            pl.BlockSpec((1, tile, D), lambda b, s: (b, s, 0)),
            const(1, D),
            const(D, IN_WIDTH),
            pl.BlockSpec(memory_space=pltpu.SMEM),
            const(len(POOL_WINDOWS), POOL_GROUP_WIDTH, POOL_GROUP_WIDTH),
            const(1, POOL_WIDTH),
            const(D, WOUT_SPLIT),
            const(D, D - WOUT_SPLIT),
        ] + [chunk(w) for w in tail_weights] + [mem_rows, mem_cache, mem_cache],
        out_specs=[
            pl.BlockSpec((1, tile, D), lambda b, s: (b, s, 0)),
            pl.BlockSpec((1, BLOCK, KV_WIDTH), lambda b, s: (b, 0, 0)),
            pl.BlockSpec((1, BLOCK, KV_WIDTH), lambda b, s: (b, 0, 0)),
            pl.BlockSpec((1, POOL_PAD, POOL_WIDTH), lambda b, s: (b, 0, 0)),
        ] + [chunk(w) for w in tail_weights] + [mem_rows],
        out_shape=[
            jax.ShapeDtypeStruct((B, S, D), _F32),
            jax.ShapeDtypeStruct((B, BLOCK, KV_WIDTH), _F32),
            jax.ShapeDtypeStruct((B, BLOCK, KV_WIDTH), _F32),
            jax.ShapeDtypeStruct((B, POOL_PAD, POOL_WIDTH), _F32),
        ] + [jax.ShapeDtypeStruct(w.shape, _BF) for w in tail_weights] + [jax.ShapeDtypeStruct(qc.shape, _F32)],
        scratch_shapes=[
            pltpu.VMEM((KV_WIDTH, BLOCK), _BF),
            pltpu.VMEM((BLOCK, KV_WIDTH), _BF),
            pltpu.VMEM((POOL_PAD, POOL_WIDTH), _F32),
            pltpu.VMEM((2, N_HEADS, BLOCK, 2 * BLOCK), _F32),
            pltpu.VMEM((2, len(POOL_WINDOWS), BLOCK, POOL_GROUP_WIDTH), _F32),
        ],
        compiler_params=pltpu.CompilerParams(
            dimension_semantics=("arbitrary", "arbitrary"), vmem_limit_bytes=VMEM_LIMIT_BYTES),
        name="prompt_mixer",
    )(x, gmix, win, sinks, wpool, pscale, *wout, *tail_weights, qc, mk, mv)


def _mem_kv_kernel(mem_ref, gmem_ref, wck_ref, wcv_ref, k_ref, v_ref, kt_ref, vb_ref, wck_scr, wcv_scr):
    @pl.when(pl.program_id(0) == 0)
    def _():
        wck_scr[...] = wck_ref[...].astype(_BF)
        wcv_scr[...] = wcv_ref[...].astype(_BF)

    hm = _rms(mem_ref[0], gmem_ref[...]).astype(_BF)
    k = _dot(hm, wck_scr[...])
    v = _dot(hm, wcv_scr[...])
    vb_ref[0] = v.astype(_BF)
    for h in range(N_CROSS_HEADS):
        kt_ref[0, h] = k[:, h * CROSS_HEAD_DIM:(h + 1) * CROSS_HEAD_DIM].T.astype(_BF)
        for half in range(CROSS_HEAD_DIM // 128):
            c0 = h * CROSS_HEAD_DIM + half * 128
            rows = pl.ds(half * N_CROSS_HEADS + h, N_MEM, stride=MEM_ROWS)
            k_ref[0, rows, :] = k[:, c0:c0 + 128]
            v_ref[0, rows, :] = v[:, c0:c0 + 128]


def _mem_kv(mem, gmem, wck, wcv):
    B, M, D = mem.shape
    const = lambda *shape: pl.BlockSpec(shape, lambda b: (0,) * len(shape))
    return pl.pallas_call(
        _mem_kv_kernel,
        grid=(B,),
        in_specs=[pl.BlockSpec((1, M, D), lambda b: (b, 0, 0)), const(1, D), const(D, D), const(D, D)],
        out_specs=[
            pl.BlockSpec((1, M * MEM_ROWS, 128), lambda b: (b, 0, 0)),
            pl.BlockSpec((1, M * MEM_ROWS, 128), lambda b: (b, 0, 0)),
            pl.BlockSpec((1, N_CROSS_HEADS, CROSS_HEAD_DIM, M), lambda b: (b, 0, 0, 0)),
            pl.BlockSpec((1, M, D), lambda b: (b, 0, 0)),
        ],
        out_shape=[
            jax.ShapeDtypeStruct((B, M * MEM_ROWS, 128), _F32),
            jax.ShapeDtypeStruct((B, M * MEM_ROWS, 128), _F32),
            jax.ShapeDtypeStruct((B, N_CROSS_HEADS, CROSS_HEAD_DIM, M), _BF),
            jax.ShapeDtypeStruct((B, M, D), _BF),
        ],
        scratch_shapes=[pltpu.VMEM((D, D), _BF), pltpu.VMEM((D, D), _BF)],
        compiler_params=pltpu.CompilerParams(
            dimension_semantics=("arbitrary",), vmem_limit_bytes=VMEM_LIMIT_BYTES),
        name="prompt_mem_kv",
    )(mem, gmem, wck, wcv)


def _mem_cache_unrows(rows):
    nb = rows.shape[0]
    c = rows.reshape(nb, N_MEM, CROSS_HEAD_DIM // 128, N_CROSS_HEADS, 128)
    return c.transpose(0, 1, 3, 2, 4).reshape(nb, N_MEM, N_CROSS_HEADS, CROSS_HEAD_DIM)


def _prompt_tail_kernel(x1_ref, gcross_ref, wcq_ref, kt_ref, vb_ref, wco_ref, gffn_ref, wup_ref, wdown_ref,
                        gfinal_ref, y_ref, *, tile, sub):
    gcross, gffn, gfinal = gcross_ref[...], gffn_ref[...], gfinal_ref[...]

    def stream(r0):
        x1 = x1_ref[0, r0:r0 + sub, :]
        hn = _rms(x1, gcross).astype(_BF)
        yield
        q = (_dot(hn, wcq_ref[...]) * CQ_SCALE).astype(_BF)
        yield
        heads = [slice(h * CROSS_HEAD_DIM, (h + 1) * CROSS_HEAD_DIM) for h in range(N_CROSS_HEADS)]
        scores = [_dot(q[:, c], kt_ref[0, h]) for h, c in enumerate(heads)]
        yield
        outs = [_dot(_softmax(sc).astype(_BF), vb_ref[0, :, c]).astype(_BF) for sc, c in zip(scores, heads)]
        yield
        x2 = x1 + _dot(jnp.concatenate(outs, axis=1), wco_ref[...])
        yield

        def store(y):
            y_ref[0, r0:r0 + sub, :] = y

        yield from _ffn_final_stages(x2, gffn, wup_ref, wdown_ref, gfinal, store)

    _interleave([stream(r0) for r0 in range(0, tile, sub)], TAIL_SKEW)


def _single(shape, index_map):
    return pl.BlockSpec(shape, index_map, pipeline_mode=pl.Buffered(1))


def _prompt_tail(x1, gcross, wcq, kt, vb, wco, gffn, wup, wdown, gfinal, tile, sub):
    B, S, D = x1.shape
    const = lambda *shape: _single(shape, lambda b, s: (0,) * len(shape))
    return pl.pallas_call(
        functools.partial(_prompt_tail_kernel, tile=tile, sub=sub),
        grid=(B, S // tile),
        in_specs=[
            pl.BlockSpec((1, tile, D), lambda b, s: (b, s, 0)),
            const(1, D),
            const(D, D),
            pl.BlockSpec((1, N_CROSS_HEADS, CROSS_HEAD_DIM, N_MEM), lambda b, s: (b, 0, 0, 0)),
            pl.BlockSpec((1, N_MEM, D), lambda b, s: (b, 0, 0)),
            const(D, D),
            const(1, D),
            const(D, D_FF),
            const(D_FF, D),
            const(1, D),
        ],
        out_specs=pl.BlockSpec((1, tile, D), lambda b, s: (b, s, 0)),
        out_shape=jax.ShapeDtypeStruct((B, S, D), _F32),
        compiler_params=pltpu.CompilerParams(
            dimension_semantics=("arbitrary", "arbitrary"), vmem_limit_bytes=VMEM_LIMIT_BYTES),
        name="prompt_tail",
    )(x1, gcross, wcq, kt, vb, wco, gffn, wup, wdown, gfinal)


def _sample_mixer_kernel(x_ref, gmix_ref, win_ref, sinkcol_ref, ck_ref, cv_ref, st_ref, wpool_ref, pscale_ref,
                         wouta_ref, woutb_ref, gcross_ref, wcq_ref,
                         x1_ref, qc_ref, wk_ref, wv_ref, pool_ref, wcq_bf_ref,
                         kc_scr, vc_scr, u_scr, *, nbatch, sub, nseq, past_len):
    @pl.when(pl.program_id(0) == 0)
    def _():
        wcq_bf_ref[...] = wcq_ref[...].astype(_BF)

    rows = sub * nseq
    nkeys = 2 * WINDOW
    gmix, gcross = gmix_ref[...], gcross_ref[...]
    sinkcol = sinkcol_ref[...][None]

    t = lax.broadcasted_iota(jnp.int32, (nseq, nkeys), 0)
    c = lax.broadcasted_iota(jnp.int32, (nseq, nkeys), 1)
    dist = jnp.where(c < WINDOW, WINDOW + t - c, t - (c - WINDOW))
    valid = (dist >= 0) & (dist <= WINDOW) & (c < WINDOW + nseq)
    distf = dist.astype(_F32)
    bias = jnp.concatenate([jnp.where(valid, -_slope(h) * distf, NEG_INF) for h in range(N_HEADS)], axis=0)[None]
    lane = lax.broadcasted_iota(jnp.int32, (rows, 2 * HEAD_DIM), 1)
    lo = lane < HEAD_DIM
    tpos = past_len + lax.broadcasted_iota(jnp.int32, (nseq, POOL_GROUP_WIDTH), 0)
    zpad = jnp.zeros((sub, nkeys - WINDOW, KV_WIDTH), _BF)

    def stream(b0):
        bs = slice(b0, b0 + sub)
        rs = slice(b0 * nseq, (b0 + sub) * nseq)
        x = x_ref[rs, :]
        proj = _dot(_rms(x, gmix).astype(_BF), win_ref[...])
        yield
        k_new = proj[:, K_OFF:K_OFF + KV_WIDTH].reshape(sub, nseq, KV_WIDTH)
        v_new = proj[:, V_OFF:V_OFF + KV_WIDTH].reshape(sub, nseq, KV_WIDTH)
        ck = jnp.swapaxes(ck_ref[bs], 1, 2)
        cv = jnp.swapaxes(cv_ref[bs], 1, 2)
        wk_ref[bs, :WINDOW - nseq, :] = ck[:, nseq:, :]
        wk_ref[bs, WINDOW - nseq:, :] = k_new
        wv_ref[bs, :WINDOW - nseq, :] = cv[:, nseq:, :]
        wv_ref[bs, WINDOW - nseq:, :] = v_new
        kc_scr[bs, WINDOW:, :] = zpad
        vc_scr[bs, WINDOW:, :] = zpad
        kc_scr[bs, :WINDOW, :] = ck.astype(_BF)
        vc_scr[bs, :WINDOW, :] = cv.astype(_BF)
        kc_scr[bs, WINDOW:WINDOW + nseq, :] = k_new.astype(_BF)
        vc_scr[bs, WINDOW:WINDOW + nseq, :] = v_new.astype(_BF)
        qsc = proj[:, :ATTN_WIDTH] * Q_SCALE
        q_lo, q_hi = [], []
        for p in range(GROUP):
            slab = qsc[:, p * 128:(p + 1) * 128]
            q_lo.append(jnp.where(lo, slab, 0.0).reshape(sub, nseq, 128))
            q_hi.append(jnp.where(lo, 0.0, slab).reshape(sub, nseq, 128))
        qm = jnp.concatenate(q_lo + q_hi, axis=1).astype(_BF)
        sc = jnp.einsum('bqd,bkd->bqk', qm, kc_scr[bs], preferred_element_type=_F32) + bias
        yield
        pr = _softmax_with_sink(sc, sinkcol).astype(_BF)
        o = jnp.einsum('bqk,bkd->bqd', pr, vc_scr[bs], preferred_element_type=_F32)
        yield
        attn = []
        for p in range(GROUP):
            o_lo = o[:, p * nseq:(p + 1) * nseq, :].reshape(rows, 128)
            o_hi = o[:, (p + GROUP) * nseq:(p + GROUP + 1) * nseq, :].reshape(rows, 128)
            attn.append(jnp.where(lo, o_lo, o_hi).astype(_BF))
        ext = POOL_PAD + nseq
        seq_rows = lambda r: pl.ds(b0 * ext + r, sub, stride=ext)
        pooled = []
        for g, w in enumerate(POOL_WINDOWS):
            c0 = g * POOL_GROUP_WIDTH
            u_scr[g, seq_rows(0), :] = jnp.zeros((sub, POOL_GROUP_WIDTH), _F32)
            for r in range(POOL_HIST):
                u_scr[g, seq_rows(POOL_PAD - POOL_HIST + r), :] = st_ref[r, bs, c0:c0 + POOL_GROUP_WIDTH]
            for i in range(sub):
                u_scr[g, (b0 + i) * ext + POOL_PAD:(b0 + i + 1) * ext, :] = (
                    proj[i * nseq:(i + 1) * nseq, U_OFF + c0:U_OFF + c0 + POOL_GROUP_WIDTH])
            for r in range(POOL_HIST):
                pool_ref[r, bs, c0:c0 + POOL_GROUP_WIDTH] = u_scr[g, seq_rows(ext - POOL_HIST + r), :]
            ug = u_scr[g, b0 * ext:(b0 + sub) * ext, :].reshape(sub, ext, POOL_GROUP_WIDTH)
            sw = _window_sums(ug, 1)[w][:, POOL_PAD:, :]
            cnt = jnp.minimum(tpos + 1, w).astype(_F32)
            d = (sw / cnt[None] - ug[:, POOL_PAD:, :]).reshape(rows, POOL_GROUP_WIDTH).astype(_BF)
            y = _dot(d, wpool_ref[g]) * pscale_ref[:, c0:c0 + POOL_GROUP_WIDTH]
            pooled.append(y.astype(_BF))
        yield
        x1 = x + _dot_panels(jnp.concatenate(attn + pooled, axis=1), (wouta_ref, woutb_ref))
        x1_ref[rs, :] = x1
        yield
        qc_ref[rs, :] = _dot(_rms(x1, gcross).astype(_BF), wcq_bf_ref[...]) * CQ_SCALE

    _interleave([stream(b0) for b0 in range(0, nbatch, sub)], SAMPLE_MIXER_SKEW)


def _sample_mixer(x, gmix, win, sinkcol, ck, cv, st, wpool, pscale, wout, gcross, wcq, nbatch, sub, nseq,
                  past_len):
    R, D = x.shape
    rows = nbatch * nseq
    const = lambda *shape: pl.BlockSpec(shape, lambda i: (0,) * len(shape))
    return pl.pallas_call(
        functools.partial(_sample_mixer_kernel, nbatch=nbatch, sub=sub, nseq=nseq, past_len=past_len),
        grid=(R // rows,),
        in_specs=[
            pl.BlockSpec((rows, D), lambda i: (i, 0)),
            const(1, D),
            const(D, IN_WIDTH),
            const(N_HEADS * nseq, 1),
            pl.BlockSpec((nbatch, WINDOW, KV_WIDTH), lambda i: (i, 0, 0)),
            pl.BlockSpec((nbatch, WINDOW, KV_WIDTH), lambda i: (i, 0, 0)),
            pl.BlockSpec((POOL_HIST, nbatch, POOL_WIDTH), lambda i: (0, i, 0)),
            const(len(POOL_WINDOWS), POOL_GROUP_WIDTH, POOL_GROUP_WIDTH),
            const(1, POOL_WIDTH),
            const(D, WOUT_SPLIT),
            const(D, D - WOUT_SPLIT),
            const(1, D),
            const(D, D),
        ],
        out_specs=[
            pl.BlockSpec((rows, D), lambda i: (i, 0)),
            pl.BlockSpec((rows, D), lambda i: (i, 0)),
            pl.BlockSpec((nbatch, WINDOW, KV_WIDTH), lambda i: (i, 0, 0)),
            pl.BlockSpec((nbatch, WINDOW, KV_WIDTH), lambda i: (i, 0, 0)),
            pl.BlockSpec((POOL_HIST, nbatch, POOL_WIDTH), lambda i: (0, i, 0)),
            const(D, D),
        ],
        out_shape=[
            jax.ShapeDtypeStruct((R, D), _F32),
            jax.ShapeDtypeStruct((R, D), _F32),
            jax.ShapeDtypeStruct((R // nseq, WINDOW, KV_WIDTH), _F32),
            jax.ShapeDtypeStruct((R // nseq, WINDOW, KV_WIDTH), _F32),
            jax.ShapeDtypeStruct((POOL_HIST, R // nseq, POOL_WIDTH), _F32),
            jax.ShapeDtypeStruct((D, D), _BF),
        ],
        scratch_shapes=[
            pltpu.VMEM((nbatch, 2 * WINDOW, KV_WIDTH), _BF),
            pltpu.VMEM((nbatch, 2 * WINDOW, KV_WIDTH), _BF),
            pltpu.VMEM((len(POOL_WINDOWS), nbatch * (POOL_PAD + nseq), POOL_GROUP_WIDTH), _F32),
        ],
        compiler_params=pltpu.CompilerParams(
            dimension_semantics=("arbitrary",), vmem_limit_bytes=VMEM_LIMIT_BYTES),
        name="sample_mixer",
    )(x, gmix, win, sinkcol, ck, cv, st, wpool, pscale, *wout, gcross, wcq)


def _mem_cache_rows(cache):
    nb = cache.shape[0]
    c = cache.reshape(nb, N_MEM, N_CROSS_HEADS, CROSS_HEAD_DIM // 128, 128)
    return c.transpose(0, 1, 3, 2, 4).reshape(nb, N_MEM * MEM_ROWS, 128)


def _load_mem_head(ref, b, h):
    halves = [ref[b, pl.ds(half * N_CROSS_HEADS + h, N_MEM, stride=MEM_ROWS), :]
              for half in range(CROSS_HEAD_DIM // 128)]
    return jnp.concatenate(halves, axis=1)


def _sample_tail_kernel(x1_ref, o_ref, wco_ref, gffn_ref, wup_ref, wdown_ref, gfinal_ref, y_ref, *, tile, sub):
    gffn, gfinal = gffn_ref[...], gfinal_ref[...]

    def stream(r0):
        x2 = x1_ref[r0:r0 + sub, :] + _dot(o_ref[r0:r0 + sub, :].astype(_BF), wco_ref[...])
        yield

        def store(y):
            y_ref[r0:r0 + sub, :] = y

        yield from _ffn_final_stages(x2, gffn, wup_ref, wdown_ref, gfinal, store)

    _interleave([stream(r0) for r0 in range(0, tile, sub)], TAIL_SKEW)


def _sample_tail(x1, o, wco, gffn, wup, wdown, gfinal, tile, sub):
    R, D = x1.shape
    const = lambda *shape: _single(shape, lambda i: (0,) * len(shape))
    return pl.pallas_call(
        functools.partial(_sample_tail_kernel, tile=tile, sub=sub),
        grid=(R // tile,),
        in_specs=[
            pl.BlockSpec((tile, D), lambda i: (i, 0)),
            pl.BlockSpec((tile, D), lambda i: (i, 0)),
            const(D, D),
            const(1, D),
            const(D, D_FF),
            const(D_FF, D),
            const(1, D),
        ],
        out_specs=pl.BlockSpec((tile, D), lambda i: (i, 0)),
        out_shape=jax.ShapeDtypeStruct((R, D), _F32),
        compiler_params=pltpu.CompilerParams(
            dimension_semantics=("arbitrary",), vmem_limit_bytes=VMEM_LIMIT_BYTES),
        name="sample_tail",
    )(x1, o, wco, gffn, wup, wdown, gfinal)


PROMPT_TILE = 512
TAIL_TILE = 1024
TAIL_SUB = 256
TAIL_SKEW = 5
SAMPLE_MIXER_BATCH = 32
SAMPLE_MIXER_SUB = 16
SAMPLE_MIXER_SKEW = 3


def kernel(x_prompt, x_sample, cache_win_k, cache_win_v, state_pool, cache_mem_k, cache_mem_v, mem_prompt,
           g_mix, w_in, attn_sinks, w_pool, pool_scale, w_out, g_cross, g_mem, w_cq, w_ck, w_cv, w_co,
           g_ffn, w_up, w_down, g_final):
    depth = g_mix.shape[0]
    assert depth == 1, "one layer per step"
    B, S, D = x_prompt.shape
    DB, T, _ = x_sample.shape
    past_len = PAST_LEN
    l = 0

    win = jnp.concatenate([_pair_heads(w_in[l][:, :ATTN_WIDTH], 1), w_in[l][:, ATTN_WIDTH:]], axis=1).astype(_BF)
    wout = jnp.concatenate([_pair_heads(w_out[l][:ATTN_WIDTH, :], 0), w_out[l][ATTN_WIDTH:, :]], axis=0)
    wout = (wout[:, :WOUT_SPLIT].astype(_BF), wout[:, WOUT_SPLIT:].astype(_BF))
    wpool = w_pool[l].astype(_BF)
    gmix, gcross, gmem, gffn = (g[l].reshape(1, D) for g in (g_mix, g_cross, g_mem, g_ffn))
    gfinal = g_final.reshape(1, D)
    pscale = pool_scale[l].reshape(1, POOL_WIDTH)
    sinks = attn_sinks[l]

    xs = x_sample.reshape(DB * T, D)
    ck = cache_win_k[l].transpose(0, 2, 3, 1).reshape(DB, KV_WIDTH, WINDOW)
    cv = cache_win_v[l].transpose(0, 2, 3, 1).reshape(DB, KV_WIDTH, WINDOW)
    st = state_pool[l].transpose(1, 0, 2)
    sinkcol = jnp.repeat(sinks, T).reshape(N_HEADS * T, 1)
    x1s, qc, wk_s, wv_s, pool_s, wcq = _sample_mixer(xs, gmix, win, sinkcol, ck, cv, st, wpool, pscale, wout,
                                                gcross, w_cq[l], SAMPLE_MIXER_BATCH, SAMPLE_MIXER_SUB, T, past_len)

    mk = _mem_cache_rows(cache_mem_k[l])
    mv = _mem_cache_rows(cache_mem_v[l])
    x1p, klast, vlast, plast, wco, wup, wdown, o_s = _prompt_mixer(
        x_prompt, gmix, win, sinks, wpool, pscale, wout, (w_co[l], w_up[l], w_down[l]), qc, mk, mv, PROMPT_TILE)
    mem_k, mem_v, mem_kt, mem_vb = _mem_kv(mem_prompt, gmem, w_ck[l], w_cv[l])
    y_prompt = _prompt_tail(x1p, gcross, wcq, mem_kt, mem_vb, wco, gffn, wup, wdown, gfinal, TAIL_TILE, TAIL_SUB)

    y_sample = _sample_tail(x1s, o_s, wco, gffn, wup, wdown, gfinal, TAIL_TILE, TAIL_SUB).reshape(DB, T, D)

    return (
        y_prompt,
        y_sample,
        klast.reshape(B, N_KV_HEADS, HEAD_DIM, WINDOW).transpose(0, 3, 1, 2)[None],
        vlast.reshape(B, N_KV_HEADS, HEAD_DIM, WINDOW).transpose(0, 3, 1, 2)[None],
        plast[:, POOL_PAD - POOL_HIST:, :][None],
        _mem_cache_unrows(mem_k)[None],
        _mem_cache_unrows(mem_v)[None],
        wk_s.reshape(1, DB, WINDOW, N_KV_HEADS, HEAD_DIM),
        wv_s.reshape(1, DB, WINDOW, N_KV_HEADS, HEAD_DIM),
        pool_s.transpose(1, 0, 2)[None],
    )
```

```python
import functools

import jax
import jax.numpy as jnp
from jax import lax
from jax.experimental import pallas as pl
from jax.experimental.pallas import tpu as pltpu

D_MODEL = 1024
PAST_LEN = 16384
HEAD_DIM = 64
N_HEADS = 8
N_KV_HEADS = 2
GROUP = N_HEADS // N_KV_HEADS
ATTN_WIDTH = N_HEADS * HEAD_DIM
KV_WIDTH = N_KV_HEADS * HEAD_DIM
WINDOW = 128
BLOCK = WINDOW
POOL_WIDTH = D_MODEL - ATTN_WIDTH
POOL_WINDOWS = (2, 4, 8, 16)
POOL_GROUP_WIDTH = 128
POOL_HIST = 15
POOL_PAD = 16
IN_WIDTH = ATTN_WIDTH + 2 * KV_WIDTH + POOL_WIDTH
N_MEM = 256
N_CROSS_HEADS = 4
CROSS_HEAD_DIM = 256
MEM_ROWS = N_CROSS_HEADS * (CROSS_HEAD_DIM // 128)
D_FF = 4 * D_MODEL
FF_CHUNK = 1024
RMS_EPS = 1e-5
NEG_INF = -1e30
Q_SCALE = HEAD_DIM ** -0.5
CQ_SCALE = CROSS_HEAD_DIM ** -0.5
K_OFF = ATTN_WIDTH
V_OFF = ATTN_WIDTH + KV_WIDTH
U_OFF = ATTN_WIDTH + 2 * KV_WIDTH


def _pair_heads(w, axis):
    shape = w.shape
    split = shape[:axis] + (N_KV_HEADS, GROUP, HEAD_DIM) + shape[axis + 1:]
    return jnp.swapaxes(w.reshape(split), axis, axis + 1).reshape(shape)


WOUT_SPLIT = 768

VMEM_LIMIT_BYTES = 56 * 1024 * 1024

_BF = jnp.bfloat16
_F32 = jnp.float32


def _slope(h):
    return 2.0 ** (-8.0 * (h + 1) / N_HEADS)


def _dot(a, b):
    return jnp.dot(a, b, preferred_element_type=_F32)


def _dot_panels(a, panel_refs):
    return jnp.concatenate([_dot(a, ref[...]) for ref in panel_refs], axis=1)


def _rms(x, g):
    ms = jnp.mean(x * x, axis=-1, keepdims=True)
    return x * lax.rsqrt(ms + RMS_EPS) * g


def _softmax_with_sink(s, sink):
    m = jnp.maximum(jnp.max(s, axis=-1, keepdims=True), sink)
    e = jnp.exp(s - m)
    den = jnp.sum(e, axis=-1, keepdims=True) + jnp.exp(sink - m)
    return e * (1.0 / den)


def _softmax_with_sink_parts(s, sink):
    m = jnp.maximum(jnp.max(s, axis=-1, keepdims=True), sink)
    e = jnp.exp(s - m)
    den = jnp.sum(e, axis=-1, keepdims=True) + jnp.exp(sink - m)
    return e, 1.0 / den


def _softmax(s):
    m = jnp.max(s, axis=-1, keepdims=True)
    e = jnp.exp(s - m)
    return e * (1.0 / jnp.sum(e, axis=-1, keepdims=True))


def _window_sums(u_ext, axis):
    out = {}
    s = u_ext
    w = 1
    while w < max(POOL_WINDOWS):
        s = s + pltpu.roll(s, w, axis)
        w *= 2
        out[w] = s
    return out


def _interleave(streams, skew):
    pending = list(streams)
    live = []
    rnd = 0
    while live or pending:
        if pending and rnd % skew == 0:
            live.append(pending.pop(0))
        for g in list(live):
            try:
                next(g)
            except StopIteration:
                live.remove(g)
        rnd += 1


def _ffn_final_stages(x2, gffn, wup_ref, wdown_ref, gfinal, store):
    hn = _rms(x2, gffn).astype(_BF)
    yield
    acc = x2
    for c in range(D_FF // FF_CHUNK):
        hc = _dot(hn, wup_ref[:, c * FF_CHUNK:(c + 1) * FF_CHUNK])
        yield
        hc = jnp.maximum(hc, 0.0)
        hc = (hc * hc).astype(_BF)
        acc = acc + _dot(hc, wdown_ref[c * FF_CHUNK:(c + 1) * FF_CHUNK, :])
        yield
    store(_rms(acc, gfinal))


def _prompt_mixer_kernel(x_ref, gmix_ref, win_ref, sink_ref, wpool_ref, pscale_ref, wouta_ref, woutb_ref,
                         wco_ref, wup_ref, wdown_ref, qc_ref, mk_ref, mv_ref,
                         x1_ref, klast_ref, vlast_ref, plast_ref, wco_bf_ref, wup_bf_ref, wdown_bf_ref,
                         os_ref, kt_scr, v_scr, u_scr, bias_scr, inv_scr, *, tile, nseq):
    b = pl.program_id(0)
    s = pl.program_id(1)
    nb = tile // BLOCK

    @pl.when((b == 0) & (s == 0))
    def _():
        qi = lax.broadcasted_iota(jnp.int32, (BLOCK, 2 * BLOCK), 0)
        kc = lax.broadcasted_iota(jnp.int32, (BLOCK, 2 * BLOCK), 1)
        dist = qi + BLOCK - kc
        valid = (dist >= 0) & (dist <= WINDOW)
        valid_first = valid & (kc >= BLOCK)
        distf = dist.astype(_F32)
        for h in range(N_HEADS):
            ali = -_slope(h) * distf
            bias_scr[0, h] = jnp.where(valid, ali, NEG_INF)
            bias_scr[1, h] = jnp.where(valid_first, ali, NEG_INF)
        prow = lax.broadcasted_iota(jnp.int32, (BLOCK, POOL_GROUP_WIDTH), 0)
        for g, w in enumerate(POOL_WINDOWS):
            inv_scr[0, g] = jnp.full((BLOCK, POOL_GROUP_WIDTH), 1.0 / w, _F32)
            inv_scr[1, g] = 1.0 / jnp.minimum(prow + 1, w).astype(_F32)

    @pl.when(s == 0)
    def _():
        kt_scr[...] = jnp.zeros((KV_WIDTH, BLOCK), _BF)
        v_scr[...] = jnp.zeros((BLOCK, KV_WIDTH), _BF)
        u_scr[...] = jnp.zeros((POOL_PAD, POOL_WIDTH), _F32)

    lane = lax.broadcasted_iota(jnp.int32, (BLOCK, 2 * HEAD_DIM), 1)
    lo = lane < HEAD_DIM
    gmix = gmix_ref[...]
    zeros_kt = jnp.zeros((HEAD_DIM, 2 * BLOCK), _BF)

    def project(j):
        x = x_ref[0, j * BLOCK:(j + 1) * BLOCK, :]
        proj = _dot(_rms(x, gmix).astype(_BF), win_ref[...])
        k = proj[:, K_OFF:K_OFF + KV_WIDTH]
        v = proj[:, V_OFF:V_OFF + KV_WIDTH]
        u = proj[:, U_OFF:]
        kt = k.T
        if j == nb - 1:
            klast_ref[0] = kt
            vlast_ref[0] = v.T
            plast_ref[0] = u[BLOCK - POOL_PAD:]
        return dict(x=x, q=(proj[:, :ATTN_WIDTH] * Q_SCALE).astype(_BF), kt=kt.astype(_BF), v=v.astype(_BF), u=u)

    def pool(j, u_hist, u):
        first = ((s == 0) & (j == 0)).astype(jnp.int32) if j == 0 else 0
        ys = []
        for g, w in enumerate(POOL_WINDOWS):
            c0 = g * POOL_GROUP_WIDTH
            u_ext = jnp.concatenate([u_hist[:, c0:c0 + POOL_GROUP_WIDTH], u[:, c0:c0 + POOL_GROUP_WIDTH]], axis=0)
            sw = _window_sums(u_ext, 0)[w][POOL_PAD:]
            d = (sw * inv_scr[first, g] - u_ext[POOL_PAD:]).astype(_BF)
            y = _dot(d, wpool_ref[g]) * pscale_ref[:, c0:c0 + POOL_GROUP_WIDTH]
            ys.append(y.astype(_BF))
        return ys

    def kv_operands(kt_prev, v_prev, blk):
        kt2 = jnp.concatenate([kt_prev, blk["kt"]], axis=1)
        v2 = jnp.concatenate([v_prev, blk["v"]], axis=0)
        kt_pair = jnp.concatenate([jnp.concatenate([kt2[:HEAD_DIM], zeros_kt], axis=0),
                                   jnp.concatenate([zeros_kt, kt2[HEAD_DIM:]], axis=0)], axis=1)
        lane2 = lax.broadcasted_iota(jnp.int32, v2.shape, 1)
        zero = jnp.zeros_like(v2)
        v_pair = jnp.concatenate([jnp.where(lane2 < HEAD_DIM, v2, zero), jnp.where(lane2 < HEAD_DIM, zero, v2)],
                                 axis=0)
        return kt_pair, v_pair

    def wave_scores(j, blk, kt_pair, pairs):
        first = ((s == 0) & (j == 0)).astype(jnp.int32) if j == 0 else 0
        scores = []
        for p in pairs:
            sc = _dot(blk["q"][:, p * 128:(p + 1) * 128], kt_pair)
            scores.append(sc[:, :2 * BLOCK] + bias_scr[first, p])
            scores.append(sc[:, 2 * BLOCK:] + bias_scr[first, p + GROUP])
        return scores

    def wave_values(v_pair, pairs, scores):
        slabs = []
        for i, p in enumerate(pairs):
            es, invs = [], []
            for half, h in ((0, p), (1, p + GROUP)):
                e, inv = _softmax_with_sink_parts(scores[2 * i + half], sink_ref[h])
                es.append(e.astype(_BF))
                invs.append(inv)
            o = _dot(jnp.concatenate(es, axis=1), v_pair)
            slabs.append((o * jnp.where(lo, invs[0], invs[1])).astype(_BF))
        return slabs

    def output(j, blk, slabs):
        cat = jnp.concatenate(slabs, axis=1)
        x1_ref[0, j * BLOCK:(j + 1) * BLOCK, :] = blk["x"] + _dot_panels(cat, (wouta_ref, woutb_ref))

    heads = [slice(h * CROSS_HEAD_DIM, (h + 1) * CROSS_HEAD_DIM) for h in range(N_CROSS_HEADS)]

    def mem_scores(i):
        q = qc_ref[i * nseq:(i + 1) * nseq, :].astype(_BF)
        return [lax.dot_general(q[:, c], _load_mem_head(mk_ref, i, h).astype(_BF), (((1,), (1,)), ((), ())),
                                preferred_element_type=_F32) for h, c in enumerate(heads)]

    def mem_values(i, scores):
        outs = [_dot(_softmax(sc).astype(_BF), _load_mem_head(mv_ref, i, h).astype(_BF))
                for h, sc in enumerate(scores)]
        os_ref[i * nseq:(i + 1) * nseq, :] = jnp.concatenate(outs, axis=1)

    kt_prev, v_prev, u_hist = kt_scr[...], v_scr[...], u_scr[...]
    blk = project(0)
    done = None
    for j in range(nb):
        kt_pair, v_pair = kv_operands(kt_prev, v_prev, blk)
        sc0 = wave_scores(j, blk, kt_pair, (0, 1))
        nxt = project(j + 1) if j + 1 < nb else None
        msc = mem_scores(j)
        sc1 = wave_scores(j, blk, kt_pair, (2, 3))
        pooled = pool(j, u_hist, blk["u"])
        at0 = wave_values(v_pair, (0, 1), sc0)
        if done is not None:
            output(*done)
        mem_values(j, msc)
        at1 = wave_values(v_pair, (2, 3), sc1)
        done = (j, blk, at0 + at1 + pooled)
        kt_prev, v_prev, u_hist = blk["kt"], blk["v"], blk["u"][BLOCK - POOL_PAD:]
        blk = nxt
    output(*done)
    kt_scr[...] = kt_prev
    v_scr[...] = v_prev
    u_scr[...] = u_hist

    for src, dst in ((wco_ref, wco_bf_ref), (wup_ref, wup_bf_ref), (wdown_ref, wdown_bf_ref)):
        dst[...] = src[...].astype(_BF)


def _prompt_mixer(x, gmix, win, sinks, wpool, pscale, wout, tail_weights, qc, mk, mv, tile):
    B, S, D = x.shape
    ns = S // tile
    nsteps = B * ns
    nmem = tile // BLOCK
    nseq = qc.shape[0] // mk.shape[0]
    assert mk.shape[0] == nsteps * nmem
    const = lambda *shape: pl.BlockSpec(shape, lambda b, s: (0,) * len(shape))
    chunk = lambda w: pl.BlockSpec((w.shape[0] // nsteps, w.shape[1]), lambda b, s: (b * ns + s, 0))
    assert all(w.shape[0] % (16 * nsteps) == 0 for w in tail_weights)
    mem_rows = pl.BlockSpec((nmem * nseq, D), lambda b, s: (b * ns + s, 0))
    mem_cache = pl.BlockSpec((nmem, N_MEM * MEM_ROWS, 128), lambda b, s: (b * ns + s, 0, 0))
    return pl.pallas_call(
        functools.partial(_prompt_mixer_kernel, tile=tile, nseq=nseq),
        grid=(B, ns),
        in_specs=[
            pl.BlockSpec((1, tile, D), lambda b, s: (b, s, 0)),
            const(1, D),
            const(D, IN_WIDTH),
            pl.BlockSpec(memory_space=pltpu.SMEM),
            const(len(POOL_WINDOWS), POOL_GROUP_WIDTH, POOL_GROUP_WIDTH),
            const(1, POOL_WIDTH),
            const(D, WOUT_SPLIT),
            const(D, D - WOUT_SPLIT),
        ] + [chunk(w) for w in tail_weights] + [mem_rows, mem_cache, mem_cache],
        out_specs=[
            pl.BlockSpec((1, tile, D), lambda b, s: (b, s, 0)),
            pl.BlockSpec((1, BLOCK, KV_WIDTH), lambda b, s: (b, 0, 0)),
            pl.BlockSpec((1, BLOCK, KV_WIDTH), lambda b, s: (b, 0, 0)),
            pl.BlockSpec((1, POOL_PAD, POOL_WIDTH), lambda b, s: (b, 0, 0)),
        ] + [chunk(w) for w in tail_weights] + [mem_rows],
        out_shape=[
            jax.ShapeDtypeStruct((B, S, D), _F32),
            jax.ShapeDtypeStruct((B, BLOCK, KV_WIDTH), _F32),
            jax.ShapeDtypeStruct((B, BLOCK, KV_WIDTH), _F32),
            jax.ShapeDtypeStruct((B, POOL_PAD, POOL_WIDTH), _F32),
        ] + [jax.ShapeDtypeStruct(w.shape, _BF) for w in tail_weights] + [jax.ShapeDtypeStruct(qc.shape, _F32)],
        scratch_shapes=[
            pltpu.VMEM((KV_WIDTH, BLOCK), _BF),
            pltpu.VMEM((BLOCK, KV_WIDTH), _BF),
            pltpu.VMEM((POOL_PAD, POOL_WIDTH), _F32),
            pltpu.VMEM((2, N_HEADS, BLOCK, 2 * BLOCK), _F32),
            pltpu.VMEM((2, len(POOL_WINDOWS), BLOCK, POOL_GROUP_WIDTH), _F32),
        ],
        compiler_params=pltpu.CompilerParams(
            dimension_semantics=("arbitrary", "arbitrary"), vmem_limit_bytes=VMEM_LIMIT_BYTES),
        name="prompt_mixer",
    )(x, gmix, win, sinks, wpool, pscale, *wout, *tail_weights, qc, mk, mv)


def _mem_kv_kernel(mem_ref, gmem_ref, wck_ref, wcv_ref, k_ref, v_ref, kt_ref, vb_ref, wck_scr, wcv_scr):
    @pl.when(pl.program_id(0) == 0)
    def _():
        wck_scr[...] = wck_ref[...].astype(_BF)
        wcv_scr[...] = wcv_ref[...].astype(_BF)

    hm = _rms(mem_ref[0], gmem_ref[...]).astype(_BF)
    k = _dot(hm, wck_scr[...])
    v = _dot(hm, wcv_scr[...])
    vb_ref[0] = v.astype(_BF)
    for h in range(N_CROSS_HEADS):
        kt_ref[0, h] = k[:, h * CROSS_HEAD_DIM:(h + 1) * CROSS_HEAD_DIM].T.astype(_BF)
        for half in range(CROSS_HEAD_DIM // 128):
            c0 = h * CROSS_HEAD_DIM + half * 128
            rows = pl.ds(half * N_CROSS_HEADS + h, N_MEM, stride=MEM_ROWS)
            k_ref[0, rows, :] = k[:, c0:c0 + 128]
            v_ref[0, rows, :] = v[:, c0:c0 + 128]


def _mem_kv(mem, gmem, wck, wcv):
    B, M, D = mem.shape
    const = lambda *shape: pl.BlockSpec(shape, lambda b: (0,) * len(shape))
    return pl.pallas_call(
        _mem_kv_kernel,
        grid=(B,),
        in_specs=[pl.BlockSpec((1, M, D), lambda b: (b, 0, 0)), const(1, D), const(D, D), const(D, D)],
        out_specs=[
            pl.BlockSpec((1, M * MEM_ROWS, 128), lambda b: (b, 0, 0)),
            pl.BlockSpec((1, M * MEM_ROWS, 128), lambda b: (b, 0, 0)),
            pl.BlockSpec((1, N_CROSS_HEADS, CROSS_HEAD_DIM, M), lambda b: (b, 0, 0, 0)),
            pl.BlockSpec((1, M, D), lambda b: (b, 0, 0)),
        ],
        out_shape=[
            jax.ShapeDtypeStruct((B, M * MEM_ROWS, 128), _F32),
            jax.ShapeDtypeStruct((B, M * MEM_ROWS, 128), _F32),
            jax.ShapeDtypeStruct((B, N_CROSS_HEADS, CROSS_HEAD_DIM, M), _BF),
            jax.ShapeDtypeStruct((B, M, D), _BF),
        ],
        scratch_shapes=[pltpu.VMEM((D, D), _BF), pltpu.VMEM((D, D), _BF)],
        compiler_params=pltpu.CompilerParams(
            dimension_semantics=("arbitrary",), vmem_limit_bytes=VMEM_LIMIT_BYTES),
        name="prompt_mem_kv",
    )(mem, gmem, wck, wcv)


def _mem_cache_unrows(rows):
    nb = rows.shape[0]
    c = rows.reshape(nb, N_MEM, CROSS_HEAD_DIM // 128, N_CROSS_HEADS, 128)
    return c.transpose(0, 1, 3, 2, 4).reshape(nb, N_MEM, N_CROSS_HEADS, CROSS_HEAD_DIM)


def _prompt_tail_kernel(x1_ref, gcross_ref, wcq_ref, kt_ref, vb_ref, wco_ref, gffn_ref, wup_ref, wdown_ref,
                        gfinal_ref, y_ref, *, tile, sub):
    gcross, gffn, gfinal = gcross_ref[...], gffn_ref[...], gfinal_ref[...]

    def stream(r0):
        x1 = x1_ref[0, r0:r0 + sub, :]
        hn = _rms(x1, gcross).astype(_BF)
        yield
        q = (_dot(hn, wcq_ref[...]) * CQ_SCALE).astype(_BF)
        yield
        heads = [slice(h * CROSS_HEAD_DIM, (h + 1) * CROSS_HEAD_DIM) for h in range(N_CROSS_HEADS)]
        scores = [_dot(q[:, c], kt_ref[0, h]) for h, c in enumerate(heads)]
        yield
        outs = [_dot(_softmax(sc).astype(_BF), vb_ref[0, :, c]).astype(_BF) for sc, c in zip(scores, heads)]
        yield
        x2 = x1 + _dot(jnp.concatenate(outs, axis=1), wco_ref[...])
        yield

        def store(y):
            y_ref[0, r0:r0 + sub, :] = y

        yield from _ffn_final_stages(x2, gffn, wup_ref, wdown_ref, gfinal, store)

    _interleave([stream(r0) for r0 in range(0, tile, sub)], TAIL_SKEW)


def _single(shape, index_map):
    return pl.BlockSpec(shape, index_map, pipeline_mode=pl.Buffered(1))


def _prompt_tail(x1, gcross, wcq, kt, vb, wco, gffn, wup, wdown, gfinal, tile, sub):
    B, S, D = x1.shape
    const = lambda *shape: _single(shape, lambda b, s: (0,) * len(shape))
    return pl.pallas_call(
        functools.partial(_prompt_tail_kernel, tile=tile, sub=sub),
        grid=(B, S // tile),
        in_specs=[
            pl.BlockSpec((1, tile, D), lambda b, s: (b, s, 0)),
            const(1, D),
            const(D, D),
            pl.BlockSpec((1, N_CROSS_HEADS, CROSS_HEAD_DIM, N_MEM), lambda b, s: (b, 0, 0, 0)),
            pl.BlockSpec((1, N_MEM, D), lambda b, s: (b, 0, 0)),
            const(D, D),
            const(1, D),
            const(D, D_FF),
            const(D_FF, D),
            const(1, D),
        ],
        out_specs=pl.BlockSpec((1, tile, D), lambda b, s: (b, s, 0)),
        out_shape=jax.ShapeDtypeStruct((B, S, D), _F32),
        compiler_params=pltpu.CompilerParams(
            dimension_semantics=("arbitrary", "arbitrary"), vmem_limit_bytes=VMEM_LIMIT_BYTES),
        name="prompt_tail",
    )(x1, gcross, wcq, kt, vb, wco, gffn, wup, wdown, gfinal)


def _sample_mixer_kernel(x_ref, gmix_ref, win_ref, sinkcol_ref, ck_ref, cv_ref, st_ref, wpool_ref, pscale_ref,
                         wouta_ref, woutb_ref, gcross_ref, wcq_ref,
                         x1_ref, qc_ref, wk_ref, wv_ref, pool_ref, wcq_bf_ref,
                         kc_scr, vc_scr, u_scr, *, nbatch, sub, nseq, past_len):
    @pl.when(pl.program_id(0) == 0)
    def _():
        wcq_bf_ref[...] = wcq_ref[...].astype(_BF)

    rows = sub * nseq
    nkeys = 2 * WINDOW
    gmix, gcross = gmix_ref[...], gcross_ref[...]
    sinkcol = sinkcol_ref[...][None]

    t = lax.broadcasted_iota(jnp.int32, (nseq, nkeys), 0)
    c = lax.broadcasted_iota(jnp.int32, (nseq, nkeys), 1)
    dist = jnp.where(c < WINDOW, WINDOW + t - c, t - (c - WINDOW))
    valid = (dist >= 0) & (dist <= WINDOW) & (c < WINDOW + nseq)
    distf = dist.astype(_F32)
    bias = jnp.concatenate([jnp.where(valid, -_slope(h) * distf, NEG_INF) for h in range(N_HEADS)], axis=0)[None]
    lane = lax.broadcasted_iota(jnp.int32, (rows, 2 * HEAD_DIM), 1)
    lo = lane < HEAD_DIM
    tpos = past_len + lax.broadcasted_iota(jnp.int32, (nseq, POOL_GROUP_WIDTH), 0)
    zpad = jnp.zeros((sub, nkeys - WINDOW, KV_WIDTH), _BF)

    def stream(b0):
        bs = slice(b0, b0 + sub)
        rs = slice(b0 * nseq, (b0 + sub) * nseq)
        x = x_ref[rs, :]
        proj = _dot(_rms(x, gmix).astype(_BF), win_ref[...])
        yield
        k_new = proj[:, K_OFF:K_OFF + KV_WIDTH].reshape(sub, nseq, KV_WIDTH)
        v_new = proj[:, V_OFF:V_OFF + KV_WIDTH].reshape(sub, nseq, KV_WIDTH)
        ck = jnp.swapaxes(ck_ref[bs], 1, 2)
        cv = jnp.swapaxes(cv_ref[bs], 1, 2)
        wk_ref[bs, :WINDOW - nseq, :] = ck[:, nseq:, :]
        wk_ref[bs, WINDOW - nseq:, :] = k_new
        wv_ref[bs, :WINDOW - nseq, :] = cv[:, nseq:, :]
        wv_ref[bs, WINDOW - nseq:, :] = v_new
        kc_scr[bs, WINDOW:, :] = zpad
        vc_scr[bs, WINDOW:, :] = zpad
        kc_scr[bs, :WINDOW, :] = ck.astype(_BF)
        vc_scr[bs, :WINDOW, :] = cv.astype(_BF)
        kc_scr[bs, WINDOW:WINDOW + nseq, :] = k_new.astype(_BF)
        vc_scr[bs, WINDOW:WINDOW + nseq, :] = v_new.astype(_BF)
        qsc = proj[:, :ATTN_WIDTH] * Q_SCALE
        q_lo, q_hi = [], []
        for p in range(GROUP):
            slab = qsc[:, p * 128:(p + 1) * 128]
            q_lo.append(jnp.where(lo, slab, 0.0).reshape(sub, nseq, 128))
            q_hi.append(jnp.where(lo, 0.0, slab).reshape(sub, nseq, 128))
        qm = jnp.concatenate(q_lo + q_hi, axis=1).astype(_BF)
        sc = jnp.einsum('bqd,bkd->bqk', qm, kc_scr[bs], preferred_element_type=_F32) + bias
        yield
        pr = _softmax_with_sink(sc, sinkcol).astype(_BF)
        o = jnp.einsum('bqk,bkd->bqd', pr, vc_scr[bs], preferred_element_type=_F32)
        yield
        attn = []
        for p in range(GROUP):
            o_lo = o[:, p * nseq:(p + 1) * nseq, :].reshape(rows, 128)
            o_hi = o[:, (p + GROUP) * nseq:(p + GROUP + 1) * nseq, :].reshape(rows, 128)
            attn.append(jnp.where(lo, o_lo, o_hi).astype(_BF))
        ext = POOL_PAD + nseq
        seq_rows = lambda r: pl.ds(b0 * ext + r, sub, stride=ext)
        pooled = []
        for g, w in enumerate(POOL_WINDOWS):
            c0 = g * POOL_GROUP_WIDTH
            u_scr[g, seq_rows(0), :] = jnp.zeros((sub, POOL_GROUP_WIDTH), _F32)
            for r in range(POOL_HIST):
                u_scr[g, seq_rows(POOL_PAD - POOL_HIST + r), :] = st_ref[r, bs, c0:c0 + POOL_GROUP_WIDTH]
            for i in range(sub):
                u_scr[g, (b0 + i) * ext + POOL_PAD:(b0 + i + 1) * ext, :] = (
                    proj[i * nseq:(i + 1) * nseq, U_OFF + c0:U_OFF + c0 + POOL_GROUP_WIDTH])
            for r in range(POOL_HIST):
                pool_ref[r, bs, c0:c0 + POOL_GROUP_WIDTH] = u_scr[g, seq_rows(ext - POOL_HIST + r), :]
            ug = u_scr[g, b0 * ext:(b0 + sub) * ext, :].reshape(sub, ext, POOL_GROUP_WIDTH)
            sw = _window_sums(ug, 1)[w][:, POOL_PAD:, :]
            cnt = jnp.minimum(tpos + 1, w).astype(_F32)
            d = (sw / cnt[None] - ug[:, POOL_PAD:, :]).reshape(rows, POOL_GROUP_WIDTH).astype(_BF)
            y = _dot(d, wpool_ref[g]) * pscale_ref[:, c0:c0 + POOL_GROUP_WIDTH]
            pooled.append(y.astype(_BF))
        yield
        x1 = x + _dot_panels(jnp.concatenate(attn + pooled, axis=1), (wouta_ref, woutb_ref))
        x1_ref[rs, :] = x1
        yield
        qc_ref[rs, :] = _dot(_rms(x1, gcross).astype(_BF), wcq_bf_ref[...]) * CQ_SCALE

    _interleave([stream(b0) for b0 in range(0, nbatch, sub)], SAMPLE_MIXER_SKEW)


def _sample_mixer(x, gmix, win, sinkcol, ck, cv, st, wpool, pscale, wout, gcross, wcq, nbatch, sub, nseq,
                  past_len):
    R, D = x.shape
    rows = nbatch * nseq
    const = lambda *shape: pl.BlockSpec(shape, lambda i: (0,) * len(shape))
    return pl.pallas_call(
        functools.partial(_sample_mixer_kernel, nbatch=nbatch, sub=sub, nseq=nseq, past_len=past_len),
        grid=(R // rows,),
        in_specs=[
            pl.BlockSpec((rows, D), lambda i: (i, 0)),
            const(1, D),
            const(D, IN_WIDTH),
            const(N_HEADS * nseq, 1),
            pl.BlockSpec((nbatch, WINDOW, KV_WIDTH), lambda i: (i, 0, 0)),
            pl.BlockSpec((nbatch, WINDOW, KV_WIDTH), lambda i: (i, 0, 0)),
            pl.BlockSpec((POOL_HIST, nbatch, POOL_WIDTH), lambda i: (0, i, 0)),
            const(len(POOL_WINDOWS), POOL_GROUP_WIDTH, POOL_GROUP_WIDTH),
            const(1, POOL_WIDTH),
            const(D, WOUT_SPLIT),
            const(D, D - WOUT_SPLIT),
            const(1, D),
            const(D, D),
        ],
        out_specs=[
            pl.BlockSpec((rows, D), lambda i: (i, 0)),
            pl.BlockSpec((rows, D), lambda i: (i, 0)),
            pl.BlockSpec((nbatch, WINDOW, KV_WIDTH), lambda i: (i, 0, 0)),
            pl.BlockSpec((nbatch, WINDOW, KV_WIDTH), lambda i: (i, 0, 0)),
            pl.BlockSpec((POOL_HIST, nbatch, POOL_WIDTH), lambda i: (0, i, 0)),
            const(D, D),
        ],
        out_shape=[
            jax.ShapeDtypeStruct((R, D), _F32),
            jax.ShapeDtypeStruct((R, D), _F32),
            jax.ShapeDtypeStruct((R // nseq, WINDOW, KV_WIDTH), _F32),
            jax.ShapeDtypeStruct((R // nseq, WINDOW, KV_WIDTH), _F32),
            jax.ShapeDtypeStruct((POOL_HIST, R // nseq, POOL_WIDTH), _F32),
            jax.ShapeDtypeStruct((D, D), _BF),
        ],
        scratch_shapes=[
            pltpu.VMEM((nbatch, 2 * WINDOW, KV_WIDTH), _BF),
            pltpu.VMEM((nbatch, 2 * WINDOW, KV_WIDTH), _BF),
            pltpu.VMEM((len(POOL_WINDOWS), nbatch * (POOL_PAD + nseq), POOL_GROUP_WIDTH), _F32),
        ],
        compiler_params=pltpu.CompilerParams(
            dimension_semantics=("arbitrary",), vmem_limit_bytes=VMEM_LIMIT_BYTES),
        name="sample_mixer",
    )(x, gmix, win, sinkcol, ck, cv, st, wpool, pscale, *wout, gcross, wcq)


def _mem_cache_rows(cache):
    nb = cache.shape[0]
    c = cache.reshape(nb, N_MEM, N_CROSS_HEADS, CROSS_HEAD_DIM // 128, 128)
    return c.transpose(0, 1, 3, 2, 4).reshape(nb, N_MEM * MEM_ROWS, 128)


def _load_mem_head(ref, b, h):
    halves = [ref[b, pl.ds(half * N_CROSS_HEADS + h, N_MEM, stride=MEM_ROWS), :]
              for half in range(CROSS_HEAD_DIM // 128)]
    return jnp.concatenate(halves, axis=1)


def _sample_tail_kernel(x1_ref, o_ref, wco_ref, gffn_ref, wup_ref, wdown_ref, gfinal_ref, y_ref, *, tile, sub):
    gffn, gfinal = gffn_ref[...], gfinal_ref[...]

    def stream(r0):
        x2 = x1_ref[r0:r0 + sub, :] + _dot(o_ref[r0:r0 + sub, :].astype(_BF), wco_ref[...])
        yield

        def store(y):
            y_ref[r0:r0 + sub, :] = y

        yield from _ffn_final_stages(x2, gffn, wup_ref, wdown_ref, gfinal, store)

    _interleave([stream(r0) for r0 in range(0, tile, sub)], TAIL_SKEW)


def _sample_tail(x1, o, wco, gffn, wup, wdown, gfinal, tile, sub):
    R, D = x1.shape
    const = lambda *shape: _single(shape, lambda i: (0,) * len(shape))
    return pl.pallas_call(
        functools.partial(_sample_tail_kernel, tile=tile, sub=sub),
        grid=(R // tile,),
        in_specs=[
            pl.BlockSpec((tile, D), lambda i: (i, 0)),
            pl.BlockSpec((tile, D), lambda i: (i, 0)),
            const(D, D),
            const(1, D),
            const(D, D_FF),
            const(D_FF, D),
            const(1, D),
        ],
        out_specs=pl.BlockSpec((tile, D), lambda i: (i, 0)),
        out_shape=jax.ShapeDtypeStruct((R, D), _F32),
        compiler_params=pltpu.CompilerParams(
            dimension_semantics=("arbitrary",), vmem_limit_bytes=VMEM_LIMIT_BYTES),
        name="sample_tail",
    )(x1, o, wco, gffn, wup, wdown, gfinal)


PROMPT_TILE = 512
TAIL_TILE = 1024
TAIL_SUB = 256
TAIL_SKEW = 4
SAMPLE_MIXER_BATCH = 32
SAMPLE_MIXER_SUB = 16
SAMPLE_MIXER_SKEW = 3


def kernel(x_prompt, x_sample, cache_win_k, cache_win_v, state_pool, cache_mem_k, cache_mem_v, mem_prompt,
           g_mix, w_in, attn_sinks, w_pool, pool_scale, w_out, g_cross, g_mem, w_cq, w_ck, w_cv, w_co,
           g_ffn, w_up, w_down, g_final):
    depth = g_mix.shape[0]
    assert depth == 1, "one layer per step"
    B, S, D = x_prompt.shape
    DB, T, _ = x_sample.shape
    past_len = PAST_LEN
    l = 0

    win = jnp.concatenate([_pair_heads(w_in[l][:, :ATTN_WIDTH], 1), w_in[l][:, ATTN_WIDTH:]], axis=1).astype(_BF)
    wout = jnp.concatenate([_pair_heads(w_out[l][:ATTN_WIDTH, :], 0), w_out[l][ATTN_WIDTH:, :]], axis=0)
    wout = (wout[:, :WOUT_SPLIT].astype(_BF), wout[:, WOUT_SPLIT:].astype(_BF))
    wpool = w_pool[l].astype(_BF)
    gmix, gcross, gmem, gffn = (g[l].reshape(1, D) for g in (g_mix, g_cross, g_mem, g_ffn))
    gfinal = g_final.reshape(1, D)
    pscale = pool_scale[l].reshape(1, POOL_WIDTH)
    sinks = attn_sinks[l]

    xs = x_sample.reshape(DB * T, D)
    ck = cache_win_k[l].transpose(0, 2, 3, 1).reshape(DB, KV_WIDTH, WINDOW)
    cv = cache_win_v[l].transpose(0, 2, 3, 1).reshape(DB, KV_WIDTH, WINDOW)
    st = state_pool[l].transpose(1, 0, 2)
    sinkcol = jnp.repeat(sinks, T).reshape(N_HEADS * T, 1)
    x1s, qc, wk_s, wv_s, pool_s, wcq = _sample_mixer(xs, gmix, win, sinkcol, ck, cv, st, wpool, pscale, wout,
                                                gcross, w_cq[l], SAMPLE_MIXER_BATCH, SAMPLE_MIXER_SUB, T, past_len)

    mk = _mem_cache_rows(cache_mem_k[l])
    mv = _mem_cache_rows(cache_mem_v[l])
    x1p, klast, vlast, plast, wco, wup, wdown, o_s = _prompt_mixer(
        x_prompt, gmix, win, sinks, wpool, pscale, wout, (w_co[l], w_up[l], w_down[l]), qc, mk, mv, PROMPT_TILE)
    mem_k, mem_v, mem_kt, mem_vb = _mem_kv(mem_prompt, gmem, w_ck[l], w_cv[l])
    y_prompt = _prompt_tail(x1p, gcross, wcq, mem_kt, mem_vb, wco, gffn, wup, wdown, gfinal, TAIL_TILE, TAIL_SUB)

    y_sample = _sample_tail(x1s, o_s, wco, gffn, wup, wdown, gfinal, TAIL_TILE, TAIL_SUB).reshape(DB, T, D)

    return (
        y_prompt,
        y_sample,
        klast.reshape(B, N_KV_HEADS, HEAD_DIM, WINDOW).transpose(0, 3, 1, 2)[None],
        vlast.reshape(B, N_KV_HEADS, HEAD_DIM, WINDOW).transpose(0, 3, 1, 2)[None],
        plast[:, POOL_PAD - POOL_HIST:, :][None],
        _mem_cache_unrows(mem_k)[None],
        _mem_cache_unrows(mem_v)[None],
        wk_s.reshape(1, DB, WINDOW, N_KV_HEADS, HEAD_DIM),
        wv_s.reshape(1, DB, WINDOW, N_KV_HEADS, HEAD_DIM),
        pool_s.transpose(1, 0, 2)[None],
    )
```

```python
import functools

import jax
import jax.numpy as jnp
from jax import lax
from jax.experimental import pallas as pl
from jax.experimental.pallas import tpu as pltpu

D_MODEL = 1024
PAST_LEN = 16384
HEAD_DIM = 64
N_HEADS = 8
N_KV_HEADS = 2
GROUP = N_HEADS // N_KV_HEADS
ATTN_WIDTH = N_HEADS * HEAD_DIM
KV_WIDTH = N_KV_HEADS * HEAD_DIM
WINDOW = 128
BLOCK = WINDOW
POOL_WIDTH = D_MODEL - ATTN_WIDTH
POOL_WINDOWS = (2, 4, 8, 16)
POOL_GROUP_WIDTH = 128
POOL_HIST = 15
POOL_PAD = 16
IN_WIDTH = ATTN_WIDTH + 2 * KV_WIDTH + POOL_WIDTH
N_MEM = 256
N_CROSS_HEADS = 4
CROSS_HEAD_DIM = 256
MEM_ROWS = N_CROSS_HEADS * (CROSS_HEAD_DIM // 128)
D_FF = 4 * D_MODEL
FF_CHUNK = 2048
RMS_EPS = 1e-5
NEG_INF = -1e30
Q_SCALE = HEAD_DIM ** -0.5
CQ_SCALE = CROSS_HEAD_DIM ** -0.5
K_OFF = ATTN_WIDTH
V_OFF = ATTN_WIDTH + KV_WIDTH
U_OFF = ATTN_WIDTH + 2 * KV_WIDTH


def _pair_heads(w, axis):
    shape = w.shape
    split = shape[:axis] + (N_KV_HEADS, GROUP, HEAD_DIM) + shape[axis + 1:]
    return jnp.swapaxes(w.reshape(split), axis, axis + 1).reshape(shape)


WOUT_SPLIT = 768

VMEM_LIMIT_BYTES = 56 * 1024 * 1024

_BF = jnp.bfloat16
_F32 = jnp.float32


def _slope(h):
    return 2.0 ** (-8.0 * (h + 1) / N_HEADS)


def _dot(a, b):
    return jnp.dot(a, b, preferred_element_type=_F32)


def _dot_panels(a, panel_refs):
    return jnp.concatenate([_dot(a, ref[...]) for ref in panel_refs], axis=1)


def _rms(x, g):
    ms = jnp.mean(x * x, axis=-1, keepdims=True)
    return x * lax.rsqrt(ms + RMS_EPS) * g


def _softmax_with_sink(s, sink):
    m = jnp.maximum(jnp.max(s, axis=-1, keepdims=True), sink)
    e = jnp.exp(s - m)
    den = jnp.sum(e, axis=-1, keepdims=True) + jnp.exp(sink - m)
    return e * (1.0 / den)


def _softmax_with_sink_parts(s, sink):
    m = jnp.maximum(jnp.max(s, axis=-1, keepdims=True), sink)
    e = jnp.exp(s - m)
    den = jnp.sum(e, axis=-1, keepdims=True) + jnp.exp(sink - m)
    return e, 1.0 / den


def _softmax(s):
    m = jnp.max(s, axis=-1, keepdims=True)
    e = jnp.exp(s - m)
    return e * (1.0 / jnp.sum(e, axis=-1, keepdims=True))


def _window_sums(u_ext, axis):
    out = {}
    s = u_ext
    w = 1
    while w < max(POOL_WINDOWS):
        s = s + pltpu.roll(s, w, axis)
        w *= 2
        out[w] = s
    return out


def _interleave(streams, skew):
    pending = list(streams)
    live = []
    rnd = 0
    while live or pending:
        if pending and rnd % skew == 0:
            live.append(pending.pop(0))
        for g in list(live):
            try:
                next(g)
            except StopIteration:
                live.remove(g)
        rnd += 1


def _ffn_final_stages(x2, gffn, wup_ref, wdown_ref, gfinal, store):
    hn = _rms(x2, gffn).astype(_BF)
    yield
    acc = x2
    for c in range(D_FF // FF_CHUNK):
        hc = _dot(hn, wup_ref[:, c * FF_CHUNK:(c + 1) * FF_CHUNK])
        yield
        hc = jnp.maximum(hc, 0.0)
        hc = (hc * hc).astype(_BF)
        acc = acc + _dot(hc, wdown_ref[c * FF_CHUNK:(c + 1) * FF_CHUNK, :])
        yield
    store(_rms(acc, gfinal))


def _prompt_mixer_kernel(x_ref, gmix_ref, win_ref, sink_ref, wpool_ref, pscale_ref, wouta_ref, woutb_ref,
                         wco_ref, wup_ref, wdown_ref, qc_ref, mk_ref, mv_ref,
                         x1_ref, klast_ref, vlast_ref, plast_ref, wco_bf_ref, wup_bf_ref, wdown_bf_ref,
                         os_ref, kt_scr, v_scr, u_scr, bias_scr, inv_scr, *, tile, nseq):
    b = pl.program_id(0)
    s = pl.program_id(1)
    nb = tile // BLOCK

    @pl.when((b == 0) & (s == 0))
    def _():
        qi = lax.broadcasted_iota(jnp.int32, (BLOCK, 2 * BLOCK), 0)
        kc = lax.broadcasted_iota(jnp.int32, (BLOCK, 2 * BLOCK), 1)
        dist = qi + BLOCK - kc
        valid = (dist >= 0) & (dist <= WINDOW)
        valid_first = valid & (kc >= BLOCK)
        distf = dist.astype(_F32)
        for h in range(N_HEADS):
            ali = -_slope(h) * distf
            bias_scr[0, h] = jnp.where(valid, ali, NEG_INF)
            bias_scr[1, h] = jnp.where(valid_first, ali, NEG_INF)
        prow = lax.broadcasted_iota(jnp.int32, (BLOCK, POOL_GROUP_WIDTH), 0)
        for g, w in enumerate(POOL_WINDOWS):
            inv_scr[0, g] = jnp.full((BLOCK, POOL_GROUP_WIDTH), 1.0 / w, _F32)
            inv_scr[1, g] = 1.0 / jnp.minimum(prow + 1, w).astype(_F32)

    @pl.when(s == 0)
    def _():
        kt_scr[...] = jnp.zeros((KV_WIDTH, BLOCK), _BF)
        v_scr[...] = jnp.zeros((BLOCK, KV_WIDTH), _BF)
        u_scr[...] = jnp.zeros((POOL_PAD, POOL_WIDTH), _F32)

    lane = lax.broadcasted_iota(jnp.int32, (BLOCK, 2 * HEAD_DIM), 1)
    lo = lane < HEAD_DIM
    gmix = gmix_ref[...]
    zeros_kt = jnp.zeros((HEAD_DIM, 2 * BLOCK), _BF)

    def project(j):
        x = x_ref[0, j * BLOCK:(j + 1) * BLOCK, :]
        proj = _dot(_rms(x, gmix).astype(_BF), win_ref[...])
        k = proj[:, K_OFF:K_OFF + KV_WIDTH]
        v = proj[:, V_OFF:V_OFF + KV_WIDTH]
        u = proj[:, U_OFF:]
        kt = k.T
        if j == nb - 1:
            klast_ref[0] = kt
            vlast_ref[0] = v.T
            plast_ref[0] = u[BLOCK - POOL_PAD:]
        return dict(x=x, q=(proj[:, :ATTN_WIDTH] * Q_SCALE).astype(_BF), kt=kt.astype(_BF), v=v.astype(_BF), u=u)

    def pool(j, u_hist, u):
        first = ((s == 0) & (j == 0)).astype(jnp.int32) if j == 0 else 0
        ys = []
        for g, w in enumerate(POOL_WINDOWS):
            c0 = g * POOL_GROUP_WIDTH
            u_ext = jnp.concatenate([u_hist[:, c0:c0 + POOL_GROUP_WIDTH], u[:, c0:c0 + POOL_GROUP_WIDTH]], axis=0)
            sw = _window_sums(u_ext, 0)[w][POOL_PAD:]
            d = (sw * inv_scr[first, g] - u_ext[POOL_PAD:]).astype(_BF)
            y = _dot(d, wpool_ref[g]) * pscale_ref[:, c0:c0 + POOL_GROUP_WIDTH]
            ys.append(y.astype(_BF))
        return ys

    def kv_operands(kt_prev, v_prev, blk):
        kt2 = jnp.concatenate([kt_prev, blk["kt"]], axis=1)
        v2 = jnp.concatenate([v_prev, blk["v"]], axis=0)
        kt_pair = jnp.concatenate([jnp.concatenate([kt2[:HEAD_DIM], zeros_kt], axis=0),
                                   jnp.concatenate([zeros_kt, kt2[HEAD_DIM:]], axis=0)], axis=1)
        lane2 = lax.broadcasted_iota(jnp.int32, v2.shape, 1)
        zero = jnp.zeros_like(v2)
        v_pair = jnp.concatenate([jnp.where(lane2 < HEAD_DIM, v2, zero), jnp.where(lane2 < HEAD_DIM, zero, v2)],
                                 axis=0)
        return kt_pair, v_pair

    def wave_scores(j, blk, kt_pair, pairs):
        first = ((s == 0) & (j == 0)).astype(jnp.int32) if j == 0 else 0
        scores = []
        for p in pairs:
            sc = _dot(blk["q"][:, p * 128:(p + 1) * 128], kt_pair)
            scores.append(sc[:, :2 * BLOCK] + bias_scr[first, p])
            scores.append(sc[:, 2 * BLOCK:] + bias_scr[first, p + GROUP])
        return scores

    def wave_values(v_pair, pairs, scores):
        slabs = []
        for i, p in enumerate(pairs):
            es, invs = [], []
            for half, h in ((0, p), (1, p + GROUP)):
                e, inv = _softmax_with_sink_parts(scores[2 * i + half], sink_ref[h])
                es.append(e.astype(_BF))
                invs.append(inv)
            o = _dot(jnp.concatenate(es, axis=1), v_pair)
            slabs.append((o * jnp.where(lo, invs[0], invs[1])).astype(_BF))
        return slabs

    def output(j, blk, slabs):
        cat = jnp.concatenate(slabs, axis=1)
        x1_ref[0, j * BLOCK:(j + 1) * BLOCK, :] = blk["x"] + _dot_panels(cat, (wouta_ref, woutb_ref))

    heads = [slice(h * CROSS_HEAD_DIM, (h + 1) * CROSS_HEAD_DIM) for h in range(N_CROSS_HEADS)]

    def mem_scores(i):
        q = qc_ref[i * nseq:(i + 1) * nseq, :].astype(_BF)
        return [lax.dot_general(q[:, c], _load_mem_head(mk_ref, i, h).astype(_BF), (((1,), (1,)), ((), ())),
                                preferred_element_type=_F32) for h, c in enumerate(heads)]

    def mem_values(i, scores):
        outs = [_dot(_softmax(sc).astype(_BF), _load_mem_head(mv_ref, i, h).astype(_BF))
                for h, sc in enumerate(scores)]
        os_ref[i * nseq:(i + 1) * nseq, :] = jnp.concatenate(outs, axis=1)

    kt_prev, v_prev, u_hist = kt_scr[...], v_scr[...], u_scr[...]
    blk = project(0)
    done = None
    for j in range(nb):
        kt_pair, v_pair = kv_operands(kt_prev, v_prev, blk)
        sc0 = wave_scores(j, blk, kt_pair, (0, 1))
        nxt = project(j + 1) if j + 1 < nb else None
        msc = mem_scores(j)
        sc1 = wave_scores(j, blk, kt_pair, (2, 3))
        pooled = pool(j, u_hist, blk["u"])
        at0 = wave_values(v_pair, (0, 1), sc0)
        if done is not None:
            output(*done)
        mem_values(j, msc)
        at1 = wave_values(v_pair, (2, 3), sc1)
        done = (j, blk, at0 + at1 + pooled)
        kt_prev, v_prev, u_hist = blk["kt"], blk["v"], blk["u"][BLOCK - POOL_PAD:]
        blk = nxt
    output(*done)
    kt_scr[...] = kt_prev
    v_scr[...] = v_prev
    u_scr[...] = u_hist

    for src, dst in ((wco_ref, wco_bf_ref), (wup_ref, wup_bf_ref), (wdown_ref, wdown_bf_ref)):
        dst[...] = src[...].astype(_BF)


def _prompt_mixer(x, gmix, win, sinks, wpool, pscale, wout, tail_weights, qc, mk, mv, tile):
    B, S, D = x.shape
    ns = S // tile
    nsteps = B * ns
    nmem = tile // BLOCK
    nseq = qc.shape[0] // mk.shape[0]
    assert mk.shape[0] == nsteps * nmem
    const = lambda *shape: pl.BlockSpec(shape, lambda b, s: (0,) * len(shape))
    chunk = lambda w: pl.BlockSpec((w.shape[0] // nsteps, w.shape[1]), lambda b, s: (b * ns + s, 0))
    assert all(w.shape[0] % (16 * nsteps) == 0 for w in tail_weights)
    mem_rows = pl.BlockSpec((nmem * nseq, D), lambda b, s: (b * ns + s, 0))
    mem_cache = pl.BlockSpec((nmem, N_MEM * MEM_ROWS, 128), lambda b, s: (b * ns + s, 0, 0))
    return pl.pallas_call(
        functools.partial(_prompt_mixer_kernel, tile=tile, nseq=nseq),
        grid=(B, ns),
        in_specs=[
            pl.BlockSpec((1, tile, D), lambda b, s: (b, s, 0)),
            const(1, D),
            const(D, IN_WIDTH),
            pl.BlockSpec(memory_space=pltpu.SMEM),
            const(len(POOL_WINDOWS), POOL_GROUP_WIDTH, POOL_GROUP_WIDTH),
            const(1, POOL_WIDTH),
            const(D, WOUT_SPLIT),
            const(D, D - WOUT_SPLIT),
        ] + [chunk(w) for w in tail_weights] + [mem_rows, mem_cache, mem_cache],
        out_specs=[
            pl.BlockSpec((1, tile, D), lambda b, s: (b, s, 0)),
            pl.BlockSpec((1, BLOCK, KV_WIDTH), lambda b, s: (b, 0, 0)),
            pl.BlockSpec((1, BLOCK, KV_WIDTH), lambda b, s: (b, 0, 0)),
            pl.BlockSpec((1, POOL_PAD, POOL_WIDTH), lambda b, s: (b, 0, 0)),
        ] + [chunk(w) for w in tail_weights] + [mem_rows],
        out_shape=[
            jax.ShapeDtypeStruct((B, S, D), _F32),
            jax.ShapeDtypeStruct((B, BLOCK, KV_WIDTH), _F32),
            jax.ShapeDtypeStruct((B, BLOCK, KV_WIDTH), _F32),
            jax.ShapeDtypeStruct((B, POOL_PAD, POOL_WIDTH), _F32),
        ] + [jax.ShapeDtypeStruct(w.shape, _BF) for w in tail_weights] + [jax.ShapeDtypeStruct(qc.shape, _F32)],
        scratch_shapes=[
            pltpu.VMEM((KV_WIDTH, BLOCK), _BF),
            pltpu.VMEM((BLOCK, KV_WIDTH), _BF),
            pltpu.VMEM((POOL_PAD, POOL_WIDTH), _F32),
            pltpu.VMEM((2, N_HEADS, BLOCK, 2 * BLOCK), _F32),
            pltpu.VMEM((2, len(POOL_WINDOWS), BLOCK, POOL_GROUP_WIDTH), _F32),
        ],
        compiler_params=pltpu.CompilerParams(
            dimension_semantics=("arbitrary", "arbitrary"), vmem_limit_bytes=VMEM_LIMIT_BYTES),
        name="prompt_mixer",
    )(x, gmix, win, sinks, wpool, pscale, *wout, *tail_weights, qc, mk, mv)


def _mem_kv_kernel(mem_ref, gmem_ref, wck_ref, wcv_ref, k_ref, v_ref, kt_ref, vb_ref, wck_scr, wcv_scr):
    @pl.when(pl.program_id(0) == 0)
    def _():
        wck_scr[...] = wck_ref[...].astype(_BF)
        wcv_scr[...] = wcv_ref[...].astype(_BF)

    hm = _rms(mem_ref[0], gmem_ref[...]).astype(_BF)
    k = _dot(hm, wck_scr[...])
    v = _dot(hm, wcv_scr[...])
    vb_ref[0] = v.astype(_BF)
    for h in range(N_CROSS_HEADS):
        kt_ref[0, h] = k[:, h * CROSS_HEAD_DIM:(h + 1) * CROSS_HEAD_DIM].T.astype(_BF)
        for half in range(CROSS_HEAD_DIM // 128):
            c0 = h * CROSS_HEAD_DIM + half * 128
            rows = pl.ds(half * N_CROSS_HEADS + h, N_MEM, stride=MEM_ROWS)
            k_ref[0, rows, :] = k[:, c0:c0 + 128]
            v_ref[0, rows, :] = v[:, c0:c0 + 128]


def _mem_kv(mem, gmem, wck, wcv):
    B, M, D = mem.shape
    const = lambda *shape: pl.BlockSpec(shape, lambda b: (0,) * len(shape))
    return pl.pallas_call(
        _mem_kv_kernel,
        grid=(B,),
        in_specs=[pl.BlockSpec((1, M, D), lambda b: (b, 0, 0)), const(1, D), const(D, D), const(D, D)],
        out_specs=[
            pl.BlockSpec((1, M * MEM_ROWS, 128), lambda b: (b, 0, 0)),
            pl.BlockSpec((1, M * MEM_ROWS, 128), lambda b: (b, 0, 0)),
            pl.BlockSpec((1, N_CROSS_HEADS, CROSS_HEAD_DIM, M), lambda b: (b, 0, 0, 0)),
            pl.BlockSpec((1, M, D), lambda b: (b, 0, 0)),
        ],
        out_shape=[
            jax.ShapeDtypeStruct((B, M * MEM_ROWS, 128), _F32),
            jax.ShapeDtypeStruct((B, M * MEM_ROWS, 128), _F32),
            jax.ShapeDtypeStruct((B, N_CROSS_HEADS, CROSS_HEAD_DIM, M), _BF),
            jax.ShapeDtypeStruct((B, M, D), _BF),
        ],
        scratch_shapes=[pltpu.VMEM((D, D), _BF), pltpu.VMEM((D, D), _BF)],
        compiler_params=pltpu.CompilerParams(
            dimension_semantics=("arbitrary",), vmem_limit_bytes=VMEM_LIMIT_BYTES),
        name="prompt_mem_kv",
    )(mem, gmem, wck, wcv)


def _mem_cache_unrows(rows):
    nb = rows.shape[0]
    c = rows.reshape(nb, N_MEM, CROSS_HEAD_DIM // 128, N_CROSS_HEADS, 128)
    return c.transpose(0, 1, 3, 2, 4).reshape(nb, N_MEM, N_CROSS_HEADS, CROSS_HEAD_DIM)


def _prompt_tail_kernel(x1_ref, gcross_ref, wcq_ref, kt_ref, vb_ref, wco_ref, gffn_ref, wup_ref, wdown_ref,
                        gfinal_ref, y_ref, *, tile, sub):
    gcross, gffn, gfinal = gcross_ref[...], gffn_ref[...], gfinal_ref[...]

    def stream(r0):
        x1 = x1_ref[0, r0:r0 + sub, :]
        hn = _rms(x1, gcross).astype(_BF)
        yield
        q = (_dot(hn, wcq_ref[...]) * CQ_SCALE).astype(_BF)
        yield
        heads = [slice(h * CROSS_HEAD_DIM, (h + 1) * CROSS_HEAD_DIM) for h in range(N_CROSS_HEADS)]
        scores = [_dot(q[:, c], kt_ref[0, h]) for h, c in enumerate(heads)]
        yield
        outs = [_dot(_softmax(sc).astype(_BF), vb_ref[0, :, c]).astype(_BF) for sc, c in zip(scores, heads)]
        yield
        x2 = x1 + _dot(jnp.concatenate(outs, axis=1), wco_ref[...])
        yield

        def store(y):
            y_ref[0, r0:r0 + sub, :] = y

        yield from _ffn_final_stages(x2, gffn, wup_ref, wdown_ref, gfinal, store)

    _interleave([stream(r0) for r0 in range(0, tile, sub)], TAIL_SKEW)


def _single(shape, index_map):
    return pl.BlockSpec(shape, index_map, pipeline_mode=pl.Buffered(1))


def _prompt_tail(x1, gcross, wcq, kt, vb, wco, gffn, wup, wdown, gfinal, tile, sub):
    B, S, D = x1.shape
    const = lambda *shape: _single(shape, lambda b, s: (0,) * len(shape))
    return pl.pallas_call(
        functools.partial(_prompt_tail_kernel, tile=tile, sub=sub),
        grid=(B, S // tile),
        in_specs=[
            pl.BlockSpec((1, tile, D), lambda b, s: (b, s, 0)),
            const(1, D),
            const(D, D),
            pl.BlockSpec((1, N_CROSS_HEADS, CROSS_HEAD_DIM, N_MEM), lambda b, s: (b, 0, 0, 0)),
            pl.BlockSpec((1, N_MEM, D), lambda b, s: (b, 0, 0)),
            const(D, D),
            const(1, D),
            const(D, D_FF),
            const(D_FF, D),
            const(1, D),
        ],
        out_specs=pl.BlockSpec((1, tile, D), lambda b, s: (b, s, 0)),
        out_shape=jax.ShapeDtypeStruct((B, S, D), _F32),
        compiler_params=pltpu.CompilerParams(
            dimension_semantics=("arbitrary", "arbitrary"), vmem_limit_bytes=VMEM_LIMIT_BYTES),
        name="prompt_tail",
    )(x1, gcross, wcq, kt, vb, wco, gffn, wup, wdown, gfinal)


def _sample_mixer_kernel(x_ref, gmix_ref, win_ref, sinkcol_ref, ck_ref, cv_ref, st_ref, wpool_ref, pscale_ref,
                         wouta_ref, woutb_ref, gcross_ref, wcq_ref,
                         x1_ref, qc_ref, wk_ref, wv_ref, pool_ref, wcq_bf_ref,
                         kc_scr, vc_scr, u_scr, *, nbatch, sub, nseq, past_len):
    @pl.when(pl.program_id(0) == 0)
    def _():
        wcq_bf_ref[...] = wcq_ref[...].astype(_BF)

    rows = sub * nseq
    nkeys = 2 * WINDOW
    gmix, gcross = gmix_ref[...], gcross_ref[...]
    sinkcol = sinkcol_ref[...][None]

    t = lax.broadcasted_iota(jnp.int32, (nseq, nkeys), 0)
    c = lax.broadcasted_iota(jnp.int32, (nseq, nkeys), 1)
    dist = jnp.where(c < WINDOW, WINDOW + t - c, t - (c - WINDOW))
    valid = (dist >= 0) & (dist <= WINDOW) & (c < WINDOW + nseq)
    distf = dist.astype(_F32)
    bias = jnp.concatenate([jnp.where(valid, -_slope(h) * distf, NEG_INF) for h in range(N_HEADS)], axis=0)[None]
    lane = lax.broadcasted_iota(jnp.int32, (rows, 2 * HEAD_DIM), 1)
    lo = lane < HEAD_DIM
    tpos = past_len + lax.broadcasted_iota(jnp.int32, (nseq, POOL_GROUP_WIDTH), 0)
    zpad = jnp.zeros((sub, nkeys - WINDOW, KV_WIDTH), _BF)

    def stream(b0):
        bs = slice(b0, b0 + sub)
        rs = slice(b0 * nseq, (b0 + sub) * nseq)
        x = x_ref[rs, :]
        proj = _dot(_rms(x, gmix).astype(_BF), win_ref[...])
        yield
        k_new = proj[:, K_OFF:K_OFF + KV_WIDTH].reshape(sub, nseq, KV_WIDTH)
        v_new = proj[:, V_OFF:V_OFF + KV_WIDTH].reshape(sub, nseq, KV_WIDTH)
        ck = jnp.swapaxes(ck_ref[bs], 1, 2)
        cv = jnp.swapaxes(cv_ref[bs], 1, 2)
        wk_ref[bs, :WINDOW - nseq, :] = ck[:, nseq:, :]
        wk_ref[bs, WINDOW - nseq:, :] = k_new
        wv_ref[bs, :WINDOW - nseq, :] = cv[:, nseq:, :]
        wv_ref[bs, WINDOW - nseq:, :] = v_new
        kc_scr[bs, WINDOW:, :] = zpad
        vc_scr[bs, WINDOW:, :] = zpad
        kc_scr[bs, :WINDOW, :] = ck.astype(_BF)
        vc_scr[bs, :WINDOW, :] = cv.astype(_BF)
        kc_scr[bs, WINDOW:WINDOW + nseq, :] = k_new.astype(_BF)
        vc_scr[bs, WINDOW:WINDOW + nseq, :] = v_new.astype(_BF)
        qsc = proj[:, :ATTN_WIDTH] * Q_SCALE
        q_lo, q_hi = [], []
        for p in range(GROUP):
            slab = qsc[:, p * 128:(p + 1) * 128]
            q_lo.append(jnp.where(lo, slab, 0.0).reshape(sub, nseq, 128))
            q_hi.append(jnp.where(lo, 0.0, slab).reshape(sub, nseq, 128))
        qm = jnp.concatenate(q_lo + q_hi, axis=1).astype(_BF)
        sc = jnp.einsum('bqd,bkd->bqk', qm, kc_scr[bs], preferred_element_type=_F32) + bias
        yield
        pr = _softmax_with_sink(sc, sinkcol).astype(_BF)
        o = jnp.einsum('bqk,bkd->bqd', pr, vc_scr[bs], preferred_element_type=_F32)
        yield
        attn = []
        for p in range(GROUP):
            o_lo = o[:, p * nseq:(p + 1) * nseq, :].reshape(rows, 128)
            o_hi = o[:, (p + GROUP) * nseq:(p + GROUP + 1) * nseq, :].reshape(rows, 128)
            attn.append(jnp.where(lo, o_lo, o_hi).astype(_BF))
        ext = POOL_PAD + nseq
        seq_rows = lambda r: pl.ds(b0 * ext + r, sub, stride=ext)
        pooled = []
        for g, w in enumerate(POOL_WINDOWS):
            c0 = g * POOL_GROUP_WIDTH
            u_scr[g, seq_rows(0), :] = jnp.zeros((sub, POOL_GROUP_WIDTH), _F32)
            for r in range(POOL_HIST):
                u_scr[g, seq_rows(POOL_PAD - POOL_HIST + r), :] = st_ref[r, bs, c0:c0 + POOL_GROUP_WIDTH]
            for i in range(sub):
                u_scr[g, (b0 + i) * ext + POOL_PAD:(b0 + i + 1) * ext, :] = (
                    proj[i * nseq:(i + 1) * nseq, U_OFF + c0:U_OFF + c0 + POOL_GROUP_WIDTH])
            for r in range(POOL_HIST):
                pool_ref[r, bs, c0:c0 + POOL_GROUP_WIDTH] = u_scr[g, seq_rows(ext - POOL_HIST + r), :]
            ug = u_scr[g, b0 * ext:(b0 + sub) * ext, :].reshape(sub, ext, POOL_GROUP_WIDTH)
            sw = _window_sums(ug, 1)[w][:, POOL_PAD:, :]
            cnt = jnp.minimum(tpos + 1, w).astype(_F32)
            d = (sw / cnt[None] - ug[:, POOL_PAD:, :]).reshape(rows, POOL_GROUP_WIDTH).astype(_BF)
            y = _dot(d, wpool_ref[g]) * pscale_ref[:, c0:c0 + POOL_GROUP_WIDTH]
            pooled.append(y.astype(_BF))
        yield
        x1 = x + _dot_panels(jnp.concatenate(attn + pooled, axis=1), (wouta_ref, woutb_ref))
        x1_ref[rs, :] = x1
        yield
        qc_ref[rs, :] = _dot(_rms(x1, gcross).astype(_BF), wcq_bf_ref[...]) * CQ_SCALE

    _interleave([stream(b0) for b0 in range(0, nbatch, sub)], SAMPLE_MIXER_SKEW)


def _sample_mixer(x, gmix, win, sinkcol, ck, cv, st, wpool, pscale, wout, gcross, wcq, nbatch, sub, nseq,
                  past_len):
    R, D = x.shape
    rows = nbatch * nseq
    const = lambda *shape: pl.BlockSpec(shape, lambda i: (0,) * len(shape))
    return pl.pallas_call(
        functools.partial(_sample_mixer_kernel, nbatch=nbatch, sub=sub, nseq=nseq, past_len=past_len),
        grid=(R // rows,),
        in_specs=[
            pl.BlockSpec((rows, D), lambda i: (i, 0)),
            const(1, D),
            const(D, IN_WIDTH),
            const(N_HEADS * nseq, 1),
            pl.BlockSpec((nbatch, WINDOW, KV_WIDTH), lambda i: (i, 0, 0)),
            pl.BlockSpec((nbatch, WINDOW, KV_WIDTH), lambda i: (i, 0, 0)),
            pl.BlockSpec((POOL_HIST, nbatch, POOL_WIDTH), lambda i: (0, i, 0)),
            const(len(POOL_WINDOWS), POOL_GROUP_WIDTH, POOL_GROUP_WIDTH),
            const(1, POOL_WIDTH),
            const(D, WOUT_SPLIT),
            const(D, D - WOUT_SPLIT),
            const(1, D),
            const(D, D),
        ],
        out_specs=[
            pl.BlockSpec((rows, D), lambda i: (i, 0)),
            pl.BlockSpec((rows, D), lambda i: (i, 0)),
            pl.BlockSpec((nbatch, WINDOW, KV_WIDTH), lambda i: (i, 0, 0)),
            pl.BlockSpec((nbatch, WINDOW, KV_WIDTH), lambda i: (i, 0, 0)),
            pl.BlockSpec((POOL_HIST, nbatch, POOL_WIDTH), lambda i: (0, i, 0)),
            const(D, D),
        ],
        out_shape=[
            jax.ShapeDtypeStruct((R, D), _F32),
            jax.ShapeDtypeStruct((R, D), _F32),
            jax.ShapeDtypeStruct((R // nseq, WINDOW, KV_WIDTH), _F32),
            jax.ShapeDtypeStruct((R // nseq, WINDOW, KV_WIDTH), _F32),
            jax.ShapeDtypeStruct((POOL_HIST, R // nseq, POOL_WIDTH), _F32),
            jax.ShapeDtypeStruct((D, D), _BF),
        ],
        scratch_shapes=[
            pltpu.VMEM((nbatch, 2 * WINDOW, KV_WIDTH), _BF),
            pltpu.VMEM((nbatch, 2 * WINDOW, KV_WIDTH), _BF),
            pltpu.VMEM((len(POOL_WINDOWS), nbatch * (POOL_PAD + nseq), POOL_GROUP_WIDTH), _F32),
        ],
        compiler_params=pltpu.CompilerParams(
            dimension_semantics=("arbitrary",), vmem_limit_bytes=VMEM_LIMIT_BYTES),
        name="sample_mixer",
    )(x, gmix, win, sinkcol, ck, cv, st, wpool, pscale, *wout, gcross, wcq)


def _mem_cache_rows(cache):
    nb = cache.shape[0]
    c = cache.reshape(nb, N_MEM, N_CROSS_HEADS, CROSS_HEAD_DIM // 128, 128)
    return c.transpose(0, 1, 3, 2, 4).reshape(nb, N_MEM * MEM_ROWS, 128)


def _load_mem_head(ref, b, h):
    halves = [ref[b, pl.ds(half * N_CROSS_HEADS + h, N_MEM, stride=MEM_ROWS), :]
              for half in range(CROSS_HEAD_DIM // 128)]
    return jnp.concatenate(halves, axis=1)


def _sample_tail_kernel(x1_ref, o_ref, wco_ref, gffn_ref, wup_ref, wdown_ref, gfinal_ref, y_ref, *, tile, sub):
    gffn, gfinal = gffn_ref[...], gfinal_ref[...]

    def stream(r0):
        x2 = x1_ref[r0:r0 + sub, :] + _dot(o_ref[r0:r0 + sub, :].astype(_BF), wco_ref[...])
        yield

        def store(y):
            y_ref[r0:r0 + sub, :] = y

        yield from _ffn_final_stages(x2, gffn, wup_ref, wdown_ref, gfinal, store)

    _interleave([stream(r0) for r0 in range(0, tile, sub)], TAIL_SKEW)


def _sample_tail(x1, o, wco, gffn, wup, wdown, gfinal, tile, sub):
    R, D = x1.shape
    const = lambda *shape: _single(shape, lambda i: (0,) * len(shape))
    return pl.pallas_call(
        functools.partial(_sample_tail_kernel, tile=tile, sub=sub),
        grid=(R // tile,),
        in_specs=[
            pl.BlockSpec((tile, D), lambda i: (i, 0)),
            pl.BlockSpec((tile, D), lambda i: (i, 0)),
            const(D, D),
            const(1, D),
            const(D, D_FF),
            const(D_FF, D),
            const(1, D),
        ],
        out_specs=pl.BlockSpec((tile, D), lambda i: (i, 0)),
        out_shape=jax.ShapeDtypeStruct((R, D), _F32),
        compiler_params=pltpu.CompilerParams(
            dimension_semantics=("arbitrary",), vmem_limit_bytes=VMEM_LIMIT_BYTES),
        name="sample_tail",
    )(x1, o, wco, gffn, wup, wdown, gfinal)


PROMPT_TILE = 512
TAIL_TILE = 1024
TAIL_SUB = 256
TAIL_SKEW = 4
SAMPLE_MIXER_BATCH = 32
SAMPLE_MIXER_SUB = 16
SAMPLE_MIXER_SKEW = 3


def kernel(x_prompt, x_sample, cache_win_k, cache_win_v, state_pool, cache_mem_k, cache_mem_v, mem_prompt,
           g_mix, w_in, attn_sinks, w_pool, pool_scale, w_out, g_cross, g_mem, w_cq, w_ck, w_cv, w_co,
           g_ffn, w_up, w_down, g_final):
    depth = g_mix.shape[0]
    assert depth == 1, "one layer per step"
    B, S, D = x_prompt.shape
    DB, T, _ = x_sample.shape
    past_len = PAST_LEN
    l = 0

    win = jnp.concatenate([_pair_heads(w_in[l][:, :ATTN_WIDTH], 1), w_in[l][:, ATTN_WIDTH:]], axis=1).astype(_BF)
    wout = jnp.concatenate([_pair_heads(w_out[l][:ATTN_WIDTH, :], 0), w_out[l][ATTN_WIDTH:, :]], axis=0)
    wout = (wout[:, :WOUT_SPLIT].astype(_BF), wout[:, WOUT_SPLIT:].astype(_BF))
    wpool = w_pool[l].astype(_BF)
    gmix, gcross, gmem, gffn = (g[l].reshape(1, D) for g in (g_mix, g_cross, g_mem, g_ffn))
    gfinal = g_final.reshape(1, D)
    pscale = pool_scale[l].reshape(1, POOL_WIDTH)
    sinks = attn_sinks[l]

    xs = x_sample.reshape(DB * T, D)
    ck = cache_win_k[l].transpose(0, 2, 3, 1).reshape(DB, KV_WIDTH, WINDOW)
    cv = cache_win_v[l].transpose(0, 2, 3, 1).reshape(DB, KV_WIDTH, WINDOW)
    st = state_pool[l].transpose(1, 0, 2)
    sinkcol = jnp.repeat(sinks, T).reshape(N_HEADS * T, 1)
    x1s, qc, wk_s, wv_s, pool_s, wcq = _sample_mixer(xs, gmix, win, sinkcol, ck, cv, st, wpool, pscale, wout,
                                                gcross, w_cq[l], SAMPLE_MIXER_BATCH, SAMPLE_MIXER_SUB, T, past_len)

    mk = _mem_cache_rows(cache_mem_k[l])
    mv = _mem_cache_rows(cache_mem_v[l])
    x1p, klast, vlast, plast, wco, wup, wdown, o_s = _prompt_mixer(
        x_prompt, gmix, win, sinks, wpool, pscale, wout, (w_co[l], w_up[l], w_down[l]), qc, mk, mv, PROMPT_TILE)
    mem_k, mem_v, mem_kt, mem_vb = _mem_kv(mem_prompt, gmem, w_ck[l], w_cv[l])
    y_prompt = _prompt_tail(x1p, gcross, wcq, mem_kt, mem_vb, wco, gffn, wup, wdown, gfinal, TAIL_TILE, TAIL_SUB)

    y_sample = _sample_tail(x1s, o_s, wco, gffn, wup, wdown, gfinal, TAIL_TILE, TAIL_SUB).reshape(DB, T, D)

    return (
        y_prompt,
        y_sample,
        klast.reshape(B, N_KV_HEADS, HEAD_DIM, WINDOW).transpose(0, 3, 1, 2)[None],
        vlast.reshape(B, N_KV_HEADS, HEAD_DIM, WINDOW).transpose(0, 3, 1, 2)[None],
        plast[:, POOL_PAD - POOL_HIST:, :][None],
        _mem_cache_unrows(mem_k)[None],
        _mem_cache_unrows(mem_v)[None],
        wk_s.reshape(1, DB, WINDOW, N_KV_HEADS, HEAD_DIM),
        wv_s.reshape(1, DB, WINDOW, N_KV_HEADS, HEAD_DIM),
        pool_s.transpose(1, 0, 2)[None],
    )
```

```python
import functools

import jax
import jax.numpy as jnp
from jax import lax
from jax.experimental import pallas as pl
from jax.experimental.pallas import tpu as pltpu

D_MODEL = 1024
PAST_LEN = 16384
HEAD_DIM = 64
N_HEADS = 8
N_KV_HEADS = 2
GROUP = N_HEADS // N_KV_HEADS
ATTN_WIDTH = N_HEADS * HEAD_DIM
KV_WIDTH = N_KV_HEADS * HEAD_DIM
WINDOW = 128
BLOCK = WINDOW
POOL_WIDTH = D_MODEL - ATTN_WIDTH
POOL_WINDOWS = (2, 4, 8, 16)
POOL_GROUP_WIDTH = 128
POOL_HIST = 15
POOL_PAD = 16
IN_WIDTH = ATTN_WIDTH + 2 * KV_WIDTH + POOL_WIDTH
N_MEM = 256
N_CROSS_HEADS = 4
CROSS_HEAD_DIM = 256
MEM_ROWS = N_CROSS_HEADS * (CROSS_HEAD_DIM // 128)
D_FF = 4 * D_MODEL
FF_CHUNK = 1024
RMS_EPS = 1e-5
NEG_INF = -1e30
Q_SCALE = HEAD_DIM ** -0.5
CQ_SCALE = CROSS_HEAD_DIM ** -0.5
K_OFF = ATTN_WIDTH
V_OFF = ATTN_WIDTH + KV_WIDTH
U_OFF = ATTN_WIDTH + 2 * KV_WIDTH


def _pair_heads(w, axis):
    shape = w.shape
    split = shape[:axis] + (N_KV_HEADS, GROUP, HEAD_DIM) + shape[axis + 1:]
    return jnp.swapaxes(w.reshape(split), axis, axis + 1).reshape(shape)


WOUT_SPLIT = 768

VMEM_LIMIT_BYTES = 56 * 1024 * 1024

_BF = jnp.bfloat16
_F32 = jnp.float32


def _slope(h):
    return 2.0 ** (-8.0 * (h + 1) / N_HEADS)


def _dot(a, b):
    return jnp.dot(a, b, preferred_element_type=_F32)


def _dot_panels(a, panel_refs):
    return jnp.concatenate([_dot(a, ref[...]) for ref in panel_refs], axis=1)


def _rms(x, g):
    ms = jnp.mean(x * x, axis=-1, keepdims=True)
    return x * lax.rsqrt(ms + RMS_EPS) * g


def _softmax_with_sink(s, sink):
    m = jnp.maximum(jnp.max(s, axis=-1, keepdims=True), sink)
    e = jnp.exp(s - m)
    den = jnp.sum(e, axis=-1, keepdims=True) + jnp.exp(sink - m)
    return e * (1.0 / den)


def _softmax_with_sink_parts(s, sink):
    m = jnp.maximum(jnp.max(s, axis=-1, keepdims=True), sink)
    e = jnp.exp(s - m)
    den = jnp.sum(e, axis=-1, keepdims=True) + jnp.exp(sink - m)
    return e, 1.0 / den


def _softmax(s):
    m = jnp.max(s, axis=-1, keepdims=True)
    e = jnp.exp(s - m)
    return e * (1.0 / jnp.sum(e, axis=-1, keepdims=True))


def _window_sums(u_ext, axis):
    out = {}
    s = u_ext
    w = 1
    while w < max(POOL_WINDOWS):
        s = s + pltpu.roll(s, w, axis)
        w *= 2
        out[w] = s
    return out


def _interleave(streams, skew):
    pending = list(streams)
    live = []
    rnd = 0
    while live or pending:
        if pending and rnd % skew == 0:
            live.append(pending.pop(0))
        for g in list(live):
            try:
                next(g)
            except StopIteration:
                live.remove(g)
        rnd += 1


def _ffn_final_stages(x2, gffn, wup_ref, wdown_ref, gfinal, store):
    hn = _rms(x2, gffn).astype(_BF)
    yield
    acc = x2
    for c in range(D_FF // FF_CHUNK):
        hc = _dot(hn, wup_ref[:, c * FF_CHUNK:(c + 1) * FF_CHUNK])
        yield
        hc = jnp.maximum(hc, 0.0)
        hc = (hc * hc).astype(_BF)
        acc = acc + _dot(hc, wdown_ref[c * FF_CHUNK:(c + 1) * FF_CHUNK, :])
        yield
    store(_rms(acc, gfinal))


def _prep_wout_kernel(wout_ref, wpool_ref, pscale_ref, outa_ref, outb_ref):
    blocks = []
    for g in range(GROUP):
        for kv in range(N_KV_HEADS):
            r0 = (kv * GROUP + g) * HEAD_DIM
            blocks.append(wout_ref[r0:r0 + HEAD_DIM, :])
    for g in range(len(POOL_WINDOWS)):
        r0 = ATTN_WIDTH + g * POOL_GROUP_WIDTH
        scaled = wpool_ref[g] * pscale_ref[:, g * POOL_GROUP_WIDTH:(g + 1) * POOL_GROUP_WIDTH]
        blocks.append(jnp.dot(scaled, wout_ref[r0:r0 + POOL_GROUP_WIDTH, :], precision=lax.Precision.HIGHEST,
                              preferred_element_type=_F32))
    full = jnp.concatenate(blocks, axis=0).astype(_BF)
    outa_ref[...] = full[:, :WOUT_SPLIT]
    outb_ref[...] = full[:, WOUT_SPLIT:]


def _prep_wout(wout, wpool, pscale):
    D = wout.shape[0]
    return pl.pallas_call(
        _prep_wout_kernel,
        out_shape=[jax.ShapeDtypeStruct((D, WOUT_SPLIT), _BF), jax.ShapeDtypeStruct((D, D - WOUT_SPLIT), _BF)],
        compiler_params=pltpu.CompilerParams(vmem_limit_bytes=VMEM_LIMIT_BYTES),
        name="prep_wout",
    )(wout, wpool, pscale)


def _prompt_mixer_kernel(x_ref, gmix_ref, win_ref, sink_ref, wouta_ref, woutb_ref,
                         wco_ref, wup_ref, wdown_ref, qc_ref, mk_ref, mv_ref,
                         x1_ref, klast_ref, vlast_ref, plast_ref, wco_bf_ref, wup_bf_ref, wdown_bf_ref,
                         os_ref, kt_scr, v_scr, u_scr, bias_scr, inv_scr, *, tile, nseq):
    b = pl.program_id(0)
    s = pl.program_id(1)
    nb = tile // BLOCK

    @pl.when((b == 0) & (s == 0))
    def _():
        qi = lax.broadcasted_iota(jnp.int32, (BLOCK, 2 * BLOCK), 0)
        kc = lax.broadcasted_iota(jnp.int32, (BLOCK, 2 * BLOCK), 1)
        dist = qi + BLOCK - kc
        valid = (dist >= 0) & (dist <= WINDOW)
        valid_first = valid & (kc >= BLOCK)
        distf = dist.astype(_F32)
        for h in range(N_HEADS):
            ali = -_slope(h) * distf
            bias_scr[0, h] = jnp.where(valid, ali, NEG_INF)
            bias_scr[1, h] = jnp.where(valid_first, ali, NEG_INF)
        prow = lax.broadcasted_iota(jnp.int32, (BLOCK, POOL_GROUP_WIDTH), 0)
        for g, w in enumerate(POOL_WINDOWS):
            inv_scr[0, g] = jnp.full((BLOCK, POOL_GROUP_WIDTH), 1.0 / w, _F32)
            inv_scr[1, g] = 1.0 / jnp.minimum(prow + 1, w).astype(_F32)

    @pl.when(s == 0)
    def _():
        kt_scr[...] = jnp.zeros((KV_WIDTH, BLOCK), _BF)
        v_scr[...] = jnp.zeros((BLOCK, KV_WIDTH), _BF)
        u_scr[...] = jnp.zeros((POOL_PAD, POOL_WIDTH), _F32)

    lane = lax.broadcasted_iota(jnp.int32, (BLOCK, 2 * HEAD_DIM), 1)
    lo = lane < HEAD_DIM
    gmix = gmix_ref[...]
    zeros_kt = jnp.zeros((HEAD_DIM, 2 * BLOCK), _BF)

    def project(j):
        x = x_ref[0, j * BLOCK:(j + 1) * BLOCK, :]
        proj = _dot(_rms(x, gmix).astype(_BF), win_ref[...])
        k = proj[:, K_OFF:K_OFF + KV_WIDTH]
        v = proj[:, V_OFF:V_OFF + KV_WIDTH]
        u = proj[:, U_OFF:]
        kt = k.T
        if j == nb - 1:
            klast_ref[0] = kt
            vlast_ref[0] = v.T
            plast_ref[0] = u[BLOCK - POOL_PAD:]
        return dict(x=x, q=(proj[:, :ATTN_WIDTH] * Q_SCALE).astype(_BF), kt=kt.astype(_BF), v=v.astype(_BF), u=u)

    def pool(j, u_hist, u):
        first = ((s == 0) & (j == 0)).astype(jnp.int32) if j == 0 else 0
        ys = []
        for g, w in enumerate(POOL_WINDOWS):
            c0 = g * POOL_GROUP_WIDTH
            u_ext = jnp.concatenate([u_hist[:, c0:c0 + POOL_GROUP_WIDTH], u[:, c0:c0 + POOL_GROUP_WIDTH]], axis=0)
            sw = _window_sums(u_ext, 0)[w][POOL_PAD:]
            ys.append((sw * inv_scr[first, g] - u_ext[POOL_PAD:]).astype(_BF))
        return ys

    def kv_operands(kt_prev, v_prev, blk):
        kt2 = jnp.concatenate([kt_prev, blk["kt"]], axis=1)
        v2 = jnp.concatenate([v_prev, blk["v"]], axis=0)
        kt_pair = jnp.concatenate([jnp.concatenate([kt2[:HEAD_DIM], zeros_kt], axis=0),
                                   jnp.concatenate([zeros_kt, kt2[HEAD_DIM:]], axis=0)], axis=1)
        lane2 = lax.broadcasted_iota(jnp.int32, v2.shape, 1)
        zero = jnp.zeros_like(v2)
        v_pair = jnp.concatenate([jnp.where(lane2 < HEAD_DIM, v2, zero), jnp.where(lane2 < HEAD_DIM, zero, v2)],
                                 axis=0)
        return kt_pair, v_pair

    def wave_scores(j, blk, kt_pair, pairs):
        first = ((s == 0) & (j == 0)).astype(jnp.int32) if j == 0 else 0
        scores = []
        for p in pairs:
            sc = _dot(blk["q"][:, p * 128:(p + 1) * 128], kt_pair)
            scores.append(sc[:, :2 * BLOCK] + bias_scr[first, p])
            scores.append(sc[:, 2 * BLOCK:] + bias_scr[first, p + GROUP])
        return scores

    def wave_values(v_pair, pairs, scores):
        slabs = []
        for i, p in enumerate(pairs):
            es, invs = [], []
            for half, h in ((0, p), (1, p + GROUP)):
                e, inv = _softmax_with_sink_parts(scores[2 * i + half], sink_ref[h])
                es.append(e.astype(_BF))
                invs.append(inv)
            o = _dot(jnp.concatenate(es, axis=1), v_pair)
            slabs.append((o * jnp.where(lo, invs[0], invs[1])).astype(_BF))
        return slabs

    def output(j, blk, slabs):
        cat = jnp.concatenate(slabs, axis=1)
        x1_ref[0, j * BLOCK:(j + 1) * BLOCK, :] = blk["x"] + _dot_panels(cat, (wouta_ref, woutb_ref))

    heads = [slice(h * CROSS_HEAD_DIM, (h + 1) * CROSS_HEAD_DIM) for h in range(N_CROSS_HEADS)]

    def mem_scores(i):
        q = qc_ref[i * nseq:(i + 1) * nseq, :].astype(_BF)
        return [lax.dot_general(q[:, c], _load_mem_head(mk_ref, i, h).astype(_BF), (((1,), (1,)), ((), ())),
                                preferred_element_type=_F32) for h, c in enumerate(heads)]

    def mem_values(i, scores):
        outs = [_dot(_softmax(sc).astype(_BF), _load_mem_head(mv_ref, i, h).astype(_BF))
                for h, sc in enumerate(scores)]
        os_ref[i * nseq:(i + 1) * nseq, :] = jnp.concatenate(outs, axis=1)

    kt_prev, v_prev, u_hist = kt_scr[...], v_scr[...], u_scr[...]
    blk = project(0)
    done = None
    for j in range(nb):
        kt_pair, v_pair = kv_operands(kt_prev, v_prev, blk)
        sc0 = wave_scores(j, blk, kt_pair, (0, 1))
        nxt = project(j + 1) if j + 1 < nb else None
        msc = mem_scores(j)
        sc1 = wave_scores(j, blk, kt_pair, (2, 3))
        pooled = pool(j, u_hist, blk["u"])
        at0 = wave_values(v_pair, (0, 1), sc0)
        if done is not None:
            output(*done)
        mem_values(j, msc)
        at1 = wave_values(v_pair, (2, 3), sc1)
        done = (j, blk, at0 + at1 + pooled)
        kt_prev, v_prev, u_hist = blk["kt"], blk["v"], blk["u"][BLOCK - POOL_PAD:]
        blk = nxt
    output(*done)
    kt_scr[...] = kt_prev
    v_scr[...] = v_prev
    u_scr[...] = u_hist

    for src, dst in ((wco_ref, wco_bf_ref), (wup_ref, wup_bf_ref), (wdown_ref, wdown_bf_ref)):
        dst[...] = src[...].astype(_BF)


def _prompt_mixer(x, gmix, win, sinks, wout, tail_weights, qc, mk, mv, tile):
    B, S, D = x.shape
    ns = S // tile
    nsteps = B * ns
    nmem = tile // BLOCK
    nseq = qc.shape[0] // mk.shape[0]
    assert mk.shape[0] == nsteps * nmem
    const = lambda *shape: pl.BlockSpec(shape, lambda b, s: (0,) * len(shape))
    chunk = lambda w: pl.BlockSpec((w.shape[0] // nsteps, w.shape[1]), lambda b, s: (b * ns + s, 0))
    assert all(w.shape[0] % (16 * nsteps) == 0 for w in tail_weights)
    mem_rows = pl.BlockSpec((nmem * nseq, D), lambda b, s: (b * ns + s, 0))
    mem_cache = pl.BlockSpec((nmem, N_MEM * MEM_ROWS, 128), lambda b, s: (b * ns + s, 0, 0))
    return pl.pallas_call(
        functools.partial(_prompt_mixer_kernel, tile=tile, nseq=nseq),
        grid=(B, ns),
        in_specs=[
            pl.BlockSpec((1, tile, D), lambda b, s: (b, s, 0)),
            const(1, D),
            const(D, IN_WIDTH),
            pl.BlockSpec(memory_space=pltpu.SMEM),
            const(D, WOUT_SPLIT),
            const(D, D - WOUT_SPLIT),
        ] + [chunk(w) for w in tail_weights] + [mem_rows, mem_cache, mem_cache],
        out_specs=[
            pl.BlockSpec((1, tile, D), lambda b, s: (b, s, 0)),
            pl.BlockSpec((1, BLOCK, KV_WIDTH), lambda b, s: (b, 0, 0)),
            pl.BlockSpec((1, BLOCK, KV_WIDTH), lambda b, s: (b, 0, 0)),
            pl.BlockSpec((1, POOL_PAD, POOL_WIDTH), lambda b, s: (b, 0, 0)),
        ] + [chunk(w) for w in tail_weights] + [mem_rows],
        out_shape=[
            jax.ShapeDtypeStruct((B, S, D), _F32),
            jax.ShapeDtypeStruct((B, BLOCK, KV_WIDTH), _F32),
            jax.ShapeDtypeStruct((B, BLOCK, KV_WIDTH), _F32),
            jax.ShapeDtypeStruct((B, POOL_PAD, POOL_WIDTH), _F32),
        ] + [jax.ShapeDtypeStruct(w.shape, _BF) for w in tail_weights] + [jax.ShapeDtypeStruct(qc.shape, _F32)],
        scratch_shapes=[
            pltpu.VMEM((KV_WIDTH, BLOCK), _BF),
            pltpu.VMEM((BLOCK, KV_WIDTH), _BF),
            pltpu.VMEM((POOL_PAD, POOL_WIDTH), _F32),
            pltpu.VMEM((2, N_HEADS, BLOCK, 2 * BLOCK), _F32),
            pltpu.VMEM((2, len(POOL_WINDOWS), BLOCK, POOL_GROUP_WIDTH), _F32),
        ],
        compiler_params=pltpu.CompilerParams(
            dimension_semantics=("arbitrary", "arbitrary"), vmem_limit_bytes=VMEM_LIMIT_BYTES),
        name="prompt_mixer",
    )(x, gmix, win, sinks, *wout, *tail_weights, qc, mk, mv)


def _mem_kv_kernel(mem_ref, gmem_ref, wck_ref, wcv_ref, k_ref, v_ref, kt_ref, vb_ref, wck_scr, wcv_scr):
    @pl.when(pl.program_id(0) == 0)
    def _():
        wck_scr[...] = wck_ref[...].astype(_BF)
        wcv_scr[...] = wcv_ref[...].astype(_BF)

    hm = _rms(mem_ref[0], gmem_ref[...]).astype(_BF)
    k = _dot(hm, wck_scr[...])
    v = _dot(hm, wcv_scr[...])
    vb_ref[0] = v.astype(_BF)
    for h in range(N_CROSS_HEADS):
        kt_ref[0, h] = k[:, h * CROSS_HEAD_DIM:(h + 1) * CROSS_HEAD_DIM].T.astype(_BF)
        for half in range(CROSS_HEAD_DIM // 128):
            c0 = h * CROSS_HEAD_DIM + half * 128
            rows = pl.ds(half * N_CROSS_HEADS + h, N_MEM, stride=MEM_ROWS)
            k_ref[0, rows, :] = k[:, c0:c0 + 128]
            v_ref[0, rows, :] = v[:, c0:c0 + 128]


def _mem_kv(mem, gmem, wck, wcv):
    B, M, D = mem.shape
    const = lambda *shape: pl.BlockSpec(shape, lambda b: (0,) * len(shape))
    return pl.pallas_call(
        _mem_kv_kernel,
        grid=(B,),
        in_specs=[pl.BlockSpec((1, M, D), lambda b: (b, 0, 0)), const(1, D), const(D, D), const(D, D)],
        out_specs=[
            pl.BlockSpec((1, M * MEM_ROWS, 128), lambda b: (b, 0, 0)),
            pl.BlockSpec((1, M * MEM_ROWS, 128), lambda b: (b, 0, 0)),
            pl.BlockSpec((1, N_CROSS_HEADS, CROSS_HEAD_DIM, M), lambda b: (b, 0, 0, 0)),
            pl.BlockSpec((1, M, D), lambda b: (b, 0, 0)),
        ],
        out_shape=[
            jax.ShapeDtypeStruct((B, M * MEM_ROWS, 128), _F32),
            jax.ShapeDtypeStruct((B, M * MEM_ROWS, 128), _F32),
            jax.ShapeDtypeStruct((B, N_CROSS_HEADS, CROSS_HEAD_DIM, M), _BF),
            jax.ShapeDtypeStruct((B, M, D), _BF),
        ],
        scratch_shapes=[pltpu.VMEM((D, D), _BF), pltpu.VMEM((D, D), _BF)],
        compiler_params=pltpu.CompilerParams(
            dimension_semantics=("arbitrary",), vmem_limit_bytes=VMEM_LIMIT_BYTES),
        name="prompt_mem_kv",
    )(mem, gmem, wck, wcv)


def _mem_cache_unrows(rows):
    nb = rows.shape[0]
    c = rows.reshape(nb, N_MEM, CROSS_HEAD_DIM // 128, N_CROSS_HEADS, 128)
    return c.transpose(0, 1, 3, 2, 4).reshape(nb, N_MEM, N_CROSS_HEADS, CROSS_HEAD_DIM)


def _prompt_tail_kernel(x1_ref, gcross_ref, wcq_ref, kt_ref, vb_ref, wco_ref, gffn_ref, wup_ref, wdown_ref,
                        gfinal_ref, y_ref, *, tile, sub):
    gcross, gffn, gfinal = gcross_ref[...], gffn_ref[...], gfinal_ref[...]

    def stream(r0):
        x1 = x1_ref[0, r0:r0 + sub, :]
        hn = _rms(x1, gcross).astype(_BF)
        yield
        q = (_dot(hn, wcq_ref[...]) * CQ_SCALE).astype(_BF)
        yield
        heads = [slice(h * CROSS_HEAD_DIM, (h + 1) * CROSS_HEAD_DIM) for h in range(N_CROSS_HEADS)]
        scores = [_dot(q[:, c], kt_ref[0, h]) for h, c in enumerate(heads)]
        yield
        outs = [_dot(_softmax(sc).astype(_BF), vb_ref[0, :, c]).astype(_BF) for sc, c in zip(scores, heads)]
        yield
        x2 = x1 + _dot(jnp.concatenate(outs, axis=1), wco_ref[...])
        yield

        def store(y):
            y_ref[0, r0:r0 + sub, :] = y

        yield from _ffn_final_stages(x2, gffn, wup_ref, wdown_ref, gfinal, store)

    _interleave([stream(r0) for r0 in range(0, tile, sub)], TAIL_SKEW)


def _single(shape, index_map):
    return pl.BlockSpec(shape, index_map, pipeline_mode=pl.Buffered(1))


def _prompt_tail(x1, gcross, wcq, kt, vb, wco, gffn, wup, wdown, gfinal, tile, sub):
    B, S, D = x1.shape
    const = lambda *shape: _single(shape, lambda b, s: (0,) * len(shape))
    return pl.pallas_call(
        functools.partial(_prompt_tail_kernel, tile=tile, sub=sub),
        grid=(B, S // tile),
        in_specs=[
            pl.BlockSpec((1, tile, D), lambda b, s: (b, s, 0)),
            const(1, D),
            const(D, D),
            pl.BlockSpec((1, N_CROSS_HEADS, CROSS_HEAD_DIM, N_MEM), lambda b, s: (b, 0, 0, 0)),
            pl.BlockSpec((1, N_MEM, D), lambda b, s: (b, 0, 0)),
            const(D, D),
            const(1, D),
            const(D, D_FF),
            const(D_FF, D),
            const(1, D),
        ],
        out_specs=pl.BlockSpec((1, tile, D), lambda b, s: (b, s, 0)),
        out_shape=jax.ShapeDtypeStruct((B, S, D), _F32),
        compiler_params=pltpu.CompilerParams(
            dimension_semantics=("arbitrary", "arbitrary"), vmem_limit_bytes=VMEM_LIMIT_BYTES),
        name="prompt_tail",
    )(x1, gcross, wcq, kt, vb, wco, gffn, wup, wdown, gfinal)


def _sample_mixer_kernel(x_ref, gmix_ref, win_ref, sinkcol_ref, ck_ref, cv_ref, st_ref,
                         wouta_ref, woutb_ref, gcross_ref, wcq_ref,
                         x1_ref, qc_ref, wk_ref, wv_ref, pool_ref, wcq_bf_ref,
                         kc_scr, vc_scr, u_scr, *, nbatch, sub, nseq, past_len):
    @pl.when(pl.program_id(0) == 0)
    def _():
        wcq_bf_ref[...] = wcq_ref[...].astype(_BF)

    rows = sub * nseq
    nkeys = 2 * WINDOW
    gmix, gcross = gmix_ref[...], gcross_ref[...]
    sinkcol = sinkcol_ref[...][None]

    t = lax.broadcasted_iota(jnp.int32, (nseq, nkeys), 0)
    c = lax.broadcasted_iota(jnp.int32, (nseq, nkeys), 1)
    dist = jnp.where(c < WINDOW, WINDOW + t - c, t - (c - WINDOW))
    valid = (dist >= 0) & (dist <= WINDOW) & (c < WINDOW + nseq)
    distf = dist.astype(_F32)
    bias = jnp.concatenate([jnp.where(valid, -_slope(h) * distf, NEG_INF) for h in range(N_HEADS)], axis=0)[None]
    lane = lax.broadcasted_iota(jnp.int32, (rows, 2 * HEAD_DIM), 1)
    lo = lane < HEAD_DIM
    tpos = past_len + lax.broadcasted_iota(jnp.int32, (nseq, POOL_GROUP_WIDTH), 0)
    zpad = jnp.zeros((sub, nkeys - WINDOW, KV_WIDTH), _BF)

    def stream(b0):
        bs = slice(b0, b0 + sub)
        rs = slice(b0 * nseq, (b0 + sub) * nseq)
        x = x_ref[rs, :]
        proj = _dot(_rms(x, gmix).astype(_BF), win_ref[...])
        yield
        k_new = proj[:, K_OFF:K_OFF + KV_WIDTH].reshape(sub, nseq, KV_WIDTH)
        v_new = proj[:, V_OFF:V_OFF + KV_WIDTH].reshape(sub, nseq, KV_WIDTH)
        ck = jnp.swapaxes(ck_ref[bs], 1, 2)
        cv = jnp.swapaxes(cv_ref[bs], 1, 2)
        wk_ref[bs, :WINDOW - nseq, :] = ck[:, nseq:, :]
        wk_ref[bs, WINDOW - nseq:, :] = k_new
        wv_ref[bs, :WINDOW - nseq, :] = cv[:, nseq:, :]
        wv_ref[bs, WINDOW - nseq:, :] = v_new
        kc_scr[bs, WINDOW:, :] = zpad
        vc_scr[bs, WINDOW:, :] = zpad
        kc_scr[bs, :WINDOW, :] = ck.astype(_BF)
        vc_scr[bs, :WINDOW, :] = cv.astype(_BF)
        kc_scr[bs, WINDOW:WINDOW + nseq, :] = k_new.astype(_BF)
        vc_scr[bs, WINDOW:WINDOW + nseq, :] = v_new.astype(_BF)
        qsc = proj[:, :ATTN_WIDTH] * Q_SCALE
        q_lo, q_hi = [], []
        for p in range(GROUP):
            slab = qsc[:, p * 128:(p + 1) * 128]
            q_lo.append(jnp.where(lo, slab, 0.0).reshape(sub, nseq, 128))
            q_hi.append(jnp.where(lo, 0.0, slab).reshape(sub, nseq, 128))
        qm = jnp.concatenate(q_lo + q_hi, axis=1).astype(_BF)
        sc = jnp.einsum('bqd,bkd->bqk', qm, kc_scr[bs], preferred_element_type=_F32) + bias
        yield
        pr = _softmax_with_sink(sc, sinkcol).astype(_BF)
        o = jnp.einsum('bqk,bkd->bqd', pr, vc_scr[bs], preferred_element_type=_F32)
        yield
        attn = []
        for p in range(GROUP):
            o_lo = o[:, p * nseq:(p + 1) * nseq, :].reshape(rows, 128)
            o_hi = o[:, (p + GROUP) * nseq:(p + GROUP + 1) * nseq, :].reshape(rows, 128)
            attn.append(jnp.where(lo, o_lo, o_hi).astype(_BF))
        ext = POOL_PAD + nseq
        seq_rows = lambda r: pl.ds(b0 * ext + r, sub, stride=ext)
        pooled = []
        for g, w in enumerate(POOL_WINDOWS):
            c0 = g * POOL_GROUP_WIDTH
            u_scr[g, seq_rows(0), :] = jnp.zeros((sub, POOL_GROUP_WIDTH), _F32)
            for r in range(POOL_HIST):
                u_scr[g, seq_rows(POOL_PAD - POOL_HIST + r), :] = st_ref[r, bs, c0:c0 + POOL_GROUP_WIDTH]
            for i in range(sub):
                u_scr[g, (b0 + i) * ext + POOL_PAD:(b0 + i + 1) * ext, :] = (
                    proj[i * nseq:(i + 1) * nseq, U_OFF + c0:U_OFF + c0 + POOL_GROUP_WIDTH])
            for r in range(POOL_HIST):
                pool_ref[r, bs, c0:c0 + POOL_GROUP_WIDTH] = u_scr[g, seq_rows(ext - POOL_HIST + r), :]
            ug = u_scr[g, b0 * ext:(b0 + sub) * ext, :].reshape(sub, ext, POOL_GROUP_WIDTH)
            sw = _window_sums(ug, 1)[w][:, POOL_PAD:, :]
            cnt = jnp.minimum(tpos + 1, w).astype(_F32)
            pooled.append((sw / cnt[None] - ug[:, POOL_PAD:, :]).reshape(rows, POOL_GROUP_WIDTH).astype(_BF))
        yield
        x1 = x + _dot_panels(jnp.concatenate(attn + pooled, axis=1), (wouta_ref, woutb_ref))
        x1_ref[rs, :] = x1
        yield
        qc_ref[rs, :] = _dot(_rms(x1, gcross).astype(_BF), wcq_bf_ref[...]) * CQ_SCALE

    _interleave([stream(b0) for b0 in range(0, nbatch, sub)], SAMPLE_MIXER_SKEW)


def _sample_mixer(x, gmix, win, sinkcol, ck, cv, st, wout, gcross, wcq, nbatch, sub, nseq,
                  past_len):
    R, D = x.shape
    rows = nbatch * nseq
    const = lambda *shape: pl.BlockSpec(shape, lambda i: (0,) * len(shape))
    return pl.pallas_call(
        functools.partial(_sample_mixer_kernel, nbatch=nbatch, sub=sub, nseq=nseq, past_len=past_len),
        grid=(R // rows,),
        in_specs=[
            pl.BlockSpec((rows, D), lambda i: (i, 0)),
            const(1, D),
            const(D, IN_WIDTH),
            const(N_HEADS * nseq, 1),
            pl.BlockSpec((nbatch, WINDOW, KV_WIDTH), lambda i: (i, 0, 0)),
            pl.BlockSpec((nbatch, WINDOW, KV_WIDTH), lambda i: (i, 0, 0)),
            pl.BlockSpec((POOL_HIST, nbatch, POOL_WIDTH), lambda i: (0, i, 0)),
            const(D, WOUT_SPLIT),
            const(D, D - WOUT_SPLIT),
            const(1, D),
            const(D, D),
        ],
        out_specs=[
            pl.BlockSpec((rows, D), lambda i: (i, 0)),
            pl.BlockSpec((rows, D), lambda i: (i, 0)),
            pl.BlockSpec((nbatch, WINDOW, KV_WIDTH), lambda i: (i, 0, 0)),
            pl.BlockSpec((nbatch, WINDOW, KV_WIDTH), lambda i: (i, 0, 0)),
            pl.BlockSpec((POOL_HIST, nbatch, POOL_WIDTH), lambda i: (0, i, 0)),
            const(D, D),
        ],
        out_shape=[
            jax.ShapeDtypeStruct((R, D), _F32),
            jax.ShapeDtypeStruct((R, D), _F32),
            jax.ShapeDtypeStruct((R // nseq, WINDOW, KV_WIDTH), _F32),
            jax.ShapeDtypeStruct((R // nseq, WINDOW, KV_WIDTH), _F32),
            jax.ShapeDtypeStruct((POOL_HIST, R // nseq, POOL_WIDTH), _F32),
            jax.ShapeDtypeStruct((D, D), _BF),
        ],
        scratch_shapes=[
            pltpu.VMEM((nbatch, 2 * WINDOW, KV_WIDTH), _BF),
            pltpu.VMEM((nbatch, 2 * WINDOW, KV_WIDTH), _BF),
            pltpu.VMEM((len(POOL_WINDOWS), nbatch * (POOL_PAD + nseq), POOL_GROUP_WIDTH), _F32),
        ],
        compiler_params=pltpu.CompilerParams(
            dimension_semantics=("arbitrary",), vmem_limit_bytes=VMEM_LIMIT_BYTES),
        name="sample_mixer",
    )(x, gmix, win, sinkcol, ck, cv, st, *wout, gcross, wcq)


def _mem_cache_rows(cache):
    nb = cache.shape[0]
    c = cache.reshape(nb, N_MEM, N_CROSS_HEADS, CROSS_HEAD_DIM // 128, 128)
    return c.transpose(0, 1, 3, 2, 4).reshape(nb, N_MEM * MEM_ROWS, 128)


def _load_mem_head(ref, b, h):
    halves = [ref[b, pl.ds(half * N_CROSS_HEADS + h, N_MEM, stride=MEM_ROWS), :]
              for half in range(CROSS_HEAD_DIM // 128)]
    return jnp.concatenate(halves, axis=1)


def _sample_tail_kernel(x1_ref, o_ref, wco_ref, gffn_ref, wup_ref, wdown_ref, gfinal_ref, y_ref, *, tile, sub):
    gffn, gfinal = gffn_ref[...], gfinal_ref[...]

    def stream(r0):
        x2 = x1_ref[r0:r0 + sub, :] + _dot(o_ref[r0:r0 + sub, :].astype(_BF), wco_ref[...])
        yield

        def store(y):
            y_ref[r0:r0 + sub, :] = y

        yield from _ffn_final_stages(x2, gffn, wup_ref, wdown_ref, gfinal, store)

    _interleave([stream(r0) for r0 in range(0, tile, sub)], TAIL_SKEW)


def _sample_tail(x1, o, wco, gffn, wup, wdown, gfinal, tile, sub):
    R, D = x1.shape
    const = lambda *shape: _single(shape, lambda i: (0,) * len(shape))
    return pl.pallas_call(
        functools.partial(_sample_tail_kernel, tile=tile, sub=sub),
        grid=(R // tile,),
        in_specs=[
            pl.BlockSpec((tile, D), lambda i: (i, 0)),
            pl.BlockSpec((tile, D), lambda i: (i, 0)),
            const(D, D),
            const(1, D),
            const(D, D_FF),
            const(D_FF, D),
            const(1, D),
        ],
        out_specs=pl.BlockSpec((tile, D), lambda i: (i, 0)),
        out_shape=jax.ShapeDtypeStruct((R, D), _F32),
        compiler_params=pltpu.CompilerParams(
            dimension_semantics=("arbitrary",), vmem_limit_bytes=VMEM_LIMIT_BYTES),
        name="sample_tail",
    )(x1, o, wco, gffn, wup, wdown, gfinal)


PROMPT_TILE = 512
TAIL_TILE = 1024
TAIL_SUB = 256
TAIL_SKEW = 4
SAMPLE_MIXER_BATCH = 32
SAMPLE_MIXER_SUB = 16
SAMPLE_MIXER_SKEW = 3


def kernel(x_prompt, x_sample, cache_win_k, cache_win_v, state_pool, cache_mem_k, cache_mem_v, mem_prompt,
           g_mix, w_in, attn_sinks, w_pool, pool_scale, w_out, g_cross, g_mem, w_cq, w_ck, w_cv, w_co,
           g_ffn, w_up, w_down, g_final):
    depth = g_mix.shape[0]
    assert depth == 1, "one layer per step"
    B, S, D = x_prompt.shape
    DB, T, _ = x_sample.shape
    past_len = PAST_LEN
    l = 0

    win = jnp.concatenate([_pair_heads(w_in[l][:, :ATTN_WIDTH], 1), w_in[l][:, ATTN_WIDTH:]], axis=1).astype(_BF)
    wout = _prep_wout(w_out[l], w_pool[l], pool_scale[l].reshape(1, POOL_WIDTH))
    gmix, gcross, gmem, gffn = (g[l].reshape(1, D) for g in (g_mix, g_cross, g_mem, g_ffn))
    gfinal = g_final.reshape(1, D)
    sinks = attn_sinks[l]

    xs = x_sample.reshape(DB * T, D)
    ck = cache_win_k[l].transpose(0, 2, 3, 1).reshape(DB, KV_WIDTH, WINDOW)
    cv = cache_win_v[l].transpose(0, 2, 3, 1).reshape(DB, KV_WIDTH, WINDOW)
    st = state_pool[l].transpose(1, 0, 2)
    sinkcol = jnp.repeat(sinks, T).reshape(N_HEADS * T, 1)
    x1s, qc, wk_s, wv_s, pool_s, wcq = _sample_mixer(xs, gmix, win, sinkcol, ck, cv, st, wout,
                                                gcross, w_cq[l], SAMPLE_MIXER_BATCH, SAMPLE_MIXER_SUB, T, past_len)

    mk = _mem_cache_rows(cache_mem_k[l])
    mv = _mem_cache_rows(cache_mem_v[l])
    x1p, klast, vlast, plast, wco, wup, wdown, o_s = _prompt_mixer(
        x_prompt, gmix, win, sinks, wout, (w_co[l], w_up[l], w_down[l]), qc, mk, mv, PROMPT_TILE)
    mem_k, mem_v, mem_kt, mem_vb = _mem_kv(mem_prompt, gmem, w_ck[l], w_cv[l])
    y_prompt = _prompt_tail(x1p, gcross, wcq, mem_kt, mem_vb, wco, gffn, wup, wdown, gfinal, TAIL_TILE, TAIL_SUB)

    y_sample = _sample_tail(x1s, o_s, wco, gffn, wup, wdown, gfinal, TAIL_TILE, TAIL_SUB).reshape(DB, T, D)

    return (
        y_prompt,
        y_sample,
        klast.reshape(B, N_KV_HEADS, HEAD_DIM, WINDOW).transpose(0, 3, 1, 2)[None],
        vlast.reshape(B, N_KV_HEADS, HEAD_DIM, WINDOW).transpose(0, 3, 1, 2)[None],
        plast[:, POOL_PAD - POOL_HIST:, :][None],
        _mem_cache_unrows(mem_k)[None],
        _mem_cache_unrows(mem_v)[None],
        wk_s.reshape(1, DB, WINDOW, N_KV_HEADS, HEAD_DIM),
        wv_s.reshape(1, DB, WINDOW, N_KV_HEADS, HEAD_DIM),
        pool_s.transpose(1, 0, 2)[None],
    )
```

```python
import functools

import jax
import jax.numpy as jnp
from jax import lax
from jax.experimental import pallas as pl
from jax.experimental.pallas import tpu as pltpu

D_MODEL = 1024
PAST_LEN = 16384
HEAD_DIM = 64
N_HEADS = 8
N_KV_HEADS = 2
GROUP = N_HEADS // N_KV_HEADS
ATTN_WIDTH = N_HEADS * HEAD_DIM
KV_WIDTH = N_KV_HEADS * HEAD_DIM
WINDOW = 128
BLOCK = WINDOW
POOL_WIDTH = D_MODEL - ATTN_WIDTH
POOL_WINDOWS = (2, 4, 8, 16)
POOL_GROUP_WIDTH = 128
POOL_HIST = 15
POOL_PAD = 16
IN_WIDTH = ATTN_WIDTH + 2 * KV_WIDTH + POOL_WIDTH
N_MEM = 256
N_CROSS_HEADS = 4
CROSS_HEAD_DIM = 256
MEM_ROWS = N_CROSS_HEADS * (CROSS_HEAD_DIM // 128)
D_FF = 4 * D_MODEL
FF_CHUNK = 1024
RMS_EPS = 1e-5
NEG_INF = -1e30
Q_SCALE = HEAD_DIM ** -0.5
CQ_SCALE = CROSS_HEAD_DIM ** -0.5
K_OFF = ATTN_WIDTH
V_OFF = ATTN_WIDTH + KV_WIDTH
U_OFF = ATTN_WIDTH + 2 * KV_WIDTH


def _pair_heads(w, axis):
    shape = w.shape
    split = shape[:axis] + (N_KV_HEADS, GROUP, HEAD_DIM) + shape[axis + 1:]
    return jnp.swapaxes(w.reshape(split), axis, axis + 1).reshape(shape)


WOUT_SPLIT = 768

VMEM_LIMIT_BYTES = 56 * 1024 * 1024

_BF = jnp.bfloat16
_F32 = jnp.float32


def _slope(h):
    return 2.0 ** (-8.0 * (h + 1) / N_HEADS)


def _dot(a, b):
    return jnp.dot(a, b, preferred_element_type=_F32)


def _dot_panels(a, panel_refs):
    return jnp.concatenate([_dot(a, ref[...]) for ref in panel_refs], axis=1)


def _rms(x, g):
    ms = jnp.mean(x * x, axis=-1, keepdims=True)
    return x * lax.rsqrt(ms + RMS_EPS) * g


def _softmax_with_sink(s, sink):
    m = jnp.maximum(jnp.max(s, axis=-1, keepdims=True), sink)
    e = jnp.exp(s - m)
    den = jnp.sum(e, axis=-1, keepdims=True) + jnp.exp(sink - m)
    return e * (1.0 / den)


def _softmax_with_sink_parts(s, sink):
    m = jnp.maximum(jnp.max(s, axis=-1, keepdims=True), sink)
    e = jnp.exp(s - m)
    den = jnp.sum(e, axis=-1, keepdims=True) + jnp.exp(sink - m)
    return e, 1.0 / den


def _softmax(s):
    m = jnp.max(s, axis=-1, keepdims=True)
    e = jnp.exp(s - m)
    return e * (1.0 / jnp.sum(e, axis=-1, keepdims=True))


def _window_sums(u_ext, axis):
    out = {}
    s = u_ext
    w = 1
    while w < max(POOL_WINDOWS):
        s = s + pltpu.roll(s, w, axis)
        w *= 2
        out[w] = s
    return out


def _interleave(streams, skew):
    pending = list(streams)
    live = []
    rnd = 0
    while live or pending:
        if pending and rnd % skew == 0:
            live.append(pending.pop(0))
        for g in list(live):
            try:
                next(g)
            except StopIteration:
                live.remove(g)
        rnd += 1


def _ffn_final_stages(x2, gffn, wup_ref, wdown_ref, gfinal, store):
    hn = _rms(x2, gffn).astype(_BF)
    yield
    acc = x2
    for c in range(D_FF // FF_CHUNK):
        hc = _dot(hn, wup_ref[:, c * FF_CHUNK:(c + 1) * FF_CHUNK])
        yield
        hc = jnp.maximum(hc, 0.0)
        hc = (hc * hc).astype(_BF)
        acc = acc + _dot(hc, wdown_ref[c * FF_CHUNK:(c + 1) * FF_CHUNK, :])
        yield
    store(_rms(acc, gfinal))


def _prep_wout_kernel(wout_ref, wpool_ref, pscale_ref, outa_ref, outb_ref):
    blocks = []
    for g in range(GROUP):
        for kv in range(N_KV_HEADS):
            r0 = (kv * GROUP + g) * HEAD_DIM
            blocks.append(wout_ref[r0:r0 + HEAD_DIM, :])
    for g in range(len(POOL_WINDOWS)):
        r0 = ATTN_WIDTH + g * POOL_GROUP_WIDTH
        scaled = wpool_ref[g] * pscale_ref[:, g * POOL_GROUP_WIDTH:(g + 1) * POOL_GROUP_WIDTH]
        blocks.append(jnp.dot(scaled, wout_ref[r0:r0 + POOL_GROUP_WIDTH, :], precision=lax.Precision.HIGHEST,
                              preferred_element_type=_F32))
    full = jnp.concatenate(blocks, axis=0).astype(_BF)
    outa_ref[...] = full[:, :WOUT_SPLIT]
    outb_ref[...] = full[:, WOUT_SPLIT:]


def _prep_wout(wout, wpool, pscale):
    D = wout.shape[0]
    return pl.pallas_call(
        _prep_wout_kernel,
        out_shape=[jax.ShapeDtypeStruct((D, WOUT_SPLIT), _BF), jax.ShapeDtypeStruct((D, D - WOUT_SPLIT), _BF)],
        compiler_params=pltpu.CompilerParams(vmem_limit_bytes=VMEM_LIMIT_BYTES),
        name="prep_wout",
    )(wout, wpool, pscale)


def _prompt_mixer_kernel(x_ref, gmix_ref, win_ref, sink_ref, wouta_ref, woutb_ref,
                         wco_ref, wup_ref, wdown_ref, qc_ref, mk_ref, mv_ref,
                         x1_ref, klast_ref, vlast_ref, plast_ref, wco_bf_ref, wup_bf_ref, wdown_bf_ref,
                         os_ref, kt_scr, v_scr, u_scr, bias_scr, inv_scr, *, tile, nseq):
    b = pl.program_id(0)
    s = pl.program_id(1)
    nb = tile // BLOCK

    @pl.when((b == 0) & (s == 0))
    def _():
        qi = lax.broadcasted_iota(jnp.int32, (BLOCK, 2 * BLOCK), 0)
        kc = lax.broadcasted_iota(jnp.int32, (BLOCK, 2 * BLOCK), 1)
        dist = qi + BLOCK - kc
        valid = (dist >= 0) & (dist <= WINDOW)
        valid_first = valid & (kc >= BLOCK)
        distf = dist.astype(_F32)
        for h in range(N_HEADS):
            ali = -_slope(h) * distf
            bias_scr[0, h] = jnp.where(valid, ali, NEG_INF)
            bias_scr[1, h] = jnp.where(valid_first, ali, NEG_INF)
        prow = lax.broadcasted_iota(jnp.int32, (BLOCK, POOL_GROUP_WIDTH), 0)
        for g, w in enumerate(POOL_WINDOWS):
            inv_scr[0, g] = jnp.full((BLOCK, POOL_GROUP_WIDTH), 1.0 / w, _F32)
            inv_scr[1, g] = 1.0 / jnp.minimum(prow + 1, w).astype(_F32)

    @pl.when(s == 0)
    def _():
        kt_scr[...] = jnp.zeros((KV_WIDTH, BLOCK), _BF)
        v_scr[...] = jnp.zeros((BLOCK, KV_WIDTH), _BF)
        u_scr[...] = jnp.zeros((POOL_PAD, POOL_WIDTH), _F32)

    lane = lax.broadcasted_iota(jnp.int32, (BLOCK, 2 * HEAD_DIM), 1)
    lo = lane < HEAD_DIM
    gmix = gmix_ref[...]
    zeros_kt = jnp.zeros((HEAD_DIM, 2 * BLOCK), _BF)

    def project(j):
        x = x_ref[0, j * BLOCK:(j + 1) * BLOCK, :]
        proj = _dot(_rms(x, gmix).astype(_BF), win_ref[...])
        k = proj[:, K_OFF:K_OFF + KV_WIDTH]
        v = proj[:, V_OFF:V_OFF + KV_WIDTH]
        u = proj[:, U_OFF:]
        kt = k.T
        if j == nb - 1:
            klast_ref[0] = kt
            vlast_ref[0] = v.T
            plast_ref[0] = u[BLOCK - POOL_PAD:]
        return dict(x=x, q=(proj[:, :ATTN_WIDTH] * Q_SCALE).astype(_BF), kt=kt.astype(_BF), v=v.astype(_BF), u=u)

    def pool(j, u_hist, u):
        first = ((s == 0) & (j == 0)).astype(jnp.int32) if j == 0 else 0
        ys = []
        for g, w in enumerate(POOL_WINDOWS):
            c0 = g * POOL_GROUP_WIDTH
            u_ext = jnp.concatenate([u_hist[:, c0:c0 + POOL_GROUP_WIDTH], u[:, c0:c0 + POOL_GROUP_WIDTH]], axis=0)
            sw = _window_sums(u_ext, 0)[w][POOL_PAD:]
            ys.append((sw * inv_scr[first, g] - u_ext[POOL_PAD:]).astype(_BF))
        return ys

    def kv_operands(kt_prev, v_prev, blk):
        kt2 = jnp.concatenate([kt_prev, blk["kt"]], axis=1)
        v2 = jnp.concatenate([v_prev, blk["v"]], axis=0)
        kt_pair = jnp.concatenate([jnp.concatenate([kt2[:HEAD_DIM], zeros_kt], axis=0),
                                   jnp.concatenate([zeros_kt, kt2[HEAD_DIM:]], axis=0)], axis=1)
        lane2 = lax.broadcasted_iota(jnp.int32, v2.shape, 1)
        zero = jnp.zeros_like(v2)
        v_pair = jnp.concatenate([jnp.where(lane2 < HEAD_DIM, v2, zero), jnp.where(lane2 < HEAD_DIM, zero, v2)],
                                 axis=0)
        return kt_pair, v_pair

    def wave_scores(j, blk, kt_pair, pairs):
        first = ((s == 0) & (j == 0)).astype(jnp.int32) if j == 0 else 0
        scores = []
        for p in pairs:
            sc = _dot(blk["q"][:, p * 128:(p + 1) * 128], kt_pair)
            scores.append(sc[:, :2 * BLOCK] + bias_scr[first, p])
            scores.append(sc[:, 2 * BLOCK:] + bias_scr[first, p + GROUP])
        return scores

    def wave_values(v_pair, pairs, scores):
        slabs = []
        for i, p in enumerate(pairs):
            es, invs = [], []
            for half, h in ((0, p), (1, p + GROUP)):
                e, inv = _softmax_with_sink_parts(scores[2 * i + half], sink_ref[h])
                es.append(e.astype(_BF))
                invs.append(inv)
            o = _dot(jnp.concatenate(es, axis=1), v_pair)
            slabs.append((o * jnp.where(lo, invs[0], invs[1])).astype(_BF))
        return slabs

    def output(j, blk, slabs):
        cat = jnp.concatenate(slabs, axis=1)
        x1_ref[0, j * BLOCK:(j + 1) * BLOCK, :] = blk["x"] + _dot_panels(cat, (wouta_ref, woutb_ref))

    heads = [slice(h * CROSS_HEAD_DIM, (h + 1) * CROSS_HEAD_DIM) for h in range(N_CROSS_HEADS)]

    def mem_scores(i):
        q = qc_ref[i * nseq:(i + 1) * nseq, :].astype(_BF)
        return [lax.dot_general(q[:, c], _load_mem_head(mk_ref, i, h).astype(_BF), (((1,), (1,)), ((), ())),
                                preferred_element_type=_F32) for h, c in enumerate(heads)]

    def mem_values(i, scores):
        outs = [_dot(_softmax(sc).astype(_BF), _load_mem_head(mv_ref, i, h).astype(_BF))
                for h, sc in enumerate(scores)]
        os_ref[i * nseq:(i + 1) * nseq, :] = jnp.concatenate(outs, axis=1)

    kt_prev, v_prev, u_hist = kt_scr[...], v_scr[...], u_scr[...]
    blk = project(0)
    done = None
    for j in range(nb):
        kt_pair, v_pair = kv_operands(kt_prev, v_prev, blk)
        sc0 = wave_scores(j, blk, kt_pair, (0, 1))
        nxt = project(j + 1) if j + 1 < nb else None
        msc = mem_scores(j)
        sc1 = wave_scores(j, blk, kt_pair, (2, 3))
        pooled = pool(j, u_hist, blk["u"])
        at0 = wave_values(v_pair, (0, 1), sc0)
        if done is not None:
            output(*done)
        mem_values(j, msc)
        at1 = wave_values(v_pair, (2, 3), sc1)
        done = (j, blk, at0 + at1 + pooled)
        kt_prev, v_prev, u_hist = blk["kt"], blk["v"], blk["u"][BLOCK - POOL_PAD:]
        blk = nxt
    output(*done)
    kt_scr[...] = kt_prev
    v_scr[...] = v_prev
    u_scr[...] = u_hist

    for src, dst in ((wco_ref, wco_bf_ref), (wup_ref, wup_bf_ref), (wdown_ref, wdown_bf_ref)):
        dst[...] = src[...].astype(_BF)


def _prompt_mixer(x, gmix, win, sinks, wout, tail_weights, qc, mk, mv, tile):
    B, S, D = x.shape
    ns = S // tile
    nsteps = B * ns
    nmem = tile // BLOCK
    nseq = qc.shape[0] // mk.shape[0]
    assert mk.shape[0] == nsteps * nmem
    const = lambda *shape: pl.BlockSpec(shape, lambda b, s: (0,) * len(shape))
    chunk = lambda w: pl.BlockSpec((w.shape[0] // nsteps, w.shape[1]), lambda b, s: (b * ns + s, 0))
    assert all(w.shape[0] % (16 * nsteps) == 0 for w in tail_weights)
    mem_rows = pl.BlockSpec((nmem * nseq, D), lambda b, s: (b * ns + s, 0))
    mem_cache = pl.BlockSpec((nmem, N_MEM * MEM_ROWS, 128), lambda b, s: (b * ns + s, 0, 0))
    return pl.pallas_call(
        functools.partial(_prompt_mixer_kernel, tile=tile, nseq=nseq),
        grid=(B, ns),
        in_specs=[
            pl.BlockSpec((1, tile, D), lambda b, s: (b, s, 0)),
            const(1, D),
            const(D, IN_WIDTH),
            pl.BlockSpec(memory_space=pltpu.SMEM),
            const(D, WOUT_SPLIT),
            const(D, D - WOUT_SPLIT),
        ] + [chunk(w) for w in tail_weights] + [mem_rows, mem_cache, mem_cache],
        out_specs=[
            pl.BlockSpec((1, tile, D), lambda b, s: (b, s, 0)),
            pl.BlockSpec((1, BLOCK, KV_WIDTH), lambda b, s: (b, 0, 0)),
            pl.BlockSpec((1, BLOCK, KV_WIDTH), lambda b, s: (b, 0, 0)),
            pl.BlockSpec((1, POOL_PAD, POOL_WIDTH), lambda b, s: (b, 0, 0)),
        ] + [chunk(w) for w in tail_weights] + [mem_rows],
        out_shape=[
            jax.ShapeDtypeStruct((B, S, D), _F32),
            jax.ShapeDtypeStruct((B, BLOCK, KV_WIDTH), _F32),
            jax.ShapeDtypeStruct((B, BLOCK, KV_WIDTH), _F32),
            jax.ShapeDtypeStruct((B, POOL_PAD, POOL_WIDTH), _F32),
        ] + [jax.ShapeDtypeStruct(w.shape, _BF) for w in tail_weights] + [jax.ShapeDtypeStruct(qc.shape, _F32)],
        scratch_shapes=[
            pltpu.VMEM((KV_WIDTH, BLOCK), _BF),
            pltpu.VMEM((BLOCK, KV_WIDTH), _BF),
            pltpu.VMEM((POOL_PAD, POOL_WIDTH), _F32),
            pltpu.VMEM((2, N_HEADS, BLOCK, 2 * BLOCK), _F32),
            pltpu.VMEM((2, len(POOL_WINDOWS), BLOCK, POOL_GROUP_WIDTH), _F32),
        ],
        compiler_params=pltpu.CompilerParams(
            dimension_semantics=("arbitrary", "arbitrary"), vmem_limit_bytes=VMEM_LIMIT_BYTES),
        name="prompt_mixer",
    )(x, gmix, win, sinks, *wout, *tail_weights, qc, mk, mv)


def _mem_kv_kernel(mem_ref, gmem_ref, wck_ref, wcv_ref, wcq_ref, wco_ref, k_ref, v_ref, kq_ref, vo_ref,
                   wck_scr, wcv_scr):
    @pl.when(pl.program_id(0) == 0)
    def _():
        wck_scr[...] = wck_ref[...].astype(_BF)
        wcv_scr[...] = wcv_ref[...].astype(_BF)

    hm = _rms(mem_ref[0], gmem_ref[...]).astype(_BF)
    k = _dot(hm, wck_scr[...])
    v = _dot(hm, wcv_scr[...])
    for h in range(N_CROSS_HEADS):
        head = slice(h * CROSS_HEAD_DIM, (h + 1) * CROSS_HEAD_DIM)
        kq_ref[0, :, head] = (_dot(wcq_ref[:, head], k[:, head].T.astype(_BF)) * CQ_SCALE).astype(_BF)
        vo_ref[0, head, :] = _dot(v[:, head].astype(_BF), wco_ref[head, :]).astype(_BF)
        for half in range(CROSS_HEAD_DIM // 128):
            c0 = h * CROSS_HEAD_DIM + half * 128
            rows = pl.ds(half * N_CROSS_HEADS + h, N_MEM, stride=MEM_ROWS)
            k_ref[0, rows, :] = k[:, c0:c0 + 128]
            v_ref[0, rows, :] = v[:, c0:c0 + 128]


def _mem_kv(mem, gmem, wck, wcv, wcq, wco):
    B, M, D = mem.shape
    HM = N_CROSS_HEADS * M
    const = lambda *shape: pl.BlockSpec(shape, lambda b: (0,) * len(shape))
    return pl.pallas_call(
        _mem_kv_kernel,
        grid=(B,),
        in_specs=[pl.BlockSpec((1, M, D), lambda b: (b, 0, 0)), const(1, D), const(D, D), const(D, D),
                  const(D, D), const(D, D)],
        out_specs=[
            pl.BlockSpec((1, M * MEM_ROWS, 128), lambda b: (b, 0, 0)),
            pl.BlockSpec((1, M * MEM_ROWS, 128), lambda b: (b, 0, 0)),
            pl.BlockSpec((1, D, HM), lambda b: (b, 0, 0)),
            pl.BlockSpec((1, HM, D), lambda b: (b, 0, 0)),
        ],
        out_shape=[
            jax.ShapeDtypeStruct((B, M * MEM_ROWS, 128), _F32),
            jax.ShapeDtypeStruct((B, M * MEM_ROWS, 128), _F32),
            jax.ShapeDtypeStruct((B, D, HM), _BF),
            jax.ShapeDtypeStruct((B, HM, D), _BF),
        ],
        scratch_shapes=[pltpu.VMEM((D, D), _BF), pltpu.VMEM((D, D), _BF)],
        compiler_params=pltpu.CompilerParams(
            dimension_semantics=("arbitrary",), vmem_limit_bytes=VMEM_LIMIT_BYTES),
        name="prompt_mem_kv",
    )(mem, gmem, wck, wcv, wcq, wco)


def _mem_cache_unrows(rows):
    nb = rows.shape[0]
    c = rows.reshape(nb, N_MEM, CROSS_HEAD_DIM // 128, N_CROSS_HEADS, 128)
    return c.transpose(0, 1, 3, 2, 4).reshape(nb, N_MEM, N_CROSS_HEADS, CROSS_HEAD_DIM)


def _prompt_tail_kernel(x1_ref, gcross_ref, kq_ref, vo_ref, gffn_ref, wup_ref, wdown_ref, gfinal_ref, y_ref,
                        *, tile, sub):
    gcross, gffn, gfinal = gcross_ref[...], gffn_ref[...], gfinal_ref[...]

    def stream(r0):
        x1 = x1_ref[0, r0:r0 + sub, :]
        hn = _rms(x1, gcross).astype(_BF)
        yield
        scores = _dot(hn, kq_ref[0])
        yield
        probs = [_softmax(scores[:, h * N_MEM:(h + 1) * N_MEM]).astype(_BF) for h in range(N_CROSS_HEADS)]
        x2 = x1 + _dot(jnp.concatenate(probs, axis=1), vo_ref[0])
        yield

        def store(y):
            y_ref[0, r0:r0 + sub, :] = y

        yield from _ffn_final_stages(x2, gffn, wup_ref, wdown_ref, gfinal, store)

    _interleave([stream(r0) for r0 in range(0, tile, sub)], TAIL_SKEW)


def _single(shape, index_map):
    return pl.BlockSpec(shape, index_map, pipeline_mode=pl.Buffered(1))


def _prompt_tail(x1, gcross, kq, vo, gffn, wup, wdown, gfinal, tile, sub):
    B, S, D = x1.shape
    const = lambda *shape: _single(shape, lambda b, s: (0,) * len(shape))
    return pl.pallas_call(
        functools.partial(_prompt_tail_kernel, tile=tile, sub=sub),
        grid=(B, S // tile),
        in_specs=[
            pl.BlockSpec((1, tile, D), lambda b, s: (b, s, 0)),
            const(1, D),
            pl.BlockSpec((1, D, N_CROSS_HEADS * N_MEM), lambda b, s: (b, 0, 0)),
            pl.BlockSpec((1, N_CROSS_HEADS * N_MEM, D), lambda b, s: (b, 0, 0)),
            const(1, D),
            const(D, D_FF),
            const(D_FF, D),
            const(1, D),
        ],
        out_specs=pl.BlockSpec((1, tile, D), lambda b, s: (b, s, 0)),
        out_shape=jax.ShapeDtypeStruct((B, S, D), _F32),
        compiler_params=pltpu.CompilerParams(
            dimension_semantics=("arbitrary", "arbitrary"), vmem_limit_bytes=VMEM_LIMIT_BYTES),
        name="prompt_tail",
    )(x1, gcross, kq, vo, gffn, wup, wdown, gfinal)


def _sample_mixer_kernel(x_ref, gmix_ref, win_ref, sinkcol_ref, ck_ref, cv_ref, st_ref,
                         wouta_ref, woutb_ref, gcross_ref, wcq_ref,
                         x1_ref, qc_ref, wk_ref, wv_ref, pool_ref, wcq_bf_ref,
                         kc_scr, vc_scr, u_scr, *, nbatch, sub, nseq, past_len):
    @pl.when(pl.program_id(0) == 0)
    def _():
        wcq_bf_ref[...] = wcq_ref[...].astype(_BF)

    rows = sub * nseq
    nkeys = 2 * WINDOW
    gmix, gcross = gmix_ref[...], gcross_ref[...]
    sinkcol = sinkcol_ref[...][None]

    t = lax.broadcasted_iota(jnp.int32, (nseq, nkeys), 0)
    c = lax.broadcasted_iota(jnp.int32, (nseq, nkeys), 1)
    dist = jnp.where(c < WINDOW, WINDOW + t - c, t - (c - WINDOW))
    valid = (dist >= 0) & (dist <= WINDOW) & (c < WINDOW + nseq)
    distf = dist.astype(_F32)
    bias = jnp.concatenate([jnp.where(valid, -_slope(h) * distf, NEG_INF) for h in range(N_HEADS)], axis=0)[None]
    lane = lax.broadcasted_iota(jnp.int32, (rows, 2 * HEAD_DIM), 1)
    lo = lane < HEAD_DIM
    tpos = past_len + lax.broadcasted_iota(jnp.int32, (nseq, POOL_GROUP_WIDTH), 0)
    zpad = jnp.zeros((sub, nkeys - WINDOW, KV_WIDTH), _BF)

    def stream(b0):
        bs = slice(b0, b0 + sub)
        rs = slice(b0 * nseq, (b0 + sub) * nseq)
        x = x_ref[rs, :]
        proj = _dot(_rms(x, gmix).astype(_BF), win_ref[...])
        yield
        k_new = proj[:, K_OFF:K_OFF + KV_WIDTH].reshape(sub, nseq, KV_WIDTH)
        v_new = proj[:, V_OFF:V_OFF + KV_WIDTH].reshape(sub, nseq, KV_WIDTH)
        ck = jnp.swapaxes(ck_ref[bs], 1, 2)
        cv = jnp.swapaxes(cv_ref[bs], 1, 2)
        wk_ref[bs, :WINDOW - nseq, :] = ck[:, nseq:, :]
        wk_ref[bs, WINDOW - nseq:, :] = k_new
        wv_ref[bs, :WINDOW - nseq, :] = cv[:, nseq:, :]
        wv_ref[bs, WINDOW - nseq:, :] = v_new
        kc_scr[bs, WINDOW:, :] = zpad
        vc_scr[bs, WINDOW:, :] = zpad
        kc_scr[bs, :WINDOW, :] = ck.astype(_BF)
        vc_scr[bs, :WINDOW, :] = cv.astype(_BF)
        kc_scr[bs, WINDOW:WINDOW + nseq, :] = k_new.astype(_BF)
        vc_scr[bs, WINDOW:WINDOW + nseq, :] = v_new.astype(_BF)
        qsc = proj[:, :ATTN_WIDTH] * Q_SCALE
        q_lo, q_hi = [], []
        for p in range(GROUP):
            slab = qsc[:, p * 128:(p + 1) * 128]
            q_lo.append(jnp.where(lo, slab, 0.0).reshape(sub, nseq, 128))
            q_hi.append(jnp.where(lo, 0.0, slab).reshape(sub, nseq, 128))
        qm = jnp.concatenate(q_lo + q_hi, axis=1).astype(_BF)
        sc = jnp.einsum('bqd,bkd->bqk', qm, kc_scr[bs], preferred_element_type=_F32) + bias
        yield
        pr = _softmax_with_sink(sc, sinkcol).astype(_BF)
        o = jnp.einsum('bqk,bkd->bqd', pr, vc_scr[bs], preferred_element_type=_F32)
        yield
        attn = []
        for p in range(GROUP):
            o_lo = o[:, p * nseq:(p + 1) * nseq, :].reshape(rows, 128)
            o_hi = o[:, (p + GROUP) * nseq:(p + GROUP + 1) * nseq, :].reshape(rows, 128)
            attn.append(jnp.where(lo, o_lo, o_hi).astype(_BF))
        ext = POOL_PAD + nseq
        seq_rows = lambda r: pl.ds(b0 * ext + r, sub, stride=ext)
        pooled = []
        for g, w in enumerate(POOL_WINDOWS):
            c0 = g * POOL_GROUP_WIDTH
            u_scr[g, seq_rows(0), :] = jnp.zeros((sub, POOL_GROUP_WIDTH), _F32)
            for r in range(POOL_HIST):
                u_scr[g, seq_rows(POOL_PAD - POOL_HIST + r), :] = st_ref[r, bs, c0:c0 + POOL_GROUP_WIDTH]
            for i in range(sub):
                u_scr[g, (b0 + i) * ext + POOL_PAD:(b0 + i + 1) * ext, :] = (
                    proj[i * nseq:(i + 1) * nseq, U_OFF + c0:U_OFF + c0 + POOL_GROUP_WIDTH])
            for r in range(POOL_HIST):
                pool_ref[r, bs, c0:c0 + POOL_GROUP_WIDTH] = u_scr[g, seq_rows(ext - POOL_HIST + r), :]
            ug = u_scr[g, b0 * ext:(b0 + sub) * ext, :].reshape(sub, ext, POOL_GROUP_WIDTH)
            sw = _window_sums(ug, 1)[w][:, POOL_PAD:, :]
            cnt = jnp.minimum(tpos + 1, w).astype(_F32)
            pooled.append((sw / cnt[None] - ug[:, POOL_PAD:, :]).reshape(rows, POOL_GROUP_WIDTH).astype(_BF))
        yield
        x1 = x + _dot_panels(jnp.concatenate(attn + pooled, axis=1), (wouta_ref, woutb_ref))
        x1_ref[rs, :] = x1
        yield
        qc_ref[rs, :] = _dot(_rms(x1, gcross).astype(_BF), wcq_bf_ref[...]) * CQ_SCALE

    _interleave([stream(b0) for b0 in range(0, nbatch, sub)], SAMPLE_MIXER_SKEW)


def _sample_mixer(x, gmix, win, sinkcol, ck, cv, st, wout, gcross, wcq, nbatch, sub, nseq,
                  past_len):
    R, D = x.shape
    rows = nbatch * nseq
    const = lambda *shape: pl.BlockSpec(shape, lambda i: (0,) * len(shape))
    return pl.pallas_call(
        functools.partial(_sample_mixer_kernel, nbatch=nbatch, sub=sub, nseq=nseq, past_len=past_len),
        grid=(R // rows,),
        in_specs=[
            pl.BlockSpec((rows, D), lambda i: (i, 0)),
            const(1, D),
            const(D, IN_WIDTH),
            const(N_HEADS * nseq, 1),
            pl.BlockSpec((nbatch, WINDOW, KV_WIDTH), lambda i: (i, 0, 0)),
            pl.BlockSpec((nbatch, WINDOW, KV_WIDTH), lambda i: (i, 0, 0)),
            pl.BlockSpec((POOL_HIST, nbatch, POOL_WIDTH), lambda i: (0, i, 0)),
            const(D, WOUT_SPLIT),
            const(D, D - WOUT_SPLIT),
            const(1, D),
            const(D, D),
        ],
        out_specs=[
            pl.BlockSpec((rows, D), lambda i: (i, 0)),
            pl.BlockSpec((rows, D), lambda i: (i, 0)),
            pl.BlockSpec((nbatch, WINDOW, KV_WIDTH), lambda i: (i, 0, 0)),
            pl.BlockSpec((nbatch, WINDOW, KV_WIDTH), lambda i: (i, 0, 0)),
            pl.BlockSpec((POOL_HIST, nbatch, POOL_WIDTH), lambda i: (0, i, 0)),
            const(D, D),
        ],
        out_shape=[
            jax.ShapeDtypeStruct((R, D), _F32),
            jax.ShapeDtypeStruct((R, D), _F32),
            jax.ShapeDtypeStruct((R // nseq, WINDOW, KV_WIDTH), _F32),
            jax.ShapeDtypeStruct((R // nseq, WINDOW, KV_WIDTH), _F32),
            jax.ShapeDtypeStruct((POOL_HIST, R // nseq, POOL_WIDTH), _F32),
            jax.ShapeDtypeStruct((D, D), _BF),
        ],
        scratch_shapes=[
            pltpu.VMEM((nbatch, 2 * WINDOW, KV_WIDTH), _BF),
            pltpu.VMEM((nbatch, 2 * WINDOW, KV_WIDTH), _BF),
            pltpu.VMEM((len(POOL_WINDOWS), nbatch * (POOL_PAD + nseq), POOL_GROUP_WIDTH), _F32),
        ],
        compiler_params=pltpu.CompilerParams(
            dimension_semantics=("arbitrary",), vmem_limit_bytes=VMEM_LIMIT_BYTES),
        name="sample_mixer",
    )(x, gmix, win, sinkcol, ck, cv, st, *wout, gcross, wcq)


def _mem_cache_rows(cache):
    nb = cache.shape[0]
    c = cache.reshape(nb, N_MEM, N_CROSS_HEADS, CROSS_HEAD_DIM // 128, 128)
    return c.transpose(0, 1, 3, 2, 4).reshape(nb, N_MEM * MEM_ROWS, 128)


def _load_mem_head(ref, b, h):
    halves = [ref[b, pl.ds(half * N_CROSS_HEADS + h, N_MEM, stride=MEM_ROWS), :]
              for half in range(CROSS_HEAD_DIM // 128)]
    return jnp.concatenate(halves, axis=1)


def _sample_tail_kernel(x1_ref, o_ref, wco_ref, gffn_ref, wup_ref, wdown_ref, gfinal_ref, y_ref, *, tile, sub):
    gffn, gfinal = gffn_ref[...], gfinal_ref[...]

    def stream(r0):
        x2 = x1_ref[r0:r0 + sub, :] + _dot(o_ref[r0:r0 + sub, :].astype(_BF), wco_ref[...])
        yield

        def store(y):
            y_ref[r0:r0 + sub, :] = y

        yield from _ffn_final_stages(x2, gffn, wup_ref, wdown_ref, gfinal, store)

    _interleave([stream(r0) for r0 in range(0, tile, sub)], TAIL_SKEW)


def _sample_tail(x1, o, wco, gffn, wup, wdown, gfinal, tile, sub):
    R, D = x1.shape
    const = lambda *shape: _single(shape, lambda i: (0,) * len(shape))
    return pl.pallas_call(
        functools.partial(_sample_tail_kernel, tile=tile, sub=sub),
        grid=(R // tile,),
        in_specs=[
            pl.BlockSpec((tile, D), lambda i: (i, 0)),
            pl.BlockSpec((tile, D), lambda i: (i, 0)),
            const(D, D),
            const(1, D),
            const(D, D_FF),
            const(D_FF, D),
            const(1, D),
        ],
        out_specs=pl.BlockSpec((tile, D), lambda i: (i, 0)),
        out_shape=jax.ShapeDtypeStruct((R, D), _F32),
        compiler_params=pltpu.CompilerParams(
            dimension_semantics=("arbitrary",), vmem_limit_bytes=VMEM_LIMIT_BYTES),
        name="sample_tail",
    )(x1, o, wco, gffn, wup, wdown, gfinal)


PROMPT_TILE = 512
TAIL_TILE = 1024
TAIL_SUB = 256
TAIL_SKEW = 4
SAMPLE_MIXER_BATCH = 32
SAMPLE_MIXER_SUB = 16
SAMPLE_MIXER_SKEW = 3


def kernel(x_prompt, x_sample, cache_win_k, cache_win_v, state_pool, cache_mem_k, cache_mem_v, mem_prompt,
           g_mix, w_in, attn_sinks, w_pool, pool_scale, w_out, g_cross, g_mem, w_cq, w_ck, w_cv, w_co,
           g_ffn, w_up, w_down, g_final):
    depth = g_mix.shape[0]
    assert depth == 1, "one layer per step"
    B, S, D = x_prompt.shape
    DB, T, _ = x_sample.shape
    past_len = PAST_LEN
    l = 0

    win = jnp.concatenate([_pair_heads(w_in[l][:, :ATTN_WIDTH], 1), w_in[l][:, ATTN_WIDTH:]], axis=1).astype(_BF)
    wout = _prep_wout(w_out[l], w_pool[l], pool_scale[l].reshape(1, POOL_WIDTH))
    gmix, gcross, gmem, gffn = (g[l].reshape(1, D) for g in (g_mix, g_cross, g_mem, g_ffn))
    gfinal = g_final.reshape(1, D)
    sinks = attn_sinks[l]

    xs = x_sample.reshape(DB * T, D)
    ck = cache_win_k[l].transpose(0, 2, 3, 1).reshape(DB, KV_WIDTH, WINDOW)
    cv = cache_win_v[l].transpose(0, 2, 3, 1).reshape(DB, KV_WIDTH, WINDOW)
    st = state_pool[l].transpose(1, 0, 2)
    sinkcol = jnp.repeat(sinks, T).reshape(N_HEADS * T, 1)
    x1s, qc, wk_s, wv_s, pool_s, wcq = _sample_mixer(xs, gmix, win, sinkcol, ck, cv, st, wout,
                                                gcross, w_cq[l], SAMPLE_MIXER_BATCH, SAMPLE_MIXER_SUB, T, past_len)

    mk = _mem_cache_rows(cache_mem_k[l])
    mv = _mem_cache_rows(cache_mem_v[l])
    x1p, klast, vlast, plast, wco, wup, wdown, o_s = _prompt_mixer(
        x_prompt, gmix, win, sinks, wout, (w_co[l], w_up[l], w_down[l]), qc, mk, mv, PROMPT_TILE)
    mem_k, mem_v, mem_kq, mem_vo = _mem_kv(mem_prompt, gmem, w_ck[l], w_cv[l], wcq, wco)
    y_prompt = _prompt_tail(x1p, gcross, mem_kq, mem_vo, gffn, wup, wdown, gfinal, TAIL_TILE, TAIL_SUB)

    y_sample = _sample_tail(x1s, o_s, wco, gffn, wup, wdown, gfinal, TAIL_TILE, TAIL_SUB).reshape(DB, T, D)

    return (
        y_prompt,
        y_sample,
        klast.reshape(B, N_KV_HEADS, HEAD_DIM, WINDOW).transpose(0, 3, 1, 2)[None],
        vlast.reshape(B, N_KV_HEADS, HEAD_DIM, WINDOW).transpose(0, 3, 1, 2)[None],
        plast[:, POOL_PAD - POOL_HIST:, :][None],
        _mem_cache_unrows(mem_k)[None],
        _mem_cache_unrows(mem_v)[None],
        wk_s.reshape(1, DB, WINDOW, N_KV_HEADS, HEAD_DIM),
        wv_s.reshape(1, DB, WINDOW, N_KV_HEADS, HEAD_DIM),
        pool_s.transpose(1, 0, 2)[None],
    )
```

```python
import functools

import jax
import jax.numpy as jnp
from jax import lax
from jax.experimental import pallas as pl
from jax.experimental.pallas import tpu as pltpu

D_MODEL = 1024
PAST_LEN = 16384
HEAD_DIM = 64
N_HEADS = 8
N_KV_HEADS = 2
GROUP = N_HEADS // N_KV_HEADS
ATTN_WIDTH = N_HEADS * HEAD_DIM
KV_WIDTH = N_KV_HEADS * HEAD_DIM
WINDOW = 128
BLOCK = WINDOW
POOL_WIDTH = D_MODEL - ATTN_WIDTH
POOL_WINDOWS = (2, 4, 8, 16)
POOL_GROUP_WIDTH = 128
POOL_HIST = 15
POOL_PAD = 16
IN_WIDTH = ATTN_WIDTH + 2 * KV_WIDTH + POOL_WIDTH
N_MEM = 256
N_CROSS_HEADS = 4
CROSS_HEAD_DIM = 256
MEM_ROWS = N_CROSS_HEADS * (CROSS_HEAD_DIM // 128)
D_FF = 4 * D_MODEL
FF_CHUNK = 1024
RMS_EPS = 1e-5
NEG_INF = -1e30
Q_SCALE = HEAD_DIM ** -0.5
CQ_SCALE = CROSS_HEAD_DIM ** -0.5
K_OFF = ATTN_WIDTH
V_OFF = ATTN_WIDTH + KV_WIDTH
U_OFF = ATTN_WIDTH + 2 * KV_WIDTH


def _pair_heads(w, axis):
    shape = w.shape
    split = shape[:axis] + (N_KV_HEADS, GROUP, HEAD_DIM) + shape[axis + 1:]
    return jnp.swapaxes(w.reshape(split), axis, axis + 1).reshape(shape)


WOUT_SPLIT = 768

VMEM_LIMIT_BYTES = 56 * 1024 * 1024

_BF = jnp.bfloat16
_F32 = jnp.float32


def _slope(h):
    return 2.0 ** (-8.0 * (h + 1) / N_HEADS)


def _dot(a, b):
    return jnp.dot(a, b, preferred_element_type=_F32)


def _dot_panels(a, panel_refs):
    return jnp.concatenate([_dot(a, ref[...]) for ref in panel_refs], axis=1)


def _rms(x, g):
    ms = jnp.mean(x * x, axis=-1, keepdims=True)
    return x * lax.rsqrt(ms + RMS_EPS) * g


def _softmax_with_sink(s, sink):
    m = jnp.maximum(jnp.max(s, axis=-1, keepdims=True), sink)
    e = jnp.exp(s - m)
    den = jnp.sum(e, axis=-1, keepdims=True) + jnp.exp(sink - m)
    return e * (1.0 / den)


def _softmax_with_sink_parts(s, sink):
    m = jnp.maximum(jnp.max(s, axis=-1, keepdims=True), sink)
    e = jnp.exp(s - m)
    den = jnp.sum(e, axis=-1, keepdims=True) + jnp.exp(sink - m)
    return e, 1.0 / den


def _softmax(s):
    m = jnp.max(s, axis=-1, keepdims=True)
    e = jnp.exp(s - m)
    return e * (1.0 / jnp.sum(e, axis=-1, keepdims=True))


def _window_sums(u_ext, axis):
    out = {}
    s = u_ext
    w = 1
    while w < max(POOL_WINDOWS):
        s = s + pltpu.roll(s, w, axis)
        w *= 2
        out[w] = s
    return out


def _interleave(streams, skew):
    pending = list(streams)
    live = []
    rnd = 0
    while live or pending:
        if pending and rnd % skew == 0:
            live.append(pending.pop(0))
        for g in list(live):
            try:
                next(g)
            except StopIteration:
                live.remove(g)
        rnd += 1


def _ffn_final_stages(x2, gffn, wup_ref, wdown_ref, gfinal, store):
    hn = _rms(x2, gffn).astype(_BF)
    yield
    acc = x2
    for c in range(D_FF // FF_CHUNK):
        hc = _dot(hn, wup_ref[:, c * FF_CHUNK:(c + 1) * FF_CHUNK])
        yield
        hc = jnp.maximum(hc, 0.0)
        hc = (hc * hc).astype(_BF)
        acc = acc + _dot(hc, wdown_ref[c * FF_CHUNK:(c + 1) * FF_CHUNK, :])
        yield
    store(_rms(acc, gfinal))


def _prep_wout_kernel(wout_ref, wpool_ref, pscale_ref, outa_ref, outb_ref):
    blocks = []
    for g in range(GROUP):
        for kv in range(N_KV_HEADS):
            r0 = (kv * GROUP + g) * HEAD_DIM
            blocks.append(wout_ref[r0:r0 + HEAD_DIM, :])
    for g in range(len(POOL_WINDOWS)):
        r0 = ATTN_WIDTH + g * POOL_GROUP_WIDTH
        scaled = wpool_ref[g] * pscale_ref[:, g * POOL_GROUP_WIDTH:(g + 1) * POOL_GROUP_WIDTH]
        blocks.append(jnp.dot(scaled, wout_ref[r0:r0 + POOL_GROUP_WIDTH, :], precision=lax.Precision.HIGHEST,
                              preferred_element_type=_F32))
    full = jnp.concatenate(blocks, axis=0).astype(_BF)
    outa_ref[...] = full[:, :WOUT_SPLIT]
    outb_ref[...] = full[:, WOUT_SPLIT:]


def _prep_wout(wout, wpool, pscale):
    D = wout.shape[0]
    return pl.pallas_call(
        _prep_wout_kernel,
        out_shape=[jax.ShapeDtypeStruct((D, WOUT_SPLIT), _BF), jax.ShapeDtypeStruct((D, D - WOUT_SPLIT), _BF)],
        compiler_params=pltpu.CompilerParams(vmem_limit_bytes=VMEM_LIMIT_BYTES),
        name="prep_wout",
    )(wout, wpool, pscale)


def _prompt_mixer_kernel(x_ref, gmix_ref, win_ref, sink_ref, wouta_ref, woutb_ref, *rest, tile, nseq, ncast):
    cast_src, (qc_ref, mk_ref, mv_ref) = rest[:ncast], rest[ncast:ncast + 3]
    x1_ref, klast_ref, vlast_ref, plast_ref = rest[ncast + 3:ncast + 7]
    cast_dst, os_ref = rest[ncast + 7:2 * ncast + 7], rest[2 * ncast + 7]
    kt_scr, v_scr, u_scr, bias_scr, inv_scr = rest[2 * ncast + 8:]
    b = pl.program_id(0)
    s = pl.program_id(1)
    nb = tile // BLOCK

    @pl.when((b == 0) & (s == 0))
    def _():
        qi = lax.broadcasted_iota(jnp.int32, (BLOCK, 2 * BLOCK), 0)
        kc = lax.broadcasted_iota(jnp.int32, (BLOCK, 2 * BLOCK), 1)
        dist = qi + BLOCK - kc
        valid = (dist >= 0) & (dist <= WINDOW)
        valid_first = valid & (kc >= BLOCK)
        distf = dist.astype(_F32)
        for h in range(N_HEADS):
            ali = -_slope(h) * distf
            bias_scr[0, h] = jnp.where(valid, ali, NEG_INF)
            bias_scr[1, h] = jnp.where(valid_first, ali, NEG_INF)
        prow = lax.broadcasted_iota(jnp.int32, (BLOCK, POOL_GROUP_WIDTH), 0)
        for g, w in enumerate(POOL_WINDOWS):
            inv_scr[0, g] = jnp.full((BLOCK, POOL_GROUP_WIDTH), 1.0 / w, _F32)
            inv_scr[1, g] = 1.0 / jnp.minimum(prow + 1, w).astype(_F32)

    @pl.when(s == 0)
    def _():
        kt_scr[...] = jnp.zeros((KV_WIDTH, BLOCK), _BF)
        v_scr[...] = jnp.zeros((BLOCK, KV_WIDTH), _BF)
        u_scr[...] = jnp.zeros((POOL_PAD, POOL_WIDTH), _F32)

    lane = lax.broadcasted_iota(jnp.int32, (BLOCK, 2 * HEAD_DIM), 1)
    lo = lane < HEAD_DIM
    gmix = gmix_ref[...]
    zeros_kt = jnp.zeros((HEAD_DIM, 2 * BLOCK), _BF)

    def project(j):
        x = x_ref[0, j * BLOCK:(j + 1) * BLOCK, :]
        proj = _dot(_rms(x, gmix).astype(_BF), win_ref[...])
        k = proj[:, K_OFF:K_OFF + KV_WIDTH]
        v = proj[:, V_OFF:V_OFF + KV_WIDTH]
        u = proj[:, U_OFF:]
        kt = k.T
        if j == nb - 1:
            klast_ref[0] = kt
            vlast_ref[0] = v.T
            plast_ref[0] = u[BLOCK - POOL_PAD:]
        return dict(x=x, q=(proj[:, :ATTN_WIDTH] * Q_SCALE).astype(_BF), kt=kt.astype(_BF), v=v.astype(_BF), u=u)

    def pool(j, u_hist, u):
        first = ((s == 0) & (j == 0)).astype(jnp.int32) if j == 0 else 0
        ys = []
        for g, w in enumerate(POOL_WINDOWS):
            c0 = g * POOL_GROUP_WIDTH
            u_ext = jnp.concatenate([u_hist[:, c0:c0 + POOL_GROUP_WIDTH], u[:, c0:c0 + POOL_GROUP_WIDTH]], axis=0)
            sw = _window_sums(u_ext, 0)[w][POOL_PAD:]
            ys.append((sw * inv_scr[first, g] - u_ext[POOL_PAD:]).astype(_BF))
        return ys

    def kv_operands(kt_prev, v_prev, blk):
        kt2 = jnp.concatenate([kt_prev, blk["kt"]], axis=1)
        v2 = jnp.concatenate([v_prev, blk["v"]], axis=0)
        kt_pair = jnp.concatenate([jnp.concatenate([kt2[:HEAD_DIM], zeros_kt], axis=0),
                                   jnp.concatenate([zeros_kt, kt2[HEAD_DIM:]], axis=0)], axis=1)
        lane2 = lax.broadcasted_iota(jnp.int32, v2.shape, 1)
        zero = jnp.zeros_like(v2)
        v_pair = jnp.concatenate([jnp.where(lane2 < HEAD_DIM, v2, zero), jnp.where(lane2 < HEAD_DIM, zero, v2)],
                                 axis=0)
        return kt_pair, v_pair

    def wave_scores(j, blk, kt_pair, pairs):
        first = ((s == 0) & (j == 0)).astype(jnp.int32) if j == 0 else 0
        scores = []
        for p in pairs:
            sc = _dot(blk["q"][:, p * 128:(p + 1) * 128], kt_pair)
            scores.append(sc[:, :2 * BLOCK] + bias_scr[first, p])
            scores.append(sc[:, 2 * BLOCK:] + bias_scr[first, p + GROUP])
        return scores

    def wave_values(v_pair, pairs, scores):
        slabs = []
        for i, p in enumerate(pairs):
            es, invs = [], []
            for half, h in ((0, p), (1, p + GROUP)):
                e, inv = _softmax_with_sink_parts(scores[2 * i + half], sink_ref[h])
                es.append(e.astype(_BF))
                invs.append(inv)
            o = _dot(jnp.concatenate(es, axis=1), v_pair)
            slabs.append((o * jnp.where(lo, invs[0], invs[1])).astype(_BF))
        return slabs

    def output(j, blk, slabs):
        cat = jnp.concatenate(slabs, axis=1)
        x1_ref[0, j * BLOCK:(j + 1) * BLOCK, :] = blk["x"] + _dot_panels(cat, (wouta_ref, woutb_ref))

    heads = [slice(h * CROSS_HEAD_DIM, (h + 1) * CROSS_HEAD_DIM) for h in range(N_CROSS_HEADS)]

    def mem_scores(i):
        q = qc_ref[i * nseq:(i + 1) * nseq, :].astype(_BF)
        return [lax.dot_general(q[:, c], _load_mem_head(mk_ref, i, h).astype(_BF), (((1,), (1,)), ((), ())),
                                preferred_element_type=_F32) for h, c in enumerate(heads)]

    def mem_values(i, scores):
        outs = [_dot(_softmax(sc).astype(_BF), _load_mem_head(mv_ref, i, h).astype(_BF))
                for h, sc in enumerate(scores)]
        os_ref[i * nseq:(i + 1) * nseq, :] = jnp.concatenate(outs, axis=1)

    kt_prev, v_prev, u_hist = kt_scr[...], v_scr[...], u_scr[...]
    blk = project(0)
    done = None
    for j in range(nb):
        kt_pair, v_pair = kv_operands(kt_prev, v_prev, blk)
        sc0 = wave_scores(j, blk, kt_pair, (0, 1))
        nxt = project(j + 1) if j + 1 < nb else None
        msc = mem_scores(j)
        sc1 = wave_scores(j, blk, kt_pair, (2, 3))
        pooled = pool(j, u_hist, blk["u"])
        at0 = wave_values(v_pair, (0, 1), sc0)
        if done is not None:
            output(*done)
        mem_values(j, msc)
        at1 = wave_values(v_pair, (2, 3), sc1)
        done = (j, blk, at0 + at1 + pooled)
        kt_prev, v_prev, u_hist = blk["kt"], blk["v"], blk["u"][BLOCK - POOL_PAD:]
        blk = nxt
    output(*done)
    kt_scr[...] = kt_prev
    v_scr[...] = v_prev
    u_scr[...] = u_hist

    for src, dst in zip(cast_src, cast_dst):
        dst[...] = src[...].astype(_BF)


def _prompt_mixer(x, gmix, win, sinks, wout, tail_weights, qc, mk, mv, tile):
    B, S, D = x.shape
    ns = S // tile
    nsteps = B * ns
    nmem = tile // BLOCK
    nseq = qc.shape[0] // mk.shape[0]
    assert mk.shape[0] == nsteps * nmem
    const = lambda *shape: pl.BlockSpec(shape, lambda b, s: (0,) * len(shape))
    chunk = lambda w: pl.BlockSpec((w.shape[0] // nsteps, w.shape[1]), lambda b, s: (b * ns + s, 0))
    assert all(w.shape[0] % (16 * nsteps) == 0 for w in tail_weights)
    mem_rows = pl.BlockSpec((nmem * nseq, D), lambda b, s: (b * ns + s, 0))
    mem_cache = pl.BlockSpec((nmem, N_MEM * MEM_ROWS, 128), lambda b, s: (b * ns + s, 0, 0))
    return pl.pallas_call(
        functools.partial(_prompt_mixer_kernel, tile=tile, nseq=nseq, ncast=len(tail_weights)),
        grid=(B, ns),
        in_specs=[
            pl.BlockSpec((1, tile, D), lambda b, s: (b, s, 0)),
            const(1, D),
            const(D, IN_WIDTH),
            pl.BlockSpec(memory_space=pltpu.SMEM),
            const(D, WOUT_SPLIT),
            const(D, D - WOUT_SPLIT),
        ] + [chunk(w) for w in tail_weights] + [mem_rows, mem_cache, mem_cache],
        out_specs=[
            pl.BlockSpec((1, tile, D), lambda b, s: (b, s, 0)),
            pl.BlockSpec((1, BLOCK, KV_WIDTH), lambda b, s: (b, 0, 0)),
            pl.BlockSpec((1, BLOCK, KV_WIDTH), lambda b, s: (b, 0, 0)),
            pl.BlockSpec((1, POOL_PAD, POOL_WIDTH), lambda b, s: (b, 0, 0)),
        ] + [chunk(w) for w in tail_weights] + [mem_rows],
        out_shape=[
            jax.ShapeDtypeStruct((B, S, D), _F32),
            jax.ShapeDtypeStruct((B, BLOCK, KV_WIDTH), _F32),
            jax.ShapeDtypeStruct((B, BLOCK, KV_WIDTH), _F32),
            jax.ShapeDtypeStruct((B, POOL_PAD, POOL_WIDTH), _F32),
        ] + [jax.ShapeDtypeStruct(w.shape, _BF) for w in tail_weights] + [jax.ShapeDtypeStruct(qc.shape, _F32)],
        scratch_shapes=[
            pltpu.VMEM((KV_WIDTH, BLOCK), _BF),
            pltpu.VMEM((BLOCK, KV_WIDTH), _BF),
            pltpu.VMEM((POOL_PAD, POOL_WIDTH), _F32),
            pltpu.VMEM((2, N_HEADS, BLOCK, 2 * BLOCK), _F32),
            pltpu.VMEM((2, len(POOL_WINDOWS), BLOCK, POOL_GROUP_WIDTH), _F32),
        ],
        compiler_params=pltpu.CompilerParams(
            dimension_semantics=("arbitrary", "arbitrary"), vmem_limit_bytes=VMEM_LIMIT_BYTES),
        name="prompt_mixer",
    )(x, gmix, win, sinks, *wout, *tail_weights, qc, mk, mv)


def _mem_kv_kernel(mem_ref, gmem_ref, wck_ref, wcv_ref, wcq_ref, wco_ref, k_ref, v_ref, kq_ref, vo_ref, *, nbatch):
    gmem = gmem_ref[...]

    def stream(i):
        hm = _rms(mem_ref[i], gmem).astype(_BF)
        k = _dot(hm, wck_ref[...])
        yield
        v = _dot(hm, wcv_ref[...])
        yield
        for h in range(N_CROSS_HEADS):
            head = slice(h * CROSS_HEAD_DIM, (h + 1) * CROSS_HEAD_DIM)
            kq_ref[i, :, head] = (_dot(wcq_ref[:, head], k[:, head].T.astype(_BF)) * CQ_SCALE).astype(_BF)
            vo_ref[i, head, :] = _dot(v[:, head].astype(_BF), wco_ref[head, :]).astype(_BF)
            for half in range(CROSS_HEAD_DIM // 128):
                c0 = h * CROSS_HEAD_DIM + half * 128
                rows = pl.ds(half * N_CROSS_HEADS + h, N_MEM, stride=MEM_ROWS)
                k_ref[i, rows, :] = k[:, c0:c0 + 128]
                v_ref[i, rows, :] = v[:, c0:c0 + 128]
            yield

    _interleave([stream(i) for i in range(nbatch)], MEM_KV_SKEW)


def _mem_kv(mem, gmem, wck, wcv, wcq, wco, nbatch):
    B, M, D = mem.shape
    HM = N_CROSS_HEADS * M
    const = lambda *shape: pl.BlockSpec(shape, lambda b: (0,) * len(shape))
    per_seq = lambda *shape: pl.BlockSpec((nbatch,) + shape, lambda b: (b, 0, 0))
    return pl.pallas_call(
        functools.partial(_mem_kv_kernel, nbatch=nbatch),
        grid=(B // nbatch,),
        in_specs=[per_seq(M, D), const(1, D), const(D, D), const(D, D), const(D, D), const(D, D)],
        out_specs=[per_seq(M * MEM_ROWS, 128), per_seq(M * MEM_ROWS, 128), per_seq(D, HM), per_seq(HM, D)],
        out_shape=[
            jax.ShapeDtypeStruct((B, M * MEM_ROWS, 128), _F32),
            jax.ShapeDtypeStruct((B, M * MEM_ROWS, 128), _F32),
            jax.ShapeDtypeStruct((B, D, HM), _BF),
            jax.ShapeDtypeStruct((B, HM, D), _BF),
        ],
        compiler_params=pltpu.CompilerParams(
            dimension_semantics=("arbitrary",), vmem_limit_bytes=VMEM_LIMIT_BYTES),
        name="prompt_mem_kv",
    )(mem, gmem, wck, wcv, wcq, wco)


def _mem_cache_unrows(rows):
    nb = rows.shape[0]
    c = rows.reshape(nb, N_MEM, CROSS_HEAD_DIM // 128, N_CROSS_HEADS, 128)
    return c.transpose(0, 1, 3, 2, 4).reshape(nb, N_MEM, N_CROSS_HEADS, CROSS_HEAD_DIM)


def _prompt_tail_kernel(x1_ref, gcross_ref, kq_ref, vo_ref, gffn_ref, wup_ref, wdown_ref, gfinal_ref, y_ref,
                        *, tile, sub):
    gcross, gffn, gfinal = gcross_ref[...], gffn_ref[...], gfinal_ref[...]

    def stream(r0):
        x1 = x1_ref[0, r0:r0 + sub, :]
        hn = _rms(x1, gcross).astype(_BF)
        yield
        scores = _dot(hn, kq_ref[0])
        yield
        probs = [_softmax(scores[:, h * N_MEM:(h + 1) * N_MEM]).astype(_BF) for h in range(N_CROSS_HEADS)]
        x2 = x1 + _dot(jnp.concatenate(probs, axis=1), vo_ref[0])
        yield

        def store(y):
            y_ref[0, r0:r0 + sub, :] = y

        yield from _ffn_final_stages(x2, gffn, wup_ref, wdown_ref, gfinal, store)

    _interleave([stream(r0) for r0 in range(0, tile, sub)], TAIL_SKEW)


def _single(shape, index_map):
    return pl.BlockSpec(shape, index_map, pipeline_mode=pl.Buffered(1))


def _prompt_tail(x1, gcross, kq, vo, gffn, wup, wdown, gfinal, tile, sub):
    B, S, D = x1.shape
    const = lambda *shape: _single(shape, lambda b, s: (0,) * len(shape))
    return pl.pallas_call(
        functools.partial(_prompt_tail_kernel, tile=tile, sub=sub),
        grid=(B, S // tile),
        in_specs=[
            pl.BlockSpec((1, tile, D), lambda b, s: (b, s, 0)),
            const(1, D),
            pl.BlockSpec((1, D, N_CROSS_HEADS * N_MEM), lambda b, s: (b, 0, 0)),
            pl.BlockSpec((1, N_CROSS_HEADS * N_MEM, D), lambda b, s: (b, 0, 0)),
            const(1, D),
            const(D, D_FF),
            const(D_FF, D),
            const(1, D),
        ],
        out_specs=pl.BlockSpec((1, tile, D), lambda b, s: (b, s, 0)),
        out_shape=jax.ShapeDtypeStruct((B, S, D), _F32),
        compiler_params=pltpu.CompilerParams(
            dimension_semantics=("arbitrary", "arbitrary"), vmem_limit_bytes=VMEM_LIMIT_BYTES),
        name="prompt_tail",
    )(x1, gcross, kq, vo, gffn, wup, wdown, gfinal)


def _sample_mixer_kernel(x_ref, gmix_ref, win_ref, sinkcol_ref, ck_ref, cv_ref, st_ref,
                         wouta_ref, woutb_ref, gcross_ref, wcq_ref,
                         x1_ref, qc_ref, wk_ref, wv_ref, pool_ref, wcq_bf_ref,
                         kc_scr, vc_scr, u_scr, *, nbatch, sub, nseq, past_len):
    @pl.when(pl.program_id(0) == 0)
    def _():
        wcq_bf_ref[...] = wcq_ref[...].astype(_BF)

    rows = sub * nseq
    nkeys = 2 * WINDOW
    gmix, gcross = gmix_ref[...], gcross_ref[...]
    sinkcol = sinkcol_ref[...][None]

    t = lax.broadcasted_iota(jnp.int32, (nseq, nkeys), 0)
    c = lax.broadcasted_iota(jnp.int32, (nseq, nkeys), 1)
    dist = jnp.where(c < WINDOW, WINDOW + t - c, t - (c - WINDOW))
    valid = (dist >= 0) & (dist <= WINDOW) & (c < WINDOW + nseq)
    distf = dist.astype(_F32)
    bias = jnp.concatenate([jnp.where(valid, -_slope(h) * distf, NEG_INF) for h in range(N_HEADS)], axis=0)[None]
    lane = lax.broadcasted_iota(jnp.int32, (rows, 2 * HEAD_DIM), 1)
    lo = lane < HEAD_DIM
    tpos = past_len + lax.broadcasted_iota(jnp.int32, (nseq, POOL_GROUP_WIDTH), 0)
    zpad = jnp.zeros((sub, nkeys - WINDOW, KV_WIDTH), _BF)

    def stream(b0):
        bs = slice(b0, b0 + sub)
        rs = slice(b0 * nseq, (b0 + sub) * nseq)
        x = x_ref[rs, :]
        proj = _dot(_rms(x, gmix).astype(_BF), win_ref[...])
        yield
        k_new = proj[:, K_OFF:K_OFF + KV_WIDTH].reshape(sub, nseq, KV_WIDTH)
        v_new = proj[:, V_OFF:V_OFF + KV_WIDTH].reshape(sub, nseq, KV_WIDTH)
        ck = jnp.swapaxes(ck_ref[bs], 1, 2)
        cv = jnp.swapaxes(cv_ref[bs], 1, 2)
        wk_ref[bs, :WINDOW - nseq, :] = ck[:, nseq:, :]
        wk_ref[bs, WINDOW - nseq:, :] = k_new
        wv_ref[bs, :WINDOW - nseq, :] = cv[:, nseq:, :]
        wv_ref[bs, WINDOW - nseq:, :] = v_new
        kc_scr[bs, WINDOW:, :] = zpad
        vc_scr[bs, WINDOW:, :] = zpad
        kc_scr[bs, :WINDOW, :] = ck.astype(_BF)
        vc_scr[bs, :WINDOW, :] = cv.astype(_BF)
        kc_scr[bs, WINDOW:WINDOW + nseq, :] = k_new.astype(_BF)
        vc_scr[bs, WINDOW:WINDOW + nseq, :] = v_new.astype(_BF)
        qsc = proj[:, :ATTN_WIDTH] * Q_SCALE
        q_lo, q_hi = [], []
        for p in range(GROUP):
            slab = qsc[:, p * 128:(p + 1) * 128]
            q_lo.append(jnp.where(lo, slab, 0.0).reshape(sub, nseq, 128))
            q_hi.append(jnp.where(lo, 0.0, slab).reshape(sub, nseq, 128))
        qm = jnp.concatenate(q_lo + q_hi, axis=1).astype(_BF)
        sc = jnp.einsum('bqd,bkd->bqk', qm, kc_scr[bs], preferred_element_type=_F32) + bias
        yield
        pr = _softmax_with_sink(sc, sinkcol).astype(_BF)
        o = jnp.einsum('bqk,bkd->bqd', pr, vc_scr[bs], preferred_element_type=_F32)
        yield
        attn = []
        for p in range(GROUP):
            o_lo = o[:, p * nseq:(p + 1) * nseq, :].reshape(rows, 128)
            o_hi = o[:, (p + GROUP) * nseq:(p + GROUP + 1) * nseq, :].reshape(rows, 128)
            attn.append(jnp.where(lo, o_lo, o_hi).astype(_BF))
        ext = POOL_PAD + nseq
        seq_rows = lambda r: pl.ds(b0 * ext + r, sub, stride=ext)
        pooled = []
        for g, w in enumerate(POOL_WINDOWS):
            c0 = g * POOL_GROUP_WIDTH
            u_scr[g, seq_rows(0), :] = jnp.zeros((sub, POOL_GROUP_WIDTH), _F32)
            for r in range(POOL_HIST):
                u_scr[g, seq_rows(POOL_PAD - POOL_HIST + r), :] = st_ref[r, bs, c0:c0 + POOL_GROUP_WIDTH]
            for i in range(sub):
                u_scr[g, (b0 + i) * ext + POOL_PAD:(b0 + i + 1) * ext, :] = (
                    proj[i * nseq:(i + 1) * nseq, U_OFF + c0:U_OFF + c0 + POOL_GROUP_WIDTH])
            for r in range(POOL_HIST):
                pool_ref[r, bs, c0:c0 + POOL_GROUP_WIDTH] = u_scr[g, seq_rows(ext - POOL_HIST + r), :]
            ug = u_scr[g, b0 * ext:(b0 + sub) * ext, :].reshape(sub, ext, POOL_GROUP_WIDTH)
            sw = _window_sums(ug, 1)[w][:, POOL_PAD:, :]
            cnt = jnp.minimum(tpos + 1, w).astype(_F32)
            pooled.append((sw / cnt[None] - ug[:, POOL_PAD:, :]).reshape(rows, POOL_GROUP_WIDTH).astype(_BF))
        yield
        x1 = x + _dot_panels(jnp.concatenate(attn + pooled, axis=1), (wouta_ref, woutb_ref))
        x1_ref[rs, :] = x1
        yield
        qc_ref[rs, :] = _dot(_rms(x1, gcross).astype(_BF), wcq_bf_ref[...]) * CQ_SCALE

    _interleave([stream(b0) for b0 in range(0, nbatch, sub)], SAMPLE_MIXER_SKEW)


def _sample_mixer(x, gmix, win, sinkcol, ck, cv, st, wout, gcross, wcq, nbatch, sub, nseq,
                  past_len):
    R, D = x.shape
    rows = nbatch * nseq
    const = lambda *shape: pl.BlockSpec(shape, lambda i: (0,) * len(shape))
    return pl.pallas_call(
        functools.partial(_sample_mixer_kernel, nbatch=nbatch, sub=sub, nseq=nseq, past_len=past_len),
        grid=(R // rows,),
        in_specs=[
            pl.BlockSpec((rows, D), lambda i: (i, 0)),
            const(1, D),
            const(D, IN_WIDTH),
            const(N_HEADS * nseq, 1),
            pl.BlockSpec((nbatch, WINDOW, KV_WIDTH), lambda i: (i, 0, 0)),
            pl.BlockSpec((nbatch, WINDOW, KV_WIDTH), lambda i: (i, 0, 0)),
            pl.BlockSpec((POOL_HIST, nbatch, POOL_WIDTH), lambda i: (0, i, 0)),
            const(D, WOUT_SPLIT),
            const(D, D - WOUT_SPLIT),
            const(1, D),
            const(D, D),
        ],
        out_specs=[
            pl.BlockSpec((rows, D), lambda i: (i, 0)),
            pl.BlockSpec((rows, D), lambda i: (i, 0)),
            pl.BlockSpec((nbatch, WINDOW, KV_WIDTH), lambda i: (i, 0, 0)),
            pl.BlockSpec((nbatch, WINDOW, KV_WIDTH), lambda i: (i, 0, 0)),
            pl.BlockSpec((POOL_HIST, nbatch, POOL_WIDTH), lambda i: (0, i, 0)),
            const(D, D),
        ],
        out_shape=[
            jax.ShapeDtypeStruct((R, D), _F32),
            jax.ShapeDtypeStruct((R, D), _F32),
            jax.ShapeDtypeStruct((R // nseq, WINDOW, KV_WIDTH), _F32),
            jax.ShapeDtypeStruct((R // nseq, WINDOW, KV_WIDTH), _F32),
            jax.ShapeDtypeStruct((POOL_HIST, R // nseq, POOL_WIDTH), _F32),
            jax.ShapeDtypeStruct((D, D), _BF),
        ],
        scratch_shapes=[
            pltpu.VMEM((nbatch, 2 * WINDOW, KV_WIDTH), _BF),
            pltpu.VMEM((nbatch, 2 * WINDOW, KV_WIDTH), _BF),
            pltpu.VMEM((len(POOL_WINDOWS), nbatch * (POOL_PAD + nseq), POOL_GROUP_WIDTH), _F32),
        ],
        compiler_params=pltpu.CompilerParams(
            dimension_semantics=("arbitrary",), vmem_limit_bytes=VMEM_LIMIT_BYTES),
        name="sample_mixer",
    )(x, gmix, win, sinkcol, ck, cv, st, *wout, gcross, wcq)


def _mem_cache_rows(cache):
    nb = cache.shape[0]
    c = cache.reshape(nb, N_MEM, N_CROSS_HEADS, CROSS_HEAD_DIM // 128, 128)
    return c.transpose(0, 1, 3, 2, 4).reshape(nb, N_MEM * MEM_ROWS, 128)


def _load_mem_head(ref, b, h):
    halves = [ref[b, pl.ds(half * N_CROSS_HEADS + h, N_MEM, stride=MEM_ROWS), :]
              for half in range(CROSS_HEAD_DIM // 128)]
    return jnp.concatenate(halves, axis=1)


def _sample_tail_kernel(x1_ref, o_ref, wco_ref, gffn_ref, wup_ref, wdown_ref, gfinal_ref, y_ref, *, tile, sub):
    gffn, gfinal = gffn_ref[...], gfinal_ref[...]

    def stream(r0):
        x2 = x1_ref[r0:r0 + sub, :] + _dot(o_ref[r0:r0 + sub, :].astype(_BF), wco_ref[...])
        yield

        def store(y):
            y_ref[r0:r0 + sub, :] = y

        yield from _ffn_final_stages(x2, gffn, wup_ref, wdown_ref, gfinal, store)

    _interleave([stream(r0) for r0 in range(0, tile, sub)], TAIL_SKEW)


def _sample_tail(x1, o, wco, gffn, wup, wdown, gfinal, tile, sub):
    R, D = x1.shape
    const = lambda *shape: _single(shape, lambda i: (0,) * len(shape))
    return pl.pallas_call(
        functools.partial(_sample_tail_kernel, tile=tile, sub=sub),
        grid=(R // tile,),
        in_specs=[
            pl.BlockSpec((tile, D), lambda i: (i, 0)),
            pl.BlockSpec((tile, D), lambda i: (i, 0)),
            const(D, D),
            const(1, D),
            const(D, D_FF),
            const(D_FF, D),
            const(1, D),
        ],
        out_specs=pl.BlockSpec((tile, D), lambda i: (i, 0)),
        out_shape=jax.ShapeDtypeStruct((R, D), _F32),
        compiler_params=pltpu.CompilerParams(
            dimension_semantics=("arbitrary",), vmem_limit_bytes=VMEM_LIMIT_BYTES),
        name="sample_tail",
    )(x1, o, wco, gffn, wup, wdown, gfinal)


PROMPT_TILE = 512
TAIL_TILE = 1024
TAIL_SUB = 256
TAIL_SKEW = 4
MEM_KV_BATCH = 2
MEM_KV_SKEW = 2
SAMPLE_MIXER_BATCH = 32
SAMPLE_MIXER_SUB = 16
SAMPLE_MIXER_SKEW = 3


def kernel(x_prompt, x_sample, cache_win_k, cache_win_v, state_pool, cache_mem_k, cache_mem_v, mem_prompt,
           g_mix, w_in, attn_sinks, w_pool, pool_scale, w_out, g_cross, g_mem, w_cq, w_ck, w_cv, w_co,
           g_ffn, w_up, w_down, g_final):
    depth = g_mix.shape[0]
    assert depth == 1, "one layer per step"
    B, S, D = x_prompt.shape
    DB, T, _ = x_sample.shape
    past_len = PAST_LEN
    l = 0

    win = jnp.concatenate([_pair_heads(w_in[l][:, :ATTN_WIDTH], 1), w_in[l][:, ATTN_WIDTH:]], axis=1).astype(_BF)
    wout = _prep_wout(w_out[l], w_pool[l], pool_scale[l].reshape(1, POOL_WIDTH))
    gmix, gcross, gmem, gffn = (g[l].reshape(1, D) for g in (g_mix, g_cross, g_mem, g_ffn))
    gfinal = g_final.reshape(1, D)
    sinks = attn_sinks[l]

    xs = x_sample.reshape(DB * T, D)
    ck = cache_win_k[l].transpose(0, 2, 3, 1).reshape(DB, KV_WIDTH, WINDOW)
    cv = cache_win_v[l].transpose(0, 2, 3, 1).reshape(DB, KV_WIDTH, WINDOW)
    st = state_pool[l].transpose(1, 0, 2)
    sinkcol = jnp.repeat(sinks, T).reshape(N_HEADS * T, 1)
    x1s, qc, wk_s, wv_s, pool_s, wcq = _sample_mixer(xs, gmix, win, sinkcol, ck, cv, st, wout,
                                                gcross, w_cq[l], SAMPLE_MIXER_BATCH, SAMPLE_MIXER_SUB, T, past_len)

    mk = _mem_cache_rows(cache_mem_k[l])
    mv = _mem_cache_rows(cache_mem_v[l])
    x1p, klast, vlast, plast, wco, wup, wdown, wck, wcv, o_s = _prompt_mixer(
        x_prompt, gmix, win, sinks, wout, (w_co[l], w_up[l], w_down[l], w_ck[l], w_cv[l]), qc, mk, mv, PROMPT_TILE)
    mem_k, mem_v, mem_kq, mem_vo = _mem_kv(mem_prompt, gmem, wck, wcv, wcq, wco, MEM_KV_BATCH)
    y_prompt = _prompt_tail(x1p, gcross, mem_kq, mem_vo, gffn, wup, wdown, gfinal, TAIL_TILE, TAIL_SUB)

    y_sample = _sample_tail(x1s, o_s, wco, gffn, wup, wdown, gfinal, TAIL_TILE, TAIL_SUB).reshape(DB, T, D)

    return (
        y_prompt,
        y_sample,
        klast.reshape(B, N_KV_HEADS, HEAD_DIM, WINDOW).transpose(0, 3, 1, 2)[None],
        vlast.reshape(B, N_KV_HEADS, HEAD_DIM, WINDOW).transpose(0, 3, 1, 2)[None],
        plast[:, POOL_PAD - POOL_HIST:, :][None],
        _mem_cache_unrows(mem_k)[None],
        _mem_cache_unrows(mem_v)[None],
        wk_s.reshape(1, DB, WINDOW, N_KV_HEADS, HEAD_DIM),
        wv_s.reshape(1, DB, WINDOW, N_KV_HEADS, HEAD_DIM),
        pool_s.transpose(1, 0, 2)[None],
    )
```

```python
import functools

import jax
import jax.numpy as jnp
from jax import lax
from jax.experimental import pallas as pl
from jax.experimental.pallas import tpu as pltpu

D_MODEL = 1024
PAST_LEN = 16384
HEAD_DIM = 64
N_HEADS = 8
N_KV_HEADS = 2
GROUP = N_HEADS // N_KV_HEADS
ATTN_WIDTH = N_HEADS * HEAD_DIM
KV_WIDTH = N_KV_HEADS * HEAD_DIM
WINDOW = 128
BLOCK = WINDOW
POOL_WIDTH = D_MODEL - ATTN_WIDTH
POOL_WINDOWS = (2, 4, 8, 16)
POOL_GROUP_WIDTH = 128
POOL_HIST = 15
POOL_PAD = 16
IN_WIDTH = ATTN_WIDTH + 2 * KV_WIDTH + POOL_WIDTH
N_MEM = 256
N_CROSS_HEADS = 4
CROSS_HEAD_DIM = 256
MEM_ROWS = N_CROSS_HEADS * (CROSS_HEAD_DIM // 128)
D_FF = 4 * D_MODEL
FF_CHUNK = 1024
RMS_EPS = 1e-5
NEG_INF = -1e30
Q_SCALE = HEAD_DIM ** -0.5
CQ_SCALE = CROSS_HEAD_DIM ** -0.5
LOG2E = 1.4426950408889634
K_OFF = ATTN_WIDTH
V_OFF = ATTN_WIDTH + KV_WIDTH
U_OFF = ATTN_WIDTH + 2 * KV_WIDTH


def _pair_heads(w, axis):
    shape = w.shape
    split = shape[:axis] + (N_KV_HEADS, GROUP, HEAD_DIM) + shape[axis + 1:]
    return jnp.swapaxes(w.reshape(split), axis, axis + 1).reshape(shape)


WOUT_SPLIT = 768

VMEM_LIMIT_BYTES = 56 * 1024 * 1024

_BF = jnp.bfloat16
_F32 = jnp.float32


def _slope(h):
    return 2.0 ** (-8.0 * (h + 1) / N_HEADS)


def _dot(a, b):
    return jnp.dot(a, b, preferred_element_type=_F32)


def _dot_panels(a, panel_refs):
    return jnp.concatenate([_dot(a, ref[...]) for ref in panel_refs], axis=1)


def _rms(x, g):
    ms = jnp.mean(x * x, axis=-1, keepdims=True)
    return x * lax.rsqrt(ms + RMS_EPS) * g


def _softmax_with_sink(s, sink):
    m = jnp.maximum(jnp.max(s, axis=-1, keepdims=True), sink)
    e = jnp.exp(s - m)
    den = jnp.sum(e, axis=-1, keepdims=True) + jnp.exp(sink - m)
    return e * (1.0 / den)


def _softmax2_with_sink_parts(s, sink):
    m = jnp.maximum(jnp.max(s, axis=-1, keepdims=True), sink)
    e = jnp.exp2(s - m)
    den = jnp.sum(e, axis=-1, keepdims=True) + jnp.exp2(sink - m)
    return e, 1.0 / den


def _softmax2(s):
    m = jnp.max(s, axis=-1, keepdims=True)
    e = jnp.exp2(s - m)
    return e * (1.0 / jnp.sum(e, axis=-1, keepdims=True))


def _softmax(s):
    m = jnp.max(s, axis=-1, keepdims=True)
    e = jnp.exp(s - m)
    return e * (1.0 / jnp.sum(e, axis=-1, keepdims=True))


def _window_sums(u_ext, axis):
    out = {}
    s = u_ext
    w = 1
    while w < max(POOL_WINDOWS):
        s = s + pltpu.roll(s, w, axis)
        w *= 2
        out[w] = s
    return out


def _interleave(streams, skew):
    pending = list(streams)
    live = []
    rnd = 0
    while live or pending:
        if pending and rnd % skew == 0:
            live.append(pending.pop(0))
        for g in list(live):
            try:
                next(g)
            except StopIteration:
                live.remove(g)
        rnd += 1


def _ffn_final_stages(x2, gffn, wup_ref, wdown_ref, gfinal, store):
    hn = _rms(x2, gffn).astype(_BF)
    yield
    acc = x2
    for c in range(D_FF // FF_CHUNK):
        hc = _dot(hn, wup_ref[:, c * FF_CHUNK:(c + 1) * FF_CHUNK])
        yield
        hc = jnp.maximum(hc, 0.0)
        hc = (hc * hc).astype(_BF)
        acc = acc + _dot(hc, wdown_ref[c * FF_CHUNK:(c + 1) * FF_CHUNK, :])
        yield
    store(_rms(acc, gfinal))


def _prep_wout_kernel(wout_ref, wpool_ref, pscale_ref, outa_ref, outb_ref):
    blocks = []
    for g in range(GROUP):
        for kv in range(N_KV_HEADS):
            r0 = (kv * GROUP + g) * HEAD_DIM
            blocks.append(wout_ref[r0:r0 + HEAD_DIM, :])
    for g in range(len(POOL_WINDOWS)):
        r0 = ATTN_WIDTH + g * POOL_GROUP_WIDTH
        scaled = wpool_ref[g] * pscale_ref[:, g * POOL_GROUP_WIDTH:(g + 1) * POOL_GROUP_WIDTH]
        blocks.append(jnp.dot(scaled, wout_ref[r0:r0 + POOL_GROUP_WIDTH, :], precision=lax.Precision.HIGHEST,
                              preferred_element_type=_F32))
    full = jnp.concatenate(blocks, axis=0).astype(_BF)
    outa_ref[...] = full[:, :WOUT_SPLIT]
    outb_ref[...] = full[:, WOUT_SPLIT:]


def _prep_wout(wout, wpool, pscale):
    D = wout.shape[0]
    return pl.pallas_call(
        _prep_wout_kernel,
        out_shape=[jax.ShapeDtypeStruct((D, WOUT_SPLIT), _BF), jax.ShapeDtypeStruct((D, D - WOUT_SPLIT), _BF)],
        compiler_params=pltpu.CompilerParams(vmem_limit_bytes=VMEM_LIMIT_BYTES),
        name="prep_wout",
    )(wout, wpool, pscale)


def _prompt_mixer_kernel(x_ref, gmix_ref, win_ref, sink_ref, wouta_ref, woutb_ref, *rest, tile, nseq, ncast):
    cast_src, (qc_ref, mk_ref, mv_ref) = rest[:ncast], rest[ncast:ncast + 3]
    x1_ref, klast_ref, vlast_ref, plast_ref = rest[ncast + 3:ncast + 7]
    cast_dst, os_ref = rest[ncast + 7:2 * ncast + 7], rest[2 * ncast + 7]
    kt_scr, v_scr, u_scr, bias_scr, inv_scr = rest[2 * ncast + 8:]
    b = pl.program_id(0)
    s = pl.program_id(1)
    nb = tile // BLOCK

    @pl.when((b == 0) & (s == 0))
    def _():
        qi = lax.broadcasted_iota(jnp.int32, (BLOCK, 2 * BLOCK), 0)
        kc = lax.broadcasted_iota(jnp.int32, (BLOCK, 2 * BLOCK), 1)
        dist = qi + BLOCK - kc
        valid = (dist >= 0) & (dist <= WINDOW)
        valid_first = valid & (kc >= BLOCK)
        distf = dist.astype(_F32)
        for h in range(N_HEADS):
            ali = (-_slope(h) * LOG2E) * distf
            bias_scr[0, h] = jnp.where(valid, ali, NEG_INF)
            bias_scr[1, h] = jnp.where(valid_first, ali, NEG_INF)
        prow = lax.broadcasted_iota(jnp.int32, (BLOCK, POOL_GROUP_WIDTH), 0)
        for g, w in enumerate(POOL_WINDOWS):
            inv_scr[0, g] = jnp.full((BLOCK, POOL_GROUP_WIDTH), 1.0 / w, _F32)
            inv_scr[1, g] = 1.0 / jnp.minimum(prow + 1, w).astype(_F32)

    @pl.when(s == 0)
    def _():
        kt_scr[...] = jnp.zeros((KV_WIDTH, BLOCK), _BF)
        v_scr[...] = jnp.zeros((BLOCK, KV_WIDTH), _BF)
        u_scr[...] = jnp.zeros((POOL_PAD, POOL_WIDTH), _F32)

    lane = lax.broadcasted_iota(jnp.int32, (BLOCK, 2 * HEAD_DIM), 1)
    lo = lane < HEAD_DIM
    gmix = gmix_ref[...]
    zeros_kt = jnp.zeros((HEAD_DIM, 2 * BLOCK), _BF)

    def project(j):
        x = x_ref[0, j * BLOCK:(j + 1) * BLOCK, :]
        proj = _dot(_rms(x, gmix).astype(_BF), win_ref[...])
        k = proj[:, K_OFF:K_OFF + KV_WIDTH]
        v = proj[:, V_OFF:V_OFF + KV_WIDTH]
        u = proj[:, U_OFF:]
        kt = k.T
        if j == nb - 1:
            klast_ref[0] = kt
            vlast_ref[0] = v.T
            plast_ref[0] = u[BLOCK - POOL_PAD:]
        q = (proj[:, :ATTN_WIDTH] * (Q_SCALE * LOG2E)).astype(_BF)
        return dict(x=x, q=q, kt=kt.astype(_BF), v=v.astype(_BF), u=u)

    def pool(j, u_hist, u):
        first = ((s == 0) & (j == 0)).astype(jnp.int32) if j == 0 else 0
        ys = []
        for g, w in enumerate(POOL_WINDOWS):
            c0 = g * POOL_GROUP_WIDTH
            u_ext = jnp.concatenate([u_hist[:, c0:c0 + POOL_GROUP_WIDTH], u[:, c0:c0 + POOL_GROUP_WIDTH]], axis=0)
            sw = _window_sums(u_ext, 0)[w][POOL_PAD:]
            ys.append((sw * inv_scr[first, g] - u_ext[POOL_PAD:]).astype(_BF))
        return ys

    def kv_operands(kt_prev, v_prev, blk):
        kt2 = jnp.concatenate([kt_prev, blk["kt"]], axis=1)
        v2 = jnp.concatenate([v_prev, blk["v"]], axis=0)
        kt_pair = jnp.concatenate([jnp.concatenate([kt2[:HEAD_DIM], zeros_kt], axis=0),
                                   jnp.concatenate([zeros_kt, kt2[HEAD_DIM:]], axis=0)], axis=1)
        lane2 = lax.broadcasted_iota(jnp.int32, v2.shape, 1)
        zero = jnp.zeros_like(v2)
        v_pair = jnp.concatenate([jnp.where(lane2 < HEAD_DIM, v2, zero), jnp.where(lane2 < HEAD_DIM, zero, v2)],
                                 axis=0)
        return kt_pair, v_pair

    def wave_scores(j, blk, kt_pair, pairs):
        first = ((s == 0) & (j == 0)).astype(jnp.int32) if j == 0 else 0
        scores = []
        for p in pairs:
            sc = _dot(blk["q"][:, p * 128:(p + 1) * 128], kt_pair)
            scores.append(sc[:, :2 * BLOCK] + bias_scr[first, p])
            scores.append(sc[:, 2 * BLOCK:] + bias_scr[first, p + GROUP])
        return scores

    def wave_values(v_pair, pairs, scores):
        slabs = []
        for i, p in enumerate(pairs):
            es, invs = [], []
            for half, h in ((0, p), (1, p + GROUP)):
                e, inv = _softmax2_with_sink_parts(scores[2 * i + half], sink_ref[h] * LOG2E)
                es.append(e.astype(_BF))
                invs.append(inv)
            o = _dot(jnp.concatenate(es, axis=1), v_pair)
            slabs.append((o * jnp.where(lo, invs[0], invs[1])).astype(_BF))
        return slabs

    def output(j, blk, slabs):
        cat = jnp.concatenate(slabs, axis=1)
        x1_ref[0, j * BLOCK:(j + 1) * BLOCK, :] = blk["x"] + _dot_panels(cat, (wouta_ref, woutb_ref))

    heads = [slice(h * CROSS_HEAD_DIM, (h + 1) * CROSS_HEAD_DIM) for h in range(N_CROSS_HEADS)]

    def mem_scores(i):
        q = qc_ref[i * nseq:(i + 1) * nseq, :].astype(_BF)
        return [lax.dot_general(q[:, c], _load_mem_head(mk_ref, i, h).astype(_BF), (((1,), (1,)), ((), ())),
                                preferred_element_type=_F32) for h, c in enumerate(heads)]

    def mem_values(i, scores):
        outs = [_dot(_softmax(sc).astype(_BF), _load_mem_head(mv_ref, i, h).astype(_BF))
                for h, sc in enumerate(scores)]
        os_ref[i * nseq:(i + 1) * nseq, :] = jnp.concatenate(outs, axis=1)

    kt_prev, v_prev, u_hist = kt_scr[...], v_scr[...], u_scr[...]
    blk = project(0)
    done = None
    for j in range(nb):
        kt_pair, v_pair = kv_operands(kt_prev, v_prev, blk)
        sc0 = wave_scores(j, blk, kt_pair, (0, 1))
        nxt = project(j + 1) if j + 1 < nb else None
        msc = mem_scores(j)
        sc1 = wave_scores(j, blk, kt_pair, (2, 3))
        pooled = pool(j, u_hist, blk["u"])
        at0 = wave_values(v_pair, (0, 1), sc0)
        if done is not None:
            output(*done)
        mem_values(j, msc)
        at1 = wave_values(v_pair, (2, 3), sc1)
        done = (j, blk, at0 + at1 + pooled)
        kt_prev, v_prev, u_hist = blk["kt"], blk["v"], blk["u"][BLOCK - POOL_PAD:]
        blk = nxt
    output(*done)
    kt_scr[...] = kt_prev
    v_scr[...] = v_prev
    u_scr[...] = u_hist

    for src, dst in zip(cast_src, cast_dst):
        dst[...] = src[...].astype(_BF)


def _prompt_mixer(x, gmix, win, sinks, wout, tail_weights, qc, mk, mv, tile):
    B, S, D = x.shape
    ns = S // tile
    nsteps = B * ns
    nmem = tile // BLOCK
    nseq = qc.shape[0] // mk.shape[0]
    assert mk.shape[0] == nsteps * nmem
    const = lambda *shape: pl.BlockSpec(shape, lambda b, s: (0,) * len(shape))
    chunk = lambda w: pl.BlockSpec((w.shape[0] // nsteps, w.shape[1]), lambda b, s: (b * ns + s, 0))
    assert all(w.shape[0] % (16 * nsteps) == 0 for w in tail_weights)
    mem_rows = pl.BlockSpec((nmem * nseq, D), lambda b, s: (b * ns + s, 0))
    mem_cache = pl.BlockSpec((nmem, N_MEM * MEM_ROWS, 128), lambda b, s: (b * ns + s, 0, 0))
    return pl.pallas_call(
        functools.partial(_prompt_mixer_kernel, tile=tile, nseq=nseq, ncast=len(tail_weights)),
        grid=(B, ns),
        in_specs=[
            pl.BlockSpec((1, tile, D), lambda b, s: (b, s, 0)),
            const(1, D),
            const(D, IN_WIDTH),
            pl.BlockSpec(memory_space=pltpu.SMEM),
            const(D, WOUT_SPLIT),
            const(D, D - WOUT_SPLIT),
        ] + [chunk(w) for w in tail_weights] + [mem_rows, mem_cache, mem_cache],
        out_specs=[
            pl.BlockSpec((1, tile, D), lambda b, s: (b, s, 0)),
            pl.BlockSpec((1, BLOCK, KV_WIDTH), lambda b, s: (b, 0, 0)),
            pl.BlockSpec((1, BLOCK, KV_WIDTH), lambda b, s: (b, 0, 0)),
            pl.BlockSpec((1, POOL_PAD, POOL_WIDTH), lambda b, s: (b, 0, 0)),
        ] + [chunk(w) for w in tail_weights] + [mem_rows],
        out_shape=[
            jax.ShapeDtypeStruct((B, S, D), _F32),
            jax.ShapeDtypeStruct((B, BLOCK, KV_WIDTH), _F32),
            jax.ShapeDtypeStruct((B, BLOCK, KV_WIDTH), _F32),
            jax.ShapeDtypeStruct((B, POOL_PAD, POOL_WIDTH), _F32),
        ] + [jax.ShapeDtypeStruct(w.shape, _BF) for w in tail_weights] + [jax.ShapeDtypeStruct(qc.shape, _F32)],
        scratch_shapes=[
            pltpu.VMEM((KV_WIDTH, BLOCK), _BF),
            pltpu.VMEM((BLOCK, KV_WIDTH), _BF),
            pltpu.VMEM((POOL_PAD, POOL_WIDTH), _F32),
            pltpu.VMEM((2, N_HEADS, BLOCK, 2 * BLOCK), _F32),
            pltpu.VMEM((2, len(POOL_WINDOWS), BLOCK, POOL_GROUP_WIDTH), _F32),
        ],
        compiler_params=pltpu.CompilerParams(
            dimension_semantics=("arbitrary", "arbitrary"), vmem_limit_bytes=VMEM_LIMIT_BYTES),
        name="prompt_mixer",
    )(x, gmix, win, sinks, *wout, *tail_weights, qc, mk, mv)


def _mem_kv_kernel(mem_ref, gmem_ref, wck_ref, wcv_ref, wcq_ref, wco_ref, k_ref, v_ref, kq_ref, vo_ref, *, nbatch):
    gmem = gmem_ref[...]

    def stream(i):
        hm = _rms(mem_ref[i], gmem).astype(_BF)
        k = _dot(hm, wck_ref[...])
        yield
        v = _dot(hm, wcv_ref[...])
        yield
        for h in range(N_CROSS_HEADS):
            head = slice(h * CROSS_HEAD_DIM, (h + 1) * CROSS_HEAD_DIM)
            kq_ref[i, :, head] = (_dot(wcq_ref[:, head], k[:, head].T.astype(_BF)) * (CQ_SCALE * LOG2E)).astype(_BF)
            vo_ref[i, head, :] = _dot(v[:, head].astype(_BF), wco_ref[head, :]).astype(_BF)
            for half in range(CROSS_HEAD_DIM // 128):
                c0 = h * CROSS_HEAD_DIM + half * 128
                rows = pl.ds(half * N_CROSS_HEADS + h, N_MEM, stride=MEM_ROWS)
                k_ref[i, rows, :] = k[:, c0:c0 + 128]
                v_ref[i, rows, :] = v[:, c0:c0 + 128]
            yield

    _interleave([stream(i) for i in range(nbatch)], MEM_KV_SKEW)


def _mem_kv(mem, gmem, wck, wcv, wcq, wco, nbatch):
    B, M, D = mem.shape
    HM = N_CROSS_HEADS * M
    const = lambda *shape: pl.BlockSpec(shape, lambda b: (0,) * len(shape))
    per_seq = lambda *shape: pl.BlockSpec((nbatch,) + shape, lambda b: (b, 0, 0))
    return pl.pallas_call(
        functools.partial(_mem_kv_kernel, nbatch=nbatch),
        grid=(B // nbatch,),
        in_specs=[per_seq(M, D), const(1, D), const(D, D), const(D, D), const(D, D), const(D, D)],
        out_specs=[per_seq(M * MEM_ROWS, 128), per_seq(M * MEM_ROWS, 128), per_seq(D, HM), per_seq(HM, D)],
        out_shape=[
            jax.ShapeDtypeStruct((B, M * MEM_ROWS, 128), _F32),
            jax.ShapeDtypeStruct((B, M * MEM_ROWS, 128), _F32),
            jax.ShapeDtypeStruct((B, D, HM), _BF),
            jax.ShapeDtypeStruct((B, HM, D), _BF),
        ],
        compiler_params=pltpu.CompilerParams(
            dimension_semantics=("arbitrary",), vmem_limit_bytes=VMEM_LIMIT_BYTES),
        name="prompt_mem_kv",
    )(mem, gmem, wck, wcv, wcq, wco)


def _mem_cache_unrows(rows):
    nb = rows.shape[0]
    c = rows.reshape(nb, N_MEM, CROSS_HEAD_DIM // 128, N_CROSS_HEADS, 128)
    return c.transpose(0, 1, 3, 2, 4).reshape(nb, N_MEM, N_CROSS_HEADS, CROSS_HEAD_DIM)


def _prompt_tail_kernel(x1_ref, gcross_ref, kq_ref, vo_ref, gffn_ref, wup_ref, wdown_ref, gfinal_ref, y_ref,
                        *, tile, sub):
    gcross, gffn, gfinal = gcross_ref[...], gffn_ref[...], gfinal_ref[...]

    def stream(r0):
        x1 = x1_ref[0, r0:r0 + sub, :]
        hn = _rms(x1, gcross).astype(_BF)
        yield
        scores = _dot(hn, kq_ref[0])
        yield
        probs = [_softmax2(scores[:, h * N_MEM:(h + 1) * N_MEM]).astype(_BF) for h in range(N_CROSS_HEADS)]
        x2 = x1 + _dot(jnp.concatenate(probs, axis=1), vo_ref[0])
        yield

        def store(y):
            y_ref[0, r0:r0 + sub, :] = y

        yield from _ffn_final_stages(x2, gffn, wup_ref, wdown_ref, gfinal, store)

    _interleave([stream(r0) for r0 in range(0, tile, sub)], TAIL_SKEW)


def _single(shape, index_map):
    return pl.BlockSpec(shape, index_map, pipeline_mode=pl.Buffered(1))


def _prompt_tail(x1, gcross, kq, vo, gffn, wup, wdown, gfinal, tile, sub):
    B, S, D = x1.shape
    const = lambda *shape: _single(shape, lambda b, s: (0,) * len(shape))
    return pl.pallas_call(
        functools.partial(_prompt_tail_kernel, tile=tile, sub=sub),
        grid=(B, S // tile),
        in_specs=[
            pl.BlockSpec((1, tile, D), lambda b, s: (b, s, 0)),
            const(1, D),
            pl.BlockSpec((1, D, N_CROSS_HEADS * N_MEM), lambda b, s: (b, 0, 0)),
            pl.BlockSpec((1, N_CROSS_HEADS * N_MEM, D), lambda b, s: (b, 0, 0)),
            const(1, D),
            const(D, D_FF),
            const(D_FF, D),
            const(1, D),
        ],
        out_specs=pl.BlockSpec((1, tile, D), lambda b, s: (b, s, 0)),
        out_shape=jax.ShapeDtypeStruct((B, S, D), _F32),
        compiler_params=pltpu.CompilerParams(
            dimension_semantics=("arbitrary", "arbitrary"), vmem_limit_bytes=VMEM_LIMIT_BYTES),
        name="prompt_tail",
    )(x1, gcross, kq, vo, gffn, wup, wdown, gfinal)


def _sample_mixer_kernel(x_ref, gmix_ref, win_ref, sinkcol_ref, ck_ref, cv_ref, st_ref,
                         wouta_ref, woutb_ref, gcross_ref, wcq_ref,
                         x1_ref, qc_ref, wk_ref, wv_ref, pool_ref, wcq_bf_ref,
                         kc_scr, vc_scr, u_scr, *, nbatch, sub, nseq, past_len):
    @pl.when(pl.program_id(0) == 0)
    def _():
        wcq_bf_ref[...] = wcq_ref[...].astype(_BF)

    rows = sub * nseq
    nkeys = 2 * WINDOW
    gmix, gcross = gmix_ref[...], gcross_ref[...]
    sinkcol = sinkcol_ref[...][None]

    t = lax.broadcasted_iota(jnp.int32, (nseq, nkeys), 0)
    c = lax.broadcasted_iota(jnp.int32, (nseq, nkeys), 1)
    dist = jnp.where(c < WINDOW, WINDOW + t - c, t - (c - WINDOW))
    valid = (dist >= 0) & (dist <= WINDOW) & (c < WINDOW + nseq)
    distf = dist.astype(_F32)
    bias = jnp.concatenate([jnp.where(valid, -_slope(h) * distf, NEG_INF) for h in range(N_HEADS)], axis=0)[None]
    lane = lax.broadcasted_iota(jnp.int32, (rows, 2 * HEAD_DIM), 1)
    lo = lane < HEAD_DIM
    tpos = past_len + lax.broadcasted_iota(jnp.int32, (nseq, POOL_GROUP_WIDTH), 0)
    zpad = jnp.zeros((sub, nkeys - WINDOW, KV_WIDTH), _BF)

    def stream(b0):
        bs = slice(b0, b0 + sub)
        rs = slice(b0 * nseq, (b0 + sub) * nseq)
        x = x_ref[rs, :]
        proj = _dot(_rms(x, gmix).astype(_BF), win_ref[...])
        yield
        k_new = proj[:, K_OFF:K_OFF + KV_WIDTH].reshape(sub, nseq, KV_WIDTH)
        v_new = proj[:, V_OFF:V_OFF + KV_WIDTH].reshape(sub, nseq, KV_WIDTH)
        ck = jnp.swapaxes(ck_ref[bs], 1, 2)
        cv = jnp.swapaxes(cv_ref[bs], 1, 2)
        wk_ref[bs, :WINDOW - nseq, :] = ck[:, nseq:, :]
        wk_ref[bs, WINDOW - nseq:, :] = k_new
        wv_ref[bs, :WINDOW - nseq, :] = cv[:, nseq:, :]
        wv_ref[bs, WINDOW - nseq:, :] = v_new
        kc_scr[bs, WINDOW:, :] = zpad
        vc_scr[bs, WINDOW:, :] = zpad
        kc_scr[bs, :WINDOW, :] = ck.astype(_BF)
        vc_scr[bs, :WINDOW, :] = cv.astype(_BF)
        kc_scr[bs, WINDOW:WINDOW + nseq, :] = k_new.astype(_BF)
        vc_scr[bs, WINDOW:WINDOW + nseq, :] = v_new.astype(_BF)
        qsc = proj[:, :ATTN_WIDTH] * Q_SCALE
        q_lo, q_hi = [], []
        for p in range(GROUP):
            slab = qsc[:, p * 128:(p + 1) * 128]
            q_lo.append(jnp.where(lo, slab, 0.0).reshape(sub, nseq, 128))
            q_hi.append(jnp.where(lo, 0.0, slab).reshape(sub, nseq, 128))
        qm = jnp.concatenate(q_lo + q_hi, axis=1).astype(_BF)
        sc = jnp.einsum('bqd,bkd->bqk', qm, kc_scr[bs], preferred_element_type=_F32) + bias
        yield
        pr = _softmax_with_sink(sc, sinkcol).astype(_BF)
        o = jnp.einsum('bqk,bkd->bqd', pr, vc_scr[bs], preferred_element_type=_F32)
        yield
        attn = []
        for p in range(GROUP):
            o_lo = o[:, p * nseq:(p + 1) * nseq, :].reshape(rows, 128)
            o_hi = o[:, (p + GROUP) * nseq:(p + GROUP + 1) * nseq, :].reshape(rows, 128)
            attn.append(jnp.where(lo, o_lo, o_hi).astype(_BF))
        ext = POOL_PAD + nseq
        seq_rows = lambda r: pl.ds(b0 * ext + r, sub, stride=ext)
        pooled = []
        for g, w in enumerate(POOL_WINDOWS):
            c0 = g * POOL_GROUP_WIDTH
            u_scr[g, seq_rows(0), :] = jnp.zeros((sub, POOL_GROUP_WIDTH), _F32)
            for r in range(POOL_HIST):
                u_scr[g, seq_rows(POOL_PAD - POOL_HIST + r), :] = st_ref[r, bs, c0:c0 + POOL_GROUP_WIDTH]
            for i in range(sub):
                u_scr[g, (b0 + i) * ext + POOL_PAD:(b0 + i + 1) * ext, :] = (
                    proj[i * nseq:(i + 1) * nseq, U_OFF + c0:U_OFF + c0 + POOL_GROUP_WIDTH])
            for r in range(POOL_HIST):
                pool_ref[r, bs, c0:c0 + POOL_GROUP_WIDTH] = u_scr[g, seq_rows(ext - POOL_HIST + r), :]
            ug = u_scr[g, b0 * ext:(b0 + sub) * ext, :].reshape(sub, ext, POOL_GROUP_WIDTH)
            sw = _window_sums(ug, 1)[w][:, POOL_PAD:, :]
            cnt = jnp.minimum(tpos + 1, w).astype(_F32)
            pooled.append((sw / cnt[None] - ug[:, POOL_PAD:, :]).reshape(rows, POOL_GROUP_WIDTH).astype(_BF))
        yield
        x1 = x + _dot_panels(jnp.concatenate(attn + pooled, axis=1), (wouta_ref, woutb_ref))
        x1_ref[rs, :] = x1
        yield
        qc_ref[rs, :] = _dot(_rms(x1, gcross).astype(_BF), wcq_bf_ref[...]) * CQ_SCALE

    _interleave([stream(b0) for b0 in range(0, nbatch, sub)], SAMPLE_MIXER_SKEW)


def _sample_mixer(x, gmix, win, sinkcol, ck, cv, st, wout, gcross, wcq, nbatch, sub, nseq,
                  past_len):
    R, D = x.shape
    rows = nbatch * nseq
    const = lambda *shape: pl.BlockSpec(shape, lambda i: (0,) * len(shape))
    return pl.pallas_call(
        functools.partial(_sample_mixer_kernel, nbatch=nbatch, sub=sub, nseq=nseq, past_len=past_len),
        grid=(R // rows,),
        in_specs=[
            pl.BlockSpec((rows, D), lambda i: (i, 0)),
            const(1, D),
            const(D, IN_WIDTH),
            const(N_HEADS * nseq, 1),
            pl.BlockSpec((nbatch, WINDOW, KV_WIDTH), lambda i: (i, 0, 0)),
            pl.BlockSpec((nbatch, WINDOW, KV_WIDTH), lambda i: (i, 0, 0)),
            pl.BlockSpec((POOL_HIST, nbatch, POOL_WIDTH), lambda i: (0, i, 0)),
            const(D, WOUT_SPLIT),
            const(D, D - WOUT_SPLIT),
            const(1, D),
            const(D, D),
        ],
        out_specs=[
            pl.BlockSpec((rows, D), lambda i: (i, 0)),
            pl.BlockSpec((rows, D), lambda i: (i, 0)),
            pl.BlockSpec((nbatch, WINDOW, KV_WIDTH), lambda i: (i, 0, 0)),
            pl.BlockSpec((nbatch, WINDOW, KV_WIDTH), lambda i: (i, 0, 0)),
            pl.BlockSpec((POOL_HIST, nbatch, POOL_WIDTH), lambda i: (0, i, 0)),
            const(D, D),
        ],
        out_shape=[
            jax.ShapeDtypeStruct((R, D), _F32),
            jax.ShapeDtypeStruct((R, D), _F32),
            jax.ShapeDtypeStruct((R // nseq, WINDOW, KV_WIDTH), _F32),
            jax.ShapeDtypeStruct((R // nseq, WINDOW, KV_WIDTH), _F32),
            jax.ShapeDtypeStruct((POOL_HIST, R // nseq, POOL_WIDTH), _F32),
            jax.ShapeDtypeStruct((D, D), _BF),
        ],
        scratch_shapes=[
            pltpu.VMEM((nbatch, 2 * WINDOW, KV_WIDTH), _BF),
            pltpu.VMEM((nbatch, 2 * WINDOW, KV_WIDTH), _BF),
            pltpu.VMEM((len(POOL_WINDOWS), nbatch * (POOL_PAD + nseq), POOL_GROUP_WIDTH), _F32),
        ],
        compiler_params=pltpu.CompilerParams(
            dimension_semantics=("arbitrary",), vmem_limit_bytes=VMEM_LIMIT_BYTES),
        name="sample_mixer",
    )(x, gmix, win, sinkcol, ck, cv, st, *wout, gcross, wcq)


def _mem_cache_rows(cache):
    nb = cache.shape[0]
    c = cache.reshape(nb, N_MEM, N_CROSS_HEADS, CROSS_HEAD_DIM // 128, 128)
    return c.transpose(0, 1, 3, 2, 4).reshape(nb, N_MEM * MEM_ROWS, 128)


def _load_mem_head(ref, b, h):
    halves = [ref[b, pl.ds(half * N_CROSS_HEADS + h, N_MEM, stride=MEM_ROWS), :]
              for half in range(CROSS_HEAD_DIM // 128)]
    return jnp.concatenate(halves, axis=1)


def _sample_tail_kernel(x1_ref, o_ref, wco_ref, gffn_ref, wup_ref, wdown_ref, gfinal_ref, y_ref, *, tile, sub):
    gffn, gfinal = gffn_ref[...], gfinal_ref[...]

    def stream(r0):
        x2 = x1_ref[r0:r0 + sub, :] + _dot(o_ref[r0:r0 + sub, :].astype(_BF), wco_ref[...])
        yield

        def store(y):
            y_ref[r0:r0 + sub, :] = y

        yield from _ffn_final_stages(x2, gffn, wup_ref, wdown_ref, gfinal, store)

    _interleave([stream(r0) for r0 in range(0, tile, sub)], TAIL_SKEW)


def _sample_tail(x1, o, wco, gffn, wup, wdown, gfinal, tile, sub):
    R, D = x1.shape
    const = lambda *shape: _single(shape, lambda i: (0,) * len(shape))
    return pl.pallas_call(
        functools.partial(_sample_tail_kernel, tile=tile, sub=sub),
        grid=(R // tile,),
        in_specs=[
            pl.BlockSpec((tile, D), lambda i: (i, 0)),
            pl.BlockSpec((tile, D), lambda i: (i, 0)),
            const(D, D),
            const(1, D),
            const(D, D_FF),
            const(D_FF, D),
            const(1, D),
        ],
        out_specs=pl.BlockSpec((tile, D), lambda i: (i, 0)),
        out_shape=jax.ShapeDtypeStruct((R, D), _F32),
        compiler_params=pltpu.CompilerParams(
            dimension_semantics=("arbitrary",), vmem_limit_bytes=VMEM_LIMIT_BYTES),
        name="sample_tail",
    )(x1, o, wco, gffn, wup, wdown, gfinal)


PROMPT_TILE = 512
TAIL_TILE = 1024
TAIL_SUB = 256
TAIL_SKEW = 4
MEM_KV_BATCH = 2
MEM_KV_SKEW = 2
SAMPLE_MIXER_BATCH = 32
SAMPLE_MIXER_SUB = 16
SAMPLE_MIXER_SKEW = 3


def kernel(x_prompt, x_sample, cache_win_k, cache_win_v, state_pool, cache_mem_k, cache_mem_v, mem_prompt,
           g_mix, w_in, attn_sinks, w_pool, pool_scale, w_out, g_cross, g_mem, w_cq, w_ck, w_cv, w_co,
           g_ffn, w_up, w_down, g_final):
    depth = g_mix.shape[0]
    assert depth == 1, "one layer per step"
    B, S, D = x_prompt.shape
    DB, T, _ = x_sample.shape
    past_len = PAST_LEN
    l = 0

    win = jnp.concatenate([_pair_heads(w_in[l][:, :ATTN_WIDTH], 1), w_in[l][:, ATTN_WIDTH:]], axis=1).astype(_BF)
    wout = _prep_wout(w_out[l], w_pool[l], pool_scale[l].reshape(1, POOL_WIDTH))
    gmix, gcross, gmem, gffn = (g[l].reshape(1, D) for g in (g_mix, g_cross, g_mem, g_ffn))
    gfinal = g_final.reshape(1, D)
    sinks = attn_sinks[l]

    xs = x_sample.reshape(DB * T, D)
    ck = cache_win_k[l].transpose(0, 2, 3, 1).reshape(DB, KV_WIDTH, WINDOW)
    cv = cache_win_v[l].transpose(0, 2, 3, 1).reshape(DB, KV_WIDTH, WINDOW)
    st = state_pool[l].transpose(1, 0, 2)
    sinkcol = jnp.repeat(sinks, T).reshape(N_HEADS * T, 1)
    x1s, qc, wk_s, wv_s, pool_s, wcq = _sample_mixer(xs, gmix, win, sinkcol, ck, cv, st, wout,
                                                gcross, w_cq[l], SAMPLE_MIXER_BATCH, SAMPLE_MIXER_SUB, T, past_len)

    mk = _mem_cache_rows(cache_mem_k[l])
    mv = _mem_cache_rows(cache_mem_v[l])
    x1p, klast, vlast, plast, wco, wup, wdown, wck, wcv, o_s = _prompt_mixer(
        x_prompt, gmix, win, sinks, wout, (w_co[l], w_up[l], w_down[l], w_ck[l], w_cv[l]), qc, mk, mv, PROMPT_TILE)
    mem_k, mem_v, mem_kq, mem_vo = _mem_kv(mem_prompt, gmem, wck, wcv, wcq, wco, MEM_KV_BATCH)
    y_prompt = _prompt_tail(x1p, gcross, mem_kq, mem_vo, gffn, wup, wdown, gfinal, TAIL_TILE, TAIL_SUB)

    y_sample = _sample_tail(x1s, o_s, wco, gffn, wup, wdown, gfinal, TAIL_TILE, TAIL_SUB).reshape(DB, T, D)

    return (
        y_prompt,
        y_sample,
        klast.reshape(B, N_KV_HEADS, HEAD_DIM, WINDOW).transpose(0, 3, 1, 2)[None],
        vlast.reshape(B, N_KV_HEADS, HEAD_DIM, WINDOW).transpose(0, 3, 1, 2)[None],
        plast[:, POOL_PAD - POOL_HIST:, :][None],
        _mem_cache_unrows(mem_k)[None],
        _mem_cache_unrows(mem_v)[None],
        wk_s.reshape(1, DB, WINDOW, N_KV_HEADS, HEAD_DIM),
        wv_s.reshape(1, DB, WINDOW, N_KV_HEADS, HEAD_DIM),
        pool_s.transpose(1, 0, 2)[None],
    )
```

```python
import functools

import jax
import jax.numpy as jnp
from jax import lax
from jax.experimental import pallas as pl
from jax.experimental.pallas import tpu as pltpu

D_MODEL = 1024
PAST_LEN = 16384
HEAD_DIM = 64
N_HEADS = 8
N_KV_HEADS = 2
GROUP = N_HEADS // N_KV_HEADS
ATTN_WIDTH = N_HEADS * HEAD_DIM
KV_WIDTH = N_KV_HEADS * HEAD_DIM
WINDOW = 128
BLOCK = WINDOW
POOL_WIDTH = D_MODEL - ATTN_WIDTH
POOL_WINDOWS = (2, 4, 8, 16)
POOL_GROUP_WIDTH = 128
POOL_HIST = 15
POOL_PAD = 16
IN_WIDTH = ATTN_WIDTH + 2 * KV_WIDTH + POOL_WIDTH
N_MEM = 256
N_CROSS_HEADS = 4
CROSS_HEAD_DIM = 256
MEM_ROWS = N_CROSS_HEADS * (CROSS_HEAD_DIM // 128)
D_FF = 4 * D_MODEL
FF_CHUNK = 1024
RMS_EPS = 1e-5
NEG_INF = -1e30
Q_SCALE = HEAD_DIM ** -0.5
CQ_SCALE = CROSS_HEAD_DIM ** -0.5
K_OFF = ATTN_WIDTH
V_OFF = ATTN_WIDTH + KV_WIDTH
U_OFF = ATTN_WIDTH + 2 * KV_WIDTH


def _pair_heads(w, axis):
    shape = w.shape
    split = shape[:axis] + (N_KV_HEADS, GROUP, HEAD_DIM) + shape[axis + 1:]
    return jnp.swapaxes(w.reshape(split), axis, axis + 1).reshape(shape)


WOUT_SPLIT = 768

VMEM_LIMIT_BYTES = 56 * 1024 * 1024

_BF = jnp.bfloat16
_F32 = jnp.float32


def _slope(h):
    return 2.0 ** (-8.0 * (h + 1) / N_HEADS)


def _dot(a, b):
    return jnp.dot(a, b, preferred_element_type=_F32)


def _dot_panels(a, panel_refs):
    return jnp.concatenate([_dot(a, ref[...]) for ref in panel_refs], axis=1)


def _rms(x, g):
    ms = jnp.mean(x * x, axis=-1, keepdims=True)
    return x * lax.rsqrt(ms + RMS_EPS) * g


def _softmax_with_sink(s, sink):
    m = jnp.maximum(jnp.max(s, axis=-1, keepdims=True), sink)
    e = jnp.exp(s - m)
    den = jnp.sum(e, axis=-1, keepdims=True) + jnp.exp(sink - m)
    return e * (1.0 / den)


def _softmax_with_sink_parts(s, sink):
    m = jnp.maximum(jnp.max(s, axis=-1, keepdims=True), sink)
    e = jnp.exp(s - m)
    den = jnp.sum(e, axis=-1, keepdims=True) + jnp.exp(sink - m)
    return e, 1.0 / den


def _softmax(s):
    m = jnp.max(s, axis=-1, keepdims=True)
    e = jnp.exp(s - m)
    return e * (1.0 / jnp.sum(e, axis=-1, keepdims=True))


def _window_sums(u_ext, axis):
    out = {}
    s = u_ext
    w = 1
    while w < max(POOL_WINDOWS):
        s = s + pltpu.roll(s, w, axis)
        w *= 2
        out[w] = s
    return out


def _interleave(streams, skew):
    pending = list(streams)
    live = []
    rnd = 0
    while live or pending:
        if pending and rnd % skew == 0:
            live.append(pending.pop(0))
        for g in list(live):
            try:
                next(g)
            except StopIteration:
                live.remove(g)
        rnd += 1


def _ffn_final_stages(x2, gffn, wup_ref, wdown_ref, gfinal, store):
    hn = _rms(x2, gffn).astype(_BF)
    yield
    acc = x2
    for c in range(D_FF // FF_CHUNK):
        hc = _dot(hn, wup_ref[:, c * FF_CHUNK:(c + 1) * FF_CHUNK])
        yield
        hc = jnp.maximum(hc, 0.0)
        hc = (hc * hc).astype(_BF)
        acc = acc + _dot(hc, wdown_ref[c * FF_CHUNK:(c + 1) * FF_CHUNK, :])
        yield
    store(_rms(acc, gfinal))


def _prep_wout_kernel(wout_ref, wpool_ref, pscale_ref, outa_ref, outb_ref):
    blocks = []
    for g in range(GROUP):
        for kv in range(N_KV_HEADS):
            r0 = (kv * GROUP + g) * HEAD_DIM
            blocks.append(wout_ref[r0:r0 + HEAD_DIM, :])
    for g in range(len(POOL_WINDOWS)):
        r0 = ATTN_WIDTH + g * POOL_GROUP_WIDTH
        scaled = wpool_ref[g] * pscale_ref[:, g * POOL_GROUP_WIDTH:(g + 1) * POOL_GROUP_WIDTH]
        blocks.append(jnp.dot(scaled, wout_ref[r0:r0 + POOL_GROUP_WIDTH, :], precision=lax.Precision.HIGHEST,
                              preferred_element_type=_F32))
    full = jnp.concatenate(blocks, axis=0).astype(_BF)
    outa_ref[...] = full[:, :WOUT_SPLIT]
    outb_ref[...] = full[:, WOUT_SPLIT:]


def _prep_wout(wout, wpool, pscale):
    D = wout.shape[0]
    return pl.pallas_call(
        _prep_wout_kernel,
        out_shape=[jax.ShapeDtypeStruct((D, WOUT_SPLIT), _BF), jax.ShapeDtypeStruct((D, D - WOUT_SPLIT), _BF)],
        compiler_params=pltpu.CompilerParams(vmem_limit_bytes=VMEM_LIMIT_BYTES),
        name="prep_wout",
    )(wout, wpool, pscale)


def _prompt_mixer_kernel(x_ref, gmix_ref, win_ref, sink_ref, wouta_ref, woutb_ref, *rest, tile, nseq, ncast):
    cast_src, (qc_ref, mk_ref, mv_ref) = rest[:ncast], rest[ncast:ncast + 3]
    x1_ref, klast_ref, vlast_ref, plast_ref = rest[ncast + 3:ncast + 7]
    cast_dst, os_ref = rest[ncast + 7:2 * ncast + 7], rest[2 * ncast + 7]
    kt_scr, v_scr, u_scr, bias_scr, inv_scr = rest[2 * ncast + 8:]
    b = pl.program_id(0)
    s = pl.program_id(1)
    nb = tile // BLOCK

    @pl.when((b == 0) & (s == 0))
    def _():
        qi = lax.broadcasted_iota(jnp.int32, (BLOCK, 2 * BLOCK), 0)
        kc = lax.broadcasted_iota(jnp.int32, (BLOCK, 2 * BLOCK), 1)
        dist = qi + BLOCK - kc
        valid = (dist >= 0) & (dist <= WINDOW)
        valid_first = valid & (kc >= BLOCK)
        distf = dist.astype(_F32)
        for h in range(N_HEADS):
            ali = -_slope(h) * distf
            bias_scr[0, h] = jnp.where(valid, ali, NEG_INF)
            bias_scr[1, h] = jnp.where(valid_first, ali, NEG_INF)
        prow = lax.broadcasted_iota(jnp.int32, (BLOCK, POOL_GROUP_WIDTH), 0)
        for g, w in enumerate(POOL_WINDOWS):
            inv_scr[0, g] = jnp.full((BLOCK, POOL_GROUP_WIDTH), 1.0 / w, _F32)
            inv_scr[1, g] = 1.0 / jnp.minimum(prow + 1, w).astype(_F32)

    @pl.when(s == 0)
    def _():
        kt_scr[...] = jnp.zeros((KV_WIDTH, BLOCK), _BF)
        v_scr[...] = jnp.zeros((BLOCK, KV_WIDTH), _BF)
        u_scr[...] = jnp.zeros((POOL_PAD, POOL_WIDTH), _F32)

    lane = lax.broadcasted_iota(jnp.int32, (BLOCK, 2 * HEAD_DIM), 1)
    lo = lane < HEAD_DIM
    gmix = gmix_ref[...]
    zeros_kt = jnp.zeros((HEAD_DIM, 2 * BLOCK), _BF)

    def project(j):
        x = x_ref[0, j * BLOCK:(j + 1) * BLOCK, :]
        proj = _dot(_rms(x, gmix).astype(_BF), win_ref[...])
        k = proj[:, K_OFF:K_OFF + KV_WIDTH]
        v = proj[:, V_OFF:V_OFF + KV_WIDTH]
        u = proj[:, U_OFF:]
        kt = k.T
        if j == nb - 1:
            klast_ref[0] = kt
            vlast_ref[0] = v.T
            plast_ref[0] = u[BLOCK - POOL_PAD:]
        return dict(x=x, q=(proj[:, :ATTN_WIDTH] * Q_SCALE).astype(_BF), kt=kt.astype(_BF), v=v.astype(_BF), u=u)

    def pool(j, u_hist, u):
        first = ((s == 0) & (j == 0)).astype(jnp.int32) if j == 0 else 0
        ys = []
        for g, w in enumerate(POOL_WINDOWS):
            c0 = g * POOL_GROUP_WIDTH
            u_ext = jnp.concatenate([u_hist[:, c0:c0 + POOL_GROUP_WIDTH], u[:, c0:c0 + POOL_GROUP_WIDTH]], axis=0)
            sw = _window_sums(u_ext, 0)[w][POOL_PAD:]
            ys.append((sw * inv_scr[first, g] - u_ext[POOL_PAD:]).astype(_BF))
        return ys

    def kv_operands(kt_prev, v_prev, blk):
        kt2 = jnp.concatenate([kt_prev, blk["kt"]], axis=1)
        v2 = jnp.concatenate([v_prev, blk["v"]], axis=0)
        kt_pair = jnp.concatenate([jnp.concatenate([kt2[:HEAD_DIM], zeros_kt], axis=0),
                                   jnp.concatenate([zeros_kt, kt2[HEAD_DIM:]], axis=0)], axis=1)
        lane2 = lax.broadcasted_iota(jnp.int32, v2.shape, 1)
        zero = jnp.zeros_like(v2)
        v_pair = jnp.concatenate([jnp.where(lane2 < HEAD_DIM, v2, zero), jnp.where(lane2 < HEAD_DIM, zero, v2)],
                                 axis=0)
        return kt_pair, v_pair

    def wave_scores(j, blk, kt_pair, pairs):
        first = ((s == 0) & (j == 0)).astype(jnp.int32) if j == 0 else 0
        scores = []
        for p in pairs:
            sc = _dot(blk["q"][:, p * 128:(p + 1) * 128], kt_pair)
            scores.append(sc[:, :2 * BLOCK] + bias_scr[first, p])
            scores.append(sc[:, 2 * BLOCK:] + bias_scr[first, p + GROUP])
        return scores

    def wave_values(v_pair, pairs, scores):
        slabs = []
        for i, p in enumerate(pairs):
            es, invs = [], []
            for half, h in ((0, p), (1, p + GROUP)):
                e, inv = _softmax_with_sink_parts(scores[2 * i + half], sink_ref[h])
                es.append(e.astype(_BF))
                invs.append(inv)
            o = _dot(jnp.concatenate(es, axis=1), v_pair)
            slabs.append((o * jnp.where(lo, invs[0], invs[1])).astype(_BF))
        return slabs

    def output(j, blk, slabs):
        cat = jnp.concatenate(slabs, axis=1)
        x1_ref[0, j * BLOCK:(j + 1) * BLOCK, :] = blk["x"] + _dot_panels(cat, (wouta_ref, woutb_ref))

    heads = [slice(h * CROSS_HEAD_DIM, (h + 1) * CROSS_HEAD_DIM) for h in range(N_CROSS_HEADS)]

    def mem_scores(i):
        q = qc_ref[i * nseq:(i + 1) * nseq, :].astype(_BF)
        return [lax.dot_general(q[:, c], _load_mem_head(mk_ref, i, h).astype(_BF), (((1,), (1,)), ((), ())),
                                preferred_element_type=_F32) for h, c in enumerate(heads)]

    def mem_values(i, scores):
        outs = [_dot(_softmax(sc).astype(_BF), _load_mem_head(mv_ref, i, h).astype(_BF))
                for h, sc in enumerate(scores)]
        os_ref[i * nseq:(i + 1) * nseq, :] = jnp.concatenate(outs, axis=1)

    kt_prev, v_prev, u_hist = kt_scr[...], v_scr[...], u_scr[...]
    blk = project(0)
    done = None
    for j in range(nb):
        kt_pair, v_pair = kv_operands(kt_prev, v_prev, blk)
        sc0 = wave_scores(j, blk, kt_pair, (0, 1))
        nxt = project(j + 1) if j + 1 < nb else None
        msc = mem_scores(j)
        sc1 = wave_scores(j, blk, kt_pair, (2, 3))
        pooled = pool(j, u_hist, blk["u"])
        at0 = wave_values(v_pair, (0, 1), sc0)
        if done is not None:
            output(*done)
        mem_values(j, msc)
        at1 = wave_values(v_pair, (2, 3), sc1)
        done = (j, blk, at0 + at1 + pooled)
        kt_prev, v_prev, u_hist = blk["kt"], blk["v"], blk["u"][BLOCK - POOL_PAD:]
        blk = nxt
    output(*done)
    kt_scr[...] = kt_prev
    v_scr[...] = v_prev
    u_scr[...] = u_hist

    for src, dst in zip(cast_src, cast_dst):
        dst[...] = src[...].astype(_BF)


def _prompt_mixer(x, gmix, win, sinks, wout, tail_weights, qc, mk, mv, tile):
    B, S, D = x.shape
    ns = S // tile
    nsteps = B * ns
    nmem = tile // BLOCK
    nseq = qc.shape[0] // mk.shape[0]
    assert mk.shape[0] == nsteps * nmem
    const = lambda *shape: pl.BlockSpec(shape, lambda b, s: (0,) * len(shape))
    chunk = lambda w: pl.BlockSpec((w.shape[0] // nsteps, w.shape[1]), lambda b, s: (b * ns + s, 0))
    assert all(w.shape[0] % (16 * nsteps) == 0 for w in tail_weights)
    mem_rows = pl.BlockSpec((nmem * nseq, D), lambda b, s: (b * ns + s, 0))
    mem_cache = pl.BlockSpec((nmem, N_MEM * MEM_ROWS, 128), lambda b, s: (b * ns + s, 0, 0))
    return pl.pallas_call(
        functools.partial(_prompt_mixer_kernel, tile=tile, nseq=nseq, ncast=len(tail_weights)),
        grid=(B, ns),
        in_specs=[
            pl.BlockSpec((1, tile, D), lambda b, s: (b, s, 0)),
            const(1, D),
            const(D, IN_WIDTH),
            pl.BlockSpec(memory_space=pltpu.SMEM),
            const(D, WOUT_SPLIT),
            const(D, D - WOUT_SPLIT),
        ] + [chunk(w) for w in tail_weights] + [mem_rows, mem_cache, mem_cache],
        out_specs=[
            pl.BlockSpec((1, tile, D), lambda b, s: (b, s, 0)),
            pl.BlockSpec((1, BLOCK, KV_WIDTH), lambda b, s: (b, 0, 0)),
            pl.BlockSpec((1, BLOCK, KV_WIDTH), lambda b, s: (b, 0, 0)),
            pl.BlockSpec((1, POOL_PAD, POOL_WIDTH), lambda b, s: (b, 0, 0)),
        ] + [chunk(w) for w in tail_weights] + [mem_rows],
        out_shape=[
            jax.ShapeDtypeStruct((B, S, D), _F32),
            jax.ShapeDtypeStruct((B, BLOCK, KV_WIDTH), _F32),
            jax.ShapeDtypeStruct((B, BLOCK, KV_WIDTH), _F32),
            jax.ShapeDtypeStruct((B, POOL_PAD, POOL_WIDTH), _F32),
        ] + [jax.ShapeDtypeStruct(w.shape, _BF) for w in tail_weights] + [jax.ShapeDtypeStruct(qc.shape, _F32)],
        scratch_shapes=[
            pltpu.VMEM((KV_WIDTH, BLOCK), _BF),
            pltpu.VMEM((BLOCK, KV_WIDTH), _BF),
            pltpu.VMEM((POOL_PAD, POOL_WIDTH), _F32),
            pltpu.VMEM((2, N_HEADS, BLOCK, 2 * BLOCK), _F32),
            pltpu.VMEM((2, len(POOL_WINDOWS), BLOCK, POOL_GROUP_WIDTH), _F32),
        ],
        compiler_params=pltpu.CompilerParams(
            dimension_semantics=("arbitrary", "arbitrary"), vmem_limit_bytes=VMEM_LIMIT_BYTES),
        name="prompt_mixer",
    )(x, gmix, win, sinks, *wout, *tail_weights, qc, mk, mv)


def _mem_kv_kernel(mem_ref, gmem_ref, wck_ref, wcv_ref, wcq_ref, wco_ref, k_ref, v_ref, kq_ref, vo_ref, *, nbatch):
    gmem = gmem_ref[...]

    def stream(i):
        hm = _rms(mem_ref[i], gmem).astype(_BF)
        k = _dot(hm, wck_ref[...])
        yield
        v = _dot(hm, wcv_ref[...])
        yield
        for h in range(N_CROSS_HEADS):
            head = slice(h * CROSS_HEAD_DIM, (h + 1) * CROSS_HEAD_DIM)
            kq_ref[i, :, head] = (_dot(wcq_ref[:, head], k[:, head].T.astype(_BF)) * CQ_SCALE).astype(_BF)
            vo_ref[i, head, :] = _dot(v[:, head].astype(_BF), wco_ref[head, :]).astype(_BF)
            for half in range(CROSS_HEAD_DIM // 128):
                c0 = h * CROSS_HEAD_DIM + half * 128
                rows = pl.ds(half * N_CROSS_HEADS + h, N_MEM, stride=MEM_ROWS)
                k_ref[i, rows, :] = k[:, c0:c0 + 128]
                v_ref[i, rows, :] = v[:, c0:c0 + 128]
            yield

    _interleave([stream(i) for i in range(nbatch)], MEM_KV_SKEW)


def _mem_kv(mem, gmem, wck, wcv, wcq, wco, nbatch):
    B, M, D = mem.shape
    HM = N_CROSS_HEADS * M
    const = lambda *shape: pl.BlockSpec(shape, lambda b: (0,) * len(shape))
    per_seq = lambda *shape: pl.BlockSpec((nbatch,) + shape, lambda b: (b, 0, 0))
    return pl.pallas_call(
        functools.partial(_mem_kv_kernel, nbatch=nbatch),
        grid=(B // nbatch,),
        in_specs=[per_seq(M, D), const(1, D), const(D, D), const(D, D), const(D, D), const(D, D)],
        out_specs=[per_seq(M * MEM_ROWS, 128), per_seq(M * MEM_ROWS, 128), per_seq(D, HM), per_seq(HM, D)],
        out_shape=[
            jax.ShapeDtypeStruct((B, M * MEM_ROWS, 128), _F32),
            jax.ShapeDtypeStruct((B, M * MEM_ROWS, 128), _F32),
            jax.ShapeDtypeStruct((B, D, HM), _BF),
            jax.ShapeDtypeStruct((B, HM, D), _BF),
        ],
        compiler_params=pltpu.CompilerParams(
            dimension_semantics=("arbitrary",), vmem_limit_bytes=VMEM_LIMIT_BYTES),
        name="prompt_mem_kv",
    )(mem, gmem, wck, wcv, wcq, wco)


def _mem_cache_unrows(rows):
    nb = rows.shape[0]
    c = rows.reshape(nb, N_MEM, CROSS_HEAD_DIM // 128, N_CROSS_HEADS, 128)
    return c.transpose(0, 1, 3, 2, 4).reshape(nb, N_MEM, N_CROSS_HEADS, CROSS_HEAD_DIM)


def _prompt_tail_kernel(x1_ref, gcross_ref, kq_ref, vo_ref, gffn_ref, wup_ref, wdown_ref, gfinal_ref, y_ref,
                        *, tile, sub):
    gcross, gffn, gfinal = gcross_ref[...], gffn_ref[...], gfinal_ref[...]

    def stream(r0):
        x1 = x1_ref[0, r0:r0 + sub, :]
        hn = _rms(x1, gcross).astype(_BF)
        yield
        scores = _dot(hn, kq_ref[0])
        yield
        probs = [_softmax(scores[:, h * N_MEM:(h + 1) * N_MEM]).astype(_BF) for h in range(N_CROSS_HEADS)]
        x2 = x1 + _dot(jnp.concatenate(probs, axis=1), vo_ref[0])
        yield

        def store(y):
            y_ref[0, r0:r0 + sub, :] = y

        yield from _ffn_final_stages(x2, gffn, wup_ref, wdown_ref, gfinal, store)

    _interleave([stream(r0) for r0 in range(0, tile, sub)], TAIL_SKEW)


def _single(shape, index_map):
    return pl.BlockSpec(shape, index_map, pipeline_mode=pl.Buffered(1))


def _prompt_tail(x1, gcross, kq, vo, gffn, wup, wdown, gfinal, tile, sub):
    B, S, D = x1.shape
    const = lambda *shape: _single(shape, lambda b, s: (0,) * len(shape))
    return pl.pallas_call(
        functools.partial(_prompt_tail_kernel, tile=tile, sub=sub),
        grid=(B, S // tile),
        in_specs=[
            pl.BlockSpec((1, tile, D), lambda b, s: (b, s, 0)),
            const(1, D),
            pl.BlockSpec((1, D, N_CROSS_HEADS * N_MEM), lambda b, s: (b, 0, 0)),
            pl.BlockSpec((1, N_CROSS_HEADS * N_MEM, D), lambda b, s: (b, 0, 0)),
            const(1, D),
            const(D, D_FF),
            const(D_FF, D),
            const(1, D),
        ],
        out_specs=pl.BlockSpec((1, tile, D), lambda b, s: (b, s, 0)),
        out_shape=jax.ShapeDtypeStruct((B, S, D), _F32),
        compiler_params=pltpu.CompilerParams(
            dimension_semantics=("arbitrary", "arbitrary"), vmem_limit_bytes=VMEM_LIMIT_BYTES),
        name="prompt_tail",
    )(x1, gcross, kq, vo, gffn, wup, wdown, gfinal)


def _sample_mixer_kernel(x_ref, gmix_ref, win_ref, sinkcol_ref, ck_ref, cv_ref, st_ref,
                         wouta_ref, woutb_ref, gcross_ref, wcq_ref,
                         x1_ref, qc_ref, wk_ref, wv_ref, pool_ref, wcq_bf_ref,
                         kc_scr, vc_scr, u_scr, *, nbatch, sub, nseq, past_len):
    @pl.when(pl.program_id(0) == 0)
    def _():
        wcq_bf_ref[...] = wcq_ref[...].astype(_BF)

    rows = sub * nseq
    nkeys = 2 * WINDOW
    gmix, gcross = gmix_ref[...], gcross_ref[...]
    sinkcol = sinkcol_ref[...][None]

    t = lax.broadcasted_iota(jnp.int32, (nseq, nkeys), 0)
    c = lax.broadcasted_iota(jnp.int32, (nseq, nkeys), 1)
    dist = jnp.where(c < WINDOW, WINDOW + t - c, t - (c - WINDOW))
    valid = (dist >= 0) & (dist <= WINDOW) & (c < WINDOW + nseq)
    distf = dist.astype(_F32)
    bias = jnp.concatenate([jnp.where(valid, -_slope(h) * distf, NEG_INF) for h in range(N_HEADS)], axis=0)[None]
    lane = lax.broadcasted_iota(jnp.int32, (rows, 2 * HEAD_DIM), 1)
    lo = lane < HEAD_DIM
    tpos = past_len + lax.broadcasted_iota(jnp.int32, (nseq, POOL_GROUP_WIDTH), 0)
    zpad = jnp.zeros((sub, nkeys - WINDOW, KV_WIDTH), _BF)

    def stream(b0):
        bs = slice(b0, b0 + sub)
        rs = slice(b0 * nseq, (b0 + sub) * nseq)
        x = x_ref[rs, :]
        proj = _dot(_rms(x, gmix).astype(_BF), win_ref[...])
        yield
        k_new = proj[:, K_OFF:K_OFF + KV_WIDTH].reshape(sub, nseq, KV_WIDTH)
        v_new = proj[:, V_OFF:V_OFF + KV_WIDTH].reshape(sub, nseq, KV_WIDTH)
        ck = jnp.swapaxes(ck_ref[bs], 1, 2)
        cv = jnp.swapaxes(cv_ref[bs], 1, 2)
        wk_ref[bs, :WINDOW - nseq, :] = ck[:, nseq:, :]
        wk_ref[bs, WINDOW - nseq:, :] = k_new
        wv_ref[bs, :WINDOW - nseq, :] = cv[:, nseq:, :]
        wv_ref[bs, WINDOW - nseq:, :] = v_new
        kc_scr[bs, WINDOW:, :] = zpad
        vc_scr[bs, WINDOW:, :] = zpad
        kc_scr[bs, :WINDOW, :] = ck.astype(_BF)
        vc_scr[bs, :WINDOW, :] = cv.astype(_BF)
        kc_scr[bs, WINDOW:WINDOW + nseq, :] = k_new.astype(_BF)
        vc_scr[bs, WINDOW:WINDOW + nseq, :] = v_new.astype(_BF)
        qsc = proj[:, :ATTN_WIDTH] * Q_SCALE
        q_lo, q_hi = [], []
        for p in range(GROUP):
            slab = qsc[:, p * 128:(p + 1) * 128]
            q_lo.append(jnp.where(lo, slab, 0.0).reshape(sub, nseq, 128))
            q_hi.append(jnp.where(lo, 0.0, slab).reshape(sub, nseq, 128))
        qm = jnp.concatenate(q_lo + q_hi, axis=1).astype(_BF)
        sc = jnp.einsum('bqd,bkd->bqk', qm, kc_scr[bs], preferred_element_type=_F32) + bias
        yield
        pr = _softmax_with_sink(sc, sinkcol).astype(_BF)
        o = jnp.einsum('bqk,bkd->bqd', pr, vc_scr[bs], preferred_element_type=_F32)
        yield
        attn = []
        for p in range(GROUP):
            o_lo = o[:, p * nseq:(p + 1) * nseq, :].reshape(rows, 128)
            o_hi = o[:, (p + GROUP) * nseq:(p + GROUP + 1) * nseq, :].reshape(rows, 128)
            attn.append(jnp.where(lo, o_lo, o_hi).astype(_BF))
        ext = POOL_PAD + nseq
        seq_rows = lambda r: pl.ds(b0 * ext + r, sub, stride=ext)
        pooled = []
        for g, w in enumerate(POOL_WINDOWS):
            c0 = g * POOL_GROUP_WIDTH
            u_scr[g, seq_rows(0), :] = jnp.zeros((sub, POOL_GROUP_WIDTH), _F32)
            for r in range(POOL_HIST):
                u_scr[g, seq_rows(POOL_PAD - POOL_HIST + r), :] = st_ref[r, bs, c0:c0 + POOL_GROUP_WIDTH]
            for i in range(sub):
                u_scr[g, (b0 + i) * ext + POOL_PAD:(b0 + i + 1) * ext, :] = (
                    proj[i * nseq:(i + 1) * nseq, U_OFF + c0:U_OFF + c0 + POOL_GROUP_WIDTH])
            for r in range(POOL_HIST):
                pool_ref[r, bs, c0:c0 + POOL_GROUP_WIDTH] = u_scr[g, seq_rows(ext - POOL_HIST + r), :]
            ug = u_scr[g, b0 * ext:(b0 + sub) * ext, :].reshape(sub, ext, POOL_GROUP_WIDTH)
            sw = _window_sums(ug, 1)[w][:, POOL_PAD:, :]
            cnt = jnp.minimum(tpos + 1, w).astype(_F32)
            pooled.append((sw / cnt[None] - ug[:, POOL_PAD:, :]).reshape(rows, POOL_GROUP_WIDTH).astype(_BF))
        yield
        x1 = x + _dot_panels(jnp.concatenate(attn + pooled, axis=1), (wouta_ref, woutb_ref))
        x1_ref[rs, :] = x1
        yield
        qc_ref[rs, :] = _dot(_rms(x1, gcross).astype(_BF), wcq_bf_ref[...]) * CQ_SCALE

    _interleave([stream(b0) for b0 in range(0, nbatch, sub)], SAMPLE_MIXER_SKEW)


def _sample_mixer(x, gmix, win, sinkcol, ck, cv, st, wout, gcross, wcq, nbatch, sub, nseq,
                  past_len):
    R, D = x.shape
    rows = nbatch * nseq
    const = lambda *shape: pl.BlockSpec(shape, lambda i: (0,) * len(shape))
    return pl.pallas_call(
        functools.partial(_sample_mixer_kernel, nbatch=nbatch, sub=sub, nseq=nseq, past_len=past_len),
        grid=(R // rows,),
        in_specs=[
            pl.BlockSpec((rows, D), lambda i: (i, 0)),
            const(1, D),
            const(D, IN_WIDTH),
            const(N_HEADS * nseq, 1),
            pl.BlockSpec((nbatch, WINDOW, KV_WIDTH), lambda i: (i, 0, 0)),
            pl.BlockSpec((nbatch, WINDOW, KV_WIDTH), lambda i: (i, 0, 0)),
            pl.BlockSpec((POOL_HIST, nbatch, POOL_WIDTH), lambda i: (0, i, 0)),
            const(D, WOUT_SPLIT),
            const(D, D - WOUT_SPLIT),
            const(1, D),
            const(D, D),
        ],
        out_specs=[
            pl.BlockSpec((rows, D), lambda i: (i, 0)),
            pl.BlockSpec((rows, D), lambda i: (i, 0)),
            pl.BlockSpec((nbatch, WINDOW, KV_WIDTH), lambda i: (i, 0, 0)),
            pl.BlockSpec((nbatch, WINDOW, KV_WIDTH), lambda i: (i, 0, 0)),
            pl.BlockSpec((POOL_HIST, nbatch, POOL_WIDTH), lambda i: (0, i, 0)),
            const(D, D),
        ],
        out_shape=[
            jax.ShapeDtypeStruct((R, D), _F32),
            jax.ShapeDtypeStruct((R, D), _F32),
            jax.ShapeDtypeStruct((R // nseq, WINDOW, KV_WIDTH), _F32),
            jax.ShapeDtypeStruct((R // nseq, WINDOW, KV_WIDTH), _F32),
            jax.ShapeDtypeStruct((POOL_HIST, R // nseq, POOL_WIDTH), _F32),
            jax.ShapeDtypeStruct((D, D), _BF),
        ],
        scratch_shapes=[
            pltpu.VMEM((nbatch, 2 * WINDOW, KV_WIDTH), _BF),
            pltpu.VMEM((nbatch, 2 * WINDOW, KV_WIDTH), _BF),
            pltpu.VMEM((len(POOL_WINDOWS), nbatch * (POOL_PAD + nseq), POOL_GROUP_WIDTH), _F32),
        ],
        compiler_params=pltpu.CompilerParams(
            dimension_semantics=("arbitrary",), vmem_limit_bytes=VMEM_LIMIT_BYTES),
        name="sample_mixer",
    )(x, gmix, win, sinkcol, ck, cv, st, *wout, gcross, wcq)


def _mem_cache_rows(cache):
    nb = cache.shape[0]
    c = cache.reshape(nb, N_MEM, N_CROSS_HEADS, CROSS_HEAD_DIM // 128, 128)
    return c.transpose(0, 1, 3, 2, 4).reshape(nb, N_MEM * MEM_ROWS, 128)


def _load_mem_head(ref, b, h):
    halves = [ref[b, pl.ds(half * N_CROSS_HEADS + h, N_MEM, stride=MEM_ROWS), :]
              for half in range(CROSS_HEAD_DIM // 128)]
    return jnp.concatenate(halves, axis=1)


def _sample_tail_kernel(x1_ref, o_ref, wco_ref, gffn_ref, wup_ref, wdown_ref, gfinal_ref, y_ref, *, tile, sub):
    gffn, gfinal = gffn_ref[...], gfinal_ref[...]

    def stream(r0):
        x2 = x1_ref[r0:r0 + sub, :] + _dot(o_ref[r0:r0 + sub, :].astype(_BF), wco_ref[...])
        yield

        def store(y):
            y_ref[r0:r0 + sub, :] = y

        yield from _ffn_final_stages(x2, gffn, wup_ref, wdown_ref, gfinal, store)

    _interleave([stream(r0) for r0 in range(0, tile, sub)], TAIL_SKEW)


def _sample_tail(x1, o, wco, gffn, wup, wdown, gfinal, tile, sub):
    R, D = x1.shape
    const = lambda *shape: _single(shape, lambda i: (0,) * len(shape))
    return pl.pallas_call(
        functools.partial(_sample_tail_kernel, tile=tile, sub=sub),
        grid=(R // tile,),
        in_specs=[
            pl.BlockSpec((tile, D), lambda i: (i, 0)),
            pl.BlockSpec((tile, D), lambda i: (i, 0)),
            const(D, D),
            const(1, D),
            const(D, D_FF),
            const(D_FF, D),
            const(1, D),
        ],
        out_specs=pl.BlockSpec((tile, D), lambda i: (i, 0)),
        out_shape=jax.ShapeDtypeStruct((R, D), _F32),
        compiler_params=pltpu.CompilerParams(
            dimension_semantics=("arbitrary",), vmem_limit_bytes=VMEM_LIMIT_BYTES),
        name="sample_tail",
    )(x1, o, wco, gffn, wup, wdown, gfinal)


PROMPT_TILE = 512
TAIL_TILE = 1024
TAIL_SUB = 256
TAIL_SKEW = 4
MEM_KV_BATCH = 2
MEM_KV_SKEW = 2
SAMPLE_MIXER_BATCH = 32
SAMPLE_MIXER_SUB = 16
SAMPLE_MIXER_SKEW = 3


def kernel(x_prompt, x_sample, cache_win_k, cache_win_v, state_pool, cache_mem_k, cache_mem_v, mem_prompt,
           g_mix, w_in, attn_sinks, w_pool, pool_scale, w_out, g_cross, g_mem, w_cq, w_ck, w_cv, w_co,
           g_ffn, w_up, w_down, g_final):
    depth = g_mix.shape[0]
    assert depth == 1, "one layer per step"
    B, S, D = x_prompt.shape
    DB, T, _ = x_sample.shape
    past_len = PAST_LEN
    l = 0

    win = jnp.concatenate([_pair_heads(w_in[l][:, :ATTN_WIDTH], 1), w_in[l][:, ATTN_WIDTH:]], axis=1).astype(_BF)
    wout = _prep_wout(w_out[l], w_pool[l], pool_scale[l].reshape(1, POOL_WIDTH))
    gmix, gcross, gmem, gffn = (g[l].reshape(1, D) for g in (g_mix, g_cross, g_mem, g_ffn))
    gfinal = g_final.reshape(1, D)
    sinks = attn_sinks[l]

    xs = x_sample.reshape(DB * T, D)
    ck = cache_win_k[l].transpose(0, 2, 3, 1).reshape(DB, KV_WIDTH, WINDOW)
    cv = cache_win_v[l].transpose(0, 2, 3, 1).reshape(DB, KV_WIDTH, WINDOW)
    st = state_pool[l].transpose(1, 0, 2)
    sinkcol = jnp.repeat(sinks, T).reshape(N_HEADS * T, 1)
    x1s, qc, wk_s, wv_s, pool_s, wcq = _sample_mixer(xs, gmix, win, sinkcol, ck, cv, st, wout,
                                                gcross, w_cq[l], SAMPLE_MIXER_BATCH, SAMPLE_MIXER_SUB, T, past_len)

    mk = _mem_cache_rows(cache_mem_k[l])
    mv = _mem_cache_rows(cache_mem_v[l])
    x1p, klast, vlast, plast, wco, wup, wdown, wck, wcv, o_s = _prompt_mixer(
        x_prompt, gmix, win, sinks, wout, (w_co[l], w_up[l], w_down[l], w_ck[l], w_cv[l]), qc, mk, mv, PROMPT_TILE)
    mem_k, mem_v, mem_kq, mem_vo = _mem_kv(mem_prompt, gmem, wck, wcv, wcq, wco, MEM_KV_BATCH)
    y_prompt = _prompt_tail(x1p, gcross, mem_kq, mem_vo, gffn, wup, wdown, gfinal, TAIL_TILE, TAIL_SUB)

    y_sample = _sample_tail(x1s, o_s, wco, gffn, wup, wdown, gfinal, TAIL_TILE, TAIL_SUB).reshape(DB, T, D)

    return (
        y_prompt,
        y_sample,
        klast.reshape(B, N_KV_HEADS, HEAD_DIM, WINDOW).transpose(0, 3, 1, 2)[None],
        vlast.reshape(B, N_KV_HEADS, HEAD_DIM, WINDOW).transpose(0, 3, 1, 2)[None],
        plast[:, POOL_PAD - POOL_HIST:, :][None],
        _mem_cache_unrows(mem_k)[None],
        _mem_cache_unrows(mem_v)[None],
        wk_s.reshape(1, DB, WINDOW, N_KV_HEADS, HEAD_DIM),
        wv_s.reshape(1, DB, WINDOW, N_KV_HEADS, HEAD_DIM),
        pool_s.transpose(1, 0, 2)[None],
    )
```

```python
import functools

import jax
import jax.numpy as jnp
from jax import lax
from jax.experimental import pallas as pl
from jax.experimental.pallas import tpu as pltpu

D_MODEL = 1024
PAST_LEN = 16384
HEAD_DIM = 64
N_HEADS = 8
N_KV_HEADS = 2
GROUP = N_HEADS // N_KV_HEADS
ATTN_WIDTH = N_HEADS * HEAD_DIM
KV_WIDTH = N_KV_HEADS * HEAD_DIM
WINDOW = 128
BLOCK = WINDOW
POOL_WIDTH = D_MODEL - ATTN_WIDTH
POOL_WINDOWS = (2, 4, 8, 16)
POOL_GROUP_WIDTH = 128
POOL_HIST = 15
POOL_PAD = 16
IN_WIDTH = ATTN_WIDTH + 2 * KV_WIDTH + POOL_WIDTH
N_MEM = 256
N_CROSS_HEADS = 4
CROSS_HEAD_DIM = 256
MEM_ROWS = N_CROSS_HEADS * (CROSS_HEAD_DIM // 128)
D_FF = 4 * D_MODEL
FF_CHUNK = 1024
RMS_EPS = 1e-5
NEG_INF = -1e30
Q_SCALE = HEAD_DIM ** -0.5
CQ_SCALE = CROSS_HEAD_DIM ** -0.5
K_OFF = ATTN_WIDTH
V_OFF = ATTN_WIDTH + KV_WIDTH
U_OFF = ATTN_WIDTH + 2 * KV_WIDTH


def _pair_heads(w, axis):
    shape = w.shape
    split = shape[:axis] + (N_KV_HEADS, GROUP, HEAD_DIM) + shape[axis + 1:]
    return jnp.swapaxes(w.reshape(split), axis, axis + 1).reshape(shape)


WOUT_SPLIT = 768

VMEM_LIMIT_BYTES = 56 * 1024 * 1024

_BF = jnp.bfloat16
_F32 = jnp.float32


def _slope(h):
    return 2.0 ** (-8.0 * (h + 1) / N_HEADS)


def _dot(a, b):
    return jnp.dot(a, b, preferred_element_type=_F32)


def _dot_panels(a, panel_refs):
    return jnp.concatenate([_dot(a, ref[...]) for ref in panel_refs], axis=1)


def _rms(x, g):
    ms = jnp.mean(x * x, axis=-1, keepdims=True)
    return x * lax.rsqrt(ms + RMS_EPS) * g


def _softmax_with_sink(s, sink):
    m = jnp.maximum(jnp.max(s, axis=-1, keepdims=True), sink)
    e = jnp.exp(s - m)
    den = jnp.sum(e, axis=-1, keepdims=True) + jnp.exp(sink - m)
    return e * (1.0 / den)


def _softmax_with_sink_parts(s, sink):
    m = jnp.maximum(jnp.max(s, axis=-1, keepdims=True), sink)
    e = jnp.exp(s - m)
    den = jnp.sum(e, axis=-1, keepdims=True) + jnp.exp(sink - m)
    return e, 1.0 / den


def _softmax(s):
    m = jnp.max(s, axis=-1, keepdims=True)
    e = jnp.exp(s - m)
    return e * (1.0 / jnp.sum(e, axis=-1, keepdims=True))


def _window_sums(u_ext, axis):
    out = {}
    s = u_ext
    w = 1
    while w < max(POOL_WINDOWS):
        s = s + pltpu.roll(s, w, axis)
        w *= 2
        out[w] = s
    return out


def _interleave(streams, skew):
    pending = list(streams)
    live = []
    rnd = 0
    while live or pending:
        if pending and rnd % skew == 0:
            live.append(pending.pop(0))
        for g in list(live):
            try:
                next(g)
            except StopIteration:
                live.remove(g)
        rnd += 1


def _ffn_final_stages(x2, gffn, wup_ref, wdown_ref, gfinal, store):
    hn = _rms(x2, gffn).astype(_BF)
    yield
    acc = x2
    for c in range(D_FF // FF_CHUNK):
        hc = _dot(hn, wup_ref[:, c * FF_CHUNK:(c + 1) * FF_CHUNK])
        yield
        hc = jnp.maximum(hc, 0.0)
        hc = (hc * hc).astype(_BF)
        acc = acc + _dot(hc, wdown_ref[c * FF_CHUNK:(c + 1) * FF_CHUNK, :])
        yield
    store(_rms(acc, gfinal))


def _prep_wout_kernel(wout_ref, wpool_ref, pscale_ref, outa_ref, outb_ref):
    blocks = []
    for g in range(GROUP):
        for kv in range(N_KV_HEADS):
            r0 = (kv * GROUP + g) * HEAD_DIM
            blocks.append(wout_ref[r0:r0 + HEAD_DIM, :])
    for g in range(len(POOL_WINDOWS)):
        r0 = ATTN_WIDTH + g * POOL_GROUP_WIDTH
        scaled = wpool_ref[g] * pscale_ref[:, g * POOL_GROUP_WIDTH:(g + 1) * POOL_GROUP_WIDTH]
        blocks.append(jnp.dot(scaled, wout_ref[r0:r0 + POOL_GROUP_WIDTH, :], precision=lax.Precision.HIGHEST,
                              preferred_element_type=_F32))
    full = jnp.concatenate(blocks, axis=0).astype(_BF)
    outa_ref[...] = full[:, :WOUT_SPLIT]
    outb_ref[...] = full[:, WOUT_SPLIT:]


def _prompt_mixer_kernel(x_ref, gmix_ref, win_ref, sink_ref, wouta_ref, woutb_ref, *rest, tile, nseq, ncast):
    cast_src, (qc_ref, mk_ref, mv_ref) = rest[:ncast], rest[ncast:ncast + 3]
    x1_ref, klast_ref, vlast_ref, plast_ref = rest[ncast + 3:ncast + 7]
    cast_dst, os_ref = rest[ncast + 7:2 * ncast + 7], rest[2 * ncast + 7]
    kt_scr, v_scr, u_scr, bias_scr, inv_scr = rest[2 * ncast + 8:]
    b = pl.program_id(0)
    s = pl.program_id(1)
    nb = tile // BLOCK

    @pl.when((b == 0) & (s == 0))
    def _():
        qi = lax.broadcasted_iota(jnp.int32, (BLOCK, 2 * BLOCK), 0)
        kc = lax.broadcasted_iota(jnp.int32, (BLOCK, 2 * BLOCK), 1)
        dist = qi + BLOCK - kc
        valid = (dist >= 0) & (dist <= WINDOW)
        valid_first = valid & (kc >= BLOCK)
        distf = dist.astype(_F32)
        for h in range(N_HEADS):
            ali = -_slope(h) * distf
            bias_scr[0, h] = jnp.where(valid, ali, NEG_INF)
            bias_scr[1, h] = jnp.where(valid_first, ali, NEG_INF)
        prow = lax.broadcasted_iota(jnp.int32, (BLOCK, POOL_GROUP_WIDTH), 0)
        for g, w in enumerate(POOL_WINDOWS):
            inv_scr[0, g] = jnp.full((BLOCK, POOL_GROUP_WIDTH), 1.0 / w, _F32)
            inv_scr[1, g] = 1.0 / jnp.minimum(prow + 1, w).astype(_F32)

    @pl.when(s == 0)
    def _():
        kt_scr[...] = jnp.zeros((KV_WIDTH, BLOCK), _BF)
        v_scr[...] = jnp.zeros((BLOCK, KV_WIDTH), _BF)
        u_scr[...] = jnp.zeros((POOL_PAD, POOL_WIDTH), _F32)

    lane = lax.broadcasted_iota(jnp.int32, (BLOCK, 2 * HEAD_DIM), 1)
    lo = lane < HEAD_DIM
    gmix = gmix_ref[...]
    zeros_kt = jnp.zeros((HEAD_DIM, 2 * BLOCK), _BF)

    def project(j):
        x = x_ref[0, j * BLOCK:(j + 1) * BLOCK, :]
        proj = _dot(_rms(x, gmix).astype(_BF), win_ref[...])
        k = proj[:, K_OFF:K_OFF + KV_WIDTH]
        v = proj[:, V_OFF:V_OFF + KV_WIDTH]
        u = proj[:, U_OFF:]
        kt = k.T
        if j == nb - 1:
            klast_ref[0] = kt
            vlast_ref[0] = v.T
            plast_ref[0] = u[BLOCK - POOL_PAD:]
        return dict(x=x, q=(proj[:, :ATTN_WIDTH] * Q_SCALE).astype(_BF), kt=kt.astype(_BF), v=v.astype(_BF), u=u)

    def pool(j, u_hist, u):
        first = ((s == 0) & (j == 0)).astype(jnp.int32) if j == 0 else 0
        ys = []
        for g, w in enumerate(POOL_WINDOWS):
            c0 = g * POOL_GROUP_WIDTH
            u_ext = jnp.concatenate([u_hist[:, c0:c0 + POOL_GROUP_WIDTH], u[:, c0:c0 + POOL_GROUP_WIDTH]], axis=0)
            sw = _window_sums(u_ext, 0)[w][POOL_PAD:]
            ys.append((sw * inv_scr[first, g] - u_ext[POOL_PAD:]).astype(_BF))
        return ys

    def kv_operands(kt_prev, v_prev, blk):
        kt2 = jnp.concatenate([kt_prev, blk["kt"]], axis=1)
        v2 = jnp.concatenate([v_prev, blk["v"]], axis=0)
        kt_pair = jnp.concatenate([jnp.concatenate([kt2[:HEAD_DIM], zeros_kt], axis=0),
                                   jnp.concatenate([zeros_kt, kt2[HEAD_DIM:]], axis=0)], axis=1)
        lane2 = lax.broadcasted_iota(jnp.int32, v2.shape, 1)
        zero = jnp.zeros_like(v2)
        v_pair = jnp.concatenate([jnp.where(lane2 < HEAD_DIM, v2, zero), jnp.where(lane2 < HEAD_DIM, zero, v2)],
                                 axis=0)
        return kt_pair, v_pair

    def wave_scores(j, blk, kt_pair, pairs):
        first = ((s == 0) & (j == 0)).astype(jnp.int32) if j == 0 else 0
        scores = []
        for p in pairs:
            sc = _dot(blk["q"][:, p * 128:(p + 1) * 128], kt_pair)
            scores.append(sc[:, :2 * BLOCK] + bias_scr[first, p])
            scores.append(sc[:, 2 * BLOCK:] + bias_scr[first, p + GROUP])
        return scores

    def wave_values(v_pair, pairs, scores):
        slabs = []
        for i, p in enumerate(pairs):
            es, invs = [], []
            for half, h in ((0, p), (1, p + GROUP)):
                e, inv = _softmax_with_sink_parts(scores[2 * i + half], sink_ref[h])
                es.append(e.astype(_BF))
                invs.append(inv)
            o = _dot(jnp.concatenate(es, axis=1), v_pair)
            slabs.append((o * jnp.where(lo, invs[0], invs[1])).astype(_BF))
        return slabs

    def output(j, blk, slabs):
        cat = jnp.concatenate(slabs, axis=1)
        x1_ref[0, j * BLOCK:(j + 1) * BLOCK, :] = blk["x"] + _dot_panels(cat, (wouta_ref, woutb_ref))

    heads = [slice(h * CROSS_HEAD_DIM, (h + 1) * CROSS_HEAD_DIM) for h in range(N_CROSS_HEADS)]

    def mem_scores(i):
        q = qc_ref[i * nseq:(i + 1) * nseq, :].astype(_BF)
        return [lax.dot_general(q[:, c], _load_mem_head(mk_ref, i, h).astype(_BF), (((1,), (1,)), ((), ())),
                                preferred_element_type=_F32) for h, c in enumerate(heads)]

    def mem_values(i, scores):
        outs = [_dot(_softmax(sc).astype(_BF), _load_mem_head(mv_ref, i, h).astype(_BF))
                for h, sc in enumerate(scores)]
        os_ref[i * nseq:(i + 1) * nseq, :] = jnp.concatenate(outs, axis=1)

    kt_prev, v_prev, u_hist = kt_scr[...], v_scr[...], u_scr[...]
    blk = project(0)
    done = None
    for j in range(nb):
        kt_pair, v_pair = kv_operands(kt_prev, v_prev, blk)
        sc0 = wave_scores(j, blk, kt_pair, (0, 1))
        nxt = project(j + 1) if j + 1 < nb else None
        msc = mem_scores(j)
        sc1 = wave_scores(j, blk, kt_pair, (2, 3))
        pooled = pool(j, u_hist, blk["u"])
        at0 = wave_values(v_pair, (0, 1), sc0)
        if done is not None:
            output(*done)
        mem_values(j, msc)
        at1 = wave_values(v_pair, (2, 3), sc1)
        done = (j, blk, at0 + at1 + pooled)
        kt_prev, v_prev, u_hist = blk["kt"], blk["v"], blk["u"][BLOCK - POOL_PAD:]
        blk = nxt
    output(*done)
    kt_scr[...] = kt_prev
    v_scr[...] = v_prev
    u_scr[...] = u_hist

    for src, dst in zip(cast_src, cast_dst):
        dst[...] = src[...].astype(_BF)


def _prompt_mixer(x, gmix, win, sinks, wout, tail_weights, qc, mk, mv, tile):
    B, S, D = x.shape
    ns = S // tile
    nsteps = B * ns
    nmem = tile // BLOCK
    nseq = qc.shape[0] // mk.shape[0]
    assert mk.shape[0] == nsteps * nmem
    const = lambda *shape: pl.BlockSpec(shape, lambda b, s: (0,) * len(shape))
    chunk = lambda w: pl.BlockSpec((w.shape[0] // nsteps, w.shape[1]), lambda b, s: (b * ns + s, 0))
    assert all(w.shape[0] % (16 * nsteps) == 0 for w in tail_weights)
    mem_rows = pl.BlockSpec((nmem * nseq, D), lambda b, s: (b * ns + s, 0))
    mem_cache = pl.BlockSpec((nmem, N_MEM * MEM_ROWS, 128), lambda b, s: (b * ns + s, 0, 0))
    return pl.pallas_call(
        functools.partial(_prompt_mixer_kernel, tile=tile, nseq=nseq, ncast=len(tail_weights)),
        grid=(B, ns),
        in_specs=[
            pl.BlockSpec((1, tile, D), lambda b, s: (b, s, 0)),
            const(1, D),
            const(D, IN_WIDTH),
            pl.BlockSpec(memory_space=pltpu.SMEM),
            const(D, WOUT_SPLIT),
            const(D, D - WOUT_SPLIT),
        ] + [chunk(w) for w in tail_weights] + [mem_rows, mem_cache, mem_cache],
        out_specs=[
            pl.BlockSpec((1, tile, D), lambda b, s: (b, s, 0)),
            pl.BlockSpec((1, BLOCK, KV_WIDTH), lambda b, s: (b, 0, 0)),
            pl.BlockSpec((1, BLOCK, KV_WIDTH), lambda b, s: (b, 0, 0)),
            pl.BlockSpec((1, POOL_PAD, POOL_WIDTH), lambda b, s: (b, 0, 0)),
        ] + [chunk(w) for w in tail_weights] + [mem_rows],
        out_shape=[
            jax.ShapeDtypeStruct((B, S, D), _F32),
            jax.ShapeDtypeStruct((B, BLOCK, KV_WIDTH), _F32),
            jax.ShapeDtypeStruct((B, BLOCK, KV_WIDTH), _F32),
            jax.ShapeDtypeStruct((B, POOL_PAD, POOL_WIDTH), _F32),
        ] + [jax.ShapeDtypeStruct(w.shape, _BF) for w in tail_weights] + [jax.ShapeDtypeStruct(qc.shape, _F32)],
        scratch_shapes=[
            pltpu.VMEM((KV_WIDTH, BLOCK), _BF),
            pltpu.VMEM((BLOCK, KV_WIDTH), _BF),
            pltpu.VMEM((POOL_PAD, POOL_WIDTH), _F32),
            pltpu.VMEM((2, N_HEADS, BLOCK, 2 * BLOCK), _F32),
            pltpu.VMEM((2, len(POOL_WINDOWS), BLOCK, POOL_GROUP_WIDTH), _F32),
        ],
        compiler_params=pltpu.CompilerParams(
            dimension_semantics=("arbitrary", "arbitrary"), vmem_limit_bytes=VMEM_LIMIT_BYTES),
        name="prompt_mixer",
    )(x, gmix, win, sinks, *wout, *tail_weights, qc, mk, mv)


def _mem_kv_kernel(mem_ref, gmem_ref, wck_ref, wcv_ref, wcq_ref, wco_ref, k_ref, v_ref, kq_ref, vo_ref, *, nbatch):
    gmem = gmem_ref[...]

    def stream(i):
        hm = _rms(mem_ref[i], gmem).astype(_BF)
        k = _dot(hm, wck_ref[...])
        yield
        v = _dot(hm, wcv_ref[...])
        yield
        for h in range(N_CROSS_HEADS):
            head = slice(h * CROSS_HEAD_DIM, (h + 1) * CROSS_HEAD_DIM)
            kq_ref[i, :, head] = (_dot(wcq_ref[:, head], k[:, head].T.astype(_BF)) * CQ_SCALE).astype(_BF)
            vo_ref[i, head, :] = _dot(v[:, head].astype(_BF), wco_ref[head, :]).astype(_BF)
            for half in range(CROSS_HEAD_DIM // 128):
                c0 = h * CROSS_HEAD_DIM + half * 128
                rows = pl.ds(half * N_CROSS_HEADS + h, N_MEM, stride=MEM_ROWS)
                k_ref[i, rows, :] = k[:, c0:c0 + 128]
                v_ref[i, rows, :] = v[:, c0:c0 + 128]
            yield

    _interleave([stream(i) for i in range(nbatch)], MEM_KV_SKEW)


def _mem_kv(mem, gmem, wck, wcv, wcq, wco, nbatch):
    B, M, D = mem.shape
    HM = N_CROSS_HEADS * M
    const = lambda *shape: pl.BlockSpec(shape, lambda b: (0,) * len(shape))
    per_seq = lambda *shape: pl.BlockSpec((nbatch,) + shape, lambda b: (b, 0, 0))
    return pl.pallas_call(
        functools.partial(_mem_kv_kernel, nbatch=nbatch),
        grid=(B // nbatch,),
        in_specs=[per_seq(M, D), const(1, D), const(D, D), const(D, D), const(D, D), const(D, D)],
        out_specs=[per_seq(M * MEM_ROWS, 128), per_seq(M * MEM_ROWS, 128), per_seq(D, HM), per_seq(HM, D)],
        out_shape=[
            jax.ShapeDtypeStruct((B, M * MEM_ROWS, 128), _F32),
            jax.ShapeDtypeStruct((B, M * MEM_ROWS, 128), _F32),
            jax.ShapeDtypeStruct((B, D, HM), _BF),
            jax.ShapeDtypeStruct((B, HM, D), _BF),
        ],
        compiler_params=pltpu.CompilerParams(
            dimension_semantics=("arbitrary",), vmem_limit_bytes=VMEM_LIMIT_BYTES),
        name="prompt_mem_kv",
    )(mem, gmem, wck, wcv, wcq, wco)


def _mem_cache_unrows(rows):
    nb = rows.shape[0]
    c = rows.reshape(nb, N_MEM, CROSS_HEAD_DIM // 128, N_CROSS_HEADS, 128)
    return c.transpose(0, 1, 3, 2, 4).reshape(nb, N_MEM, N_CROSS_HEADS, CROSS_HEAD_DIM)


def _prompt_tail_kernel(x1_ref, gcross_ref, kq_ref, vo_ref, gffn_ref, wup_ref, wdown_ref, gfinal_ref, y_ref,
                        *, tile, sub):
    gcross, gffn, gfinal = gcross_ref[...], gffn_ref[...], gfinal_ref[...]

    def stream(r0):
        x1 = x1_ref[0, r0:r0 + sub, :]
        hn = _rms(x1, gcross).astype(_BF)
        yield
        scores = _dot(hn, kq_ref[0])
        yield
        probs = [_softmax(scores[:, h * N_MEM:(h + 1) * N_MEM]).astype(_BF) for h in range(N_CROSS_HEADS)]
        x2 = x1 + _dot(jnp.concatenate(probs, axis=1), vo_ref[0])
        yield

        def store(y):
            y_ref[0, r0:r0 + sub, :] = y

        yield from _ffn_final_stages(x2, gffn, wup_ref, wdown_ref, gfinal, store)

    _interleave([stream(r0) for r0 in range(0, tile, sub)], TAIL_SKEW)


def _single(shape, index_map):
    return pl.BlockSpec(shape, index_map, pipeline_mode=pl.Buffered(1))


def _prompt_tail(x1, gcross, kq, vo, gffn, wup, wdown, gfinal, tile, sub):
    B, S, D = x1.shape
    const = lambda *shape: _single(shape, lambda b, s: (0,) * len(shape))
    return pl.pallas_call(
        functools.partial(_prompt_tail_kernel, tile=tile, sub=sub),
        grid=(B, S // tile),
        in_specs=[
            pl.BlockSpec((1, tile, D), lambda b, s: (b, s, 0)),
            const(1, D),
            pl.BlockSpec((1, D, N_CROSS_HEADS * N_MEM), lambda b, s: (b, 0, 0)),
            pl.BlockSpec((1, N_CROSS_HEADS * N_MEM, D), lambda b, s: (b, 0, 0)),
            const(1, D),
            const(D, D_FF),
            const(D_FF, D),
            const(1, D),
        ],
        out_specs=pl.BlockSpec((1, tile, D), lambda b, s: (b, s, 0)),
        out_shape=jax.ShapeDtypeStruct((B, S, D), _F32),
        compiler_params=pltpu.CompilerParams(
            dimension_semantics=("arbitrary", "arbitrary"), vmem_limit_bytes=VMEM_LIMIT_BYTES),
        name="prompt_tail",
    )(x1, gcross, kq, vo, gffn, wup, wdown, gfinal)


def _sample_mixer_kernel(x_ref, gmix_ref, win_ref, sinkcol_ref, ck_ref, cv_ref, st_ref,
                         wout_ref, wpool_ref, pscale_ref, gcross_ref, wcq_ref,
                         x1_ref, qc_ref, wk_ref, wv_ref, pool_ref, wcq_bf_ref, wouta_ref, woutb_ref,
                         kc_scr, vc_scr, u_scr, *, nbatch, sub, nseq, past_len):
    @pl.when(pl.program_id(0) == 0)
    def _():
        wcq_bf_ref[...] = wcq_ref[...].astype(_BF)
        _prep_wout_kernel(wout_ref, wpool_ref, pscale_ref, wouta_ref, woutb_ref)

    rows = sub * nseq
    nkeys = 2 * WINDOW
    gmix, gcross = gmix_ref[...], gcross_ref[...]
    sinkcol = sinkcol_ref[...][None]

    t = lax.broadcasted_iota(jnp.int32, (nseq, nkeys), 0)
    c = lax.broadcasted_iota(jnp.int32, (nseq, nkeys), 1)
    dist = jnp.where(c < WINDOW, WINDOW + t - c, t - (c - WINDOW))
    valid = (dist >= 0) & (dist <= WINDOW) & (c < WINDOW + nseq)
    distf = dist.astype(_F32)
    bias = jnp.concatenate([jnp.where(valid, -_slope(h) * distf, NEG_INF) for h in range(N_HEADS)], axis=0)[None]
    lane = lax.broadcasted_iota(jnp.int32, (rows, 2 * HEAD_DIM), 1)
    lo = lane < HEAD_DIM
    tpos = past_len + lax.broadcasted_iota(jnp.int32, (nseq, POOL_GROUP_WIDTH), 0)
    zpad = jnp.zeros((sub, nkeys - WINDOW, KV_WIDTH), _BF)

    def stream(b0):
        bs = slice(b0, b0 + sub)
        rs = slice(b0 * nseq, (b0 + sub) * nseq)
        x = x_ref[rs, :]
        proj = _dot(_rms(x, gmix).astype(_BF), win_ref[...])
        yield
        k_new = proj[:, K_OFF:K_OFF + KV_WIDTH].reshape(sub, nseq, KV_WIDTH)
        v_new = proj[:, V_OFF:V_OFF + KV_WIDTH].reshape(sub, nseq, KV_WIDTH)
        ck = jnp.swapaxes(ck_ref[bs], 1, 2)
        cv = jnp.swapaxes(cv_ref[bs], 1, 2)
        wk_ref[bs, :WINDOW - nseq, :] = ck[:, nseq:, :]
        wk_ref[bs, WINDOW - nseq:, :] = k_new
        wv_ref[bs, :WINDOW - nseq, :] = cv[:, nseq:, :]
        wv_ref[bs, WINDOW - nseq:, :] = v_new
        kc_scr[bs, WINDOW:, :] = zpad
        vc_scr[bs, WINDOW:, :] = zpad
        kc_scr[bs, :WINDOW, :] = ck.astype(_BF)
        vc_scr[bs, :WINDOW, :] = cv.astype(_BF)
        kc_scr[bs, WINDOW:WINDOW + nseq, :] = k_new.astype(_BF)
        vc_scr[bs, WINDOW:WINDOW + nseq, :] = v_new.astype(_BF)
        qsc = proj[:, :ATTN_WIDTH] * Q_SCALE
        q_lo, q_hi = [], []
        for p in range(GROUP):
            slab = qsc[:, p * 128:(p + 1) * 128]
            q_lo.append(jnp.where(lo, slab, 0.0).reshape(sub, nseq, 128))
            q_hi.append(jnp.where(lo, 0.0, slab).reshape(sub, nseq, 128))
        qm = jnp.concatenate(q_lo + q_hi, axis=1).astype(_BF)
        sc = jnp.einsum('bqd,bkd->bqk', qm, kc_scr[bs], preferred_element_type=_F32) + bias
        yield
        pr = _softmax_with_sink(sc, sinkcol).astype(_BF)
        o = jnp.einsum('bqk,bkd->bqd', pr, vc_scr[bs], preferred_element_type=_F32)
        yield
        attn = []
        for p in range(GROUP):
            o_lo = o[:, p * nseq:(p + 1) * nseq, :].reshape(rows, 128)
            o_hi = o[:, (p + GROUP) * nseq:(p + GROUP + 1) * nseq, :].reshape(rows, 128)
            attn.append(jnp.where(lo, o_lo, o_hi).astype(_BF))
        ext = POOL_PAD + nseq
        seq_rows = lambda r: pl.ds(b0 * ext + r, sub, stride=ext)
        pooled = []
        for g, w in enumerate(POOL_WINDOWS):
            c0 = g * POOL_GROUP_WIDTH
            u_scr[g, seq_rows(0), :] = jnp.zeros((sub, POOL_GROUP_WIDTH), _F32)
            for r in range(POOL_HIST):
                u_scr[g, seq_rows(POOL_PAD - POOL_HIST + r), :] = st_ref[r, bs, c0:c0 + POOL_GROUP_WIDTH]
            for i in range(sub):
                u_scr[g, (b0 + i) * ext + POOL_PAD:(b0 + i + 1) * ext, :] = (
                    proj[i * nseq:(i + 1) * nseq, U_OFF + c0:U_OFF + c0 + POOL_GROUP_WIDTH])
            for r in range(POOL_HIST):
                pool_ref[r, bs, c0:c0 + POOL_GROUP_WIDTH] = u_scr[g, seq_rows(ext - POOL_HIST + r), :]
            ug = u_scr[g, b0 * ext:(b0 + sub) * ext, :].reshape(sub, ext, POOL_GROUP_WIDTH)
            sw = _window_sums(ug, 1)[w][:, POOL_PAD:, :]
            cnt = jnp.minimum(tpos + 1, w).astype(_F32)
            pooled.append((sw / cnt[None] - ug[:, POOL_PAD:, :]).reshape(rows, POOL_GROUP_WIDTH).astype(_BF))
        yield
        x1 = x + _dot_panels(jnp.concatenate(attn + pooled, axis=1), (wouta_ref, woutb_ref))
        x1_ref[rs, :] = x1
        yield
        qc_ref[rs, :] = _dot(_rms(x1, gcross).astype(_BF), wcq_bf_ref[...]) * CQ_SCALE

    _interleave([stream(b0) for b0 in range(0, nbatch, sub)], SAMPLE_MIXER_SKEW)


def _sample_mixer(x, gmix, win, sinkcol, ck, cv, st, wout, wpool, pscale, gcross, wcq, nbatch, sub, nseq,
                  past_len):
    R, D = x.shape
    rows = nbatch * nseq
    const = lambda *shape: pl.BlockSpec(shape, lambda i: (0,) * len(shape))
    return pl.pallas_call(
        functools.partial(_sample_mixer_kernel, nbatch=nbatch, sub=sub, nseq=nseq, past_len=past_len),
        grid=(R // rows,),
        in_specs=[
            pl.BlockSpec((rows, D), lambda i: (i, 0)),
            const(1, D),
            const(D, IN_WIDTH),
            const(N_HEADS * nseq, 1),
            pl.BlockSpec((nbatch, WINDOW, KV_WIDTH), lambda i: (i, 0, 0)),
            pl.BlockSpec((nbatch, WINDOW, KV_WIDTH), lambda i: (i, 0, 0)),
            pl.BlockSpec((POOL_HIST, nbatch, POOL_WIDTH), lambda i: (0, i, 0)),
            const(D, D),
            const(len(POOL_WINDOWS), POOL_GROUP_WIDTH, POOL_GROUP_WIDTH),
            const(1, POOL_WIDTH),
            const(1, D),
            const(D, D),
        ],
        out_specs=[
            pl.BlockSpec((rows, D), lambda i: (i, 0)),
            pl.BlockSpec((rows, D), lambda i: (i, 0)),
            pl.BlockSpec((nbatch, WINDOW, KV_WIDTH), lambda i: (i, 0, 0)),
            pl.BlockSpec((nbatch, WINDOW, KV_WIDTH), lambda i: (i, 0, 0)),
            pl.BlockSpec((POOL_HIST, nbatch, POOL_WIDTH), lambda i: (0, i, 0)),
            const(D, D),
            const(D, WOUT_SPLIT),
            const(D, D - WOUT_SPLIT),
        ],
        out_shape=[
            jax.ShapeDtypeStruct((R, D), _F32),
            jax.ShapeDtypeStruct((R, D), _F32),
            jax.ShapeDtypeStruct((R // nseq, WINDOW, KV_WIDTH), _F32),
            jax.ShapeDtypeStruct((R // nseq, WINDOW, KV_WIDTH), _F32),
            jax.ShapeDtypeStruct((POOL_HIST, R // nseq, POOL_WIDTH), _F32),
            jax.ShapeDtypeStruct((D, D), _BF),
            jax.ShapeDtypeStruct((D, WOUT_SPLIT), _BF),
            jax.ShapeDtypeStruct((D, D - WOUT_SPLIT), _BF),
        ],
        scratch_shapes=[
            pltpu.VMEM((nbatch, 2 * WINDOW, KV_WIDTH), _BF),
            pltpu.VMEM((nbatch, 2 * WINDOW, KV_WIDTH), _BF),
            pltpu.VMEM((len(POOL_WINDOWS), nbatch * (POOL_PAD + nseq), POOL_GROUP_WIDTH), _F32),
        ],
        compiler_params=pltpu.CompilerParams(
            dimension_semantics=("arbitrary",), vmem_limit_bytes=VMEM_LIMIT_BYTES),
        name="sample_mixer",
    )(x, gmix, win, sinkcol, ck, cv, st, wout, wpool, pscale, gcross, wcq)


def _mem_cache_rows(cache):
    nb = cache.shape[0]
    c = cache.reshape(nb, N_MEM, N_CROSS_HEADS, CROSS_HEAD_DIM // 128, 128)
    return c.transpose(0, 1, 3, 2, 4).reshape(nb, N_MEM * MEM_ROWS, 128)


def _load_mem_head(ref, b, h):
    halves = [ref[b, pl.ds(half * N_CROSS_HEADS + h, N_MEM, stride=MEM_ROWS), :]
              for half in range(CROSS_HEAD_DIM // 128)]
    return jnp.concatenate(halves, axis=1)


def _sample_tail_kernel(x1_ref, o_ref, wco_ref, gffn_ref, wup_ref, wdown_ref, gfinal_ref, y_ref, *, tile, sub):
    gffn, gfinal = gffn_ref[...], gfinal_ref[...]

    def stream(r0):
        x2 = x1_ref[r0:r0 + sub, :] + _dot(o_ref[r0:r0 + sub, :].astype(_BF), wco_ref[...])
        yield

        def store(y):
            y_ref[r0:r0 + sub, :] = y

        yield from _ffn_final_stages(x2, gffn, wup_ref, wdown_ref, gfinal, store)

    _interleave([stream(r0) for r0 in range(0, tile, sub)], TAIL_SKEW)


def _sample_tail(x1, o, wco, gffn, wup, wdown, gfinal, tile, sub):
    R, D = x1.shape
    const = lambda *shape: _single(shape, lambda i: (0,) * len(shape))
    return pl.pallas_call(
        functools.partial(_sample_tail_kernel, tile=tile, sub=sub),
        grid=(R // tile,),
        in_specs=[
            pl.BlockSpec((tile, D), lambda i: (i, 0)),
            pl.BlockSpec((tile, D), lambda i: (i, 0)),
            const(D, D),
            const(1, D),
            const(D, D_FF),
            const(D_FF, D),
            const(1, D),
        ],
        out_specs=pl.BlockSpec((tile, D), lambda i: (i, 0)),
        out_shape=jax.ShapeDtypeStruct((R, D), _F32),
        compiler_params=pltpu.CompilerParams(
            dimension_semantics=("arbitrary",), vmem_limit_bytes=VMEM_LIMIT_BYTES),
        name="sample_tail",
    )(x1, o, wco, gffn, wup, wdown, gfinal)


PROMPT_TILE = 512
TAIL_TILE = 1024
TAIL_SUB = 256
TAIL_SKEW = 4
MEM_KV_BATCH = 2
MEM_KV_SKEW = 2
SAMPLE_MIXER_BATCH = 32
SAMPLE_MIXER_SUB = 16
SAMPLE_MIXER_SKEW = 3


def kernel(x_prompt, x_sample, cache_win_k, cache_win_v, state_pool, cache_mem_k, cache_mem_v, mem_prompt,
           g_mix, w_in, attn_sinks, w_pool, pool_scale, w_out, g_cross, g_mem, w_cq, w_ck, w_cv, w_co,
           g_ffn, w_up, w_down, g_final):
    depth = g_mix.shape[0]
    assert depth == 1, "one layer per step"
    B, S, D = x_prompt.shape
    DB, T, _ = x_sample.shape
    past_len = PAST_LEN
    l = 0

    win = jnp.concatenate([_pair_heads(w_in[l][:, :ATTN_WIDTH], 1), w_in[l][:, ATTN_WIDTH:]], axis=1).astype(_BF)
    gmix, gcross, gmem, gffn = (g[l].reshape(1, D) for g in (g_mix, g_cross, g_mem, g_ffn))
    gfinal = g_final.reshape(1, D)
    sinks = attn_sinks[l]

    xs = x_sample.reshape(DB * T, D)
    ck = cache_win_k[l].transpose(0, 2, 3, 1).reshape(DB, KV_WIDTH, WINDOW)
    cv = cache_win_v[l].transpose(0, 2, 3, 1).reshape(DB, KV_WIDTH, WINDOW)
    st = state_pool[l].transpose(1, 0, 2)
    sinkcol = jnp.repeat(sinks, T).reshape(N_HEADS * T, 1)
    x1s, qc, wk_s, wv_s, pool_s, wcq, *wout = _sample_mixer(
        xs, gmix, win, sinkcol, ck, cv, st, w_out[l], w_pool[l], pool_scale[l].reshape(1, POOL_WIDTH), gcross,
        w_cq[l], SAMPLE_MIXER_BATCH, SAMPLE_MIXER_SUB, T, past_len)

    mk = _mem_cache_rows(cache_mem_k[l])
    mv = _mem_cache_rows(cache_mem_v[l])
    x1p, klast, vlast, plast, wco, wup, wdown, wck, wcv, o_s = _prompt_mixer(
        x_prompt, gmix, win, sinks, wout, (w_co[l], w_up[l], w_down[l], w_ck[l], w_cv[l]), qc, mk, mv, PROMPT_TILE)
    mem_k, mem_v, mem_kq, mem_vo = _mem_kv(mem_prompt, gmem, wck, wcv, wcq, wco, MEM_KV_BATCH)
    y_prompt = _prompt_tail(x1p, gcross, mem_kq, mem_vo, gffn, wup, wdown, gfinal, TAIL_TILE, TAIL_SUB)

    y_sample = _sample_tail(x1s, o_s, wco, gffn, wup, wdown, gfinal, TAIL_TILE, TAIL_SUB).reshape(DB, T, D)

    return (
        y_prompt,
        y_sample,
        klast.reshape(B, N_KV_HEADS, HEAD_DIM, WINDOW).transpose(0, 3, 1, 2)[None],
        vlast.reshape(B, N_KV_HEADS, HEAD_DIM, WINDOW).transpose(0, 3, 1, 2)[None],
        plast[:, POOL_PAD - POOL_HIST:, :][None],
        _mem_cache_unrows(mem_k)[None],
        _mem_cache_unrows(mem_v)[None],
        wk_s.reshape(1, DB, WINDOW, N_KV_HEADS, HEAD_DIM),
        wv_s.reshape(1, DB, WINDOW, N_KV_HEADS, HEAD_DIM),
        pool_s.transpose(1, 0, 2)[None],
    )
```

```python
import functools

import jax
import jax.numpy as jnp
from jax import lax
from jax.experimental import pallas as pl
from jax.experimental.pallas import tpu as pltpu

D_MODEL = 1024
PAST_LEN = 16384
HEAD_DIM = 64
N_HEADS = 8
N_KV_HEADS = 2
GROUP = N_HEADS // N_KV_HEADS
ATTN_WIDTH = N_HEADS * HEAD_DIM
KV_WIDTH = N_KV_HEADS * HEAD_DIM
WINDOW = 128
BLOCK = WINDOW
POOL_WIDTH = D_MODEL - ATTN_WIDTH
POOL_WINDOWS = (2, 4, 8, 16)
POOL_GROUP_WIDTH = 128
POOL_HIST = 15
POOL_PAD = 16
IN_WIDTH = ATTN_WIDTH + 2 * KV_WIDTH + POOL_WIDTH
N_MEM = 256
N_CROSS_HEADS = 4
CROSS_HEAD_DIM = 256
MEM_ROWS = N_CROSS_HEADS * (CROSS_HEAD_DIM // 128)
D_FF = 4 * D_MODEL
FF_CHUNK = 1024
RMS_EPS = 1e-5
NEG_INF = -1e30
Q_SCALE = HEAD_DIM ** -0.5
CQ_SCALE = CROSS_HEAD_DIM ** -0.5
K_OFF = ATTN_WIDTH
V_OFF = ATTN_WIDTH + KV_WIDTH
U_OFF = ATTN_WIDTH + 2 * KV_WIDTH


def _pair_heads(w, axis):
    shape = w.shape
    split = shape[:axis] + (N_KV_HEADS, GROUP, HEAD_DIM) + shape[axis + 1:]
    return jnp.swapaxes(w.reshape(split), axis, axis + 1).reshape(shape)


WOUT_SPLIT = 768

VMEM_LIMIT_BYTES = 56 * 1024 * 1024

_BF = jnp.bfloat16
_F32 = jnp.float32


def _slope(h):
    return 2.0 ** (-8.0 * (h + 1) / N_HEADS)


def _dot(a, b):
    return jnp.dot(a, b, preferred_element_type=_F32)


def _dot_panels(a, panel_refs):
    return jnp.concatenate([_dot(a, ref[...]) for ref in panel_refs], axis=1)


def _rms(x, g):
    ms = jnp.mean(x * x, axis=-1, keepdims=True)
    return x * lax.rsqrt(ms + RMS_EPS) * g


def _softmax_with_sink(s, sink):
    m = jnp.maximum(jnp.max(s, axis=-1, keepdims=True), sink)
    e = jnp.exp(s - m)
    den = jnp.sum(e, axis=-1, keepdims=True) + jnp.exp(sink - m)
    return e * (1.0 / den)


def _softmax_with_sink_parts(s, sink):
    m = jnp.maximum(jnp.max(s, axis=-1, keepdims=True), sink)
    e = jnp.exp(s - m)
    den = jnp.sum(e, axis=-1, keepdims=True) + jnp.exp(sink - m)
    return e, 1.0 / den


def _softmax(s):
    m = jnp.max(s, axis=-1, keepdims=True)
    e = jnp.exp(s - m)
    return e * (1.0 / jnp.sum(e, axis=-1, keepdims=True))


def _window_sums(u_ext, axis):
    out = {}
    s = u_ext
    w = 1
    while w < max(POOL_WINDOWS):
        s = s + pltpu.roll(s, w, axis)
        w *= 2
        out[w] = s
    return out


def _interleave(streams, skew):
    pending = list(streams)
    live = []
    rnd = 0
    while live or pending:
        if pending and rnd % skew == 0:
            live.append(pending.pop(0))
        for g in list(live):
            try:
                next(g)
            except StopIteration:
                live.remove(g)
        rnd += 1


def _ffn_final_stages(x2, gffn, wup_ref, wdown_ref, gfinal, store):
    hn = _rms(x2, gffn).astype(_BF)
    yield
    acc = x2
    for c in range(D_FF // FF_CHUNK):
        hc = _dot(hn, wup_ref[:, c * FF_CHUNK:(c + 1) * FF_CHUNK])
        yield
        hc = jnp.maximum(hc, 0.0)
        hc = (hc * hc).astype(_BF)
        acc = acc + _dot(hc, wdown_ref[c * FF_CHUNK:(c + 1) * FF_CHUNK, :])
        yield
    store(_rms(acc, gfinal))


def _prep_wout_kernel(wout_ref, wpool_ref, pscale_ref, outa_ref, outb_ref):
    blocks = []
    for g in range(GROUP):
        for kv in range(N_KV_HEADS):
            r0 = (kv * GROUP + g) * HEAD_DIM
            blocks.append(wout_ref[r0:r0 + HEAD_DIM, :])
    for g in range(len(POOL_WINDOWS)):
        r0 = ATTN_WIDTH + g * POOL_GROUP_WIDTH
        scaled = wpool_ref[g] * pscale_ref[:, g * POOL_GROUP_WIDTH:(g + 1) * POOL_GROUP_WIDTH]
        blocks.append(jnp.dot(scaled, wout_ref[r0:r0 + POOL_GROUP_WIDTH, :], precision=lax.Precision.HIGHEST,
                              preferred_element_type=_F32))
    full = jnp.concatenate(blocks, axis=0).astype(_BF)
    outa_ref[...] = full[:, :WOUT_SPLIT]
    outb_ref[...] = full[:, WOUT_SPLIT:]


def _prompt_mixer_kernel(x_ref, gmix_ref, win_ref, sink_ref, wouta_ref, woutb_ref, *rest, tile, nseq, ncast):
    cast_src, (qc_ref, mk_ref, mv_ref) = rest[:ncast], rest[ncast:ncast + 3]
    x1_ref, klast_ref, vlast_ref, plast_ref = rest[ncast + 3:ncast + 7]
    cast_dst, os_ref = rest[ncast + 7:2 * ncast + 7], rest[2 * ncast + 7]
    kt_scr, v_scr, u_scr, bias_scr, inv_scr = rest[2 * ncast + 8:]
    b = pl.program_id(0)
    s = pl.program_id(1)
    nb = tile // BLOCK

    @pl.when((b == 0) & (s == 0))
    def _():
        qi = lax.broadcasted_iota(jnp.int32, (BLOCK, 2 * BLOCK), 0)
        kc = lax.broadcasted_iota(jnp.int32, (BLOCK, 2 * BLOCK), 1)
        dist = qi + BLOCK - kc
        valid = (dist >= 0) & (dist <= WINDOW)
        valid_first = valid & (kc >= BLOCK)
        distf = dist.astype(_F32)
        for h in range(N_HEADS):
            ali = -_slope(h) * distf
            bias_scr[0, h] = jnp.where(valid, ali, NEG_INF)
            bias_scr[1, h] = jnp.where(valid_first, ali, NEG_INF)
        prow = lax.broadcasted_iota(jnp.int32, (BLOCK, POOL_GROUP_WIDTH), 0)
        for g, w in enumerate(POOL_WINDOWS):
            inv_scr[0, g] = jnp.full((BLOCK, POOL_GROUP_WIDTH), 1.0 / w, _F32)
            inv_scr[1, g] = 1.0 / jnp.minimum(prow + 1, w).astype(_F32)

    @pl.when(s == 0)
    def _():
        kt_scr[...] = jnp.zeros((KV_WIDTH, BLOCK), _BF)
        v_scr[...] = jnp.zeros((BLOCK, KV_WIDTH), _BF)
        u_scr[...] = jnp.zeros((POOL_PAD, POOL_WIDTH), _F32)

    lane = lax.broadcasted_iota(jnp.int32, (BLOCK, 2 * HEAD_DIM), 1)
    lo = lane < HEAD_DIM
    gmix = gmix_ref[...]
    zeros_kt = jnp.zeros((HEAD_DIM, 2 * BLOCK), _BF)

    def project(j):
        x = x_ref[0, j * BLOCK:(j + 1) * BLOCK, :]
        proj = _dot(_rms(x, gmix).astype(_BF), win_ref[...])
        k = proj[:, K_OFF:K_OFF + KV_WIDTH]
        v = proj[:, V_OFF:V_OFF + KV_WIDTH]
        u = proj[:, U_OFF:]
        kt = k.T
        if j == nb - 1:
            klast_ref[0] = kt
            vlast_ref[0] = v.T
            plast_ref[0] = u[BLOCK - POOL_PAD:]
        return dict(x=x, q=(proj[:, :ATTN_WIDTH] * Q_SCALE).astype(_BF), kt=kt.astype(_BF), v=v.astype(_BF), u=u)

    def pool(j, u_hist, u):
        first = ((s == 0) & (j == 0)).astype(jnp.int32) if j == 0 else 0
        ys = []
        for g, w in enumerate(POOL_WINDOWS):
            c0 = g * POOL_GROUP_WIDTH
            u_ext = jnp.concatenate([u_hist[:, c0:c0 + POOL_GROUP_WIDTH], u[:, c0:c0 + POOL_GROUP_WIDTH]], axis=0)
            sw = _window_sums(u_ext, 0)[w][POOL_PAD:]
            ys.append((sw * inv_scr[first, g] - u_ext[POOL_PAD:]).astype(_BF))
        return ys

    def kv_operands(kt_prev, v_prev, blk):
        kt2 = jnp.concatenate([kt_prev, blk["kt"]], axis=1)
        v2 = jnp.concatenate([v_prev, blk["v"]], axis=0)
        kt_pair = jnp.concatenate([jnp.concatenate([kt2[:HEAD_DIM], zeros_kt], axis=0),
                                   jnp.concatenate([zeros_kt, kt2[HEAD_DIM:]], axis=0)], axis=1)
        lane2 = lax.broadcasted_iota(jnp.int32, v2.shape, 1)
        zero = jnp.zeros_like(v2)
        v_pair = jnp.concatenate([jnp.where(lane2 < HEAD_DIM, v2, zero), jnp.where(lane2 < HEAD_DIM, zero, v2)],
                                 axis=0)
        return kt_pair, v_pair

    def wave_scores(j, blk, kt_pair, pairs):
        first = ((s == 0) & (j == 0)).astype(jnp.int32) if j == 0 else 0
        scores = []
        for p in pairs:
            sc = _dot(blk["q"][:, p * 128:(p + 1) * 128], kt_pair)
            scores.append(sc[:, :2 * BLOCK] + bias_scr[first, p])
            scores.append(sc[:, 2 * BLOCK:] + bias_scr[first, p + GROUP])
        return scores

    def wave_values(v_pair, pairs, scores):
        slabs = []
        for i, p in enumerate(pairs):
            es, invs = [], []
            for half, h in ((0, p), (1, p + GROUP)):
                e, inv = _softmax_with_sink_parts(scores[2 * i + half], sink_ref[h])
                es.append(e.astype(_BF))
                invs.append(inv)
            o = _dot(jnp.concatenate(es, axis=1), v_pair)
            slabs.append((o * jnp.where(lo, invs[0], invs[1])).astype(_BF))
        return slabs

    def output(j, blk, slabs):
        cat = jnp.concatenate(slabs, axis=1)
        x1_ref[0, j * BLOCK:(j + 1) * BLOCK, :] = blk["x"] + _dot_panels(cat, (wouta_ref, woutb_ref))

    heads = [slice(h * CROSS_HEAD_DIM, (h + 1) * CROSS_HEAD_DIM) for h in range(N_CROSS_HEADS)]

    def mem_scores(i):
        q = qc_ref[i * nseq:(i + 1) * nseq, :].astype(_BF)
        return [lax.dot_general(q[:, c], _load_mem_head(mk_ref, i, h).astype(_BF), (((1,), (1,)), ((), ())),
                                preferred_element_type=_F32) for h, c in enumerate(heads)]

    def mem_values(i, scores):
        outs = [_dot(_softmax(sc).astype(_BF), _load_mem_head(mv_ref, i, h).astype(_BF))
                for h, sc in enumerate(scores)]
        os_ref[i * nseq:(i + 1) * nseq, :] = jnp.concatenate(outs, axis=1)

    kt_prev, v_prev, u_hist = kt_scr[...], v_scr[...], u_scr[...]
    blk = project(0)
    done = None
    for j in range(nb):
        kt_pair, v_pair = kv_operands(kt_prev, v_prev, blk)
        sc0 = wave_scores(j, blk, kt_pair, (0, 1))
        nxt = project(j + 1) if j + 1 < nb else None
        msc = mem_scores(j)
        sc1 = wave_scores(j, blk, kt_pair, (2, 3))
        pooled = pool(j, u_hist, blk["u"])
        at0 = wave_values(v_pair, (0, 1), sc0)
        if done is not None:
            output(*done)
        mem_values(j, msc)
        at1 = wave_values(v_pair, (2, 3), sc1)
        done = (j, blk, at0 + at1 + pooled)
        kt_prev, v_prev, u_hist = blk["kt"], blk["v"], blk["u"][BLOCK - POOL_PAD:]
        blk = nxt
    output(*done)
    kt_scr[...] = kt_prev
    v_scr[...] = v_prev
    u_scr[...] = u_hist

    for src, dst in zip(cast_src, cast_dst):
        dst[...] = src[...].astype(_BF)


def _prompt_mixer(x, gmix, win, sinks, wout, tail_weights, qc, mk, mv, tile):
    B, S, D = x.shape
    ns = S // tile
    nsteps = B * ns
    nmem = tile // BLOCK
    nseq = qc.shape[0] // mk.shape[0]
    assert mk.shape[0] == nsteps * nmem
    const = lambda *shape: pl.BlockSpec(shape, lambda b, s: (0,) * len(shape))
    chunk = lambda w: pl.BlockSpec((w.shape[0] // nsteps, w.shape[1]), lambda b, s: (b * ns + s, 0))
    assert all(w.shape[0] % (16 * nsteps) == 0 for w in tail_weights)
    mem_rows = pl.BlockSpec((nmem * nseq, D), lambda b, s: (b * ns + s, 0))
    mem_cache = pl.BlockSpec((nmem, N_MEM * MEM_ROWS, 128), lambda b, s: (b * ns + s, 0, 0))
    return pl.pallas_call(
        functools.partial(_prompt_mixer_kernel, tile=tile, nseq=nseq, ncast=len(tail_weights)),
        grid=(B, ns),
        in_specs=[
            pl.BlockSpec((1, tile, D), lambda b, s: (b, s, 0)),
            const(1, D),
            const(D, IN_WIDTH),
            pl.BlockSpec(memory_space=pltpu.SMEM),
            const(D, WOUT_SPLIT),
            const(D, D - WOUT_SPLIT),
        ] + [chunk(w) for w in tail_weights] + [mem_rows, mem_cache, mem_cache],
        out_specs=[
            pl.BlockSpec((1, tile, D), lambda b, s: (b, s, 0)),
            pl.BlockSpec((1, BLOCK, KV_WIDTH), lambda b, s: (b, 0, 0)),
            pl.BlockSpec((1, BLOCK, KV_WIDTH), lambda b, s: (b, 0, 0)),
            pl.BlockSpec((1, POOL_PAD, POOL_WIDTH), lambda b, s: (b, 0, 0)),
        ] + [chunk(w) for w in tail_weights] + [mem_rows],
        out_shape=[
            jax.ShapeDtypeStruct((B, S, D), _F32),
            jax.ShapeDtypeStruct((B, BLOCK, KV_WIDTH), _F32),
            jax.ShapeDtypeStruct((B, BLOCK, KV_WIDTH), _F32),
            jax.ShapeDtypeStruct((B, POOL_PAD, POOL_WIDTH), _F32),
        ] + [jax.ShapeDtypeStruct(w.shape, _BF) for w in tail_weights] + [jax.ShapeDtypeStruct(qc.shape, _F32)],
        scratch_shapes=[
            pltpu.VMEM((KV_WIDTH, BLOCK), _BF),
            pltpu.VMEM((BLOCK, KV_WIDTH), _BF),
            pltpu.VMEM((POOL_PAD, POOL_WIDTH), _F32),
            pltpu.VMEM((2, N_HEADS, BLOCK, 2 * BLOCK), _F32),
            pltpu.VMEM((2, len(POOL_WINDOWS), BLOCK, POOL_GROUP_WIDTH), _F32),
        ],
        compiler_params=pltpu.CompilerParams(
            dimension_semantics=("arbitrary", "arbitrary"), vmem_limit_bytes=VMEM_LIMIT_BYTES),
        name="prompt_mixer",
    )(x, gmix, win, sinks, *wout, *tail_weights, qc, mk, mv)


def _mem_kv_kernel(mem_ref, gmem_ref, wck_ref, wcv_ref, wcq_ref, wco_ref, gcq_ref, k_ref, v_ref, kq_ref, vo_ref,
                   *, nbatch):
    gmem = gmem_ref[...]
    gcq = gcq_ref[...] * CQ_SCALE

    def stream(i):
        hm = _rms(mem_ref[i], gmem).astype(_BF)
        k = _dot(hm, wck_ref[...])
        yield
        v = _dot(hm, wcv_ref[...])
        yield
        for h in range(N_CROSS_HEADS):
            head = slice(h * CROSS_HEAD_DIM, (h + 1) * CROSS_HEAD_DIM)
            kq_ref[i, :, head] = (_dot(wcq_ref[:, head], k[:, head].T.astype(_BF)) * gcq).astype(_BF)
            vo_ref[i, head, :] = _dot(v[:, head].astype(_BF), wco_ref[head, :]).astype(_BF)
            for half in range(CROSS_HEAD_DIM // 128):
                c0 = h * CROSS_HEAD_DIM + half * 128
                rows = pl.ds(half * N_CROSS_HEADS + h, N_MEM, stride=MEM_ROWS)
                k_ref[i, rows, :] = k[:, c0:c0 + 128]
                v_ref[i, rows, :] = v[:, c0:c0 + 128]
            yield

    _interleave([stream(i) for i in range(nbatch)], MEM_KV_SKEW)


def _mem_kv(mem, gmem, wck, wcv, wcq, wco, gcross_col, nbatch):
    B, M, D = mem.shape
    HM = N_CROSS_HEADS * M
    const = lambda *shape: pl.BlockSpec(shape, lambda b: (0,) * len(shape))
    per_seq = lambda *shape: pl.BlockSpec((nbatch,) + shape, lambda b: (b, 0, 0))
    return pl.pallas_call(
        functools.partial(_mem_kv_kernel, nbatch=nbatch),
        grid=(B // nbatch,),
        in_specs=[per_seq(M, D), const(1, D), const(D, D), const(D, D), const(D, D), const(D, D), const(D, 1)],
        out_specs=[per_seq(M * MEM_ROWS, 128), per_seq(M * MEM_ROWS, 128), per_seq(D, HM), per_seq(HM, D)],
        out_shape=[
            jax.ShapeDtypeStruct((B, M * MEM_ROWS, 128), _F32),
            jax.ShapeDtypeStruct((B, M * MEM_ROWS, 128), _F32),
            jax.ShapeDtypeStruct((B, D, HM), _BF),
            jax.ShapeDtypeStruct((B, HM, D), _BF),
        ],
        compiler_params=pltpu.CompilerParams(
            dimension_semantics=("arbitrary",), vmem_limit_bytes=VMEM_LIMIT_BYTES),
        name="prompt_mem_kv",
    )(mem, gmem, wck, wcv, wcq, wco, gcross_col)


def _mem_cache_unrows(rows):
    nb = rows.shape[0]
    c = rows.reshape(nb, N_MEM, CROSS_HEAD_DIM // 128, N_CROSS_HEADS, 128)
    return c.transpose(0, 1, 3, 2, 4).reshape(nb, N_MEM, N_CROSS_HEADS, CROSS_HEAD_DIM)


def _prompt_tail_kernel(x1_ref, kq_ref, vo_ref, gffn_ref, wup_ref, wdown_ref, gfinal_ref, y_ref, *, tile, sub):
    gffn, gfinal = gffn_ref[...], gfinal_ref[...]

    def stream(r0):
        x1 = x1_ref[0, r0:r0 + sub, :]
        hn = (x1 * lax.rsqrt(jnp.mean(x1 * x1, axis=-1, keepdims=True) + RMS_EPS)).astype(_BF)
        yield
        scores = _dot(hn, kq_ref[0])
        yield
        probs = [_softmax(scores[:, h * N_MEM:(h + 1) * N_MEM]).astype(_BF) for h in range(N_CROSS_HEADS)]
        x2 = x1 + _dot(jnp.concatenate(probs, axis=1), vo_ref[0])
        yield

        def store(y):
            y_ref[0, r0:r0 + sub, :] = y

        yield from _ffn_final_stages(x2, gffn, wup_ref, wdown_ref, gfinal, store)

    _interleave([stream(r0) for r0 in range(0, tile, sub)], TAIL_SKEW)


def _single(shape, index_map):
    return pl.BlockSpec(shape, index_map, pipeline_mode=pl.Buffered(1))


def _prompt_tail(x1, kq, vo, gffn, wup, wdown, gfinal, tile, sub):
    B, S, D = x1.shape
    const = lambda *shape: _single(shape, lambda b, s: (0,) * len(shape))
    return pl.pallas_call(
        functools.partial(_prompt_tail_kernel, tile=tile, sub=sub),
        grid=(B, S // tile),
        in_specs=[
            pl.BlockSpec((1, tile, D), lambda b, s: (b, s, 0)),
            pl.BlockSpec((1, D, N_CROSS_HEADS * N_MEM), lambda b, s: (b, 0, 0)),
            pl.BlockSpec((1, N_CROSS_HEADS * N_MEM, D), lambda b, s: (b, 0, 0)),
            const(1, D),
            const(D, D_FF),
            const(D_FF, D),
            const(1, D),
        ],
        out_specs=pl.BlockSpec((1, tile, D), lambda b, s: (b, s, 0)),
        out_shape=jax.ShapeDtypeStruct((B, S, D), _F32),
        compiler_params=pltpu.CompilerParams(
            dimension_semantics=("arbitrary", "arbitrary"), vmem_limit_bytes=VMEM_LIMIT_BYTES),
        name="prompt_tail",
    )(x1, kq, vo, gffn, wup, wdown, gfinal)


def _sample_mixer_kernel(x_ref, gmix_ref, win_ref, sinkcol_ref, ck_ref, cv_ref, st_ref,
                         wout_ref, wpool_ref, pscale_ref, gcross_ref, wcq_ref,
                         x1_ref, qc_ref, wk_ref, wv_ref, pool_ref, wcq_bf_ref, wouta_ref, woutb_ref,
                         kc_scr, vc_scr, u_scr, *, nbatch, sub, nseq, past_len):
    @pl.when(pl.program_id(0) == 0)
    def _():
        wcq_bf_ref[...] = wcq_ref[...].astype(_BF)
        _prep_wout_kernel(wout_ref, wpool_ref, pscale_ref, wouta_ref, woutb_ref)

    rows = sub * nseq
    nkeys = 2 * WINDOW
    gmix, gcross = gmix_ref[...], gcross_ref[...]
    sinkcol = sinkcol_ref[...][None]

    t = lax.broadcasted_iota(jnp.int32, (nseq, nkeys), 0)
    c = lax.broadcasted_iota(jnp.int32, (nseq, nkeys), 1)
    dist = jnp.where(c < WINDOW, WINDOW + t - c, t - (c - WINDOW))
    valid = (dist >= 0) & (dist <= WINDOW) & (c < WINDOW + nseq)
    distf = dist.astype(_F32)
    bias = jnp.concatenate([jnp.where(valid, -_slope(h) * distf, NEG_INF) for h in range(N_HEADS)], axis=0)[None]
    lane = lax.broadcasted_iota(jnp.int32, (rows, 2 * HEAD_DIM), 1)
    lo = lane < HEAD_DIM
    tpos = past_len + lax.broadcasted_iota(jnp.int32, (nseq, POOL_GROUP_WIDTH), 0)
    zpad = jnp.zeros((sub, nkeys - WINDOW, KV_WIDTH), _BF)

    def stream(b0):
        bs = slice(b0, b0 + sub)
        rs = slice(b0 * nseq, (b0 + sub) * nseq)
        x = x_ref[rs, :]
        proj = _dot(_rms(x, gmix).astype(_BF), win_ref[...])
        yield
        k_new = proj[:, K_OFF:K_OFF + KV_WIDTH].reshape(sub, nseq, KV_WIDTH)
        v_new = proj[:, V_OFF:V_OFF + KV_WIDTH].reshape(sub, nseq, KV_WIDTH)
        ck = jnp.swapaxes(ck_ref[bs], 1, 2)
        cv = jnp.swapaxes(cv_ref[bs], 1, 2)
        wk_ref[bs, :WINDOW - nseq, :] = ck[:, nseq:, :]
        wk_ref[bs, WINDOW - nseq:, :] = k_new
        wv_ref[bs, :WINDOW - nseq, :] = cv[:, nseq:, :]
        wv_ref[bs, WINDOW - nseq:, :] = v_new
        kc_scr[bs, WINDOW:, :] = zpad
        vc_scr[bs, WINDOW:, :] = zpad
        kc_scr[bs, :WINDOW, :] = ck.astype(_BF)
        vc_scr[bs, :WINDOW, :] = cv.astype(_BF)
        kc_scr[bs, WINDOW:WINDOW + nseq, :] = k_new.astype(_BF)
        vc_scr[bs, WINDOW:WINDOW + nseq, :] = v_new.astype(_BF)
        qsc = proj[:, :ATTN_WIDTH] * Q_SCALE
        q_lo, q_hi = [], []
        for p in range(GROUP):
            slab = qsc[:, p * 128:(p + 1) * 128]
            q_lo.append(jnp.where(lo, slab, 0.0).reshape(sub, nseq, 128))
            q_hi.append(jnp.where(lo, 0.0, slab).reshape(sub, nseq, 128))
        qm = jnp.concatenate(q_lo + q_hi, axis=1).astype(_BF)
        sc = jnp.einsum('bqd,bkd->bqk', qm, kc_scr[bs], preferred_element_type=_F32) + bias
        yield
        pr = _softmax_with_sink(sc, sinkcol).astype(_BF)
        o = jnp.einsum('bqk,bkd->bqd', pr, vc_scr[bs], preferred_element_type=_F32)
        yield
        attn = []
        for p in range(GROUP):
            o_lo = o[:, p * nseq:(p + 1) * nseq, :].reshape(rows, 128)
            o_hi = o[:, (p + GROUP) * nseq:(p + GROUP + 1) * nseq, :].reshape(rows, 128)
            attn.append(jnp.where(lo, o_lo, o_hi).astype(_BF))
        ext = POOL_PAD + nseq
        seq_rows = lambda r: pl.ds(b0 * ext + r, sub, stride=ext)
        pooled = []
        for g, w in enumerate(POOL_WINDOWS):
            c0 = g * POOL_GROUP_WIDTH
            u_scr[g, seq_rows(0), :] = jnp.zeros((sub, POOL_GROUP_WIDTH), _F32)
            for r in range(POOL_HIST):
                u_scr[g, seq_rows(POOL_PAD - POOL_HIST + r), :] = st_ref[r, bs, c0:c0 + POOL_GROUP_WIDTH]
            for i in range(sub):
                u_scr[g, (b0 + i) * ext + POOL_PAD:(b0 + i + 1) * ext, :] = (
                    proj[i * nseq:(i + 1) * nseq, U_OFF + c0:U_OFF + c0 + POOL_GROUP_WIDTH])
            for r in range(POOL_HIST):
                pool_ref[r, bs, c0:c0 + POOL_GROUP_WIDTH] = u_scr[g, seq_rows(ext - POOL_HIST + r), :]
            ug = u_scr[g, b0 * ext:(b0 + sub) * ext, :].reshape(sub, ext, POOL_GROUP_WIDTH)
            sw = _window_sums(ug, 1)[w][:, POOL_PAD:, :]
            cnt = jnp.minimum(tpos + 1, w).astype(_F32)
            pooled.append((sw / cnt[None] - ug[:, POOL_PAD:, :]).reshape(rows, POOL_GROUP_WIDTH).astype(_BF))
        yield
        x1 = x + _dot_panels(jnp.concatenate(attn + pooled, axis=1), (wouta_ref, woutb_ref))
        x1_ref[rs, :] = x1
        yield
        qc_ref[rs, :] = _dot(_rms(x1, gcross).astype(_BF), wcq_bf_ref[...]) * CQ_SCALE

    _interleave([stream(b0) for b0 in range(0, nbatch, sub)], SAMPLE_MIXER_SKEW)


def _sample_mixer(x, gmix, win, sinkcol, ck, cv, st, wout, wpool, pscale, gcross, wcq, nbatch, sub, nseq,
                  past_len):
    R, D = x.shape
    rows = nbatch * nseq
    const = lambda *shape: pl.BlockSpec(shape, lambda i: (0,) * len(shape))
    return pl.pallas_call(
        functools.partial(_sample_mixer_kernel, nbatch=nbatch, sub=sub, nseq=nseq, past_len=past_len),
        grid=(R // rows,),
        in_specs=[
            pl.BlockSpec((rows, D), lambda i: (i, 0)),
            const(1, D),
            const(D, IN_WIDTH),
            const(N_HEADS * nseq, 1),
            pl.BlockSpec((nbatch, WINDOW, KV_WIDTH), lambda i: (i, 0, 0)),
            pl.BlockSpec((nbatch, WINDOW, KV_WIDTH), lambda i: (i, 0, 0)),
            pl.BlockSpec((POOL_HIST, nbatch, POOL_WIDTH), lambda i: (0, i, 0)),
            const(D, D),
            const(len(POOL_WINDOWS), POOL_GROUP_WIDTH, POOL_GROUP_WIDTH),
            const(1, POOL_WIDTH),
            const(1, D),
            const(D, D),
        ],
        out_specs=[
            pl.BlockSpec((rows, D), lambda i: (i, 0)),
            pl.BlockSpec((rows, D), lambda i: (i, 0)),
            pl.BlockSpec((nbatch, WINDOW, KV_WIDTH), lambda i: (i, 0, 0)),
            pl.BlockSpec((nbatch, WINDOW, KV_WIDTH), lambda i: (i, 0, 0)),
            pl.BlockSpec((POOL_HIST, nbatch, POOL_WIDTH), lambda i: (0, i, 0)),
            const(D, D),
            const(D, WOUT_SPLIT),
            const(D, D - WOUT_SPLIT),
        ],
        out_shape=[
            jax.ShapeDtypeStruct((R, D), _F32),
            jax.ShapeDtypeStruct((R, D), _F32),
            jax.ShapeDtypeStruct((R // nseq, WINDOW, KV_WIDTH), _F32),
            jax.ShapeDtypeStruct((R // nseq, WINDOW, KV_WIDTH), _F32),
            jax.ShapeDtypeStruct((POOL_HIST, R // nseq, POOL_WIDTH), _F32),
            jax.ShapeDtypeStruct((D, D), _BF),
            jax.ShapeDtypeStruct((D, WOUT_SPLIT), _BF),
            jax.ShapeDtypeStruct((D, D - WOUT_SPLIT), _BF),
        ],
        scratch_shapes=[
            pltpu.VMEM((nbatch, 2 * WINDOW, KV_WIDTH), _BF),
            pltpu.VMEM((nbatch, 2 * WINDOW, KV_WIDTH), _BF),
            pltpu.VMEM((len(POOL_WINDOWS), nbatch * (POOL_PAD + nseq), POOL_GROUP_WIDTH), _F32),
        ],
        compiler_params=pltpu.CompilerParams(
            dimension_semantics=("arbitrary",), vmem_limit_bytes=VMEM_LIMIT_BYTES),
        name="sample_mixer",
    )(x, gmix, win, sinkcol, ck, cv, st, wout, wpool, pscale, gcross, wcq)


def _mem_cache_rows(cache):
    nb = cache.shape[0]
    c = cache.reshape(nb, N_MEM, N_CROSS_HEADS, CROSS_HEAD_DIM // 128, 128)
    return c.transpose(0, 1, 3, 2, 4).reshape(nb, N_MEM * MEM_ROWS, 128)


def _load_mem_head(ref, b, h):
    halves = [ref[b, pl.ds(half * N_CROSS_HEADS + h, N_MEM, stride=MEM_ROWS), :]
              for half in range(CROSS_HEAD_DIM // 128)]
    return jnp.concatenate(halves, axis=1)


def _sample_tail_kernel(x1_ref, o_ref, wco_ref, gffn_ref, wup_ref, wdown_ref, gfinal_ref, y_ref, *, tile, sub):
    gffn, gfinal = gffn_ref[...], gfinal_ref[...]

    def stream(r0):
        x2 = x1_ref[r0:r0 + sub, :] + _dot(o_ref[r0:r0 + sub, :].astype(_BF), wco_ref[...])
        yield

        def store(y):
            y_ref[r0:r0 + sub, :] = y

        yield from _ffn_final_stages(x2, gffn, wup_ref, wdown_ref, gfinal, store)

    _interleave([stream(r0) for r0 in range(0, tile, sub)], TAIL_SKEW)


def _sample_tail(x1, o, wco, gffn, wup, wdown, gfinal, tile, sub):
    R, D = x1.shape
    const = lambda *shape: _single(shape, lambda i: (0,) * len(shape))
    return pl.pallas_call(
        functools.partial(_sample_tail_kernel, tile=tile, sub=sub),
        grid=(R // tile,),
        in_specs=[
            pl.BlockSpec((tile, D), lambda i: (i, 0)),
            pl.BlockSpec((tile, D), lambda i: (i, 0)),
            const(D, D),
            const(1, D),
            const(D, D_FF),
            const(D_FF, D),
            const(1, D),
        ],
        out_specs=pl.BlockSpec((tile, D), lambda i: (i, 0)),
        out_shape=jax.ShapeDtypeStruct((R, D), _F32),
        compiler_params=pltpu.CompilerParams(
            dimension_semantics=("arbitrary",), vmem_limit_bytes=VMEM_LIMIT_BYTES),
        name="sample_tail",
    )(x1, o, wco, gffn, wup, wdown, gfinal)


PROMPT_TILE = 512
TAIL_TILE = 1024
TAIL_SUB = 256
TAIL_SKEW = 4
MEM_KV_BATCH = 2
MEM_KV_SKEW = 2
SAMPLE_MIXER_BATCH = 32
SAMPLE_MIXER_SUB = 16
SAMPLE_MIXER_SKEW = 3


def kernel(x_prompt, x_sample, cache_win_k, cache_win_v, state_pool, cache_mem_k, cache_mem_v, mem_prompt,
           g_mix, w_in, attn_sinks, w_pool, pool_scale, w_out, g_cross, g_mem, w_cq, w_ck, w_cv, w_co,
           g_ffn, w_up, w_down, g_final):
    depth = g_mix.shape[0]
    assert depth == 1, "one layer per step"
    B, S, D = x_prompt.shape
    DB, T, _ = x_sample.shape
    past_len = PAST_LEN
    l = 0

    win = jnp.concatenate([_pair_heads(w_in[l][:, :ATTN_WIDTH], 1), w_in[l][:, ATTN_WIDTH:]], axis=1).astype(_BF)
    gmix, gcross, gmem, gffn = (g[l].reshape(1, D) for g in (g_mix, g_cross, g_mem, g_ffn))
    gfinal = g_final.reshape(1, D)
    sinks = attn_sinks[l]

    xs = x_sample.reshape(DB * T, D)
    ck = cache_win_k[l].transpose(0, 2, 3, 1).reshape(DB, KV_WIDTH, WINDOW)
    cv = cache_win_v[l].transpose(0, 2, 3, 1).reshape(DB, KV_WIDTH, WINDOW)
    st = state_pool[l].transpose(1, 0, 2)
    sinkcol = jnp.repeat(sinks, T).reshape(N_HEADS * T, 1)
    x1s, qc, wk_s, wv_s, pool_s, wcq, *wout = _sample_mixer(
        xs, gmix, win, sinkcol, ck, cv, st, w_out[l], w_pool[l], pool_scale[l].reshape(1, POOL_WIDTH), gcross,
        w_cq[l], SAMPLE_MIXER_BATCH, SAMPLE_MIXER_SUB, T, past_len)

    mk = _mem_cache_rows(cache_mem_k[l])
    mv = _mem_cache_rows(cache_mem_v[l])
    x1p, klast, vlast, plast, wco, wup, wdown, wck, wcv, o_s = _prompt_mixer(
        x_prompt, gmix, win, sinks, wout, (w_co[l], w_up[l], w_down[l], w_ck[l], w_cv[l]), qc, mk, mv, PROMPT_TILE)
    mem_k, mem_v, mem_kq, mem_vo = _mem_kv(mem_prompt, gmem, wck, wcv, wcq, wco, g_cross[l].reshape(D, 1),
                                           MEM_KV_BATCH)
    y_prompt = _prompt_tail(x1p, mem_kq, mem_vo, gffn, wup, wdown, gfinal, TAIL_TILE, TAIL_SUB)

    y_sample = _sample_tail(x1s, o_s, wco, gffn, wup, wdown, gfinal, TAIL_TILE, TAIL_SUB).reshape(DB, T, D)

    return (
        y_prompt,
        y_sample,
        klast.reshape(B, N_KV_HEADS, HEAD_DIM, WINDOW).transpose(0, 3, 1, 2)[None],
        vlast.reshape(B, N_KV_HEADS, HEAD_DIM, WINDOW).transpose(0, 3, 1, 2)[None],
        plast[:, POOL_PAD - POOL_HIST:, :][None],
        _mem_cache_unrows(mem_k)[None],
        _mem_cache_unrows(mem_v)[None],
        wk_s.reshape(1, DB, WINDOW, N_KV_HEADS, HEAD_DIM),
        wv_s.reshape(1, DB, WINDOW, N_KV_HEADS, HEAD_DIM),
        pool_s.transpose(1, 0, 2)[None],
    )
```
